```python
import math
import jax, jax.numpy as jnp
from jax import lax
import numpy as np

D_MODEL = 1024
BATCH = 8
SEQ = 2048
DEPTH = 1
DEC_BATCH = 16
DEC_SEQ = 16
PAST_LEN = 4096

CHUNK = 64
WINDOW = 128
WIN_CHUNKS = WINDOW // CHUNK
N_HEADS = 16
N_KV_HEADS = 4
HEAD_DIM = 64
GROUP_REP = N_HEADS // N_KV_HEADS
ATTN_WIDTH = N_HEADS * HEAD_DIM
KV_WIDTH = N_KV_HEADS * HEAD_DIM
POOL_WINDOWS = (2, 4, 8, 16)
N_POOL_GROUPS = len(POOL_WINDOWS)
POOL_WIDTH = D_MODEL
POOL_GROUP = POOL_WIDTH // N_POOL_GROUPS
POOL_HIST = max(POOL_WINDOWS) - 1
N_BUCKETS = 32
MAX_DISTANCE = 128
EPS = 1e-6
NEG_INF = -1e30
SPLIT_SIZES = (ATTN_WIDTH, KV_WIDTH, KV_WIDTH, ATTN_WIDTH, POOL_WIDTH, POOL_WIDTH, D_MODEL, D_MODEL)
IN_WIDTH = sum(SPLIT_SIZES)

kernel_name = "hybrid_swa_sink_pool_stream_step"


def rmsnorm(x, gain):
    xf = x.astype(jnp.float32)
    y = xf * lax.rsqrt(jnp.mean(xf * xf, axis=-1, keepdims=True) + EPS)
    return (y * gain.astype(jnp.float32)).astype(x.dtype)


def t5_bucket(rel):
    nb = N_BUCKETS // 2
    ret = jnp.where(rel > 0, nb, 0)
    n = jnp.abs(rel)
    max_exact = nb // 2
    large = max_exact + (jnp.log(jnp.maximum(n, 1).astype(jnp.float32) / max_exact)
                         / math.log(MAX_DISTANCE / max_exact) * (nb - max_exact)).astype(jnp.int32)
    large = jnp.minimum(large, nb - 1)
    return ret + jnp.where(n < max_exact, n, large)


def rel_position_bias(table, nq, nk):
    rel = jnp.arange(nk, dtype=jnp.int32)[None, :] - WINDOW - jnp.arange(nq, dtype=jnp.int32)[:, None]
    return jnp.transpose(table[t5_bucket(rel)], (2, 0, 1)).astype(jnp.float32)


def branch_inputs(x, norm_gain, w_in, q_gain, k_gain):
    h = rmsnorm(x, norm_gain)
    z = h @ w_in
    pts, acc = [], 0
    for s in SPLIT_SIZES[:-1]:
        acc += s
        pts.append(acc)
    q, k, v, ag, pu, pg, ma, mp = jnp.split(z, pts, axis=-1)
    B, N = x.shape[:2]
    q = rmsnorm(q.reshape(B, N, N_HEADS, HEAD_DIM), q_gain)
    k = rmsnorm(k.reshape(B, N, N_KV_HEADS, HEAD_DIM), k_gain)
    v = v.reshape(B, N, N_KV_HEADS, HEAD_DIM)
    return q, k, v, ag, pu, pg, ma, mp


def sink_attention(q, k, v, bias, sinks, mask):
    Q, K = q.shape[-3], k.shape[-3]
    qg = q.reshape(*q.shape[:-2], N_KV_HEADS, GROUP_REP, HEAD_DIM)
    logits = jnp.einsum('...qgrd,...kgd->...grqk', qg, k).astype(jnp.float32) * (HEAD_DIM ** -0.5)
    logits = logits + bias.reshape(N_KV_HEADS, GROUP_REP, Q, K)
    if mask is not None:
        logits = jnp.where(mask, logits, NEG_INF)
    s = sinks.astype(jnp.float32).reshape(N_KV_HEADS, GROUP_REP, 1, 1)
    m = jnp.maximum(jnp.max(logits, axis=-1, keepdims=True), s)
    p = jnp.exp(logits - m)
    denom = jnp.sum(p, axis=-1, keepdims=True) + jnp.exp(s - m)
    out = jnp.einsum('...grqk,...kgd->...qgrd', (p / denom).astype(v.dtype), v)
    return out.reshape(*out.shape[:-3], ATTN_WIDTH)


def prompt_window_attention(q, k, v, bias, sinks):
    B, S = q.shape[:2]
    nc = S // CHUNK

    def band(t):
        tp = jnp.pad(t, ((0, 0), (WINDOW, 0), (0, 0), (0, 0)))
        tb = tp.reshape(B, nc + WIN_CHUNKS, CHUNK, *t.shape[2:])
        return jnp.concatenate([tb[:, i:i + nc] for i in range(WIN_CHUNKS + 1)], axis=2)

    kw, vw = band(k), band(v)
    qb = q.reshape(B, nc, CHUNK, N_HEADS, HEAD_DIM)
    kpos = jnp.arange(nc)[:, None] * CHUNK - WINDOW + jnp.arange(CHUNK + WINDOW)[None, :]
    mask = (kpos >= 0)[None, :, None, None, None, :]
    out = sink_attention(qb, kw, vw, bias, sinks, mask)
    return out.reshape(B, S, ATTN_WIDTH)


def pool_branch(u, hist, start, pool_w, pool_scale):
    B, N, _ = u.shape
    full = jnp.concatenate([hist.astype(u.dtype), u], axis=1).astype(jnp.float32)
    cs0 = jnp.pad(jnp.cumsum(full, axis=1), ((0, 0), (1, 0), (0, 0)))
    pos = start + jnp.arange(N)
    means = []
    for g, w in enumerate(POOL_WINDOWS):
        sl = slice(g * POOL_GROUP, (g + 1) * POOL_GROUP)
        ssum = cs0[:, POOL_HIST + 1:POOL_HIST + 1 + N, sl] - cs0[:, POOL_HIST + 1 - w:POOL_HIST + 1 - w + N, sl]
        cnt = jnp.minimum(w, pos + 1).astype(jnp.float32)[None, :, None]
        means.append(ssum / cnt)
    mixed = (jnp.concatenate(means, axis=-1) - u.astype(jnp.float32)).astype(u.dtype)
    mixed = jnp.einsum('bngc,gcd->bngd', mixed.reshape(B, N, N_POOL_GROUPS, POOL_GROUP), pool_w)
    return mixed.reshape(B, N, POOL_WIDTH) * pool_scale


def merge(x, attn_o, ag, pool_o, pg, ma, mp, w_attn_br, w_pool_br, w_out):
    a = (attn_o * jax.nn.silu(ag)) @ w_attn_br
    p = (pool_o * jax.nn.silu(pg)) @ w_pool_br
    return x + (jax.nn.sigmoid(ma) * a + jax.nn.sigmoid(mp) * p) @ w_out


def setup_inputs(seed: int = 0) -> dict:
    key = jax.random.key(seed)
    ks = jax.random.split(key, 16)

    def nrm(k, shape, scale):
        return jax.random.normal(k, shape, jnp.float32) * scale

    return {
        "x_prompt": nrm(ks[0], (BATCH, SEQ, D_MODEL), 1.0),
        "x_sample": nrm(ks[1], (DEC_BATCH, DEC_SEQ, D_MODEL), 1.0),
        "state_attn_k": nrm(ks[2], (DEPTH, DEC_BATCH, WINDOW, N_KV_HEADS, HEAD_DIM), 1.0),
        "state_attn_v": nrm(ks[3], (DEPTH, DEC_BATCH, WINDOW, N_KV_HEADS, HEAD_DIM), 1.0),
        "state_pool": nrm(ks[4], (DEPTH, DEC_BATCH, POOL_HIST, POOL_WIDTH), 1.0),
        "norm_gain": 1.0 + nrm(ks[5], (DEPTH, D_MODEL), 0.1),
        "w_in": nrm(ks[6], (DEPTH, D_MODEL, IN_WIDTH), D_MODEL ** -0.5),
        "q_norm_gain": 1.0 + nrm(ks[7], (DEPTH, HEAD_DIM), 0.1),
        "k_norm_gain": 1.0 + nrm(ks[8], (DEPTH, HEAD_DIM), 0.1),
        "attn_sinks": nrm(ks[9], (DEPTH, N_HEADS), 1.0),
        "rel_bias": nrm(ks[10], (N_BUCKETS, N_HEADS), 0.5),
        "pool_w": nrm(ks[11], (DEPTH, N_POOL_GROUPS, POOL_GROUP, POOL_GROUP), POOL_GROUP ** -0.5),
        "pool_scale": 1.0 + nrm(ks[12], (DEPTH, POOL_WIDTH), 0.1),
        "w_attn_br": nrm(ks[13], (DEPTH, ATTN_WIDTH, D_MODEL), ATTN_WIDTH ** -0.5),
        "w_pool_br": nrm(ks[14], (DEPTH, POOL_WIDTH, D_MODEL), POOL_WIDTH ** -0.5),
        "w_out": nrm(ks[15], (DEPTH, D_MODEL, D_MODEL), D_MODEL ** -0.5),
    }


def reference(x_prompt, x_sample, state_attn_k, state_attn_v, state_pool, norm_gain, w_in,
              q_norm_gain, k_norm_gain, attn_sinks, rel_bias, pool_w, pool_scale,
              w_attn_br, w_pool_br, w_out):
    n_s = x_sample.shape[1]
    bias_p = rel_position_bias(rel_bias, CHUNK, CHUNK + WINDOW)
    bias_s = rel_position_bias(rel_bias, n_s, WINDOW + n_s)
    xp, xs = x_prompt, x_sample
    pk, pv, pp, sk, sv, sp = [], [], [], [], [], []
    for l in range(DEPTH):
        q, k, v, ag, pu, pg, ma, mp = branch_inputs(xp, norm_gain[l], w_in[l], q_norm_gain[l], k_norm_gain[l])
        attn_o = prompt_window_attention(q, k, v, bias_p, attn_sinks[l])
        hist0 = jnp.zeros((xp.shape[0], POOL_HIST, POOL_WIDTH), pu.dtype)
        pool_o = pool_branch(pu, hist0, 0, pool_w[l], pool_scale[l])
        pk.append(k[:, -WINDOW:])
        pv.append(v[:, -WINDOW:])
        pp.append(pu[:, -POOL_HIST:])
        xp = merge(xp, attn_o, ag, pool_o, pg, ma, mp, w_attn_br[l], w_pool_br[l], w_out[l])
        q, k, v, ag, pu, pg, ma, mp = branch_inputs(xs, norm_gain[l], w_in[l], q_norm_gain[l], k_norm_gain[l])
        k_all = jnp.concatenate([state_attn_k[l].astype(k.dtype), k], axis=1)
        v_all = jnp.concatenate([state_attn_v[l].astype(v.dtype), v], axis=1)
        attn_o = sink_attention(q, k_all, v_all, bias_s, attn_sinks[l], None)
        pool_o = pool_branch(pu, state_pool[l], PAST_LEN, pool_w[l], pool_scale[l])
        sk.append(k_all[:, -WINDOW:])
        sv.append(v_all[:, -WINDOW:])
        sp.append(jnp.concatenate([state_pool[l].astype(pu.dtype), pu], axis=1)[:, -POOL_HIST:])
        xs = merge(xs, attn_o, ag, pool_o, pg, ma, mp, w_attn_br[l], w_pool_br[l], w_out[l])
    return (xp, xs, jnp.stack(pk), jnp.stack(pv), jnp.stack(pp), jnp.stack(sk), jnp.stack(sv), jnp.stack(sp))
```

```python
import functools
import math

import jax
import jax.numpy as jnp
from jax import lax
from jax.experimental import pallas as pl
from jax.experimental.pallas import tpu as pltpu

D_MODEL = 1024
CHUNK = 64
WINDOW = 128
N_HEADS = 16
N_KV_HEADS = 4
HEAD_DIM = 64
ATTN_WIDTH = N_HEADS * HEAD_DIM
KV_WIDTH = N_KV_HEADS * HEAD_DIM
POOL_WINDOWS = (2, 4, 8, 16)
POOL_WIDTH = D_MODEL
POOL_GROUP = POOL_WIDTH // len(POOL_WINDOWS)
POOL_HIST = max(POOL_WINDOWS) - 1
HIST_ROWS = POOL_HIST + 1
N_BUCKETS = 32
MAX_DISTANCE = 128
EPS = 1e-6
NEG_INF = -1e30

_SPLITS = (ATTN_WIDTH, KV_WIDTH, KV_WIDTH, ATTN_WIDTH, POOL_WIDTH, POOL_WIDTH, D_MODEL, D_MODEL)
_OFFS = tuple(sum(_SPLITS[:i]) for i in range(len(_SPLITS) + 1))
C_Q, C_K, C_V, C_AG, C_PU, C_PG, C_MA, C_MP = (slice(_OFFS[i], _OFFS[i + 1]) for i in range(8))

LANES = 128
MXU_DIM = 256
PAIR = 2 * CHUNK
TM = 256
VMEM_LIMIT = 60 * 1024 * 1024

BF16 = jnp.bfloat16
F32 = jnp.float32


def _dot(a, b):
    return jnp.dot(a, b, preferred_element_type=F32)


def _dot_nt(a, b):
    return lax.dot_general(a, b, (((1,), (1,)), ((), ())), preferred_element_type=F32)


def _rmsnorm_rows(x, gain):
    y = x * lax.rsqrt(jnp.mean(x * x, axis=-1, keepdims=True) + EPS)
    return y * gain


def _head_rmsnorm(z, ones_ref, gain):
    sq = z * z
    hi = sq.astype(BF16)
    lo = (sq - hi.astype(F32)).astype(BF16)
    ones = ones_ref[...]
    parts = []
    for c in range(z.shape[1] // MXU_DIM):
        sl = slice(c * MXU_DIM, (c + 1) * MXU_DIM)
        parts.append(_dot(hi[:, sl], ones) + _dot(lo[:, sl], ones))
    ss = parts[0] if len(parts) == 1 else jnp.concatenate(parts, axis=1)
    return (z * lax.rsqrt(ss * (1.0 / HEAD_DIM) + EPS)) * gain


def _split_lo_hi(a):
    ar = pltpu.roll(a, HEAD_DIM, axis=1)
    c0, c1 = a[:, :LANES], a[:, LANES:]
    r0, r1 = ar[:, :LANES], ar[:, LANES:]
    low = lax.broadcasted_iota(jnp.int32, (1, LANES), 1) < HEAD_DIM
    lo_src = (c0, r1, c1, r0)
    hi_src = (r0, c0, r1, c1)
    lo = [jnp.where(low, s, 0.0).astype(BF16) for s in lo_src]
    hi = [jnp.where(low, 0.0, s).astype(BF16) for s in hi_src]
    return lo, hi


def _softmax_rows(s, sink_col):
    m = jnp.maximum(jnp.max(s, axis=-1, keepdims=True), sink_col)
    p = jnp.exp(s - m)
    denom = jnp.sum(p, axis=-1, keepdims=True) + jnp.exp(sink_col - m)
    return (p * (1.0 / denom)).astype(BF16)


def _window_sums(full):
    outs = []
    for gi, w in enumerate(POOL_WINDOWS):
        s = full[:, gi * POOL_GROUP:(gi + 1) * POOL_GROUP]
        sh = 1
        while sh < w:
            s = s + pltpu.roll(s, sh, axis=0)
            sh *= 2
        outs.append(s)
    return outs


def _merge_tail(x, hb, attn_o, pool_mixed, win_ref, pw_ref, ps_ref, wabr_ref, wpbr_ref, wout_ref):
    ag = _dot(hb, win_ref[:, C_AG])
    a = _dot((attn_o * (ag * jax.nn.sigmoid(ag))).astype(BF16), wabr_ref[...])
    ma = _dot(hb, win_ref[:, C_MA])
    acc = jax.nn.sigmoid(ma) * a

    po = []
    for gi in range(len(POOL_WINDOWS)):
        po.append(_dot(pool_mixed[gi].astype(BF16), pw_ref[gi]))
    po = jnp.concatenate(po, axis=1) * ps_ref[...]
    pg = _dot(hb, win_ref[:, C_PG])
    p = _dot((po * (pg * jax.nn.sigmoid(pg))).astype(BF16), wpbr_ref[...])
    mp = _dot(hb, win_ref[:, C_MP])
    acc = acc + jax.nn.sigmoid(mp) * p
    return x + _dot(acc.astype(BF16), wout_ref[...])


def _bias_kernel(tbl_ref, bkp_ref, bks_ref, bp_ref, bs_ref):
    h = pl.program_id(0)

    def gather(bucket):
        acc = jnp.zeros(bucket.shape, F32)
        for b in range(N_BUCKETS):
            acc = jnp.where(bucket == b, tbl_ref[b, h], acc)
        return acc

    bk0 = bkp_ref[0]
    bk1 = bkp_ref[1]
    base = gather(bk0)
    bp_ref[0] = jnp.where(bk0 < 0, NEG_INF, base)
    bp_ref[1] = jnp.where(bk1 < 0, NEG_INF, base)
    bs_ref[...] = gather(bks_ref[...])


def _t5_bucket(rel):
    nb = N_BUCKETS // 2
    ret = jnp.where(rel > 0, nb, 0)
    n = jnp.abs(rel)
    max_exact = nb // 2
    large = max_exact + (jnp.log(jnp.maximum(n, 1).astype(jnp.float32) / max_exact)
                         / math.log(MAX_DISTANCE / max_exact) * (nb - max_exact)).astype(jnp.int32)
    large = jnp.minimum(large, nb - 1)
    return ret + jnp.where(n < max_exact, n, large)


def _bias_tables(rel_bias, n_s):
    qq = jnp.arange(PAIR, dtype=jnp.int32)[:, None]
    kk = jnp.arange(PAIR + WINDOW, dtype=jnp.int32)[None, :]
    bk = _t5_bucket(kk - WINDOW - qq)
    c0 = (qq // CHUNK) * CHUNK
    valid = (kk >= c0) & (kk < c0 + CHUNK + WINDOW)
    bkp = jnp.stack([jnp.where(valid, bk, -1), jnp.where(valid & (kk >= WINDOW), bk, -1)])
    nk_s = WINDOW + n_s
    bks = _t5_bucket(jnp.arange(nk_s, dtype=jnp.int32)[None, :] - WINDOW
                     - jnp.arange(n_s, dtype=jnp.int32)[:, None])

    def head_idx(h):
        return h // 4, h % 2, (h % 4) // 2

    bp, bs = pl.pallas_call(
        _bias_kernel,
        grid=(N_HEADS,),
        in_specs=[
            pl.BlockSpec(memory_space=pltpu.SMEM),
            pl.BlockSpec((2, PAIR, PAIR + WINDOW), lambda h: (0, 0, 0)),
            pl.BlockSpec((n_s, nk_s), lambda h: (0, 0)),
        ],
        out_specs=[
            pl.BlockSpec((2, None, None, None, PAIR, PAIR + WINDOW),
                         lambda h: (0,) + head_idx(h) + (0, 0)),
            pl.BlockSpec((None, None, None, n_s, nk_s), lambda h: head_idx(h) + (0, 0)),
        ],
        out_shape=[
            jax.ShapeDtypeStruct((2, N_KV_HEADS, 2, 2, PAIR, PAIR + WINDOW), F32),
            jax.ShapeDtypeStruct((N_KV_HEADS, 2, 2, n_s, nk_s), F32),
        ],
        name="bias_tables",
    )(rel_bias, bkp, bks)
    return (bp.reshape(2, N_KV_HEADS, 2, 2 * PAIR, PAIR + WINDOW),
            bs.reshape(N_KV_HEADS, 2, 2 * n_s, nk_s))


def _prompt_kernel(sinks_ref, x_ref, ng_ref, win_ref, qg_ref, kg_ref, ones_ref, bias_ref,
                   pw_ref, ps_ref, wabr_ref, wpbr_ref, wout_ref,
                   y_ref, pk_ref, pv_ref, pp_ref,
                   h_s, q_s, kv_s, attn_s, pbuf):
    t = pl.program_id(1)
    last = pl.num_programs(1) - 1

    @pl.when(t == 0)
    def _():
        kv_s[:, :, 0:WINDOW, :] = jnp.zeros((4, N_KV_HEADS, WINDOW, LANES), BF16)
        pbuf[0:HIST_ROWS, :] = jnp.zeros((HIST_ROWS, POOL_WIDTH), F32)

    x = x_ref[...]
    h_s[...] = _rmsnorm_rows(x, ng_ref[...]).astype(BF16)
    hb = h_s[...]

    q_s[...] = _head_rmsnorm(_dot(hb, win_ref[:, C_Q]), ones_ref, qg_ref[...]).astype(BF16)
    k = _head_rmsnorm(_dot(hb, win_ref[:, C_K]), ones_ref, kg_ref[...])
    v = _dot(hb, win_ref[:, C_V])

    @pl.when(t == last)
    def _():
        pk_ref[...] = k[TM - WINDOW:, :]
        pv_ref[...] = v[TM - WINDOW:, :]

    k_lo, k_hi = _split_lo_hi(k)
    v_lo, v_hi = _split_lo_hi(v)
    for g in range(N_KV_HEADS):
        kv_s[0, g, WINDOW:, :] = k_lo[g]
        kv_s[1, g, WINDOW:, :] = k_hi[g]
        kv_s[2, g, WINDOW:, :] = v_lo[g]
        kv_s[3, g, WINDOW:, :] = v_hi[g]

    top = lax.broadcasted_iota(jnp.int32, (2 * PAIR, 1), 0) < PAIR
    for p in range(TM // PAIR):
        r0 = p * PAIR
        rows = slice(r0, r0 + PAIR)
        keys = slice(r0, r0 + PAIR + WINDOW)
        var = jnp.where(t == 0, 1, 0) if p == 0 else 0
        for g in range(N_KV_HEADS):
            c = g * MXU_DIM
            lhs = jnp.concatenate([q_s[rows, c:c + LANES], q_s[rows, c + LANES:c + MXU_DIM]], axis=0)
            probs = []
            for j in range(2):
                s = _dot_nt(lhs, kv_s[j, g, keys, :]) + bias_ref[var, g, j]
                sink = jnp.where(top, sinks_ref[4 * g + j], sinks_ref[4 * g + 2 + j])
                probs.append(_softmax_rows(s, sink))
            pcat = jnp.concatenate(probs, axis=1)
            vcat = jnp.concatenate([kv_s[2, g, keys, :], kv_s[3, g, keys, :]], axis=0)
            o = _dot(pcat, vcat)
            attn_s[rows, c:c + LANES] = o[:PAIR]
            attn_s[rows, c + LANES:c + MXU_DIM] = o[PAIR:]

    pu = _dot(hb, win_ref[:, C_PU])
    pbuf[HIST_ROWS:, :] = pu

    @pl.when(t == last)
    def _():
        pp_ref[...] = pu[TM - HIST_ROWS:, :]

    sums = _window_sums(pbuf[...])
    pos1 = (t * TM + 1 + lax.broadcasted_iota(jnp.int32, (TM, 1), 0)).astype(F32)
    mixed = []
    for gi, w in enumerate(POOL_WINDOWS):
        inv_cnt = 1.0 / jnp.minimum(float(w), pos1)
        mixed.append(sums[gi][HIST_ROWS:, :] * inv_cnt - pu[:, gi * POOL_GROUP:(gi + 1) * POOL_GROUP])

    y_ref[...] = _merge_tail(x, hb, attn_s[...], mixed, win_ref, pw_ref, ps_ref,
                             wabr_ref, wpbr_ref, wout_ref)

    kv_s[:, :, 0:WINDOW, :] = kv_s[:, :, TM:TM + WINDOW, :]
    pbuf[0:HIST_ROWS, :] = pbuf[TM:TM + HIST_ROWS, :]


def _resident(shape):
    nd = len(shape)
    return pl.BlockSpec(shape, lambda b, t: (0,) * nd, pipeline_mode=pl.Buffered(1))


def _prompt_layer(x, sinks, ng, w_in, qg, kg, ones, bias_p, pw, ps, wabr, wpbr, wout):
    B, S, _ = x.shape
    assert S % TM == 0 and TM % PAIR == 0 and TM >= WINDOW
    in_w = w_in.shape[1]
    return pl.pallas_call(
        _prompt_kernel,
        grid=(B, S // TM),
        in_specs=[
            pl.BlockSpec(memory_space=pltpu.SMEM),
            pl.BlockSpec((None, TM, D_MODEL), lambda b, t: (b, t, 0)),
            _resident((1, D_MODEL)),
            _resident((D_MODEL, in_w)),
            _resident((1, ATTN_WIDTH)),
            _resident((1, KV_WIDTH)),
            _resident((MXU_DIM, MXU_DIM)),
            _resident(bias_p.shape),
            _resident(pw.shape),
            _resident((1, POOL_WIDTH)),
            _resident((ATTN_WIDTH, D_MODEL)),
            _resident((POOL_WIDTH, D_MODEL)),
            _resident((D_MODEL, D_MODEL)),
        ],
        out_specs=[
            pl.BlockSpec((None, TM, D_MODEL), lambda b, t: (b, t, 0)),
            pl.BlockSpec((None, WINDOW, KV_WIDTH), lambda b, t: (b, 0, 0)),
            pl.BlockSpec((None, WINDOW, KV_WIDTH), lambda b, t: (b, 0, 0)),
            pl.BlockSpec((None, HIST_ROWS, POOL_WIDTH), lambda b, t: (b, 0, 0)),
        ],
        out_shape=[
            jax.ShapeDtypeStruct((B, S, D_MODEL), F32),
            jax.ShapeDtypeStruct((B, WINDOW, KV_WIDTH), F32),
            jax.ShapeDtypeStruct((B, WINDOW, KV_WIDTH), F32),
            jax.ShapeDtypeStruct((B, HIST_ROWS, POOL_WIDTH), F32),
        ],
        scratch_shapes=[
            pltpu.VMEM((TM, D_MODEL), BF16),
            pltpu.VMEM((TM, ATTN_WIDTH), BF16),
            pltpu.VMEM((4, N_KV_HEADS, WINDOW + TM, LANES), BF16),
            pltpu.VMEM((TM, ATTN_WIDTH), F32),
            pltpu.VMEM((HIST_ROWS + TM, POOL_WIDTH), F32),
        ],
        compiler_params=pltpu.CompilerParams(
            dimension_semantics=("arbitrary", "arbitrary"),
            vmem_limit_bytes=VMEM_LIMIT,
        ),
        name="prompt_layer",
    )(sinks, x, ng, w_in, qg, kg, ones, bias_p, pw, ps, wabr, wpbr, wout)


def _sample_kernel(n_b, n_s, start,
                   sinks_ref, x_ref, sk_ref, sv_ref, sp_ref, ng_ref, win_ref, qg_ref, kg_ref,
                   ones_ref, bias_ref, pw_ref, ps_ref, wabr_ref, wpbr_ref, wout_ref,
                   y_ref, ok_ref, ov_ref, op_ref,
                   q_s, kv_s, attn_s):
    nk = WINDOW + n_s
    x = x_ref[...]
    hb = _rmsnorm_rows(x, ng_ref[...]).astype(BF16)

    q_s[...] = _head_rmsnorm(_dot(hb, win_ref[:, C_Q]), ones_ref, qg_ref[...]).astype(BF16)
    k = _head_rmsnorm(_dot(hb, win_ref[:, C_K]), ones_ref, kg_ref[...])
    v = _dot(hb, win_ref[:, C_V])

    k_all = jnp.concatenate([sk_ref[...], k.reshape(n_b, n_s, KV_WIDTH)], axis=1)
    v_all = jnp.concatenate([sv_ref[...], v.reshape(n_b, n_s, KV_WIDTH)], axis=1)
    ok_ref[...] = k_all[:, nk - WINDOW:, :]
    ov_ref[...] = v_all[:, nk - WINDOW:, :]

    k_lo, k_hi = _split_lo_hi(k_all.reshape(n_b * nk, KV_WIDTH))
    v_lo, v_hi = _split_lo_hi(v_all.reshape(n_b * nk, KV_WIDTH))
    for g in range(N_KV_HEADS):
        kv_s[0, g] = k_lo[g]
        kv_s[1, g] = k_hi[g]
        kv_s[2, g] = v_lo[g]
        kv_s[3, g] = v_hi[g]

    top = lax.broadcasted_iota(jnp.int32, (2 * n_s, 1), 0) < n_s

    def per_stream(bb, carry):
        rows = pl.ds(pl.multiple_of(bb * n_s, n_s), n_s)
        keys = pl.ds(pl.multiple_of(bb * nk, 16), nk)
        for g in range(N_KV_HEADS):
            c = g * MXU_DIM
            lhs = jnp.concatenate([q_s[rows, c:c + LANES], q_s[rows, c + LANES:c + MXU_DIM]], axis=0)
            o = jnp.zeros((2 * n_s, LANES), F32)
            for j in range(2):
                s = _dot_nt(lhs, kv_s[j, g, keys, :]) + bias_ref[g, j]
                sink = jnp.where(top, sinks_ref[4 * g + j], sinks_ref[4 * g + 2 + j])
                o = o + _dot(_softmax_rows(s, sink), kv_s[2 + j, g, keys, :])
            attn_s[rows, c:c + LANES] = o[:n_s]
            attn_s[rows, c + LANES:c + MXU_DIM] = o[n_s:]
        return carry

    lax.fori_loop(0, n_b, per_stream, 0)

    pu = _dot(hb, win_ref[:, C_PU])
    pu3 = pu.reshape(n_b, n_s, POOL_WIDTH)
    full3 = jnp.concatenate([sp_ref[...], pu3], axis=1)
    op_ref[...] = full3[:, n_s:, :]
    seg = HIST_ROWS + n_s
    sums = _window_sums(full3.reshape(n_b * seg, POOL_WIDTH))
    pos1 = (start + 1 + lax.broadcasted_iota(jnp.int32, (n_s, 1), 0)).astype(F32)
    mixed = []
    for gi, w in enumerate(POOL_WINDOWS):
        inv_cnt = 1.0 / jnp.minimum(float(w), pos1)
        sw = sums[gi].reshape(n_b, seg, POOL_GROUP)[:, HIST_ROWS:, :] * inv_cnt
        mixed.append(sw.reshape(n_b * n_s, POOL_GROUP) - pu[:, gi * POOL_GROUP:(gi + 1) * POOL_GROUP])

    y_ref[...] = _merge_tail(x, hb, attn_s[...], mixed, win_ref, pw_ref, ps_ref,
                             wabr_ref, wpbr_ref, wout_ref)


def _sample_layer(x, sk, sv, sp, start, sinks, ng, w_in, qg, kg, ones, bias_s, pw, ps, wabr, wpbr, wout):
    n_b, n_s, _ = x.shape
    nk = WINDOW + n_s
    assert n_s % 16 == 0 and n_s >= POOL_HIST
    n_tok = n_b * n_s
    vmem = pl.BlockSpec(memory_space=pltpu.VMEM)
    y, ok, ov, op = pl.pallas_call(
        functools.partial(_sample_kernel, n_b, n_s, start),
        in_specs=[pl.BlockSpec(memory_space=pltpu.SMEM)] + [vmem] * 15,
        out_specs=[vmem] * 4,
        out_shape=[
            jax.ShapeDtypeStruct((n_tok, D_MODEL), F32),
            jax.ShapeDtypeStruct((n_b, WINDOW, KV_WIDTH), F32),
            jax.ShapeDtypeStruct((n_b, WINDOW, KV_WIDTH), F32),
            jax.ShapeDtypeStruct((n_b, HIST_ROWS, POOL_WIDTH), F32),
        ],
        scratch_shapes=[
            pltpu.VMEM((n_tok, ATTN_WIDTH), BF16),
            pltpu.VMEM((4, N_KV_HEADS, n_b * nk, LANES), BF16),
            pltpu.VMEM((n_tok, ATTN_WIDTH), F32),
        ],
        compiler_params=pltpu.CompilerParams(vmem_limit_bytes=VMEM_LIMIT),
        name="sample_layer",
    )(sinks, x.reshape(n_tok, D_MODEL), sk, sv, sp, ng, w_in, qg, kg, ones, bias_s,
      pw, ps, wabr, wpbr, wout)
    return y.reshape(n_b, n_s, D_MODEL), ok, ov, op


def kernel(x_prompt, x_sample, state_attn_k, state_attn_v, state_pool, norm_gain, w_in, q_norm_gain, k_norm_gain, attn_sinks, rel_bias, pool_w, pool_scale, w_attn_br, w_pool_br, w_out):
    depth = w_in.shape[0]
    n_b, n_s, _ = x_sample.shape
    past_len = 4096
    bias_p, bias_s = _bias_tables(rel_bias, n_s)
    seg = jnp.arange(MXU_DIM) // HEAD_DIM
    ones = (seg[:, None] == seg[None, :]).astype(BF16)

    xp, xs = x_prompt, x_sample
    pk, pv, pp, sk, sv, sp = [], [], [], [], [], []
    for l in range(depth):
        params = (
            attn_sinks[l],
            norm_gain[l].reshape(1, D_MODEL),
            w_in[l].astype(BF16),
            jnp.tile(q_norm_gain[l] * (HEAD_DIM ** -0.5), N_HEADS).reshape(1, ATTN_WIDTH),
            jnp.tile(k_norm_gain[l], N_KV_HEADS).reshape(1, KV_WIDTH),
            ones,
        )
        tail = (
            pool_w[l].astype(BF16),
            pool_scale[l].reshape(1, POOL_WIDTH),
            w_attn_br[l].astype(BF16),
            w_pool_br[l].astype(BF16),
            w_out[l].astype(BF16),
        )
        xp, k_l, v_l, p_l = _prompt_layer(xp, *params, bias_p, *tail)
        pk.append(k_l.reshape(-1, WINDOW, N_KV_HEADS, HEAD_DIM))
        pv.append(v_l.reshape(-1, WINDOW, N_KV_HEADS, HEAD_DIM))
        pp.append(p_l[:, 1:, :])

        xs, k_l, v_l, p_l = _sample_layer(
            xs,
            state_attn_k[l].reshape(n_b, WINDOW, KV_WIDTH),
            state_attn_v[l].reshape(n_b, WINDOW, KV_WIDTH),
            jnp.pad(state_pool[l], ((0, 0), (1, 0), (0, 0))),
            past_len, *params, bias_s, *tail)
        sk.append(k_l.reshape(n_b, WINDOW, N_KV_HEADS, HEAD_DIM))
        sv.append(v_l.reshape(n_b, WINDOW, N_KV_HEADS, HEAD_DIM))
        sp.append(p_l[:, 1:, :])
    return (xp, xs, jnp.stack(pk), jnp.stack(pv), jnp.stack(pp),
            jnp.stack(sk), jnp.stack(sv), jnp.stack(sp))
```

```python
import functools
import math

import jax
import jax.numpy as jnp
from jax import lax
from jax.experimental import pallas as pl
from jax.experimental.pallas import tpu as pltpu

D_MODEL = 1024
CHUNK = 64
WINDOW = 128
N_HEADS = 16
N_KV_HEADS = 4
GROUP_REP = N_HEADS // N_KV_HEADS
HEAD_DIM = 64
ATTN_WIDTH = N_HEADS * HEAD_DIM
KV_WIDTH = N_KV_HEADS * HEAD_DIM
POOL_WINDOWS = (2, 4, 8, 16)
POOL_WIDTH = D_MODEL
POOL_GROUP = POOL_WIDTH // len(POOL_WINDOWS)
POOL_HIST = max(POOL_WINDOWS) - 1
HIST_ROWS = POOL_HIST + 1
N_BUCKETS = 32
MAX_DISTANCE = 128
EPS = 1e-6
NEG_INF = -1e30

_SPLITS = (ATTN_WIDTH, KV_WIDTH, KV_WIDTH, ATTN_WIDTH, POOL_WIDTH, POOL_WIDTH, D_MODEL, D_MODEL)
_OFFS = tuple(sum(_SPLITS[:i]) for i in range(len(_SPLITS) + 1))
C_Q, C_K, C_V, C_AG, C_PU, C_PG, C_MA, C_MP = (slice(_OFFS[i], _OFFS[i + 1]) for i in range(8))

LANES = 128
MXU_DIM = 256
PAIR = 2 * CHUNK
PAIR_KEYS = PAIR + WINDOW
TM = 256
PREFILL = 8
VMEM_LIMIT = 60 * 1024 * 1024

BF16 = jnp.bfloat16
F32 = jnp.float32


def _dot(a, b):
    return jnp.dot(a, b, preferred_element_type=F32)


def _dot_nt(a, b):
    return lax.dot_general(a, b, (((1,), (1,)), ((), ())), preferred_element_type=F32)


def _rmsnorm_rows(x, gain):
    y = x * lax.rsqrt(jnp.mean(x * x, axis=-1, keepdims=True) + EPS)
    return y * gain


def _head_rmsnorm(z, ones_ref, gain):
    sq = z * z
    hi = sq.astype(BF16)
    lo = (sq - hi.astype(F32)).astype(BF16)
    ones = ones_ref[...]
    parts = []
    for c in range(z.shape[1] // MXU_DIM):
        sl = slice(c * MXU_DIM, (c + 1) * MXU_DIM)
        parts.append(_dot(hi[:, sl], ones) + _dot(lo[:, sl], ones))
    ss = parts[0] if len(parts) == 1 else jnp.concatenate(parts, axis=1)
    return (z * lax.rsqrt(ss * (1.0 / HEAD_DIM) + EPS)) * gain


def _split_lo_hi(a):
    ar = pltpu.roll(a, HEAD_DIM, axis=1)
    c0, c1 = a[:, :LANES], a[:, LANES:]
    r0, r1 = ar[:, :LANES], ar[:, LANES:]
    low = lax.broadcasted_iota(jnp.int32, (1, LANES), 1) < HEAD_DIM
    lo_src = (c0, r1, c1, r0)
    hi_src = (r0, c0, r1, c1)
    lo = [jnp.where(low, s, 0.0).astype(BF16) for s in lo_src]
    hi = [jnp.where(low, 0.0, s).astype(BF16) for s in hi_src]
    return lo, hi


def _softmax_rows(s, sink_col):
    m = jnp.maximum(jnp.max(s, axis=-1, keepdims=True), sink_col)
    p = jnp.exp(s - m)
    denom = jnp.sum(p, axis=-1, keepdims=True) + jnp.exp(sink_col - m)
    return (p * (1.0 / denom)).astype(BF16)


def _window_sums(full):
    outs = []
    for gi, w in enumerate(POOL_WINDOWS):
        s = full[:, gi * POOL_GROUP:(gi + 1) * POOL_GROUP]
        sh = 1
        while sh < w:
            s = s + pltpu.roll(s, sh, axis=0)
            sh *= 2
        outs.append(s)
    return outs


def _merge_tail(x, attn_o, pool_mixed, gates, pw_ref, ps_ref, wabr_ref, wpbr_ref, wout_ref):
    ag, pg, ma, mp = gates
    po = []
    for gi in range(len(POOL_WINDOWS)):
        po.append(_dot(pool_mixed[gi].astype(BF16), pw_ref[gi]))
    po = jnp.concatenate(po, axis=1) * ps_ref[...]
    p = _dot((po * (pg * jax.nn.sigmoid(pg))).astype(BF16), wpbr_ref[...])
    a = _dot((attn_o * (ag * jax.nn.sigmoid(ag))).astype(BF16), wabr_ref[...])
    acc = jax.nn.sigmoid(ma) * a + jax.nn.sigmoid(mp) * p
    return x + _dot(acc.astype(BF16), wout_ref[...])


def _bias_kernel(tbl_ref, bkp_ref, bks_ref, bp_ref, bs_ref):
    h = pl.program_id(0)

    def gather(bucket):
        acc = jnp.zeros(bucket.shape, F32)
        for b in range(N_BUCKETS):
            acc = jnp.where(bucket == b, tbl_ref[b, h], acc)
        return acc

    bk0 = bkp_ref[0]
    bk1 = bkp_ref[1]
    base = gather(bk0)
    bp_ref[0] = jnp.where(bk0 < 0, NEG_INF, base)
    bp_ref[1] = jnp.where(bk1 < 0, NEG_INF, base)
    bs_ref[...] = gather(bks_ref[...])


def _t5_bucket(rel):
    nb = N_BUCKETS // 2
    ret = jnp.where(rel > 0, nb, 0)
    n = jnp.abs(rel)
    max_exact = nb // 2
    large = max_exact + (jnp.log(jnp.maximum(n, 1).astype(jnp.float32) / max_exact)
                         / math.log(MAX_DISTANCE / max_exact) * (nb - max_exact)).astype(jnp.int32)
    large = jnp.minimum(large, nb - 1)
    return ret + jnp.where(n < max_exact, n, large)


def _bias_tables(rel_bias, n_s):
    kk = jnp.arange(PAIR_KEYS, dtype=jnp.int32)[:, None]
    qq = jnp.arange(PAIR, dtype=jnp.int32)[None, :]
    bk = _t5_bucket(kk - WINDOW - qq)
    c0 = (qq // CHUNK) * CHUNK
    valid = (kk >= c0) & (kk < c0 + CHUNK + WINDOW)
    bkp = jnp.stack([jnp.where(valid, bk, -1), jnp.where(valid & (kk >= WINDOW), bk, -1)])
    nk_s = WINDOW + n_s
    bks = _t5_bucket(jnp.arange(nk_s, dtype=jnp.int32)[None, :] - WINDOW
                     - jnp.arange(n_s, dtype=jnp.int32)[:, None])

    def sample_idx(h):
        return h // 4, h % 2, (h % 4) // 2

    bp, bs = pl.pallas_call(
        _bias_kernel,
        grid=(N_HEADS,),
        in_specs=[
            pl.BlockSpec(memory_space=pltpu.SMEM),
            pl.BlockSpec((2, PAIR_KEYS, PAIR), lambda h: (0, 0, 0)),
            pl.BlockSpec((n_s, nk_s), lambda h: (0, 0)),
        ],
        out_specs=[
            pl.BlockSpec((2, None, PAIR_KEYS, PAIR), lambda h: (0, h // GROUP_REP, 0, h % GROUP_REP)),
            pl.BlockSpec((None, None, None, n_s, nk_s), lambda h: sample_idx(h) + (0, 0)),
        ],
        out_shape=[
            jax.ShapeDtypeStruct((2, N_KV_HEADS, PAIR_KEYS, GROUP_REP * PAIR), F32),
            jax.ShapeDtypeStruct((N_KV_HEADS, 2, 2, n_s, nk_s), F32),
        ],
        name="bias_tables",
    )(rel_bias, bkp, bks)
    return bp, bs.reshape(N_KV_HEADS, 2, 2 * n_s, nk_s)


def _prompt_kernel(x_ref, ng_ref, win_ref, wqt_ref, wvt_ref, qgt_ref, kg_ref, ones_ref,
                   bias_ref, sink_ref, pw_ref, ps_ref, wabr_ref, wpbr_ref, wout_ref,
                   y_ref, pk_ref, pv_ref, pp_ref,
                   h_s, qt_s, k_s, vt_s, attn_s, pbuf):
    t = pl.program_id(1)
    last = pl.num_programs(1) - 1

    @pl.when(t == 0)
    def _():
        k_s[:, 0:WINDOW, :] = jnp.zeros((2, WINDOW, LANES), BF16)
        vt_s[:, 0:WINDOW] = jnp.zeros((KV_WIDTH, WINDOW), BF16)
        pbuf[0:HIST_ROWS, :] = jnp.zeros((HIST_ROWS, POOL_WIDTH), F32)

    x = x_ref[...]
    h_s[...] = _rmsnorm_rows(x, ng_ref[...]).astype(BF16)
    hb = h_s[...]

    n_c = D_MODEL // MXU_DIM
    fill_cols = [slice(c.start + i * MXU_DIM, c.start + (i + 1) * MXU_DIM)
                 for c in (C_PU, C_PG, C_AG, C_MA, C_MP) for i in range(n_c)]
    filled = []

    def fill(upto):
        for cols in fill_cols[len(filled):upto]:
            filled.append(_dot(hb, win_ref[:, cols]))

    zqt = _dot_nt(wqt_ref[...], hb)
    zk = _dot(hb, win_ref[:, C_K])
    vt = _dot_nt(wvt_ref[...], hb)
    fill(PREFILL)

    for h in range(N_HEADS):
        hs = slice(h * HEAD_DIM, (h + 1) * HEAD_DIM)
        zh = zqt[hs, :]
        ms = jnp.sum(zh * zh, axis=0, keepdims=True) * (1.0 / HEAD_DIM)
        qt_s[hs, :] = ((zh * lax.rsqrt(ms + EPS)) * qgt_ref[hs, :]).astype(BF16)
    k = _head_rmsnorm(zk, ones_ref, kg_ref[...])

    pk_ref[...] = k[TM - WINDOW:, :]
    pv_ref[...] = vt[:, TM - WINDOW:].T

    k_s[0, WINDOW:, :] = k[:, :LANES].astype(BF16)
    k_s[1, WINDOW:, :] = k[:, LANES:].astype(BF16)
    vt_s[:, WINDOW:] = vt.astype(BF16)

    zeros_half = jnp.zeros((HEAD_DIM, PAIR), BF16)

    def scores(p, g):
        tok = slice(p * PAIR, (p + 1) * PAIR)
        keys = slice(p * PAIR, p * PAIR + PAIR_KEYS)
        var = jnp.where(t == 0, 1, 0) if p == 0 else 0
        cols = []
        for r in range(GROUP_REP):
            h = g * GROUP_REP + r
            qh = qt_s[h * HEAD_DIM:(h + 1) * HEAD_DIM, tok]
            cols.append(jnp.concatenate([qh, zeros_half] if g % 2 == 0 else [zeros_half, qh], axis=0))
        rhs = jnp.concatenate(cols, axis=1)
        return _dot(k_s[g // 2, keys, :], rhs) + bias_ref[var, g]

    def finish(p, g, s):
        tok = slice(p * PAIR, (p + 1) * PAIR)
        keys = slice(p * PAIR, p * PAIR + PAIR_KEYS)
        sink = sink_ref[g]
        m = jnp.maximum(jnp.max(s, axis=0, keepdims=True), sink)
        e = jnp.exp(s - m)
        denom = jnp.sum(e, axis=0, keepdims=True) + jnp.exp(sink - m)
        vg = vt_s[g * HEAD_DIM:(g + 1) * HEAD_DIM, keys]
        o_t = _dot(vg, e.astype(BF16)) * (1.0 / denom)
        c = g * MXU_DIM
        attn_s[tok, c:c + LANES] = jnp.concatenate(
            [o_t[:, 0:PAIR], o_t[:, PAIR:2 * PAIR]], axis=0).T
        attn_s[tok, c + LANES:c + MXU_DIM] = jnp.concatenate(
            [o_t[:, 2 * PAIR:3 * PAIR], o_t[:, 3 * PAIR:]], axis=0).T

    pu = jnp.concatenate(filled[:n_c], axis=1)
    pbuf[HIST_ROWS:, :] = pu
    pp_ref[...] = pu[TM - HIST_ROWS:, :]
    sums = _window_sums(pbuf[...])
    pos1 = (t * TM + 1 + lax.broadcasted_iota(jnp.int32, (TM, 1), 0)).astype(F32)
    mixed = []
    for gi, w in enumerate(POOL_WINDOWS):
        inv_cnt = 1.0 / jnp.minimum(float(w), pos1)
        mixed.append(sums[gi][HIST_ROWS:, :] * inv_cnt - pu[:, gi * POOL_GROUP:(gi + 1) * POOL_GROUP])

    blocks = [(p, g) for p in range(TM // PAIR) for g in range(N_KV_HEADS)]
    s_next = scores(*blocks[0])
    for i, blk in enumerate(blocks):
        s_cur = s_next
        if i + 1 < len(blocks):
            s_next = scores(*blocks[i + 1])
        fill(PREFILL + (len(fill_cols) - PREFILL) * (i + 1) // len(blocks))
        finish(*blk, s_cur)
    pg, ag, ma, mp = (jnp.concatenate(filled[j * n_c:(j + 1) * n_c], axis=1) for j in range(1, 5))

    y_ref[...] = _merge_tail(x, attn_s[...], mixed, (ag, pg, ma, mp), pw_ref, ps_ref,
                             wabr_ref, wpbr_ref, wout_ref)

    k_s[:, 0:WINDOW, :] = k_s[:, TM:TM + WINDOW, :]
    vt_s[:, 0:WINDOW] = vt_s[:, TM:TM + WINDOW]
    pbuf[0:HIST_ROWS, :] = pbuf[TM:TM + HIST_ROWS, :]


def _resident(shape):
    nd = len(shape)
    return pl.BlockSpec(shape, lambda b, t: (0,) * nd, pipeline_mode=pl.Buffered(1))


def _prompt_layer(x, ng, w_in, wqt, wvt, qgt, kg, ones, bias_p, sink_rows, pw, ps, wabr, wpbr, wout):
    B, S, _ = x.shape
    assert S % TM == 0 and TM % PAIR == 0 and TM >= WINDOW
    consts = (ng, w_in, wqt, wvt, qgt, kg, ones, bias_p, sink_rows, pw, ps, wabr, wpbr, wout)
    return pl.pallas_call(
        _prompt_kernel,
        grid=(B, S // TM),
        in_specs=[pl.BlockSpec((None, TM, D_MODEL), lambda b, t: (b, t, 0))]
                 + [_resident(c.shape) for c in consts],
        out_specs=[
            pl.BlockSpec((None, TM, D_MODEL), lambda b, t: (b, t, 0)),
            pl.BlockSpec((None, WINDOW, KV_WIDTH), lambda b, t: (b, 0, 0)),
            pl.BlockSpec((None, WINDOW, KV_WIDTH), lambda b, t: (b, 0, 0)),
            pl.BlockSpec((None, HIST_ROWS, POOL_WIDTH), lambda b, t: (b, 0, 0)),
        ],
        out_shape=[
            jax.ShapeDtypeStruct((B, S, D_MODEL), F32),
            jax.ShapeDtypeStruct((B, WINDOW, KV_WIDTH), F32),
            jax.ShapeDtypeStruct((B, WINDOW, KV_WIDTH), F32),
            jax.ShapeDtypeStruct((B, HIST_ROWS, POOL_WIDTH), F32),
        ],
        scratch_shapes=[
            pltpu.VMEM((TM, D_MODEL), BF16),
            pltpu.VMEM((ATTN_WIDTH, TM), BF16),
            pltpu.VMEM((2, WINDOW + TM, LANES), BF16),
            pltpu.VMEM((KV_WIDTH, WINDOW + TM), BF16),
            pltpu.VMEM((TM, ATTN_WIDTH), F32),
            pltpu.VMEM((HIST_ROWS + TM, POOL_WIDTH), F32),
        ],
        compiler_params=pltpu.CompilerParams(
            dimension_semantics=("arbitrary", "arbitrary"),
            vmem_limit_bytes=VMEM_LIMIT,
        ),
        name="prompt_layer",
    )(x, *consts)


def _sample_kernel(n_b, n_s, start,
                   sinks_ref, x_ref, sk_ref, sv_ref, sp_ref, ng_ref, win_ref, qg_ref, kg_ref,
                   ones_ref, bias_ref, pw_ref, ps_ref, wabr_ref, wpbr_ref, wout_ref,
                   y_ref, ok_ref, ov_ref, op_ref,
                   q_s, kv_s, attn_s):
    nk = WINDOW + n_s
    x = x_ref[...]
    hb = _rmsnorm_rows(x, ng_ref[...]).astype(BF16)

    q_s[...] = _head_rmsnorm(_dot(hb, win_ref[:, C_Q]), ones_ref, qg_ref[...]).astype(BF16)
    k = _head_rmsnorm(_dot(hb, win_ref[:, C_K]), ones_ref, kg_ref[...])
    v = _dot(hb, win_ref[:, C_V])

    k_all = jnp.concatenate([sk_ref[...], k.reshape(n_b, n_s, KV_WIDTH)], axis=1)
    v_all = jnp.concatenate([sv_ref[...], v.reshape(n_b, n_s, KV_WIDTH)], axis=1)
    ok_ref[...] = k_all[:, nk - WINDOW:, :]
    ov_ref[...] = v_all[:, nk - WINDOW:, :]

    k_lo, k_hi = _split_lo_hi(k_all.reshape(n_b * nk, KV_WIDTH))
    v_lo, v_hi = _split_lo_hi(v_all.reshape(n_b * nk, KV_WIDTH))
    for g in range(N_KV_HEADS):
        kv_s[0, g] = k_lo[g]
        kv_s[1, g] = k_hi[g]
        kv_s[2, g] = v_lo[g]
        kv_s[3, g] = v_hi[g]

    top = lax.broadcasted_iota(jnp.int32, (2 * n_s, 1), 0) < n_s

    def per_stream(bb, carry):
        rows = pl.ds(pl.multiple_of(bb * n_s, n_s), n_s)
        keys = pl.ds(pl.multiple_of(bb * nk, 16), nk)
        for g in range(N_KV_HEADS):
            c = g * MXU_DIM
            lhs = jnp.concatenate([q_s[rows, c:c + LANES], q_s[rows, c + LANES:c + MXU_DIM]], axis=0)
            o = jnp.zeros((2 * n_s, LANES), F32)
            for j in range(2):
                s = _dot_nt(lhs, kv_s[j, g, keys, :]) + bias_ref[g, j]
                sink = jnp.where(top, sinks_ref[4 * g + j], sinks_ref[4 * g + 2 + j])
                o = o + _dot(_softmax_rows(s, sink), kv_s[2 + j, g, keys, :])
            attn_s[rows, c:c + LANES] = o[:n_s]
            attn_s[rows, c + LANES:c + MXU_DIM] = o[n_s:]
        return carry

    lax.fori_loop(0, n_b, per_stream, 0)

    pu = _dot(hb, win_ref[:, C_PU])
    pu3 = pu.reshape(n_b, n_s, POOL_WIDTH)
    full3 = jnp.concatenate([sp_ref[...], pu3], axis=1)
    op_ref[...] = full3[:, n_s:, :]
    seg = HIST_ROWS + n_s
    sums = _window_sums(full3.reshape(n_b * seg, POOL_WIDTH))
    pos1 = (start + 1 + lax.broadcasted_iota(jnp.int32, (n_s, 1), 0)).astype(F32)
    mixed = []
    for gi, w in enumerate(POOL_WINDOWS):
        inv_cnt = 1.0 / jnp.minimum(float(w), pos1)
        sw = sums[gi].reshape(n_b, seg, POOL_GROUP)[:, HIST_ROWS:, :] * inv_cnt
        mixed.append(sw.reshape(n_b * n_s, POOL_GROUP) - pu[:, gi * POOL_GROUP:(gi + 1) * POOL_GROUP])

    gates = tuple(_dot(hb, win_ref[:, c]) for c in (C_AG, C_PG, C_MA, C_MP))
    y_ref[...] = _merge_tail(x, attn_s[...], mixed, gates, pw_ref, ps_ref,
                             wabr_ref, wpbr_ref, wout_ref)


def _sample_layer(x, sk, sv, sp, start, sinks, ng, w_in, qg, kg, ones, bias_s, pw, ps, wabr, wpbr, wout):
    n_b, n_s, _ = x.shape
    nk = WINDOW + n_s
    assert n_s % 16 == 0 and n_s >= POOL_HIST
    n_tok = n_b * n_s
    vmem = pl.BlockSpec(memory_space=pltpu.VMEM)
    y, ok, ov, op = pl.pallas_call(
        functools.partial(_sample_kernel, n_b, n_s, start),
        in_specs=[pl.BlockSpec(memory_space=pltpu.SMEM)] + [vmem] * 15,
        out_specs=[vmem] * 4,
        out_shape=[
            jax.ShapeDtypeStruct((n_tok, D_MODEL), F32),
            jax.ShapeDtypeStruct((n_b, WINDOW, KV_WIDTH), F32),
            jax.ShapeDtypeStruct((n_b, WINDOW, KV_WIDTH), F32),
            jax.ShapeDtypeStruct((n_b, HIST_ROWS, POOL_WIDTH), F32),
        ],
        scratch_shapes=[
            pltpu.VMEM((n_tok, ATTN_WIDTH), BF16),
            pltpu.VMEM((4, N_KV_HEADS, n_b * nk, LANES), BF16),
            pltpu.VMEM((n_tok, ATTN_WIDTH), F32),
        ],
        compiler_params=pltpu.CompilerParams(vmem_limit_bytes=VMEM_LIMIT),
        name="sample_layer",
    )(sinks, x.reshape(n_tok, D_MODEL), sk, sv, sp, ng, w_in, qg, kg, ones, bias_s,
      pw, ps, wabr, wpbr, wout)
    return y.reshape(n_b, n_s, D_MODEL), ok, ov, op


def kernel(x_prompt, x_sample, state_attn_k, state_attn_v, state_pool, norm_gain, w_in, q_norm_gain, k_norm_gain, attn_sinks, rel_bias, pool_w, pool_scale, w_attn_br, w_pool_br, w_out):
    depth = w_in.shape[0]
    n_b, n_s, _ = x_sample.shape
    past_len = 4096
    bias_p, bias_s = _bias_tables(rel_bias, n_s)
    seg = jnp.arange(MXU_DIM) // HEAD_DIM
    ones = (seg[:, None] == seg[None, :]).astype(BF16)

    xp, xs = x_prompt, x_sample
    pk, pv, pp, sk, sv, sp = [], [], [], [], [], []
    for l in range(depth):
        ng = norm_gain[l].reshape(1, D_MODEL)
        w_in_b = w_in[l].astype(BF16)
        qg = jnp.tile(q_norm_gain[l] * (HEAD_DIM ** -0.5), N_HEADS)
        kg = jnp.tile(k_norm_gain[l], N_KV_HEADS).reshape(1, KV_WIDTH)
        tail = (
            pool_w[l].astype(BF16),
            pool_scale[l].reshape(1, POOL_WIDTH),
            w_attn_br[l].astype(BF16),
            w_pool_br[l].astype(BF16),
            w_out[l].astype(BF16),
        )
        sink_rows = jnp.repeat(attn_sinks[l].reshape(N_KV_HEADS, GROUP_REP), PAIR, axis=1)
        xp, k_l, v_l, p_l = _prompt_layer(
            xp, ng, w_in_b, w_in[l][:, C_Q].T.astype(BF16), w_in[l][:, C_V].T.astype(BF16),
            qg.reshape(ATTN_WIDTH, 1), kg, ones, bias_p,
            sink_rows.reshape(N_KV_HEADS, 1, GROUP_REP * PAIR), *tail)
        pk.append(k_l.reshape(-1, WINDOW, N_KV_HEADS, HEAD_DIM))
        pv.append(v_l.reshape(-1, WINDOW, N_KV_HEADS, HEAD_DIM))
        pp.append(p_l[:, 1:, :])

        xs, k_l, v_l, p_l = _sample_layer(
            xs,
            state_attn_k[l].reshape(n_b, WINDOW, KV_WIDTH),
            state_attn_v[l].reshape(n_b, WINDOW, KV_WIDTH),
            jnp.pad(state_pool[l], ((0, 0), (1, 0), (0, 0))),
            past_len, attn_sinks[l], ng, w_in_b, qg.reshape(1, ATTN_WIDTH), kg, ones, bias_s, *tail)
        sk.append(k_l.reshape(n_b, WINDOW, N_KV_HEADS, HEAD_DIM))
        sv.append(v_l.reshape(n_b, WINDOW, N_KV_HEADS, HEAD_DIM))
        sp.append(p_l[:, 1:, :])
    return (xp, xs, jnp.stack(pk), jnp.stack(pv), jnp.stack(pp),
            jnp.stack(sk), jnp.stack(sv), jnp.stack(sp))
```

```python
import functools
import math

import jax
import jax.numpy as jnp
from jax import lax
from jax.experimental import pallas as pl
from jax.experimental.pallas import tpu as pltpu

D_MODEL = 1024
CHUNK = 64
WINDOW = 128
N_HEADS = 16
N_KV_HEADS = 4
GROUP_REP = N_HEADS // N_KV_HEADS
HEAD_DIM = 64
ATTN_WIDTH = N_HEADS * HEAD_DIM
KV_WIDTH = N_KV_HEADS * HEAD_DIM
POOL_WINDOWS = (2, 4, 8, 16)
POOL_WIDTH = D_MODEL
POOL_GROUP = POOL_WIDTH // len(POOL_WINDOWS)
POOL_HIST = max(POOL_WINDOWS) - 1
HIST_ROWS = POOL_HIST + 1
N_BUCKETS = 32
MAX_DISTANCE = 128
EPS = 1e-6
NEG_INF = -1e30

_SPLITS = (ATTN_WIDTH, KV_WIDTH, KV_WIDTH, ATTN_WIDTH, POOL_WIDTH, POOL_WIDTH, D_MODEL, D_MODEL)
_OFFS = tuple(sum(_SPLITS[:i]) for i in range(len(_SPLITS) + 1))
C_Q, C_K, C_V, C_AG, C_PU, C_PG, C_MA, C_MP = (slice(_OFFS[i], _OFFS[i + 1]) for i in range(8))

LANES = 128
MXU_DIM = 256
PAIR = 2 * CHUNK
PAIR_KEYS = PAIR + WINDOW
TM = 512
PREFILL = 8
VMEM_LIMIT = 60 * 1024 * 1024

BF16 = jnp.bfloat16
F32 = jnp.float32


def _dot(a, b):
    return jnp.dot(a, b, preferred_element_type=F32)


def _dot_nt(a, b):
    return lax.dot_general(a, b, (((1,), (1,)), ((), ())), preferred_element_type=F32)


def _rmsnorm_rows(x, gain):
    y = x * lax.rsqrt(jnp.mean(x * x, axis=-1, keepdims=True) + EPS)
    return y * gain


def _head_rmsnorm(z, ones_ref, gain):
    sq = z * z
    hi = sq.astype(BF16)
    lo = (sq - hi.astype(F32)).astype(BF16)
    ones = ones_ref[...]
    parts = []
    for c in range(z.shape[1] // MXU_DIM):
        sl = slice(c * MXU_DIM, (c + 1) * MXU_DIM)
        parts.append(_dot(hi[:, sl], ones) + _dot(lo[:, sl], ones))
    ss = parts[0] if len(parts) == 1 else jnp.concatenate(parts, axis=1)
    return (z * lax.rsqrt(ss * (1.0 / HEAD_DIM) + EPS)) * gain


def _split_lo_hi(a):
    ar = pltpu.roll(a, HEAD_DIM, axis=1)
    c0, c1 = a[:, :LANES], a[:, LANES:]
    r0, r1 = ar[:, :LANES], ar[:, LANES:]
    low = lax.broadcasted_iota(jnp.int32, (1, LANES), 1) < HEAD_DIM
    lo_src = (c0, r1, c1, r0)
    hi_src = (r0, c0, r1, c1)
    lo = [jnp.where(low, s, 0.0).astype(BF16) for s in lo_src]
    hi = [jnp.where(low, 0.0, s).astype(BF16) for s in hi_src]
    return lo, hi


def _softmax_rows(s, sink_col):
    m = jnp.maximum(jnp.max(s, axis=-1, keepdims=True), sink_col)
    p = jnp.exp(s - m)
    denom = jnp.sum(p, axis=-1, keepdims=True) + jnp.exp(sink_col - m)
    return (p * (1.0 / denom)).astype(BF16)


def _window_sums(full):
    outs = []
    for gi, w in enumerate(POOL_WINDOWS):
        s = full[:, gi * POOL_GROUP:(gi + 1) * POOL_GROUP]
        sh = 1
        while sh < w:
            s = s + pltpu.roll(s, sh, axis=0)
            sh *= 2
        outs.append(s)
    return outs


def _merge_tail(x, attn_o, pool_mixed, gates, pw_ref, ps_ref, wabr_ref, wpbr_ref, wout_ref):
    ag, pg, ma, mp = gates
    po = []
    for gi in range(len(POOL_WINDOWS)):
        po.append(_dot(pool_mixed[gi].astype(BF16), pw_ref[gi]))
    po = jnp.concatenate(po, axis=1) * ps_ref[...]
    p = _dot((po * (pg * jax.nn.sigmoid(pg))).astype(BF16), wpbr_ref[...])
    a = _dot((attn_o * (ag * jax.nn.sigmoid(ag))).astype(BF16), wabr_ref[...])
    acc = jax.nn.sigmoid(ma) * a + jax.nn.sigmoid(mp) * p
    return x + _dot(acc.astype(BF16), wout_ref[...])


def _bias_kernel(tbl_ref, bkp_ref, bks_ref, bp_ref, bs_ref):
    h = pl.program_id(0)

    def gather(bucket):
        acc = jnp.zeros(bucket.shape, F32)
        for b in range(N_BUCKETS):
            acc = jnp.where(bucket == b, tbl_ref[b, h], acc)
        return acc

    bk0 = bkp_ref[0]
    bk1 = bkp_ref[1]
    base = gather(bk0)
    bp_ref[0] = jnp.where(bk0 < 0, NEG_INF, base)
    bp_ref[1] = jnp.where(bk1 < 0, NEG_INF, base)
    bs_ref[...] = gather(bks_ref[...])


def _t5_bucket(rel):
    nb = N_BUCKETS // 2
    ret = jnp.where(rel > 0, nb, 0)
    n = jnp.abs(rel)
    max_exact = nb // 2
    large = max_exact + (jnp.log(jnp.maximum(n, 1).astype(jnp.float32) / max_exact)
                         / math.log(MAX_DISTANCE / max_exact) * (nb - max_exact)).astype(jnp.int32)
    large = jnp.minimum(large, nb - 1)
    return ret + jnp.where(n < max_exact, n, large)


def _bias_tables(rel_bias, n_s):
    kk = jnp.arange(PAIR_KEYS, dtype=jnp.int32)[:, None]
    qq = jnp.arange(PAIR, dtype=jnp.int32)[None, :]
    bk = _t5_bucket(kk - WINDOW - qq)
    c0 = (qq // CHUNK) * CHUNK
    valid = (kk >= c0) & (kk < c0 + CHUNK + WINDOW)
    bkp = jnp.stack([jnp.where(valid, bk, -1), jnp.where(valid & (kk >= WINDOW), bk, -1)])
    nk_s = WINDOW + n_s
    bks = _t5_bucket(jnp.arange(nk_s, dtype=jnp.int32)[None, :] - WINDOW
                     - jnp.arange(n_s, dtype=jnp.int32)[:, None])

    def sample_idx(h):
        return h // 4, h % 2, (h % 4) // 2

    bp, bs = pl.pallas_call(
        _bias_kernel,
        grid=(N_HEADS,),
        in_specs=[
            pl.BlockSpec(memory_space=pltpu.SMEM),
            pl.BlockSpec((2, PAIR_KEYS, PAIR), lambda h: (0, 0, 0)),
            pl.BlockSpec((n_s, nk_s), lambda h: (0, 0)),
        ],
        out_specs=[
            pl.BlockSpec((2, None, PAIR_KEYS, PAIR), lambda h: (0, h // GROUP_REP, 0, h % GROUP_REP)),
            pl.BlockSpec((None, None, None, n_s, nk_s), lambda h: sample_idx(h) + (0, 0)),
        ],
        out_shape=[
            jax.ShapeDtypeStruct((2, N_KV_HEADS, PAIR_KEYS, GROUP_REP * PAIR), F32),
            jax.ShapeDtypeStruct((N_KV_HEADS, 2, 2, n_s, nk_s), F32),
        ],
        name="bias_tables",
    )(rel_bias, bkp, bks)
    return bp, bs.reshape(N_KV_HEADS, 2, 2 * n_s, nk_s)


def _prompt_kernel(x_ref, ng_ref, win_ref, wqt_ref, wvt_ref, qgt_ref, kg_ref, ones_ref,
                   bias_ref, sink_ref, pw_ref, ps_ref, wabr_ref, wpbr_ref, wout_ref,
                   y_ref, pk_ref, pv_ref, pp_ref,
                   h_s, qt_s, k_s, vt_s, attn_s, pbuf):
    t = pl.program_id(1)
    last = pl.num_programs(1) - 1

    @pl.when(t == 0)
    def _():
        k_s[:, 0:WINDOW, :] = jnp.zeros((2, WINDOW, LANES), BF16)
        vt_s[:, 0:WINDOW] = jnp.zeros((KV_WIDTH, WINDOW), BF16)
        pbuf[0:HIST_ROWS, :] = jnp.zeros((HIST_ROWS, POOL_WIDTH), F32)

    x = x_ref[...]
    h_s[...] = _rmsnorm_rows(x, ng_ref[...]).astype(BF16)
    hb = h_s[...]

    n_c = D_MODEL // MXU_DIM
    fill_cols = [slice(c.start + i * MXU_DIM, c.start + (i + 1) * MXU_DIM)
                 for c in (C_PU, C_PG, C_AG, C_MA, C_MP) for i in range(n_c)]
    filled = []

    def fill(upto):
        for cols in fill_cols[len(filled):upto]:
            filled.append(_dot(hb, win_ref[:, cols]))

    zqt = _dot_nt(wqt_ref[...], hb)
    zk = _dot(hb, win_ref[:, C_K])
    vt = _dot_nt(wvt_ref[...], hb)
    fill(PREFILL)

    for h in range(N_HEADS):
        hs = slice(h * HEAD_DIM, (h + 1) * HEAD_DIM)
        zh = zqt[hs, :]
        ms = jnp.sum(zh * zh, axis=0, keepdims=True) * (1.0 / HEAD_DIM)
        qt_s[hs, :] = ((zh * lax.rsqrt(ms + EPS)) * qgt_ref[hs, :]).astype(BF16)
    k = _head_rmsnorm(zk, ones_ref, kg_ref[...])

    pk_ref[...] = k[TM - WINDOW:, :]
    pv_ref[...] = vt[:, TM - WINDOW:].T

    k_s[0, WINDOW:, :] = k[:, :LANES].astype(BF16)
    k_s[1, WINDOW:, :] = k[:, LANES:].astype(BF16)
    vt_s[:, WINDOW:] = vt.astype(BF16)

    zeros_half = jnp.zeros((HEAD_DIM, PAIR), BF16)

    def scores(p, g):
        tok = slice(p * PAIR, (p + 1) * PAIR)
        keys = slice(p * PAIR, p * PAIR + PAIR_KEYS)
        var = jnp.where(t == 0, 1, 0) if p == 0 else 0
        cols = []
        for r in range(GROUP_REP):
            h = g * GROUP_REP + r
            qh = qt_s[h * HEAD_DIM:(h + 1) * HEAD_DIM, tok]
            cols.append(jnp.concatenate([qh, zeros_half] if g % 2 == 0 else [zeros_half, qh], axis=0))
        rhs = jnp.concatenate(cols, axis=1)
        return _dot(k_s[g // 2, keys, :], rhs) + bias_ref[var, g]

    def finish(p, g, s):
        tok = slice(p * PAIR, (p + 1) * PAIR)
        keys = slice(p * PAIR, p * PAIR + PAIR_KEYS)
        sink = sink_ref[g]
        m = jnp.maximum(jnp.max(s, axis=0, keepdims=True), sink)
        e = jnp.exp(s - m)
        denom = jnp.sum(e, axis=0, keepdims=True) + jnp.exp(sink - m)
        vg = vt_s[g * HEAD_DIM:(g + 1) * HEAD_DIM, keys]
        o_t = _dot(vg, e.astype(BF16)) * (1.0 / denom)
        c = g * MXU_DIM
        attn_s[tok, c:c + LANES] = jnp.concatenate(
            [o_t[:, 0:PAIR], o_t[:, PAIR:2 * PAIR]], axis=0).T
        attn_s[tok, c + LANES:c + MXU_DIM] = jnp.concatenate(
            [o_t[:, 2 * PAIR:3 * PAIR], o_t[:, 3 * PAIR:]], axis=0).T

    pu = jnp.concatenate(filled[:n_c], axis=1)
    pbuf[HIST_ROWS:, :] = pu
    pp_ref[...] = pu[TM - HIST_ROWS:, :]
    sums = _window_sums(pbuf[...])
    pos1 = (t * TM + 1 + lax.broadcasted_iota(jnp.int32, (TM, 1), 0)).astype(F32)
    mixed = []
    for gi, w in enumerate(POOL_WINDOWS):
        inv_cnt = 1.0 / jnp.minimum(float(w), pos1)
        mixed.append(sums[gi][HIST_ROWS:, :] * inv_cnt - pu[:, gi * POOL_GROUP:(gi + 1) * POOL_GROUP])

    blocks = [(p, g) for p in range(TM // PAIR) for g in range(N_KV_HEADS)]
    s_next = scores(*blocks[0])
    for i, blk in enumerate(blocks):
        s_cur = s_next
        if i + 1 < len(blocks):
            s_next = scores(*blocks[i + 1])
        fill(PREFILL + (len(fill_cols) - PREFILL) * (i + 1) // len(blocks))
        finish(*blk, s_cur)
    pg, ag, ma, mp = (jnp.concatenate(filled[j * n_c:(j + 1) * n_c], axis=1) for j in range(1, 5))

    y_ref[...] = _merge_tail(x, attn_s[...], mixed, (ag, pg, ma, mp), pw_ref, ps_ref,
                             wabr_ref, wpbr_ref, wout_ref)

    k_s[:, 0:WINDOW, :] = k_s[:, TM:TM + WINDOW, :]
    vt_s[:, 0:WINDOW] = vt_s[:, TM:TM + WINDOW]
    pbuf[0:HIST_ROWS, :] = pbuf[TM:TM + HIST_ROWS, :]


def _resident(shape):
    nd = len(shape)
    return pl.BlockSpec(shape, lambda b, t: (0,) * nd, pipeline_mode=pl.Buffered(1))


def _prompt_layer(x, ng, w_in, wqt, wvt, qgt, kg, ones, bias_p, sink_rows, pw, ps, wabr, wpbr, wout):
    B, S, _ = x.shape
    assert S % TM == 0 and TM % PAIR == 0 and TM >= WINDOW
    consts = (ng, w_in, wqt, wvt, qgt, kg, ones, bias_p, sink_rows, pw, ps, wabr, wpbr, wout)
    return pl.pallas_call(
        _prompt_kernel,
        grid=(B, S // TM),
        in_specs=[pl.BlockSpec((None, TM, D_MODEL), lambda b, t: (b, t, 0))]
                 + [_resident(c.shape) for c in consts],
        out_specs=[
            pl.BlockSpec((None, TM, D_MODEL), lambda b, t: (b, t, 0)),
            pl.BlockSpec((None, WINDOW, KV_WIDTH), lambda b, t: (b, 0, 0)),
            pl.BlockSpec((None, WINDOW, KV_WIDTH), lambda b, t: (b, 0, 0)),
            pl.BlockSpec((None, HIST_ROWS, POOL_WIDTH), lambda b, t: (b, 0, 0)),
        ],
        out_shape=[
            jax.ShapeDtypeStruct((B, S, D_MODEL), F32),
            jax.ShapeDtypeStruct((B, WINDOW, KV_WIDTH), F32),
            jax.ShapeDtypeStruct((B, WINDOW, KV_WIDTH), F32),
            jax.ShapeDtypeStruct((B, HIST_ROWS, POOL_WIDTH), F32),
        ],
        scratch_shapes=[
            pltpu.VMEM((TM, D_MODEL), BF16),
            pltpu.VMEM((ATTN_WIDTH, TM), BF16),
            pltpu.VMEM((2, WINDOW + TM, LANES), BF16),
            pltpu.VMEM((KV_WIDTH, WINDOW + TM), BF16),
            pltpu.VMEM((TM, ATTN_WIDTH), F32),
            pltpu.VMEM((HIST_ROWS + TM, POOL_WIDTH), F32),
        ],
        compiler_params=pltpu.CompilerParams(
            dimension_semantics=("arbitrary", "arbitrary"),
            vmem_limit_bytes=VMEM_LIMIT,
        ),
        name="prompt_layer",
    )(x, *consts)


def _sample_kernel(n_b, n_s, start,
                   sinks_ref, x_ref, sk_ref, sv_ref, sp_ref, ng_ref, win_ref, qg_ref, kg_ref,
                   ones_ref, bias_ref, pw_ref, ps_ref, wabr_ref, wpbr_ref, wout_ref,
                   y_ref, ok_ref, ov_ref, op_ref,
                   q_s, kv_s, attn_s):
    nk = WINDOW + n_s
    x = x_ref[...]
    hb = _rmsnorm_rows(x, ng_ref[...]).astype(BF16)

    q_s[...] = _head_rmsnorm(_dot(hb, win_ref[:, C_Q]), ones_ref, qg_ref[...]).astype(BF16)
    k = _head_rmsnorm(_dot(hb, win_ref[:, C_K]), ones_ref, kg_ref[...])
    v = _dot(hb, win_ref[:, C_V])

    k_all = jnp.concatenate([sk_ref[...], k.reshape(n_b, n_s, KV_WIDTH)], axis=1)
    v_all = jnp.concatenate([sv_ref[...], v.reshape(n_b, n_s, KV_WIDTH)], axis=1)
    ok_ref[...] = k_all[:, nk - WINDOW:, :]
    ov_ref[...] = v_all[:, nk - WINDOW:, :]

    k_lo, k_hi = _split_lo_hi(k_all.reshape(n_b * nk, KV_WIDTH))
    v_lo, v_hi = _split_lo_hi(v_all.reshape(n_b * nk, KV_WIDTH))
    for g in range(N_KV_HEADS):
        kv_s[0, g] = k_lo[g]
        kv_s[1, g] = k_hi[g]
        kv_s[2, g] = v_lo[g]
        kv_s[3, g] = v_hi[g]

    top = lax.broadcasted_iota(jnp.int32, (2 * n_s, 1), 0) < n_s

    def per_stream(bb, carry):
        rows = pl.ds(pl.multiple_of(bb * n_s, n_s), n_s)
        keys = pl.ds(pl.multiple_of(bb * nk, 16), nk)
        for g in range(N_KV_HEADS):
            c = g * MXU_DIM
            lhs = jnp.concatenate([q_s[rows, c:c + LANES], q_s[rows, c + LANES:c + MXU_DIM]], axis=0)
            o = jnp.zeros((2 * n_s, LANES), F32)
            for j in range(2):
                s = _dot_nt(lhs, kv_s[j, g, keys, :]) + bias_ref[g, j]
                sink = jnp.where(top, sinks_ref[4 * g + j], sinks_ref[4 * g + 2 + j])
                o = o + _dot(_softmax_rows(s, sink), kv_s[2 + j, g, keys, :])
            attn_s[rows, c:c + LANES] = o[:n_s]
            attn_s[rows, c + LANES:c + MXU_DIM] = o[n_s:]
        return carry

    lax.fori_loop(0, n_b, per_stream, 0)

    pu = _dot(hb, win_ref[:, C_PU])
    pu3 = pu.reshape(n_b, n_s, POOL_WIDTH)
    full3 = jnp.concatenate([sp_ref[...], pu3], axis=1)
    op_ref[...] = full3[:, n_s:, :]
    seg = HIST_ROWS + n_s
    sums = _window_sums(full3.reshape(n_b * seg, POOL_WIDTH))
    pos1 = (start + 1 + lax.broadcasted_iota(jnp.int32, (n_s, 1), 0)).astype(F32)
    mixed = []
    for gi, w in enumerate(POOL_WINDOWS):
        inv_cnt = 1.0 / jnp.minimum(float(w), pos1)
        sw = sums[gi].reshape(n_b, seg, POOL_GROUP)[:, HIST_ROWS:, :] * inv_cnt
        mixed.append(sw.reshape(n_b * n_s, POOL_GROUP) - pu[:, gi * POOL_GROUP:(gi + 1) * POOL_GROUP])

    gates = tuple(_dot(hb, win_ref[:, c]) for c in (C_AG, C_PG, C_MA, C_MP))
    y_ref[...] = _merge_tail(x, attn_s[...], mixed, gates, pw_ref, ps_ref,
                             wabr_ref, wpbr_ref, wout_ref)


def _sample_layer(x, sk, sv, sp, start, sinks, ng, w_in, qg, kg, ones, bias_s, pw, ps, wabr, wpbr, wout):
    n_b, n_s, _ = x.shape
    nk = WINDOW + n_s
    assert n_s % 16 == 0 and n_s >= POOL_HIST
    n_tok = n_b * n_s
    vmem = pl.BlockSpec(memory_space=pltpu.VMEM)
    y, ok, ov, op = pl.pallas_call(
        functools.partial(_sample_kernel, n_b, n_s, start),
        in_specs=[pl.BlockSpec(memory_space=pltpu.SMEM)] + [vmem] * 15,
        out_specs=[vmem] * 4,
        out_shape=[
            jax.ShapeDtypeStruct((n_tok, D_MODEL), F32),
            jax.ShapeDtypeStruct((n_b, WINDOW, KV_WIDTH), F32),
            jax.ShapeDtypeStruct((n_b, WINDOW, KV_WIDTH), F32),
            jax.ShapeDtypeStruct((n_b, HIST_ROWS, POOL_WIDTH), F32),
        ],
        scratch_shapes=[
            pltpu.VMEM((n_tok, ATTN_WIDTH), BF16),
            pltpu.VMEM((4, N_KV_HEADS, n_b * nk, LANES), BF16),
            pltpu.VMEM((n_tok, ATTN_WIDTH), F32),
        ],
        compiler_params=pltpu.CompilerParams(vmem_limit_bytes=VMEM_LIMIT),
        name="sample_layer",
    )(sinks, x.reshape(n_tok, D_MODEL), sk, sv, sp, ng, w_in, qg, kg, ones, bias_s,
      pw, ps, wabr, wpbr, wout)
    return y.reshape(n_b, n_s, D_MODEL), ok, ov, op


def kernel(x_prompt, x_sample, state_attn_k, state_attn_v, state_pool, norm_gain, w_in, q_norm_gain, k_norm_gain, attn_sinks, rel_bias, pool_w, pool_scale, w_attn_br, w_pool_br, w_out):
    depth = w_in.shape[0]
    n_b, n_s, _ = x_sample.shape
    past_len = 4096
    bias_p, bias_s = _bias_tables(rel_bias, n_s)
    seg = jnp.arange(MXU_DIM) // HEAD_DIM
    ones = (seg[:, None] == seg[None, :]).astype(BF16)

    xp, xs = x_prompt, x_sample
    pk, pv, pp, sk, sv, sp = [], [], [], [], [], []
    for l in range(depth):
        ng = norm_gain[l].reshape(1, D_MODEL)
        w_in_b = w_in[l].astype(BF16)
        qg = jnp.tile(q_norm_gain[l] * (HEAD_DIM ** -0.5), N_HEADS)
        kg = jnp.tile(k_norm_gain[l], N_KV_HEADS).reshape(1, KV_WIDTH)
        tail = (
            pool_w[l].astype(BF16),
            pool_scale[l].reshape(1, POOL_WIDTH),
            w_attn_br[l].astype(BF16),
            w_pool_br[l].astype(BF16),
            w_out[l].astype(BF16),
        )
        sink_rows = jnp.repeat(attn_sinks[l].reshape(N_KV_HEADS, GROUP_REP), PAIR, axis=1)
        xp, k_l, v_l, p_l = _prompt_layer(
            xp, ng, w_in_b, w_in[l][:, C_Q].T.astype(BF16), w_in[l][:, C_V].T.astype(BF16),
            qg.reshape(ATTN_WIDTH, 1), kg, ones, bias_p,
            sink_rows.reshape(N_KV_HEADS, 1, GROUP_REP * PAIR), *tail)
        pk.append(k_l.reshape(-1, WINDOW, N_KV_HEADS, HEAD_DIM))
        pv.append(v_l.reshape(-1, WINDOW, N_KV_HEADS, HEAD_DIM))
        pp.append(p_l[:, 1:, :])

        xs, k_l, v_l, p_l = _sample_layer(
            xs,
            state_attn_k[l].reshape(n_b, WINDOW, KV_WIDTH),
            state_attn_v[l].reshape(n_b, WINDOW, KV_WIDTH),
            jnp.pad(state_pool[l], ((0, 0), (1, 0), (0, 0))),
            past_len, attn_sinks[l], ng, w_in_b, qg.reshape(1, ATTN_WIDTH), kg, ones, bias_s, *tail)
        sk.append(k_l.reshape(n_b, WINDOW, N_KV_HEADS, HEAD_DIM))
        sv.append(v_l.reshape(n_b, WINDOW, N_KV_HEADS, HEAD_DIM))
        sp.append(p_l[:, 1:, :])
    return (xp, xs, jnp.stack(pk), jnp.stack(pv), jnp.stack(pp),
            jnp.stack(sk), jnp.stack(sv), jnp.stack(sp))
```

```python
import functools
import math

import jax
import jax.numpy as jnp
from jax import lax
from jax.experimental import pallas as pl
from jax.experimental.pallas import tpu as pltpu

D_MODEL = 1024
CHUNK = 64
WINDOW = 128
N_HEADS = 16
N_KV_HEADS = 4
GROUP_REP = N_HEADS // N_KV_HEADS
HEAD_DIM = 64
ATTN_WIDTH = N_HEADS * HEAD_DIM
KV_WIDTH = N_KV_HEADS * HEAD_DIM
POOL_WINDOWS = (2, 4, 8, 16)
POOL_WIDTH = D_MODEL
POOL_GROUP = POOL_WIDTH // len(POOL_WINDOWS)
POOL_HIST = max(POOL_WINDOWS) - 1
HIST_ROWS = POOL_HIST + 1
N_BUCKETS = 32
MAX_DISTANCE = 128
EPS = 1e-6
NEG_INF = -1e30

_SPLITS = (ATTN_WIDTH, KV_WIDTH, KV_WIDTH, ATTN_WIDTH, POOL_WIDTH, POOL_WIDTH, D_MODEL, D_MODEL)
_OFFS = tuple(sum(_SPLITS[:i]) for i in range(len(_SPLITS) + 1))
C_Q, C_K, C_V, C_AG, C_PU, C_PG, C_MA, C_MP = (slice(_OFFS[i], _OFFS[i + 1]) for i in range(8))

LANES = 128
MXU_DIM = 256
PAIR = 2 * CHUNK
PAIR_KEYS = PAIR + WINDOW
TM = 512
PREFILL = 8
VMEM_LIMIT = 60 * 1024 * 1024

BF16 = jnp.bfloat16
F32 = jnp.float32


def _dot(a, b):
    return jnp.dot(a, b, preferred_element_type=F32)


def _dot_nt(a, b):
    return lax.dot_general(a, b, (((1,), (1,)), ((), ())), preferred_element_type=F32)


def _rmsnorm_rows(x, gain):
    y = x * lax.rsqrt(jnp.mean(x * x, axis=-1, keepdims=True) + EPS)
    return y * gain


def _head_rmsnorm(z, ones_ref, gain):
    sq = z * z
    hi = sq.astype(BF16)
    lo = (sq - hi.astype(F32)).astype(BF16)
    ones = ones_ref[...]
    parts = []
    for c in range(z.shape[1] // MXU_DIM):
        sl = slice(c * MXU_DIM, (c + 1) * MXU_DIM)
        parts.append(_dot(hi[:, sl], ones) + _dot(lo[:, sl], ones))
    ss = parts[0] if len(parts) == 1 else jnp.concatenate(parts, axis=1)
    return (z * lax.rsqrt(ss * (1.0 / HEAD_DIM) + EPS)) * gain


def _head_rmsnorm_fm(zt, gain_ref, h):
    hs = slice(h * HEAD_DIM, (h + 1) * HEAD_DIM)
    zh = zt[hs, :]
    ms = jnp.sum(zh * zh, axis=0, keepdims=True) * (1.0 / HEAD_DIM)
    return (zh * lax.rsqrt(ms + EPS)) * gain_ref[hs, :]


def _window_sums(full):
    outs = []
    for gi, w in enumerate(POOL_WINDOWS):
        s = full[:, gi * POOL_GROUP:(gi + 1) * POOL_GROUP]
        sh = 1
        while sh < w:
            s = s + pltpu.roll(s, sh, axis=0)
            sh *= 2
        outs.append(s)
    return outs


def _merge_tail(x, attn_o, pool_mixed, gates, pw_ref, ps_ref, wabr_ref, wpbr_ref, wout_ref):
    ag, pg, ma, mp = gates
    po = []
    for gi in range(len(POOL_WINDOWS)):
        po.append(_dot(pool_mixed[gi].astype(BF16), pw_ref[gi]))
    po = jnp.concatenate(po, axis=1) * ps_ref[...]
    p = _dot((po * (pg * jax.nn.sigmoid(pg))).astype(BF16), wpbr_ref[...])
    a = _dot((attn_o * (ag * jax.nn.sigmoid(ag))).astype(BF16), wabr_ref[...])
    acc = jax.nn.sigmoid(ma) * a + jax.nn.sigmoid(mp) * p
    return x + _dot(acc.astype(BF16), wout_ref[...])


def _diag2(a, axis_r, axis_c):
    z = jnp.zeros_like(a)
    return jnp.concatenate([jnp.concatenate([a, z], axis=axis_c),
                            jnp.concatenate([z, a], axis=axis_c)], axis=axis_r)


def _bias_kernel(n_s, tbl_ref, bkp_ref, bks_ref, bkn_ref, bp_ref, ts_ref, tn_ref):
    g = pl.program_id(0)

    def gather(bucket, h):
        acc = jnp.zeros(bucket.shape, F32)
        for b in range(N_BUCKETS):
            acc = jnp.where(bucket == b, tbl_ref[b, h], acc)
        return acc

    bkp = bkp_ref[...]
    for r in range(GROUP_REP):
        bp_ref[:, r * PAIR:(r + 1) * PAIR] = jnp.where(bkp < 0, NEG_INF, gather(bkp, g * GROUP_REP + r))
    bks = bks_ref[...]
    bkn = bkn_ref[...]
    for c in range(2):
        for j in range(2):
            h = g * GROUP_REP + 2 * c + j
            rows = slice(c * n_s, (c + 1) * n_s)
            ts_ref[rows, j * WINDOW:(j + 1) * WINDOW] = gather(bks, h)
            tn_ref[rows, j * bkn.shape[1]:(j + 1) * bkn.shape[1]] = gather(bkn, h)


def _t5_bucket(rel):
    assert (N_BUCKETS, MAX_DISTANCE) == (32, 128)
    nb = N_BUCKETS // 2
    max_exact = nb // 2
    n = jnp.abs(rel)
    large = jnp.minimum(max_exact + (31 - lax.clz(n * n)) - 6, nb - 1)
    return jnp.where(rel > 0, nb, 0) + jnp.where(n < max_exact, n, large)


def _bias_tables(rel_bias, n_b, n_s):
    kk = jnp.arange(PAIR_KEYS, dtype=jnp.int32)[:, None]
    qq = jnp.arange(PAIR, dtype=jnp.int32)[None, :]
    c0 = (qq // CHUNK) * CHUNK
    bkp = jnp.where((kk >= c0) & (kk < c0 + CHUNK + WINDOW), _t5_bucket(kk - WINDOW - qq), -1)
    fr = jnp.arange(n_s, dtype=jnp.int32)[:, None]
    bks = _t5_bucket(jnp.arange(WINDOW, dtype=jnp.int32)[None, :] - WINDOW - fr)
    bkn = _t5_bucket(jnp.arange(n_b * n_s, dtype=jnp.int32)[None, :] % n_s - fr)
    n_new = n_b * n_s
    whole = lambda g: (0, 0)
    per_g = lambda g: (g, 0, 0)
    return pl.pallas_call(
        functools.partial(_bias_kernel, n_s),
        grid=(N_KV_HEADS,),
        in_specs=[
            pl.BlockSpec(memory_space=pltpu.SMEM),
            pl.BlockSpec((PAIR_KEYS, PAIR), whole),
            pl.BlockSpec((n_s, WINDOW), whole),
            pl.BlockSpec((n_s, n_new), whole),
        ],
        out_specs=[
            pl.BlockSpec((None, PAIR_KEYS, GROUP_REP * PAIR), per_g),
            pl.BlockSpec((None, 2 * n_s, 2 * WINDOW), per_g),
            pl.BlockSpec((None, 2 * n_s, 2 * n_new), per_g),
        ],
        out_shape=[
            jax.ShapeDtypeStruct((N_KV_HEADS, PAIR_KEYS, GROUP_REP * PAIR), F32),
            jax.ShapeDtypeStruct((N_KV_HEADS, 2 * n_s, 2 * WINDOW), F32),
            jax.ShapeDtypeStruct((N_KV_HEADS, 2 * n_s, 2 * n_new), F32),
        ],
        name="bias_tables",
    )(rel_bias, bkp, bks, bkn)


def _prompt_kernel(x_ref, ng_ref, win_ref, qgt_ref, kg_ref, ones_ref,
                   bias_ref, sink_ref, pw_ref, ps_ref, wabr_ref, wpbr_ref, wout_ref,
                   y_ref, pk_ref, pv_ref, pp_ref,
                   wt_s, h_s, qt_s, k_s, vt_s, attn_s, pbuf):
    t = pl.program_id(1)

    @pl.when((pl.program_id(0) == 0) & (t == 0))
    def _():
        wt_s[0:ATTN_WIDTH, :] = win_ref[:, C_Q].T
        wt_s[ATTN_WIDTH:, :] = win_ref[:, C_V].T

    @pl.when(t == 0)
    def _():
        k_s[:, 0:WINDOW, :] = jnp.zeros((2, WINDOW, LANES), BF16)
        vt_s[:, 0:WINDOW] = jnp.zeros((KV_WIDTH, WINDOW), BF16)
        pbuf[0:HIST_ROWS, :] = jnp.zeros((HIST_ROWS, POOL_WIDTH), F32)

    x = x_ref[...]
    h_s[...] = _rmsnorm_rows(x, ng_ref[...]).astype(BF16)
    hb = h_s[...]

    n_c = D_MODEL // MXU_DIM
    fill_cols = [slice(c.start + i * MXU_DIM, c.start + (i + 1) * MXU_DIM)
                 for c in (C_PU, C_PG, C_AG, C_MA, C_MP) for i in range(n_c)]
    filled = []

    def fill(upto):
        for cols in fill_cols[len(filled):upto]:
            filled.append(_dot(hb, win_ref[:, cols]))

    zqt = _dot_nt(wt_s[0:ATTN_WIDTH, :], hb)
    zk = _dot(hb, win_ref[:, C_K])
    vt = _dot_nt(wt_s[ATTN_WIDTH:, :], hb)
    fill(PREFILL)

    for h in range(N_HEADS):
        qt_s[h * HEAD_DIM:(h + 1) * HEAD_DIM, :] = _head_rmsnorm_fm(zqt, qgt_ref, h).astype(BF16)
    k = _head_rmsnorm(zk, ones_ref, kg_ref[...])

    pk_ref[...] = k[TM - WINDOW:, :].T
    pv_ref[...] = vt[:, TM - WINDOW:]

    k_s[0, WINDOW:, :] = k[:, :LANES].astype(BF16)
    k_s[1, WINDOW:, :] = k[:, LANES:].astype(BF16)
    vt_s[:, WINDOW:] = vt.astype(BF16)

    zeros_half = jnp.zeros((HEAD_DIM, PAIR), BF16)

    def scores(p, g):
        tok = slice(p * PAIR, (p + 1) * PAIR)
        keys = slice(p * PAIR, p * PAIR + PAIR_KEYS)
        cols = []
        for r in range(GROUP_REP):
            h = g * GROUP_REP + r
            qh = qt_s[h * HEAD_DIM:(h + 1) * HEAD_DIM, tok]
            cols.append(jnp.concatenate([qh, zeros_half] if g % 2 == 0 else [zeros_half, qh], axis=0))
        rhs = jnp.concatenate(cols, axis=1)
        s = _dot(k_s[g // 2, keys, :], rhs) + bias_ref[g]
        if p == 0:
            kk = lax.broadcasted_iota(jnp.int32, s.shape, 0)
            s = jnp.where((kk >= WINDOW) | (t > 0), s, NEG_INF)
        return s

    def finish(p, g, s):
        tok = slice(p * PAIR, (p + 1) * PAIR)
        keys = slice(p * PAIR, p * PAIR + PAIR_KEYS)
        sink = sink_ref[g]
        m = jnp.maximum(jnp.max(s, axis=0, keepdims=True), sink)
        e = jnp.exp(s - m)
        denom = jnp.sum(e, axis=0, keepdims=True) + jnp.exp(sink - m)
        vg = vt_s[g * HEAD_DIM:(g + 1) * HEAD_DIM, keys]
        o_t = _dot(vg, e.astype(BF16)) * (1.0 / denom)
        c = g * MXU_DIM
        attn_s[tok, c:c + LANES] = jnp.concatenate(
            [o_t[:, 0:PAIR], o_t[:, PAIR:2 * PAIR]], axis=0).T
        attn_s[tok, c + LANES:c + MXU_DIM] = jnp.concatenate(
            [o_t[:, 2 * PAIR:3 * PAIR], o_t[:, 3 * PAIR:]], axis=0).T

    pu = jnp.concatenate(filled[:n_c], axis=1)
    pbuf[HIST_ROWS:, :] = pu
    pp_ref[...] = pu[TM - HIST_ROWS:, :]
    sums = _window_sums(pbuf[...])
    pos1 = (t * TM + 1 + lax.broadcasted_iota(jnp.int32, (TM, 1), 0)).astype(F32)
    mixed = []
    for gi, w in enumerate(POOL_WINDOWS):
        inv_cnt = 1.0 / jnp.minimum(float(w), pos1)
        mixed.append(sums[gi][HIST_ROWS:, :] * inv_cnt - pu[:, gi * POOL_GROUP:(gi + 1) * POOL_GROUP])

    blocks = [(p, g) for p in range(TM // PAIR) for g in range(N_KV_HEADS)]
    s_next = scores(*blocks[0])
    for i, blk in enumerate(blocks):
        s_cur = s_next
        if i + 1 < len(blocks):
            s_next = scores(*blocks[i + 1])
        fill(PREFILL + (len(fill_cols) - PREFILL) * (i + 1) // len(blocks))
        finish(*blk, s_cur)
    pg, ag, ma, mp = (jnp.concatenate(filled[j * n_c:(j + 1) * n_c], axis=1) for j in range(1, 5))

    y_ref[...] = _merge_tail(x, attn_s[...], mixed, (ag, pg, ma, mp), pw_ref, ps_ref,
                             wabr_ref, wpbr_ref, wout_ref)

    k_s[:, 0:WINDOW, :] = k_s[:, TM:TM + WINDOW, :]
    vt_s[:, 0:WINDOW] = vt_s[:, TM:TM + WINDOW]
    pbuf[0:HIST_ROWS, :] = pbuf[TM:TM + HIST_ROWS, :]


def _resident(shape):
    nd = len(shape)
    return pl.BlockSpec(shape, lambda b, t: (0,) * nd, pipeline_mode=pl.Buffered(1))


def _prompt_layer(x, ng, w_in, qgt, kg, ones, bias_p, sink_rows, pw, ps, wabr, wpbr, wout):
    B, S, _ = x.shape
    assert S % TM == 0 and TM % PAIR == 0 and TM >= WINDOW
    consts = (ng, w_in, qgt, kg, ones, bias_p, sink_rows, pw, ps, wabr, wpbr, wout)
    return pl.pallas_call(
        _prompt_kernel,
        grid=(B, S // TM),
        in_specs=[pl.BlockSpec((None, TM, D_MODEL), lambda b, t: (b, t, 0))]
                 + [_resident(c.shape) for c in consts],
        out_specs=[
            pl.BlockSpec((None, TM, D_MODEL), lambda b, t: (b, t, 0)),
            pl.BlockSpec((None, KV_WIDTH, WINDOW), lambda b, t: (b, 0, 0)),
            pl.BlockSpec((None, KV_WIDTH, WINDOW), lambda b, t: (b, 0, 0)),
            pl.BlockSpec((None, HIST_ROWS, POOL_WIDTH), lambda b, t: (b, 0, 0)),
        ],
        out_shape=[
            jax.ShapeDtypeStruct((B, S, D_MODEL), F32),
            jax.ShapeDtypeStruct((B, KV_WIDTH, WINDOW), F32),
            jax.ShapeDtypeStruct((B, KV_WIDTH, WINDOW), F32),
            jax.ShapeDtypeStruct((B, HIST_ROWS, POOL_WIDTH), F32),
        ],
        scratch_shapes=[
            pltpu.VMEM((ATTN_WIDTH + KV_WIDTH, D_MODEL), BF16),
            pltpu.VMEM((TM, D_MODEL), BF16),
            pltpu.VMEM((ATTN_WIDTH, TM), BF16),
            pltpu.VMEM((2, WINDOW + TM, LANES), BF16),
            pltpu.VMEM((KV_WIDTH, WINDOW + TM), BF16),
            pltpu.VMEM((TM, ATTN_WIDTH), F32),
            pltpu.VMEM((HIST_ROWS + TM, POOL_WIDTH), F32),
        ],
        compiler_params=pltpu.CompilerParams(
            dimension_semantics=("arbitrary", "arbitrary"),
            vmem_limit_bytes=VMEM_LIMIT,
        ),
        name="prompt_layer",
    )(x, *consts)


def _sample_kernel(n_b, n_s, start,
                   sinks_ref, x_ref, skt_ref, svt_ref, sp_ref, ng_ref, win_ref,
                   qg_ref, kgt_ref, ones_ref, ts_ref, tn_ref, pw_ref, ps_ref, wabr_ref, wpbr_ref, wout_ref,
                   y_ref, ok_ref, ov_ref, op_ref):
    n_tok = n_b * n_s
    x = x_ref[...]
    hb = _rmsnorm_rows(x, ng_ref[...]).astype(BF16)

    q = _head_rmsnorm(_dot(hb, win_ref[:, C_Q]), ones_ref, qg_ref[...]).astype(BF16)
    q3 = q.reshape(n_b, n_s, ATTN_WIDTH)
    zkt = _dot_nt(win_ref[:, C_K].T, hb)
    knt = jnp.concatenate([_head_rmsnorm_fm(zkt, kgt_ref, g) for g in range(N_KV_HEADS)], axis=0)
    vnt = _dot_nt(win_ref[:, C_V].T, hb)

    lane = lax.broadcasted_iota(jnp.int32, (1, WINDOW), 1)
    per_tile = LANES // n_s
    for s in range(n_b):
        tile = slice((s // per_tile) * LANES, (s // per_tile + 1) * LANES)
        shift = (WINDOW - n_s - (s % per_tile) * n_s) % LANES
        for new, st_ref, o_ref in ((knt, skt_ref, ok_ref), (vnt, svt_ref, ov_ref)):
            fresh = new[:, tile] if shift == 0 else pltpu.roll(new[:, tile], shift, axis=1)
            o_ref[s] = jnp.where(lane < WINDOW - n_s, pltpu.roll(st_ref[s], WINDOW - n_s, axis=1), fresh)

    knb = knt.astype(BF16)
    vnb = vnt.astype(BF16)
    key_stream = (lax.broadcasted_iota(jnp.int32, (n_b, 1, 2 * n_tok), 2) & (n_tok - 1)) >> (n_s.bit_length() - 1)
    own = key_stream == lax.broadcasted_iota(jnp.int32, (n_b, 1, 2 * n_tok), 0)
    top = lax.broadcasted_iota(jnp.int32, (1, 2 * n_s, 1), 1) < n_s
    low = lax.broadcasted_iota(jnp.int32, (1, 1, LANES), 2) < HEAD_DIM

    def scores(g):
        hd = slice(g * HEAD_DIM, (g + 1) * HEAD_DIM)
        c = g * MXU_DIM
        lhs = jnp.concatenate([q3[:, :, c:c + LANES], q3[:, :, c + LANES:c + MXU_DIM]], axis=1)
        kst = _diag2(skt_ref[:, hd, :].astype(BF16), 1, 2)
        s_st = jnp.stack([_dot(lhs[s], kst[s]) for s in range(n_b)]) + ts_ref[g][None]
        s_nw = _dot(lhs.reshape(2 * n_tok, LANES), _diag2(knb[hd, :], 0, 1))
        s_nw = jnp.where(own, s_nw.reshape(n_b, 2 * n_s, 2 * n_tok) + tn_ref[g][None], NEG_INF)
        return s_st, s_nw

    def finish(g, s_st, s_nw):
        hd = slice(g * HEAD_DIM, (g + 1) * HEAD_DIM)
        p_st, p_nw, inv = [], [], []
        for j in range(2):
            a = s_st[:, :, j * WINDOW:(j + 1) * WINDOW]
            b = s_nw[:, :, j * n_tok:(j + 1) * n_tok]
            sink = jnp.where(top, sinks_ref[4 * g + j], sinks_ref[4 * g + 2 + j])
            m = jnp.maximum(jnp.maximum(jnp.max(a, axis=-1, keepdims=True),
                                        jnp.max(b, axis=-1, keepdims=True)), sink)
            ea = jnp.exp(a - m)
            eb = jnp.exp(b - m)
            inv.append(1.0 / (jnp.sum(ea, axis=-1, keepdims=True) + jnp.sum(eb, axis=-1, keepdims=True)
                              + jnp.exp(sink - m)))
            p_st.append(ea.astype(BF16))
            p_nw.append(eb.astype(BF16))
        p_st = jnp.concatenate(p_st, axis=2)
        p_nw = jnp.concatenate(p_nw, axis=2).reshape(2 * n_tok, 2 * n_tok)
        vst = _diag2(svt_ref[:, hd, :].astype(BF16), 1, 2)
        o = jnp.stack([_dot_nt(p_st[s], vst[s]) for s in range(n_b)])
        o = o + _dot_nt(p_nw, _diag2(vnb[hd, :], 0, 1)).reshape(n_b, 2 * n_s, LANES)
        o = o * jnp.where(low, inv[0], inv[1])
        return jnp.concatenate([o[:, :n_s, :], o[:, n_s:, :]], axis=2)

    outs = []
    nxt = scores(0)
    for g in range(N_KV_HEADS):
        cur = nxt
        if g + 1 < N_KV_HEADS:
            nxt = scores(g + 1)
        outs.append(finish(g, *cur))
    attn_o = jnp.concatenate(outs, axis=2).reshape(n_tok, ATTN_WIDTH)

    pu = _dot(hb, win_ref[:, C_PU])
    pu3 = pu.reshape(n_b, n_s, POOL_WIDTH)
    full3 = jnp.concatenate([sp_ref[...], pu3], axis=1)
    op_ref[...] = full3[:, n_s:, :]
    seg = HIST_ROWS + n_s
    sums = _window_sums(full3.reshape(n_b * seg, POOL_WIDTH))
    pos1 = (start + 1 + lax.broadcasted_iota(jnp.int32, (n_s, 1), 0)).astype(F32)
    mixed = []
    for gi, w in enumerate(POOL_WINDOWS):
        inv_cnt = 1.0 / jnp.minimum(float(w), pos1)
        sw = sums[gi].reshape(n_b, seg, POOL_GROUP)[:, HIST_ROWS:, :] * inv_cnt
        mixed.append(sw.reshape(n_tok, POOL_GROUP) - pu[:, gi * POOL_GROUP:(gi + 1) * POOL_GROUP])

    gates = tuple(_dot(hb, win_ref[:, c]) for c in (C_AG, C_PG, C_MA, C_MP))
    y_ref[...] = _merge_tail(x, attn_o, mixed, gates, pw_ref, ps_ref, wabr_ref, wpbr_ref, wout_ref)


def _sample_layer(x, skt, svt, sp, start, sinks, ng, w_in, qg, kgt, ones, ts, tn,
                  pw, ps, wabr, wpbr, wout):
    n_b, n_s, _ = x.shape
    n_tok = n_b * n_s
    assert n_s % 16 == 0 and LANES % n_s == 0 and n_s >= POOL_HIST and n_tok % LANES == 0
    assert n_s & (n_s - 1) == 0 and n_tok & (n_tok - 1) == 0
    vmem = pl.BlockSpec(memory_space=pltpu.VMEM)
    y, ok, ov, op = pl.pallas_call(
        functools.partial(_sample_kernel, n_b, n_s, start),
        in_specs=[pl.BlockSpec(memory_space=pltpu.SMEM)] + [vmem] * 16,
        out_specs=[vmem] * 4,
        out_shape=[
            jax.ShapeDtypeStruct((n_tok, D_MODEL), F32),
            jax.ShapeDtypeStruct((n_b, KV_WIDTH, WINDOW), F32),
            jax.ShapeDtypeStruct((n_b, KV_WIDTH, WINDOW), F32),
            jax.ShapeDtypeStruct((n_b, HIST_ROWS, POOL_WIDTH), F32),
        ],
        compiler_params=pltpu.CompilerParams(vmem_limit_bytes=VMEM_LIMIT),
        name="sample_layer",
    )(sinks, x.reshape(n_tok, D_MODEL), skt, svt, sp, ng, w_in, qg, kgt, ones, ts, tn,
      pw, ps, wabr, wpbr, wout)
    return y.reshape(n_b, n_s, D_MODEL), ok, ov, op


def _kv_to_feature_major(a):
    return jnp.transpose(a, (0, 2, 3, 1)).reshape(a.shape[0], KV_WIDTH, WINDOW)


def _kv_from_feature_major(a):
    return jnp.transpose(a.reshape(a.shape[0], N_KV_HEADS, HEAD_DIM, WINDOW), (0, 3, 1, 2))


def kernel(x_prompt, x_sample, state_attn_k, state_attn_v, state_pool, norm_gain, w_in, q_norm_gain, k_norm_gain, attn_sinks, rel_bias, pool_w, pool_scale, w_attn_br, w_pool_br, w_out):
    depth = w_in.shape[0]
    n_b, n_s, _ = x_sample.shape
    past_len = 4096
    bias_p, tab_state, tab_new = _bias_tables(rel_bias, n_b, n_s)
    seg = jnp.arange(MXU_DIM) // HEAD_DIM
    ones = (seg[:, None] == seg[None, :]).astype(BF16)

    xp, xs = x_prompt, x_sample
    pk, pv, pp, sk, sv, sp = [], [], [], [], [], []
    for l in range(depth):
        ng = norm_gain[l].reshape(1, D_MODEL)
        w_in_b = w_in[l].astype(BF16)
        qg = jnp.tile(q_norm_gain[l] * (HEAD_DIM ** -0.5), N_HEADS)
        kg = jnp.tile(k_norm_gain[l], N_KV_HEADS)
        tail = (
            pool_w[l].astype(BF16),
            pool_scale[l].reshape(1, POOL_WIDTH),
            w_attn_br[l].astype(BF16),
            w_pool_br[l].astype(BF16),
            w_out[l].astype(BF16),
        )
        sink_rows = jnp.repeat(attn_sinks[l].reshape(N_KV_HEADS, GROUP_REP), PAIR, axis=1)
        xp, k_l, v_l, p_l = _prompt_layer(
            xp, ng, w_in_b,
            qg.reshape(ATTN_WIDTH, 1), kg.reshape(1, KV_WIDTH), ones, bias_p,
            sink_rows.reshape(N_KV_HEADS, 1, GROUP_REP * PAIR), *tail)
        pk.append(_kv_from_feature_major(k_l))
        pv.append(_kv_from_feature_major(v_l))
        pp.append(p_l[:, 1:, :])

        xs, k_l, v_l, p_l = _sample_layer(
            xs,
            _kv_to_feature_major(state_attn_k[l]),
            _kv_to_feature_major(state_attn_v[l]),
            jnp.pad(state_pool[l], ((0, 0), (1, 0), (0, 0))),
            past_len, attn_sinks[l], ng, w_in_b,
            qg.reshape(1, ATTN_WIDTH), kg.reshape(KV_WIDTH, 1), ones, tab_state, tab_new, *tail)
        sk.append(_kv_from_feature_major(k_l))
        sv.append(_kv_from_feature_major(v_l))
        sp.append(p_l[:, 1:, :])
    return (xp, xs, jnp.stack(pk), jnp.stack(pv), jnp.stack(pp),
            jnp.stack(sk), jnp.stack(sv), jnp.stack(sp))
```

```python
import functools
import math

import jax
import jax.numpy as jnp
from jax import lax
from jax.experimental import pallas as pl
from jax.experimental.pallas import tpu as pltpu

D_MODEL = 1024
CHUNK = 64
WINDOW = 128
N_HEADS = 16
N_KV_HEADS = 4
GROUP_REP = N_HEADS // N_KV_HEADS
HEAD_DIM = 64
ATTN_WIDTH = N_HEADS * HEAD_DIM
KV_WIDTH = N_KV_HEADS * HEAD_DIM
POOL_WINDOWS = (2, 4, 8, 16)
POOL_WIDTH = D_MODEL
POOL_GROUP = POOL_WIDTH // len(POOL_WINDOWS)
POOL_HIST = max(POOL_WINDOWS) - 1
HIST_ROWS = POOL_HIST + 1
N_BUCKETS = 32
MAX_DISTANCE = 128
EPS = 1e-6
NEG_INF = -1e30

_SPLITS = (ATTN_WIDTH, KV_WIDTH, KV_WIDTH, ATTN_WIDTH, POOL_WIDTH, POOL_WIDTH, D_MODEL, D_MODEL)
_OFFS = tuple(sum(_SPLITS[:i]) for i in range(len(_SPLITS) + 1))
C_Q, C_K, C_V, C_AG, C_PU, C_PG, C_MA, C_MP = (slice(_OFFS[i], _OFFS[i + 1]) for i in range(8))

LANES = 128
MXU_DIM = 256
PAIR = 2 * CHUNK
PAIR_KEYS = PAIR + WINDOW
TM = 512
PREFILL = 8
STAGE_ROWS, STAGE_COLS, N_STAGE = 512, 1024, 3
VMEM_LIMIT = 60 * 1024 * 1024

BF16 = jnp.bfloat16
F32 = jnp.float32


def _dot(a, b):
    return jnp.dot(a, b, preferred_element_type=F32)


def _dot_nt(a, b):
    return lax.dot_general(a, b, (((1,), (1,)), ((), ())), preferred_element_type=F32)


def _rmsnorm_rows(x, gain):
    y = x * lax.rsqrt(jnp.mean(x * x, axis=-1, keepdims=True) + EPS)
    return y * gain


def _head_rmsnorm(z, ones_ref, gain):
    sq = z * z
    hi = sq.astype(BF16)
    lo = (sq - hi.astype(F32)).astype(BF16)
    ones = ones_ref[...]
    parts = []
    for c in range(z.shape[1] // MXU_DIM):
        sl = slice(c * MXU_DIM, (c + 1) * MXU_DIM)
        parts.append(_dot(hi[:, sl], ones) + _dot(lo[:, sl], ones))
    ss = parts[0] if len(parts) == 1 else jnp.concatenate(parts, axis=1)
    return (z * lax.rsqrt(ss * (1.0 / HEAD_DIM) + EPS)) * gain


def _head_rmsnorm_fm(zt, gain_ref, h):
    hs = slice(h * HEAD_DIM, (h + 1) * HEAD_DIM)
    zh = zt[hs, :]
    ms = jnp.sum(zh * zh, axis=0, keepdims=True) * (1.0 / HEAD_DIM)
    return (zh * lax.rsqrt(ms + EPS)) * gain_ref[hs, :]


def _window_sums(full):
    outs = []
    for gi, w in enumerate(POOL_WINDOWS):
        s = full[:, gi * POOL_GROUP:(gi + 1) * POOL_GROUP]
        sh = 1
        while sh < w:
            s = s + pltpu.roll(s, sh, axis=0)
            sh *= 2
        outs.append(s)
    return outs


def _merge_tail(x, attn_o, pool_mixed, gates, pw_ref, ps_ref, wabr_ref, wpbr_ref, wout_ref):
    ag, pg, ma, mp = gates
    po = []
    for gi in range(len(POOL_WINDOWS)):
        po.append(_dot(pool_mixed[gi].astype(BF16), pw_ref[gi * POOL_GROUP:(gi + 1) * POOL_GROUP, :]))
    po = jnp.concatenate(po, axis=1) * ps_ref[...]
    p = _dot((po * (pg * jax.nn.sigmoid(pg))).astype(BF16), wpbr_ref[...])
    a = _dot((attn_o * (ag * jax.nn.sigmoid(ag))).astype(BF16), wabr_ref[...])
    acc = jax.nn.sigmoid(ma) * a + jax.nn.sigmoid(mp) * p
    return x + _dot(acc.astype(BF16), wout_ref[...])


def _diag2(a, axis_r, axis_c):
    z = jnp.zeros_like(a)
    return jnp.concatenate([jnp.concatenate([a, z], axis=axis_c),
                            jnp.concatenate([z, a], axis=axis_c)], axis=axis_r)


def _bias_kernel(n_s, tbl_ref, bkp_ref, bks_ref, bkn_ref, bp_ref, ts_ref, tn_ref):
    g = pl.program_id(0)

    def gather(bucket, h):
        acc = jnp.zeros(bucket.shape, F32)
        for b in range(N_BUCKETS):
            acc = jnp.where(bucket == b, tbl_ref[b, h], acc)
        return acc

    bkp = bkp_ref[...]
    for r in range(GROUP_REP):
        bp_ref[:, r * PAIR:(r + 1) * PAIR] = jnp.where(bkp < 0, NEG_INF, gather(bkp, g * GROUP_REP + r))
    bks = bks_ref[...]
    bkn = bkn_ref[...]
    for c in range(2):
        for j in range(2):
            h = g * GROUP_REP + 2 * c + j
            rows = slice(c * n_s, (c + 1) * n_s)
            ts_ref[rows, j * WINDOW:(j + 1) * WINDOW] = gather(bks, h)
            tn_ref[rows, j * bkn.shape[1]:(j + 1) * bkn.shape[1]] = gather(bkn, h)


def _t5_bucket(rel):
    assert (N_BUCKETS, MAX_DISTANCE) == (32, 128)
    nb = N_BUCKETS // 2
    max_exact = nb // 2
    n = jnp.abs(rel)
    large = jnp.minimum(max_exact + (31 - lax.clz(n * n)) - 6, nb - 1)
    return jnp.where(rel > 0, nb, 0) + jnp.where(n < max_exact, n, large)


def _bias_tables(rel_bias, n_b, n_s):
    kk = jnp.arange(PAIR_KEYS, dtype=jnp.int32)[:, None]
    qq = jnp.arange(PAIR, dtype=jnp.int32)[None, :]
    c0 = (qq // CHUNK) * CHUNK
    bkp = jnp.where((kk >= c0) & (kk < c0 + CHUNK + WINDOW), _t5_bucket(kk - WINDOW - qq), -1)
    fr = jnp.arange(n_s, dtype=jnp.int32)[:, None]
    bks = _t5_bucket(jnp.arange(WINDOW, dtype=jnp.int32)[None, :] - WINDOW - fr)
    bkn = _t5_bucket(jnp.arange(n_b * n_s, dtype=jnp.int32)[None, :] % n_s - fr)
    n_new = n_b * n_s
    whole = lambda g: (0, 0)
    per_g = lambda g: (g, 0, 0)
    return pl.pallas_call(
        functools.partial(_bias_kernel, n_s),
        grid=(N_KV_HEADS,),
        in_specs=[
            pl.BlockSpec(memory_space=pltpu.SMEM),
            pl.BlockSpec((PAIR_KEYS, PAIR), whole),
            pl.BlockSpec((n_s, WINDOW), whole),
            pl.BlockSpec((n_s, n_new), whole),
        ],
        out_specs=[
            pl.BlockSpec((None, PAIR_KEYS, GROUP_REP * PAIR), per_g),
            pl.BlockSpec((None, 2 * n_s, 2 * WINDOW), per_g),
            pl.BlockSpec((None, 2 * n_s, 2 * n_new), per_g),
        ],
        out_shape=[
            jax.ShapeDtypeStruct((N_KV_HEADS, PAIR_KEYS, GROUP_REP * PAIR), F32),
            jax.ShapeDtypeStruct((N_KV_HEADS, 2 * n_s, 2 * WINDOW), F32),
            jax.ShapeDtypeStruct((N_KV_HEADS, 2 * n_s, 2 * n_new), F32),
        ],
        name="bias_tables",
    )(rel_bias, bkp, bks, bkn)


def _weight_jobs(pairs):
    jobs = []
    for src, dst in pairs:
        n_rows, n_cols = dst.shape
        for r0 in range(0, n_rows, STAGE_ROWS):
            for c0 in range(0, n_cols, STAGE_COLS):
                jobs.append((src, dst, r0, c0, min(STAGE_COLS, n_cols - c0)))
    return jobs


def _prompt_kernel(x_ref, ng_ref, qgt_ref, kg_ref, ones_ref, bias_ref, sink_ref, ps_ref,
                   win_hbm, wabr_hbm, wpbr_hbm, wout_hbm, pw_hbm,
                   y_ref, pk_ref, pv_ref, pp_ref, winb_hbm, wabrb_hbm, wpbrb_hbm, woutb_hbm, pwb_hbm,
                   win_ref, wabr_ref, wpbr_ref, wout_ref, pw_ref, stage, sem_in, sem_out,
                   wt_s, h_s, qt_s, k_s, vt_s, attn_s, pbuf):
    t = pl.program_id(1)
    first = (pl.program_id(0) == 0) & (t == 0)
    weights = ((win_hbm, win_ref), (wabr_hbm, wabr_ref), (wpbr_hbm, wpbr_ref), (wout_hbm, wout_ref),
               (pw_hbm, pw_ref))
    exports = (winb_hbm, wabrb_hbm, wpbrb_hbm, woutb_hbm, pwb_hbm)

    def export_copy(i):
        return pltpu.make_async_copy(weights[i][1], exports[i], sem_out.at[i])

    @pl.when(first)
    def _():
        jobs = _weight_jobs(weights)

        def load(i):
            src, _, r0, c0, w = jobs[i]
            return pltpu.make_async_copy(src.at[pl.ds(r0, STAGE_ROWS), pl.ds(c0, w)],
                                         stage.at[i % N_STAGE, :, pl.ds(0, w)], sem_in.at[i % N_STAGE])

        for i in range(min(N_STAGE - 1, len(jobs))):
            load(i).start()
        for i in range(len(jobs)):
            if i + N_STAGE - 1 < len(jobs):
                load(i + N_STAGE - 1).start()
            load(i).wait()
            _, dst, r0, c0, w = jobs[i]
            dst[r0:r0 + STAGE_ROWS, c0:c0 + w] = stage[i % N_STAGE, :, 0:w].astype(BF16)
        for i in range(len(weights)):
            export_copy(i).start()
        wt_s[0:ATTN_WIDTH, :] = win_ref[:, C_Q].T
        wt_s[ATTN_WIDTH:, :] = win_ref[:, C_V].T

    @pl.when(t == 0)
    def _():
        k_s[:, 0:WINDOW, :] = jnp.zeros((2, WINDOW, LANES), BF16)
        vt_s[:, 0:WINDOW] = jnp.zeros((KV_WIDTH, WINDOW), BF16)
        pbuf[0:HIST_ROWS, :] = jnp.zeros((HIST_ROWS, POOL_WIDTH), F32)

    x = x_ref[...]
    h_s[...] = _rmsnorm_rows(x, ng_ref[...]).astype(BF16)
    hb = h_s[...]

    n_c = D_MODEL // MXU_DIM
    fill_cols = [slice(c.start + i * MXU_DIM, c.start + (i + 1) * MXU_DIM)
                 for c in (C_PU, C_PG, C_AG, C_MA, C_MP) for i in range(n_c)]
    filled = []

    def fill(upto):
        for cols in fill_cols[len(filled):upto]:
            filled.append(_dot(hb, win_ref[:, cols]))

    zqt = _dot_nt(wt_s[0:ATTN_WIDTH, :], hb)
    zk = _dot(hb, win_ref[:, C_K])
    vt = _dot_nt(wt_s[ATTN_WIDTH:, :], hb)
    fill(PREFILL)

    for h in range(N_HEADS):
        qt_s[h * HEAD_DIM:(h + 1) * HEAD_DIM, :] = _head_rmsnorm_fm(zqt, qgt_ref, h).astype(BF16)
    k = _head_rmsnorm(zk, ones_ref, kg_ref[...])

    pk_ref[...] = k[TM - WINDOW:, :].T
    pv_ref[...] = vt[:, TM - WINDOW:]

    k_s[0, WINDOW:, :] = k[:, :LANES].astype(BF16)
    k_s[1, WINDOW:, :] = k[:, LANES:].astype(BF16)
    vt_s[:, WINDOW:] = vt.astype(BF16)

    zeros_half = jnp.zeros((HEAD_DIM, PAIR), BF16)

    def scores(p, g):
        tok = slice(p * PAIR, (p + 1) * PAIR)
        keys = slice(p * PAIR, p * PAIR + PAIR_KEYS)
        cols = []
        for r in range(GROUP_REP):
            h = g * GROUP_REP + r
            qh = qt_s[h * HEAD_DIM:(h + 1) * HEAD_DIM, tok]
            cols.append(jnp.concatenate([qh, zeros_half] if g % 2 == 0 else [zeros_half, qh], axis=0))
        rhs = jnp.concatenate(cols, axis=1)
        s = _dot(k_s[g // 2, keys, :], rhs) + bias_ref[g]
        if p == 0:
            kk = lax.broadcasted_iota(jnp.int32, s.shape, 0)
            s = jnp.where((kk >= WINDOW) | (t > 0), s, NEG_INF)
        return s

    def finish(p, g, s):
        tok = slice(p * PAIR, (p + 1) * PAIR)
        keys = slice(p * PAIR, p * PAIR + PAIR_KEYS)
        sink = sink_ref[g]
        m = jnp.maximum(jnp.max(s, axis=0, keepdims=True), sink)
        e = jnp.exp(s - m)
        denom = jnp.sum(e, axis=0, keepdims=True) + jnp.exp(sink - m)
        vg = vt_s[g * HEAD_DIM:(g + 1) * HEAD_DIM, keys]
        o_t = _dot(vg, e.astype(BF16)) * (1.0 / denom)
        c = g * MXU_DIM
        attn_s[tok, c:c + LANES] = jnp.concatenate(
            [o_t[:, 0:PAIR], o_t[:, PAIR:2 * PAIR]], axis=0).T
        attn_s[tok, c + LANES:c + MXU_DIM] = jnp.concatenate(
            [o_t[:, 2 * PAIR:3 * PAIR], o_t[:, 3 * PAIR:]], axis=0).T

    pu = jnp.concatenate(filled[:n_c], axis=1)
    pbuf[HIST_ROWS:, :] = pu
    pp_ref[...] = pu[TM - HIST_ROWS:, :]
    sums = _window_sums(pbuf[...])
    pos1 = (t * TM + 1 + lax.broadcasted_iota(jnp.int32, (TM, 1), 0)).astype(F32)
    mixed = []
    for gi, w in enumerate(POOL_WINDOWS):
        inv_cnt = 1.0 / jnp.minimum(float(w), pos1)
        mixed.append(sums[gi][HIST_ROWS:, :] * inv_cnt - pu[:, gi * POOL_GROUP:(gi + 1) * POOL_GROUP])

    blocks = [(p, g) for p in range(TM // PAIR) for g in range(N_KV_HEADS)]
    s_next = scores(*blocks[0])
    for i, blk in enumerate(blocks):
        s_cur = s_next
        if i + 1 < len(blocks):
            s_next = scores(*blocks[i + 1])
        fill(PREFILL + (len(fill_cols) - PREFILL) * (i + 1) // len(blocks))
        finish(*blk, s_cur)
    pg, ag, ma, mp = (jnp.concatenate(filled[j * n_c:(j + 1) * n_c], axis=1) for j in range(1, 5))

    y_ref[...] = _merge_tail(x, attn_s[...], mixed, (ag, pg, ma, mp), pw_ref, ps_ref,
                             wabr_ref, wpbr_ref, wout_ref)

    k_s[:, 0:WINDOW, :] = k_s[:, TM:TM + WINDOW, :]
    vt_s[:, 0:WINDOW] = vt_s[:, TM:TM + WINDOW]
    pbuf[0:HIST_ROWS, :] = pbuf[TM:TM + HIST_ROWS, :]

    @pl.when(first)
    def _():
        for i in range(len(weights)):
            export_copy(i).wait()


def _resident(shape):
    nd = len(shape)
    return pl.BlockSpec(shape, lambda b, t: (0,) * nd, pipeline_mode=pl.Buffered(1))


def _prompt_layer(x, ng, qgt, kg, ones, bias_p, sink_rows, ps, weights):
    B, S, _ = x.shape
    assert S % TM == 0 and TM % PAIR == 0 and TM >= WINDOW
    assert all(w.shape[0] % STAGE_ROWS == 0 for w in weights)
    consts = (ng, qgt, kg, ones, bias_p, sink_rows, ps)
    hbm = pl.BlockSpec(memory_space=pl.ANY)
    return pl.pallas_call(
        _prompt_kernel,
        grid=(B, S // TM),
        in_specs=[pl.BlockSpec((None, TM, D_MODEL), lambda b, t: (b, t, 0))]
                 + [_resident(c.shape) for c in consts] + [hbm] * len(weights),
        out_specs=[
            pl.BlockSpec((None, TM, D_MODEL), lambda b, t: (b, t, 0)),
            pl.BlockSpec((None, KV_WIDTH, WINDOW), lambda b, t: (b, 0, 0)),
            pl.BlockSpec((None, KV_WIDTH, WINDOW), lambda b, t: (b, 0, 0)),
            pl.BlockSpec((None, HIST_ROWS, POOL_WIDTH), lambda b, t: (b, 0, 0)),
        ] + [hbm] * len(weights),
        out_shape=[
            jax.ShapeDtypeStruct((B, S, D_MODEL), F32),
            jax.ShapeDtypeStruct((B, KV_WIDTH, WINDOW), F32),
            jax.ShapeDtypeStruct((B, KV_WIDTH, WINDOW), F32),
            jax.ShapeDtypeStruct((B, HIST_ROWS, POOL_WIDTH), F32),
        ] + [jax.ShapeDtypeStruct(w.shape, BF16) for w in weights],
        scratch_shapes=[pltpu.VMEM(w.shape, BF16) for w in weights] + [
            pltpu.VMEM((N_STAGE, STAGE_ROWS, STAGE_COLS), F32),
            pltpu.SemaphoreType.DMA((N_STAGE,)),
            pltpu.SemaphoreType.DMA((len(weights),)),
            pltpu.VMEM((ATTN_WIDTH + KV_WIDTH, D_MODEL), BF16),
            pltpu.VMEM((TM, D_MODEL), BF16),
            pltpu.VMEM((ATTN_WIDTH, TM), BF16),
            pltpu.VMEM((2, WINDOW + TM, LANES), BF16),
            pltpu.VMEM((KV_WIDTH, WINDOW + TM), BF16),
            pltpu.VMEM((TM, ATTN_WIDTH), F32),
            pltpu.VMEM((HIST_ROWS + TM, POOL_WIDTH), F32),
        ],
        compiler_params=pltpu.CompilerParams(
            dimension_semantics=("arbitrary", "arbitrary"),
            vmem_limit_bytes=VMEM_LIMIT,
        ),
        name="prompt_layer",
    )(x, *consts, *weights)


def _sample_kernel(n_b, n_s, start,
                   sinks_ref, x_ref, skt_ref, svt_ref, sp_ref, ng_ref, win_ref,
                   qg_ref, kgt_ref, ones_ref, ts_ref, tn_ref, pw_ref, ps_ref, wabr_ref, wpbr_ref, wout_ref,
                   y_ref, ok_ref, ov_ref, op_ref):
    n_tok = n_b * n_s
    x = x_ref[...]
    hb = _rmsnorm_rows(x, ng_ref[...]).astype(BF16)

    q = _head_rmsnorm(_dot(hb, win_ref[:, C_Q]), ones_ref, qg_ref[...]).astype(BF16)
    q3 = q.reshape(n_b, n_s, ATTN_WIDTH)
    zkt = _dot_nt(win_ref[:, C_K].T, hb)
    knt = jnp.concatenate([_head_rmsnorm_fm(zkt, kgt_ref, g) for g in range(N_KV_HEADS)], axis=0)
    vnt = _dot_nt(win_ref[:, C_V].T, hb)

    lane = lax.broadcasted_iota(jnp.int32, (1, WINDOW), 1)
    per_tile = LANES // n_s
    for s in range(n_b):
        tile = slice((s // per_tile) * LANES, (s // per_tile + 1) * LANES)
        shift = (WINDOW - n_s - (s % per_tile) * n_s) % LANES
        for new, st_ref, o_ref in ((knt, skt_ref, ok_ref), (vnt, svt_ref, ov_ref)):
            fresh = new[:, tile] if shift == 0 else pltpu.roll(new[:, tile], shift, axis=1)
            o_ref[s] = jnp.where(lane < WINDOW - n_s, pltpu.roll(st_ref[s], WINDOW - n_s, axis=1), fresh)

    knb = knt.astype(BF16)
    vnb = vnt.astype(BF16)
    key_stream = (lax.broadcasted_iota(jnp.int32, (n_b, 1, 2 * n_tok), 2) & (n_tok - 1)) >> (n_s.bit_length() - 1)
    own = key_stream == lax.broadcasted_iota(jnp.int32, (n_b, 1, 2 * n_tok), 0)
    top = lax.broadcasted_iota(jnp.int32, (1, 2 * n_s, 1), 1) < n_s
    low = lax.broadcasted_iota(jnp.int32, (1, 1, LANES), 2) < HEAD_DIM

    def scores(g):
        hd = slice(g * HEAD_DIM, (g + 1) * HEAD_DIM)
        c = g * MXU_DIM
        lhs = jnp.concatenate([q3[:, :, c:c + LANES], q3[:, :, c + LANES:c + MXU_DIM]], axis=1)
        kst = _diag2(skt_ref[:, hd, :].astype(BF16), 1, 2)
        s_st = jnp.stack([_dot(lhs[s], kst[s]) for s in range(n_b)]) + ts_ref[g][None]
        s_nw = _dot(lhs.reshape(2 * n_tok, LANES), _diag2(knb[hd, :], 0, 1))
        s_nw = jnp.where(own, s_nw.reshape(n_b, 2 * n_s, 2 * n_tok) + tn_ref[g][None], NEG_INF)
        return s_st, s_nw

    def finish(g, s_st, s_nw):
        hd = slice(g * HEAD_DIM, (g + 1) * HEAD_DIM)
        p_st, p_nw, inv = [], [], []
        for j in range(2):
            a = s_st[:, :, j * WINDOW:(j + 1) * WINDOW]
            b = s_nw[:, :, j * n_tok:(j + 1) * n_tok]
            sink = jnp.where(top, sinks_ref[4 * g + j], sinks_ref[4 * g + 2 + j])
            m = jnp.maximum(jnp.maximum(jnp.max(a, axis=-1, keepdims=True),
                                        jnp.max(b, axis=-1, keepdims=True)), sink)
            ea = jnp.exp(a - m)
            eb = jnp.exp(b - m)
            inv.append(1.0 / (jnp.sum(ea, axis=-1, keepdims=True) + jnp.sum(eb, axis=-1, keepdims=True)
                              + jnp.exp(sink - m)))
            p_st.append(ea.astype(BF16))
            p_nw.append(eb.astype(BF16))
        p_st = jnp.concatenate(p_st, axis=2)
        p_nw = jnp.concatenate(p_nw, axis=2).reshape(2 * n_tok, 2 * n_tok)
        vst = _diag2(svt_ref[:, hd, :].astype(BF16), 1, 2)
        o = jnp.stack([_dot_nt(p_st[s], vst[s]) for s in range(n_b)])
        o = o + _dot_nt(p_nw, _diag2(vnb[hd, :], 0, 1)).reshape(n_b, 2 * n_s, LANES)
        o = o * jnp.where(low, inv[0], inv[1])
        return jnp.concatenate([o[:, :n_s, :], o[:, n_s:, :]], axis=2)

    outs = []
    nxt = scores(0)
    for g in range(N_KV_HEADS):
        cur = nxt
        if g + 1 < N_KV_HEADS:
            nxt = scores(g + 1)
        outs.append(finish(g, *cur))
    attn_o = jnp.concatenate(outs, axis=2).reshape(n_tok, ATTN_WIDTH)

    pu = _dot(hb, win_ref[:, C_PU])
    pu3 = pu.reshape(n_b, n_s, POOL_WIDTH)
    full3 = jnp.concatenate([sp_ref[...], pu3], axis=1)
    op_ref[...] = full3[:, n_s:, :]
    seg = HIST_ROWS + n_s
    sums = _window_sums(full3.reshape(n_b * seg, POOL_WIDTH))
    pos1 = (start + 1 + lax.broadcasted_iota(jnp.int32, (n_s, 1), 0)).astype(F32)
    mixed = []
    for gi, w in enumerate(POOL_WINDOWS):
        inv_cnt = 1.0 / jnp.minimum(float(w), pos1)
        sw = sums[gi].reshape(n_b, seg, POOL_GROUP)[:, HIST_ROWS:, :] * inv_cnt
        mixed.append(sw.reshape(n_tok, POOL_GROUP) - pu[:, gi * POOL_GROUP:(gi + 1) * POOL_GROUP])

    gates = tuple(_dot(hb, win_ref[:, c]) for c in (C_AG, C_PG, C_MA, C_MP))
    y_ref[...] = _merge_tail(x, attn_o, mixed, gates, pw_ref, ps_ref, wabr_ref, wpbr_ref, wout_ref)


def _sample_layer(x, skt, svt, sp, start, sinks, ng, w_in, qg, kgt, ones, ts, tn,
                  pw, ps, wabr, wpbr, wout):
    n_b, n_s, _ = x.shape
    n_tok = n_b * n_s
    assert n_s % 16 == 0 and LANES % n_s == 0 and n_s >= POOL_HIST and n_tok % LANES == 0
    assert n_s & (n_s - 1) == 0 and n_tok & (n_tok - 1) == 0
    vmem = pl.BlockSpec(memory_space=pltpu.VMEM)
    y, ok, ov, op = pl.pallas_call(
        functools.partial(_sample_kernel, n_b, n_s, start),
        in_specs=[pl.BlockSpec(memory_space=pltpu.SMEM)] + [vmem] * 16,
        out_specs=[vmem] * 4,
        out_shape=[
            jax.ShapeDtypeStruct((n_tok, D_MODEL), F32),
            jax.ShapeDtypeStruct((n_b, KV_WIDTH, WINDOW), F32),
            jax.ShapeDtypeStruct((n_b, KV_WIDTH, WINDOW), F32),
            jax.ShapeDtypeStruct((n_b, HIST_ROWS, POOL_WIDTH), F32),
        ],
        compiler_params=pltpu.CompilerParams(vmem_limit_bytes=VMEM_LIMIT),
        name="sample_layer",
    )(sinks, x.reshape(n_tok, D_MODEL), skt, svt, sp, ng, w_in, qg, kgt, ones, ts, tn,
      pw, ps, wabr, wpbr, wout)
    return y.reshape(n_b, n_s, D_MODEL), ok, ov, op


def _kv_to_feature_major(a):
    return jnp.transpose(a, (0, 2, 3, 1)).reshape(a.shape[0], KV_WIDTH, WINDOW)


def _kv_from_feature_major(a):
    return jnp.transpose(a.reshape(a.shape[0], N_KV_HEADS, HEAD_DIM, WINDOW), (0, 3, 1, 2))


def kernel(x_prompt, x_sample, state_attn_k, state_attn_v, state_pool, norm_gain, w_in, q_norm_gain, k_norm_gain, attn_sinks, rel_bias, pool_w, pool_scale, w_attn_br, w_pool_br, w_out):
    depth = w_in.shape[0]
    n_b, n_s, _ = x_sample.shape
    past_len = 4096
    bias_p, tab_state, tab_new = _bias_tables(rel_bias, n_b, n_s)
    seg = jnp.arange(MXU_DIM) // HEAD_DIM
    ones = (seg[:, None] == seg[None, :]).astype(BF16)

    xp, xs = x_prompt, x_sample
    pk, pv, pp, sk, sv, sp = [], [], [], [], [], []
    for l in range(depth):
        ng = norm_gain[l].reshape(1, D_MODEL)
        qg = jnp.tile(q_norm_gain[l] * (HEAD_DIM ** -0.5), N_HEADS)
        kg = jnp.tile(k_norm_gain[l], N_KV_HEADS)
        ps = pool_scale[l].reshape(1, POOL_WIDTH)
        sink_rows = jnp.repeat(attn_sinks[l].reshape(N_KV_HEADS, GROUP_REP), PAIR, axis=1)
        xp, k_l, v_l, p_l, w_in_b, wabr_b, wpbr_b, wout_b, pw_b = _prompt_layer(
            xp, ng, qg.reshape(ATTN_WIDTH, 1), kg.reshape(1, KV_WIDTH), ones, bias_p,
            sink_rows.reshape(N_KV_HEADS, 1, GROUP_REP * PAIR), ps,
            (w_in[l], w_attn_br[l], w_pool_br[l], w_out[l], pool_w[l].reshape(POOL_WIDTH, POOL_GROUP)))
        tail = (pw_b, ps, wabr_b, wpbr_b, wout_b)
        pk.append(_kv_from_feature_major(k_l))
        pv.append(_kv_from_feature_major(v_l))
        pp.append(p_l[:, 1:, :])

        xs, k_l, v_l, p_l = _sample_layer(
            xs,
            _kv_to_feature_major(state_attn_k[l]),
            _kv_to_feature_major(state_attn_v[l]),
            jnp.pad(state_pool[l], ((0, 0), (1, 0), (0, 0))),
            past_len, attn_sinks[l], ng, w_in_b,
            qg.reshape(1, ATTN_WIDTH), kg.reshape(KV_WIDTH, 1), ones, tab_state, tab_new, *tail)
        sk.append(_kv_from_feature_major(k_l))
        sv.append(_kv_from_feature_major(v_l))
        sp.append(p_l[:, 1:, :])
    return (xp, xs, jnp.stack(pk), jnp.stack(pv), jnp.stack(pp),
            jnp.stack(sk), jnp.stack(sv), jnp.stack(sp))
```

```python
import functools
import math

import jax
import jax.numpy as jnp
from jax import lax
from jax.experimental import pallas as pl
from jax.experimental.pallas import tpu as pltpu

D_MODEL = 1024
CHUNK = 64
WINDOW = 128
N_HEADS = 16
N_KV_HEADS = 4
GROUP_REP = N_HEADS // N_KV_HEADS
HEAD_DIM = 64
ATTN_WIDTH = N_HEADS * HEAD_DIM
KV_WIDTH = N_KV_HEADS * HEAD_DIM
POOL_WINDOWS = (2, 4, 8, 16)
POOL_WIDTH = D_MODEL
POOL_GROUP = POOL_WIDTH // len(POOL_WINDOWS)
POOL_HIST = max(POOL_WINDOWS) - 1
HIST_ROWS = POOL_HIST + 1
N_BUCKETS = 32
MAX_DISTANCE = 128
EPS = 1e-6
NEG_INF = -1e30

_SPLITS = (ATTN_WIDTH, KV_WIDTH, KV_WIDTH, ATTN_WIDTH, POOL_WIDTH, POOL_WIDTH, D_MODEL, D_MODEL)
_OFFS = tuple(sum(_SPLITS[:i]) for i in range(len(_SPLITS) + 1))
C_Q, C_K, C_V, C_AG, C_PU, C_PG, C_MA, C_MP = (slice(_OFFS[i], _OFFS[i + 1]) for i in range(8))

LANES = 128
MXU_DIM = 256
PAIR = 2 * CHUNK
PAIR_KEYS = PAIR + WINDOW
TM = 512
PREFILL_K, PREFILL = 2, 8
STAGE_ROWS, STAGE_COLS, N_STAGE = ATTN_WIDTH, TM, 3
VMEM_LIMIT = 60 * 1024 * 1024

BF16 = jnp.bfloat16
F32 = jnp.float32


def _dot(a, b):
    return jnp.dot(a, b, preferred_element_type=F32)


def _dot_nt(a, b):
    return lax.dot_general(a, b, (((1,), (1,)), ((), ())), preferred_element_type=F32)


def _rmsnorm_rows(x, gain):
    y = x * lax.rsqrt(jnp.mean(x * x, axis=-1, keepdims=True) + EPS)
    return y * gain


def _head_rmsnorm(z, ones_ref, gain):
    sq = z * z
    hi = sq.astype(BF16)
    lo = (sq - hi.astype(F32)).astype(BF16)
    ones = ones_ref[...]
    parts = []
    for c in range(z.shape[1] // MXU_DIM):
        sl = slice(c * MXU_DIM, (c + 1) * MXU_DIM)
        parts.append(_dot(hi[:, sl], ones) + _dot(lo[:, sl], ones))
    ss = parts[0] if len(parts) == 1 else jnp.concatenate(parts, axis=1)
    return (z * lax.rsqrt(ss * (1.0 / HEAD_DIM) + EPS)) * gain


def _head_rmsnorm_fm(zt, gain_ref, h):
    hs = slice(h * HEAD_DIM, (h + 1) * HEAD_DIM)
    zh = zt[hs, :]
    ms = jnp.sum(zh * zh, axis=0, keepdims=True) * (1.0 / HEAD_DIM)
    return (zh * lax.rsqrt(ms + EPS)) * gain_ref[hs, :]


def _window_sums(full):
    outs = []
    for gi, w in enumerate(POOL_WINDOWS):
        s = full[:, gi * POOL_GROUP:(gi + 1) * POOL_GROUP]
        sh = 1
        while sh < w:
            s = s + pltpu.roll(s, sh, axis=0)
            sh *= 2
        outs.append(s)
    return outs


def _merge_branches(attn_o, pool_mixed, gates, pw_ref, ps_ref, wabr_ref, wpbr_ref):
    ag, pg, ma, mp = gates
    po = []
    for gi in range(len(POOL_WINDOWS)):
        po.append(_dot(pool_mixed[gi].astype(BF16), pw_ref[gi * POOL_GROUP:(gi + 1) * POOL_GROUP, :]))
    po = jnp.concatenate(po, axis=1) * ps_ref[...]
    p = _dot((po * (pg * jax.nn.sigmoid(pg))).astype(BF16), wpbr_ref[...])
    a = _dot((attn_o * (ag * jax.nn.sigmoid(ag))).astype(BF16), wabr_ref[...])
    acc = jax.nn.sigmoid(ma) * a + jax.nn.sigmoid(mp) * p
    return acc.astype(BF16)


def _diag2(a, axis_r, axis_c):
    z = jnp.zeros_like(a)
    return jnp.concatenate([jnp.concatenate([a, z], axis=axis_c),
                            jnp.concatenate([z, a], axis=axis_c)], axis=axis_r)


def _bias_kernel(n_s, tbl_ref, bkp_ref, bks_ref, bkn_ref, bp_ref, ts_ref, tn_ref):
    g = pl.program_id(0)

    def gather(bucket, h):
        acc = jnp.zeros(bucket.shape, F32)
        for b in range(N_BUCKETS):
            acc = jnp.where(bucket == b, tbl_ref[b, h], acc)
        return acc

    bkp = bkp_ref[...]
    for r in range(GROUP_REP):
        bp_ref[:, r * PAIR:(r + 1) * PAIR] = jnp.where(bkp < 0, NEG_INF, gather(bkp, g * GROUP_REP + r))
    bks = bks_ref[...]
    bkn = bkn_ref[...]
    for c in range(2):
        for j in range(2):
            h = g * GROUP_REP + 2 * c + j
            rows = slice(c * n_s, (c + 1) * n_s)
            ts_ref[rows, j * WINDOW:(j + 1) * WINDOW] = gather(bks, h)
            tn_ref[rows, j * bkn.shape[1]:(j + 1) * bkn.shape[1]] = gather(bkn, h)


def _t5_bucket(rel):
    assert (N_BUCKETS, MAX_DISTANCE) == (32, 128)
    nb = N_BUCKETS // 2
    max_exact = nb // 2
    n = jnp.abs(rel)
    large = jnp.minimum(max_exact + (31 - lax.clz(n * n)) - 6, nb - 1)
    return jnp.where(rel > 0, nb, 0) + jnp.where(n < max_exact, n, large)


def _bias_tables(rel_bias, n_b, n_s):
    kk = jnp.arange(PAIR_KEYS, dtype=jnp.int32)[:, None]
    qq = jnp.arange(PAIR, dtype=jnp.int32)[None, :]
    c0 = (qq // CHUNK) * CHUNK
    bkp = jnp.where((kk >= c0) & (kk < c0 + CHUNK + WINDOW), _t5_bucket(kk - WINDOW - qq), -1)
    fr = jnp.arange(n_s, dtype=jnp.int32)[:, None]
    bks = _t5_bucket(jnp.arange(WINDOW, dtype=jnp.int32)[None, :] - WINDOW - fr)
    bkn = _t5_bucket(jnp.arange(n_b * n_s, dtype=jnp.int32)[None, :] % n_s - fr)
    n_new = n_b * n_s
    whole = lambda g: (0, 0)
    per_g = lambda g: (g, 0, 0)
    return pl.pallas_call(
        functools.partial(_bias_kernel, n_s),
        grid=(N_KV_HEADS,),
        in_specs=[
            pl.BlockSpec(memory_space=pltpu.SMEM),
            pl.BlockSpec((PAIR_KEYS, PAIR), whole),
            pl.BlockSpec((n_s, WINDOW), whole),
            pl.BlockSpec((n_s, n_new), whole),
        ],
        out_specs=[
            pl.BlockSpec((None, PAIR_KEYS, GROUP_REP * PAIR), per_g),
            pl.BlockSpec((None, 2 * n_s, 2 * WINDOW), per_g),
            pl.BlockSpec((None, 2 * n_s, 2 * n_new), per_g),
        ],
        out_shape=[
            jax.ShapeDtypeStruct((N_KV_HEADS, PAIR_KEYS, GROUP_REP * PAIR), F32),
            jax.ShapeDtypeStruct((N_KV_HEADS, 2 * n_s, 2 * WINDOW), F32),
            jax.ShapeDtypeStruct((N_KV_HEADS, 2 * n_s, 2 * n_new), F32),
        ],
        name="bias_tables",
    )(rel_bias, bkp, bks, bkn)


def _weight_jobs(pairs):
    jobs = []
    for src, dst in pairs:
        n_rows, n_cols = dst.shape
        for r0 in range(0, n_rows, STAGE_ROWS):
            for c0 in range(0, n_cols, STAGE_COLS):
                jobs.append((src, dst, r0, c0, min(STAGE_COLS, n_cols - c0)))
    return jobs


def _prompt_kernel(x_ref, xn_ref, ng_ref, qgt_ref, kg_ref, ones_ref, bias_ref, sink_ref, ps_ref,
                   win_hbm, wabr_hbm, wpbr_hbm, wout_hbm, pw_hbm,
                   y_ref, pk_ref, pv_ref, pp_ref, winb_hbm, wabrb_hbm, wpbrb_hbm, woutb_hbm, pwb_hbm,
                   win_ref, wabr_ref, wpbr_ref, wout_ref, pw_ref, stage, sem_in, sem_out,
                   wt_s, h_s, qt_s, k_s, vt_s, attn_s, pbuf):
    t = pl.program_id(1)
    first = (pl.program_id(0) == 0) & (t == 0)
    zq_s = stage.at[0]
    weights = ((win_hbm, win_ref), (wabr_hbm, wabr_ref), (wpbr_hbm, wpbr_ref), (wout_hbm, wout_ref),
               (pw_hbm, pw_ref))
    exports = (winb_hbm, wabrb_hbm, wpbrb_hbm, woutb_hbm, pwb_hbm)

    def export_copy(i):
        return pltpu.make_async_copy(weights[i][1], exports[i], sem_out.at[i])

    @pl.when(first)
    def _():
        jobs = _weight_jobs(weights)

        def load(i):
            src, _, r0, c0, w = jobs[i]
            return pltpu.make_async_copy(src.at[pl.ds(r0, STAGE_ROWS), pl.ds(c0, w)],
                                         stage.at[i % N_STAGE, :, pl.ds(0, w)], sem_in.at[i % N_STAGE])

        for i in range(min(N_STAGE - 1, len(jobs))):
            load(i).start()
        for i in range(len(jobs)):
            if i + N_STAGE - 1 < len(jobs):
                load(i + N_STAGE - 1).start()
            load(i).wait()
            _, dst, r0, c0, w = jobs[i]
            dst[r0:r0 + STAGE_ROWS, c0:c0 + w] = stage[i % N_STAGE, :, 0:w].astype(BF16)
        for i in range(len(weights)):
            export_copy(i).start()
        wt_s[0:ATTN_WIDTH, :] = win_ref[:, C_Q].T
        wt_s[ATTN_WIDTH:, :] = win_ref[:, C_V].T
        h_s[...] = _rmsnorm_rows(x_ref[...], ng_ref[...]).astype(BF16)
        zq_s[...] = _dot_nt(wt_s[0:ATTN_WIDTH, :], h_s[...])

    @pl.when(t == 0)
    def _():
        k_s[:, 0:WINDOW, :] = jnp.zeros((2, WINDOW, LANES), BF16)
        vt_s[:, 0:WINDOW] = jnp.zeros((KV_WIDTH, WINDOW), BF16)
        pbuf[0:HIST_ROWS, :] = jnp.zeros((HIST_ROWS, POOL_WIDTH), F32)

    hb = h_s[...]

    n_c = D_MODEL // MXU_DIM
    fill_cols = [slice(c.start + i * MXU_DIM, c.start + (i + 1) * MXU_DIM)
                 for c in (C_PU, C_PG, C_AG, C_MA, C_MP) for i in range(n_c)]
    filled = []

    def fill(upto):
        for cols in fill_cols[len(filled):upto]:
            filled.append(_dot(hb, win_ref[:, cols]))

    zqt = zq_s[...]
    zk = _dot(hb, win_ref[:, C_K])
    vt = _dot_nt(wt_s[ATTN_WIDTH:, :], hb)
    fill(PREFILL_K)
    k = _head_rmsnorm(zk, ones_ref, kg_ref[...])
    fill(PREFILL)
    for h in range(N_HEADS):
        qt_s[h * HEAD_DIM:(h + 1) * HEAD_DIM, :] = _head_rmsnorm_fm(zqt, qgt_ref, h).astype(BF16)

    pk_ref[...] = k[TM - WINDOW:, :].T
    pv_ref[...] = vt[:, TM - WINDOW:]

    k_s[0, WINDOW:, :] = k[:, :LANES].astype(BF16)
    k_s[1, WINDOW:, :] = k[:, LANES:].astype(BF16)
    vt_s[:, WINDOW:] = vt.astype(BF16)

    zeros_half = jnp.zeros((HEAD_DIM, PAIR), BF16)

    def scores(p, g):
        tok = slice(p * PAIR, (p + 1) * PAIR)
        keys = slice(p * PAIR, p * PAIR + PAIR_KEYS)
        cols = []
        for r in range(GROUP_REP):
            h = g * GROUP_REP + r
            qh = qt_s[h * HEAD_DIM:(h + 1) * HEAD_DIM, tok]
            cols.append(jnp.concatenate([qh, zeros_half] if g % 2 == 0 else [zeros_half, qh], axis=0))
        rhs = jnp.concatenate(cols, axis=1)
        s = _dot(k_s[g // 2, keys, :], rhs) + bias_ref[g]
        if p == 0:
            kk = lax.broadcasted_iota(jnp.int32, s.shape, 0)
            s = jnp.where((kk >= WINDOW) | (t > 0), s, NEG_INF)
        return s

    def finish(p, g, s):
        tok = slice(p * PAIR, (p + 1) * PAIR)
        keys = slice(p * PAIR, p * PAIR + PAIR_KEYS)
        sink = sink_ref[g]
        m = jnp.maximum(jnp.max(s, axis=0, keepdims=True), sink)
        e = jnp.exp(s - m)
        denom = jnp.sum(e, axis=0, keepdims=True) + jnp.exp(sink - m)
        vg = vt_s[g * HEAD_DIM:(g + 1) * HEAD_DIM, keys]
        o_t = _dot(vg, e.astype(BF16)) * (1.0 / denom)
        c = g * MXU_DIM
        attn_s[tok, c:c + LANES] = jnp.concatenate(
            [o_t[:, 0:PAIR], o_t[:, PAIR:2 * PAIR]], axis=0).T
        attn_s[tok, c + LANES:c + MXU_DIM] = jnp.concatenate(
            [o_t[:, 2 * PAIR:3 * PAIR], o_t[:, 3 * PAIR:]], axis=0).T

    pu = jnp.concatenate(filled[:n_c], axis=1)
    pbuf[HIST_ROWS:, :] = pu
    pp_ref[...] = pu[TM - HIST_ROWS:, :]
    sums = _window_sums(pbuf[...])
    pos1 = (t * TM + 1 + lax.broadcasted_iota(jnp.int32, (TM, 1), 0)).astype(F32)
    mixed = []
    for gi, w in enumerate(POOL_WINDOWS):
        inv_cnt = 1.0 / jnp.minimum(float(w), pos1)
        mixed.append(sums[gi][HIST_ROWS:, :] * inv_cnt - pu[:, gi * POOL_GROUP:(gi + 1) * POOL_GROUP])

    blocks = [(p, g) for p in range(TM // PAIR) for g in range(N_KV_HEADS)]
    s_next = scores(*blocks[0])
    for i, blk in enumerate(blocks):
        s_cur = s_next
        if i + 1 < len(blocks):
            s_next = scores(*blocks[i + 1])
        fill(PREFILL + (len(fill_cols) - PREFILL) * (i + 1) // len(blocks))
        finish(*blk, s_cur)
    pg, ag, ma, mp = (jnp.concatenate(filled[j * n_c:(j + 1) * n_c], axis=1) for j in range(1, 5))

    merged = _merge_branches(attn_s[...], mixed, (ag, pg, ma, mp), pw_ref, ps_ref, wabr_ref, wpbr_ref)

    h_s[...] = _rmsnorm_rows(xn_ref[...], ng_ref[...]).astype(BF16)
    zq_s[...] = _dot_nt(wt_s[0:ATTN_WIDTH, :], h_s[...])

    y_ref[...] = x_ref[...] + _dot(merged, wout_ref[...])

    k_s[:, 0:WINDOW, :] = k_s[:, TM:TM + WINDOW, :]
    vt_s[:, 0:WINDOW] = vt_s[:, TM:TM + WINDOW]
    pbuf[0:HIST_ROWS, :] = pbuf[TM:TM + HIST_ROWS, :]

    @pl.when(first)
    def _():
        for i in range(len(weights)):
            export_copy(i).wait()


def _resident(shape):
    nd = len(shape)
    return pl.BlockSpec(shape, lambda b, t: (0,) * nd, pipeline_mode=pl.Buffered(1))


def _prompt_layer(x, ng, qgt, kg, ones, bias_p, sink_rows, ps, weights):
    B, S, _ = x.shape
    assert S % TM == 0 and TM % PAIR == 0 and TM >= WINDOW
    assert all(w.shape[0] % STAGE_ROWS == 0 for w in weights)
    consts = (ng, qgt, kg, ones, bias_p, sink_rows, ps)
    hbm = pl.BlockSpec(memory_space=pl.ANY)
    n_t = S // TM

    def next_tile(b, t):
        i = jnp.minimum(b * n_t + t + 1, B * n_t - 1)
        return i // n_t, i % n_t, 0

    return pl.pallas_call(
        _prompt_kernel,
        grid=(B, S // TM),
        in_specs=[pl.BlockSpec((None, TM, D_MODEL), lambda b, t: (b, t, 0)),
                  pl.BlockSpec((None, TM, D_MODEL), next_tile)]
                 + [_resident(c.shape) for c in consts] + [hbm] * len(weights),
        out_specs=[
            pl.BlockSpec((None, TM, D_MODEL), lambda b, t: (b, t, 0)),
            pl.BlockSpec((None, KV_WIDTH, WINDOW), lambda b, t: (b, 0, 0)),
            pl.BlockSpec((None, KV_WIDTH, WINDOW), lambda b, t: (b, 0, 0)),
            pl.BlockSpec((None, HIST_ROWS, POOL_WIDTH), lambda b, t: (b, 0, 0)),
        ] + [hbm] * len(weights),
        out_shape=[
            jax.ShapeDtypeStruct((B, S, D_MODEL), F32),
            jax.ShapeDtypeStruct((B, KV_WIDTH, WINDOW), F32),
            jax.ShapeDtypeStruct((B, KV_WIDTH, WINDOW), F32),
            jax.ShapeDtypeStruct((B, HIST_ROWS, POOL_WIDTH), F32),
        ] + [jax.ShapeDtypeStruct(w.shape, BF16) for w in weights],
        scratch_shapes=[pltpu.VMEM(w.shape, BF16) for w in weights] + [
            pltpu.VMEM((N_STAGE, STAGE_ROWS, STAGE_COLS), F32),
            pltpu.SemaphoreType.DMA((N_STAGE,)),
            pltpu.SemaphoreType.DMA((len(weights),)),
            pltpu.VMEM((ATTN_WIDTH + KV_WIDTH, D_MODEL), BF16),
            pltpu.VMEM((TM, D_MODEL), BF16),
            pltpu.VMEM((ATTN_WIDTH, TM), BF16),
            pltpu.VMEM((2, WINDOW + TM, LANES), BF16),
            pltpu.VMEM((KV_WIDTH, WINDOW + TM), BF16),
            pltpu.VMEM((TM, ATTN_WIDTH), F32),
            pltpu.VMEM((HIST_ROWS + TM, POOL_WIDTH), F32),
        ],
        compiler_params=pltpu.CompilerParams(
            dimension_semantics=("arbitrary", "arbitrary"),
            vmem_limit_bytes=VMEM_LIMIT,
        ),
        name="prompt_layer",
    )(x, x, *consts, *weights)


def _sample_kernel(n_b, n_s, start,
                   sinks_ref, x_ref, skt_ref, svt_ref, sp_ref, ng_ref, win_ref,
                   qg_ref, kgt_ref, ones_ref, ts_ref, tn_ref, pw_ref, ps_ref, wabr_ref, wpbr_ref, wout_ref,
                   y_ref, ok_ref, ov_ref, op_ref):
    n_tok = n_b * n_s
    x = x_ref[...]
    hb = _rmsnorm_rows(x, ng_ref[...]).astype(BF16)

    q = _head_rmsnorm(_dot(hb, win_ref[:, C_Q]), ones_ref, qg_ref[...]).astype(BF16)
    q3 = q.reshape(n_b, n_s, ATTN_WIDTH)
    zkt = _dot_nt(win_ref[:, C_K].T, hb)
    knt = jnp.concatenate([_head_rmsnorm_fm(zkt, kgt_ref, g) for g in range(N_KV_HEADS)], axis=0)
    vnt = _dot_nt(win_ref[:, C_V].T, hb)

    lane = lax.broadcasted_iota(jnp.int32, (1, WINDOW), 1)
    per_tile = LANES // n_s
    for s in range(n_b):
        tile = slice((s // per_tile) * LANES, (s // per_tile + 1) * LANES)
        shift = (WINDOW - n_s - (s % per_tile) * n_s) % LANES
        for new, st_ref, o_ref in ((knt, skt_ref, ok_ref), (vnt, svt_ref, ov_ref)):
            fresh = new[:, tile] if shift == 0 else pltpu.roll(new[:, tile], shift, axis=1)
            o_ref[s] = jnp.where(lane < WINDOW - n_s, pltpu.roll(st_ref[s], WINDOW - n_s, axis=1), fresh)

    knb = knt.astype(BF16)
    vnb = vnt.astype(BF16)
    key_stream = (lax.broadcasted_iota(jnp.int32, (n_b, 1, 2 * n_tok), 2) & (n_tok - 1)) >> (n_s.bit_length() - 1)
    own = key_stream == lax.broadcasted_iota(jnp.int32, (n_b, 1, 2 * n_tok), 0)
    top = lax.broadcasted_iota(jnp.int32, (1, 2 * n_s, 1), 1) < n_s
    low = lax.broadcasted_iota(jnp.int32, (1, 1, LANES), 2) < HEAD_DIM

    def scores(g):
        hd = slice(g * HEAD_DIM, (g + 1) * HEAD_DIM)
        c = g * MXU_DIM
        lhs = jnp.concatenate([q3[:, :, c:c + LANES], q3[:, :, c + LANES:c + MXU_DIM]], axis=1)
        kst = _diag2(skt_ref[:, hd, :].astype(BF16), 1, 2)
        s_st = jnp.stack([_dot(lhs[s], kst[s]) for s in range(n_b)]) + ts_ref[g][None]
        s_nw = _dot(lhs.reshape(2 * n_tok, LANES), _diag2(knb[hd, :], 0, 1))
        s_nw = jnp.where(own, s_nw.reshape(n_b, 2 * n_s, 2 * n_tok) + tn_ref[g][None], NEG_INF)
        return s_st, s_nw

    def finish(g, s_st, s_nw):
        hd = slice(g * HEAD_DIM, (g + 1) * HEAD_DIM)
        p_st, p_nw, inv = [], [], []
        for j in range(2):
            a = s_st[:, :, j * WINDOW:(j + 1) * WINDOW]
            b = s_nw[:, :, j * n_tok:(j + 1) * n_tok]
            sink = jnp.where(top, sinks_ref[4 * g + j], sinks_ref[4 * g + 2 + j])
            m = jnp.maximum(jnp.maximum(jnp.max(a, axis=-1, keepdims=True),
                                        jnp.max(b, axis=-1, keepdims=True)), sink)
            ea = jnp.exp(a - m)
            eb = jnp.exp(b - m)
            inv.append(1.0 / (jnp.sum(ea, axis=-1, keepdims=True) + jnp.sum(eb, axis=-1, keepdims=True)
                              + jnp.exp(sink - m)))
            p_st.append(ea.astype(BF16))
            p_nw.append(eb.astype(BF16))
        p_st = jnp.concatenate(p_st, axis=2)
        p_nw = jnp.concatenate(p_nw, axis=2).reshape(2 * n_tok, 2 * n_tok)
        vst = _diag2(svt_ref[:, hd, :].astype(BF16), 1, 2)
        o = jnp.stack([_dot_nt(p_st[s], vst[s]) for s in range(n_b)])
        o = o + _dot_nt(p_nw, _diag2(vnb[hd, :], 0, 1)).reshape(n_b, 2 * n_s, LANES)
        o = o * jnp.where(low, inv[0], inv[1])
        return jnp.concatenate([o[:, :n_s, :], o[:, n_s:, :]], axis=2)

    outs = []
    nxt = scores(0)
    for g in range(N_KV_HEADS):
        cur = nxt
        if g + 1 < N_KV_HEADS:
            nxt = scores(g + 1)
        outs.append(finish(g, *cur))
    attn_o = jnp.concatenate(outs, axis=2).reshape(n_tok, ATTN_WIDTH)

    pu = _dot(hb, win_ref[:, C_PU])
    pu3 = pu.reshape(n_b, n_s, POOL_WIDTH)
    full3 = jnp.concatenate([sp_ref[...], pu3], axis=1)
    op_ref[...] = full3[:, n_s:, :]
    seg = HIST_ROWS + n_s
    sums = _window_sums(full3.reshape(n_b * seg, POOL_WIDTH))
    pos1 = (start + 1 + lax.broadcasted_iota(jnp.int32, (n_s, 1), 0)).astype(F32)
    mixed = []
    for gi, w in enumerate(POOL_WINDOWS):
        inv_cnt = 1.0 / jnp.minimum(float(w), pos1)
        sw = sums[gi].reshape(n_b, seg, POOL_GROUP)[:, HIST_ROWS:, :] * inv_cnt
        mixed.append(sw.reshape(n_tok, POOL_GROUP) - pu[:, gi * POOL_GROUP:(gi + 1) * POOL_GROUP])

    gates = tuple(_dot(hb, win_ref[:, c]) for c in (C_AG, C_PG, C_MA, C_MP))
    merged = _merge_branches(attn_o, mixed, gates, pw_ref, ps_ref, wabr_ref, wpbr_ref)
    y_ref[...] = x + _dot(merged, wout_ref[...])


def _sample_layer(x, skt, svt, sp, start, sinks, ng, w_in, qg, kgt, ones, ts, tn,
                  pw, ps, wabr, wpbr, wout):
    n_b, n_s, _ = x.shape
    n_tok = n_b * n_s
    assert n_s % 16 == 0 and LANES % n_s == 0 and n_s >= POOL_HIST and n_tok % LANES == 0
    assert n_s & (n_s - 1) == 0 and n_tok & (n_tok - 1) == 0
    vmem = pl.BlockSpec(memory_space=pltpu.VMEM)
    y, ok, ov, op = pl.pallas_call(
        functools.partial(_sample_kernel, n_b, n_s, start),
        in_specs=[pl.BlockSpec(memory_space=pltpu.SMEM)] + [vmem] * 16,
        out_specs=[vmem] * 4,
        out_shape=[
            jax.ShapeDtypeStruct((n_tok, D_MODEL), F32),
            jax.ShapeDtypeStruct((n_b, KV_WIDTH, WINDOW), F32),
            jax.ShapeDtypeStruct((n_b, KV_WIDTH, WINDOW), F32),
            jax.ShapeDtypeStruct((n_b, HIST_ROWS, POOL_WIDTH), F32),
        ],
        compiler_params=pltpu.CompilerParams(vmem_limit_bytes=VMEM_LIMIT),
        name="sample_layer",
    )(sinks, x.reshape(n_tok, D_MODEL), skt, svt, sp, ng, w_in, qg, kgt, ones, ts, tn,
      pw, ps, wabr, wpbr, wout)
    return y.reshape(n_b, n_s, D_MODEL), ok, ov, op


def _kv_to_feature_major(a):
    return jnp.transpose(a, (0, 2, 3, 1)).reshape(a.shape[0], KV_WIDTH, WINDOW)


def _kv_from_feature_major(a):
    return jnp.transpose(a.reshape(a.shape[0], N_KV_HEADS, HEAD_DIM, WINDOW), (0, 3, 1, 2))


def kernel(x_prompt, x_sample, state_attn_k, state_attn_v, state_pool, norm_gain, w_in, q_norm_gain, k_norm_gain, attn_sinks, rel_bias, pool_w, pool_scale, w_attn_br, w_pool_br, w_out):
    depth = w_in.shape[0]
    n_b, n_s, _ = x_sample.shape
    past_len = 4096
    bias_p, tab_state, tab_new = _bias_tables(rel_bias, n_b, n_s)
    seg = jnp.arange(MXU_DIM) // HEAD_DIM
    ones = (seg[:, None] == seg[None, :]).astype(BF16)

    xp, xs = x_prompt, x_sample
    pk, pv, pp, sk, sv, sp = [], [], [], [], [], []
    for l in range(depth):
        ng = norm_gain[l].reshape(1, D_MODEL)
        qg = jnp.tile(q_norm_gain[l] * (HEAD_DIM ** -0.5), N_HEADS)
        kg = jnp.tile(k_norm_gain[l], N_KV_HEADS)
        ps = pool_scale[l].reshape(1, POOL_WIDTH)
        sink_rows = jnp.repeat(attn_sinks[l].reshape(N_KV_HEADS, GROUP_REP), PAIR, axis=1)
        xp, k_l, v_l, p_l, w_in_b, wabr_b, wpbr_b, wout_b, pw_b = _prompt_layer(
            xp, ng, qg.reshape(ATTN_WIDTH, 1), kg.reshape(1, KV_WIDTH), ones, bias_p,
            sink_rows.reshape(N_KV_HEADS, 1, GROUP_REP * PAIR), ps,
            (w_in[l], w_attn_br[l], w_pool_br[l], w_out[l], pool_w[l].reshape(POOL_WIDTH, POOL_GROUP)))
        tail = (pw_b, ps, wabr_b, wpbr_b, wout_b)
        pk.append(_kv_from_feature_major(k_l))
        pv.append(_kv_from_feature_major(v_l))
        pp.append(p_l[:, 1:, :])

        xs, k_l, v_l, p_l = _sample_layer(
            xs,
            _kv_to_feature_major(state_attn_k[l]),
            _kv_to_feature_major(state_attn_v[l]),
            jnp.pad(state_pool[l], ((0, 0), (1, 0), (0, 0))),
            past_len, attn_sinks[l], ng, w_in_b,
            qg.reshape(1, ATTN_WIDTH), kg.reshape(KV_WIDTH, 1), ones, tab_state, tab_new, *tail)
        sk.append(_kv_from_feature_major(k_l))
        sv.append(_kv_from_feature_major(v_l))
        sp.append(p_l[:, 1:, :])
    return (xp, xs, jnp.stack(pk), jnp.stack(pv), jnp.stack(pp),
            jnp.stack(sk), jnp.stack(sv), jnp.stack(sp))
```

```python
import functools
import numpy as np

import jax
import jax.numpy as jnp
from jax import lax
from jax.experimental import pallas as pl
from jax.experimental.pallas import tpu as pltpu

D_MODEL = 1024
CHUNK = 64
WINDOW = 128
N_HEADS = 16
N_KV_HEADS = 4
GROUP_REP = N_HEADS // N_KV_HEADS
HEAD_DIM = 64
ATTN_WIDTH = N_HEADS * HEAD_DIM
KV_WIDTH = N_KV_HEADS * HEAD_DIM
POOL_WINDOWS = (2, 4, 8, 16)
POOL_WIDTH = D_MODEL
POOL_GROUP = POOL_WIDTH // len(POOL_WINDOWS)
POOL_HIST = max(POOL_WINDOWS) - 1
HIST_ROWS = POOL_HIST + 1
N_BUCKETS = 32
MAX_DISTANCE = 128
EPS = 1e-6
NEG_INF = -1e30

_SPLITS = (ATTN_WIDTH, KV_WIDTH, KV_WIDTH, ATTN_WIDTH, POOL_WIDTH, POOL_WIDTH, D_MODEL, D_MODEL)
_OFFS = tuple(sum(_SPLITS[:i]) for i in range(len(_SPLITS) + 1))
C_Q, C_K, C_V, C_AG, C_PU, C_PG, C_MA, C_MP = (slice(_OFFS[i], _OFFS[i + 1]) for i in range(8))

LANES = 128
MXU_DIM = 256
PAIR = 2 * CHUNK
PAIR_KEYS = PAIR + WINDOW
TM = 512
PREFILL_K, PREFILL = 2, 8
STAGE_ROWS, STAGE_COLS, N_STAGE = ATTN_WIDTH, TM, 3
VMEM_LIMIT = 60 * 1024 * 1024

BF16 = jnp.bfloat16
F32 = jnp.float32


def _dot(a, b):
    return jnp.dot(a, b, preferred_element_type=F32)


def _dot_nt(a, b):
    return lax.dot_general(a, b, (((1,), (1,)), ((), ())), preferred_element_type=F32)


def _rmsnorm_rows(x, gain):
    y = x * lax.rsqrt(jnp.mean(x * x, axis=-1, keepdims=True) + EPS)
    return y * gain


def _head_rmsnorm(z, ones_ref, gain):
    sq = z * z
    hi = sq.astype(BF16)
    lo = (sq - hi.astype(F32)).astype(BF16)
    ones = ones_ref[...]
    parts = []
    for c in range(z.shape[1] // MXU_DIM):
        sl = slice(c * MXU_DIM, (c + 1) * MXU_DIM)
        parts.append(_dot(hi[:, sl], ones) + _dot(lo[:, sl], ones))
    ss = parts[0] if len(parts) == 1 else jnp.concatenate(parts, axis=1)
    return (z * lax.rsqrt(ss * (1.0 / HEAD_DIM) + EPS)) * gain


def _head_rmsnorm_fm(zt, gain_col, h):
    zh = zt[h * HEAD_DIM:(h + 1) * HEAD_DIM, :]
    ms = jnp.sum(zh * zh, axis=0, keepdims=True) * (1.0 / HEAD_DIM)
    return (zh * lax.rsqrt(ms + EPS)) * gain_col


def _window_sums(full, delay):
    outs = []
    for gi, w in enumerate(POOL_WINDOWS):
        s = full[..., gi * POOL_GROUP:(gi + 1) * POOL_GROUP]
        sh = 1
        while sh < w:
            s = s + delay(s, sh)
            sh *= 2
        outs.append(s)
    return outs


def _delay_rows(a, n):
    return pltpu.roll(a, n, axis=0)


def _delay_leading(a, n):
    return jnp.concatenate([a[-n:], a[:-n]], axis=0)


def _merge_branches(attn_o, pool_mixed, gates, pw_ref, ps_ref, wabr_ref, wpbr_ref, pool_rows=None):
    ag, pg, ma, mp = gates
    po = []
    for gi in range(len(POOL_WINDOWS)):
        po.append(_dot(pool_mixed[gi].astype(BF16), pw_ref[gi * POOL_GROUP:(gi + 1) * POOL_GROUP, :]))
    po = jnp.concatenate(po, axis=1)
    if pool_rows is not None:
        po = pool_rows(po)
    po = po * ps_ref[...]
    p = _dot((po * (pg * jax.nn.sigmoid(pg))).astype(BF16), wpbr_ref[...])
    a = _dot((attn_o * (ag * jax.nn.sigmoid(ag))).astype(BF16), wabr_ref[...])
    acc = jax.nn.sigmoid(ma) * a + jax.nn.sigmoid(mp) * p
    return acc.astype(BF16)


def _diag2(a, axis_r, axis_c):
    z = jnp.zeros_like(a)
    return jnp.concatenate([jnp.concatenate([a, z], axis=axis_c),
                            jnp.concatenate([z, a], axis=axis_c)], axis=axis_r)


def _bias_kernel(n_s, tbl_ref, bkp_ref, bks_ref, bkn_ref, bp_ref, ts_ref, tn_ref):
    g = pl.program_id(0)

    def gather(bucket, h):
        acc = jnp.zeros(bucket.shape, F32)
        for b in range(N_BUCKETS):
            acc = jnp.where(bucket == b, tbl_ref[h, b], acc)
        return acc

    bkp = bkp_ref[...]
    for r in range(GROUP_REP):
        bp_ref[:, r * PAIR:(r + 1) * PAIR] = jnp.where(bkp < 0, NEG_INF, gather(bkp, g * GROUP_REP + r))
    bks = bks_ref[...]
    bkn = bkn_ref[...]
    for c in range(2):
        for j in range(2):
            h = g * GROUP_REP + 2 * c + j
            rows = slice(c * n_s, (c + 1) * n_s)
            ts_ref[rows, j * WINDOW:(j + 1) * WINDOW] = gather(bks, h)
            tn_ref[rows, j * bkn.shape[1]:(j + 1) * bkn.shape[1]] = gather(bkn, h)


def _t5_bucket(rel):
    assert (N_BUCKETS, MAX_DISTANCE) == (32, 128)
    nb = N_BUCKETS // 2
    max_exact = nb // 2
    n = np.abs(rel)
    log2_sq = np.vectorize(lambda v: max(int(v), 1).bit_length() - 1)(n * n)
    large = np.minimum(max_exact + log2_sq - 6, nb - 1)
    return (np.where(rel > 0, nb, 0) + np.where(n < max_exact, n, large)).astype(np.int32)


def _bias_tables(rel_bias, n_b, n_s):
    kk = np.arange(PAIR_KEYS)[:, None]
    qq = np.arange(PAIR)[None, :]
    c0 = (qq // CHUNK) * CHUNK
    bkp = np.where((kk >= c0) & (kk < c0 + CHUNK + WINDOW), _t5_bucket(kk - WINDOW - qq), -1).astype(np.int32)
    fr = np.arange(n_s)[:, None]
    bks = _t5_bucket(np.arange(WINDOW)[None, :] - WINDOW - fr)
    bkn = _t5_bucket(np.arange(n_b * n_s)[None, :] % n_s - fr)
    n_new = n_b * n_s
    whole = lambda g: (0, 0)
    per_g = lambda g: (g, 0, 0)
    return pl.pallas_call(
        functools.partial(_bias_kernel, n_s),
        grid=(N_KV_HEADS,),
        in_specs=[
            pl.BlockSpec(memory_space=pltpu.SMEM),
            pl.BlockSpec((PAIR_KEYS, PAIR), whole),
            pl.BlockSpec((n_s, WINDOW), whole),
            pl.BlockSpec((n_s, n_new), whole),
        ],
        out_specs=[
            pl.BlockSpec((None, PAIR_KEYS, GROUP_REP * PAIR), per_g),
            pl.BlockSpec((None, 2 * n_s, 2 * WINDOW), per_g),
            pl.BlockSpec((None, 2 * n_s, 2 * n_new), per_g),
        ],
        out_shape=[
            jax.ShapeDtypeStruct((N_KV_HEADS, PAIR_KEYS, GROUP_REP * PAIR), F32),
            jax.ShapeDtypeStruct((N_KV_HEADS, 2 * n_s, 2 * WINDOW), F32),
            jax.ShapeDtypeStruct((N_KV_HEADS, 2 * n_s, 2 * n_new), F32),
        ],
        name="bias_tables",
    )(rel_bias.T, bkp, bks, bkn)


def _weight_jobs(pairs):
    jobs = []
    for src, dst in pairs:
        n_rows, n_cols = dst.shape
        for r0 in range(0, n_rows, STAGE_ROWS):
            for c0 in range(0, n_cols, STAGE_COLS):
                jobs.append((src, dst, r0, c0, min(STAGE_COLS, n_cols - c0)))
    return jobs


def _prompt_kernel(sinks_ref, x_ref, xn_ref, ng_ref, gcol_ref, grow_ref, ones_ref, bias_ref, ps_ref,
                   win_hbm, wabr_hbm, wpbr_hbm, wout_hbm, pw_hbm,
                   y_ref, pk_ref, pv_ref, pp_ref, winb_hbm, wabrb_hbm, wpbrb_hbm, woutb_hbm, pwb_hbm,
                   win_ref, wabr_ref, wpbr_ref, wout_ref, pw_ref, stage, sem_in, sem_out,
                   wt_s, h_s, qt_s, k_s, vt_s, attn_s, pbuf, hist_s):
    b = pl.program_id(0)
    t = pl.program_id(1)
    first = (b == 0) & (t == 0)
    q_gain = gcol_ref[:, 0:1]
    k_gain = grow_ref[:, ATTN_WIDTH:]
    zq_s = stage.at[0]
    weights = ((win_hbm, win_ref), (wabr_hbm, wabr_ref), (wpbr_hbm, wpbr_ref), (wout_hbm, wout_ref),
               (pw_hbm, pw_ref))
    exports = (winb_hbm, wabrb_hbm, wpbrb_hbm, woutb_hbm, pwb_hbm)

    def export_copy(i):
        return pltpu.make_async_copy(weights[i][1], exports[i], sem_out.at[i])

    @pl.when(first)
    def _():
        jobs = _weight_jobs(weights)

        def load(i):
            src, _, r0, c0, w = jobs[i]
            return pltpu.make_async_copy(src.at[pl.ds(r0, STAGE_ROWS), pl.ds(c0, w)],
                                         stage.at[i % N_STAGE, :, pl.ds(0, w)], sem_in.at[i % N_STAGE])

        for i in range(min(N_STAGE - 1, len(jobs))):
            load(i).start()
        for i in range(len(jobs)):
            if i + N_STAGE - 1 < len(jobs):
                load(i + N_STAGE - 1).start()
            load(i).wait()
            _, dst, r0, c0, w = jobs[i]
            dst[r0:r0 + STAGE_ROWS, c0:c0 + w] = stage[i % N_STAGE, :, 0:w].astype(BF16)
        for i in range(len(weights)):
            export_copy(i).start()
        wt_s[0:ATTN_WIDTH, :] = win_ref[:, C_Q].T
        wt_s[ATTN_WIDTH:, :] = win_ref[:, C_V].T
        h_s[...] = _rmsnorm_rows(x_ref[...], ng_ref[...]).astype(BF16)
        zq_s[...] = _dot_nt(wt_s[0:ATTN_WIDTH, :], h_s[...])

    @pl.when(t == 0)
    def _():
        k_s[:, 0:WINDOW, :] = jnp.zeros((2, WINDOW, LANES), BF16)
        vt_s[:, 0:WINDOW] = jnp.zeros((KV_WIDTH, WINDOW), BF16)
        pbuf[0:HIST_ROWS, :] = jnp.zeros((HIST_ROWS, POOL_WIDTH), F32)

    hb = h_s[...]

    n_c = D_MODEL // MXU_DIM
    fill_cols = [slice(c.start + i * MXU_DIM, c.start + (i + 1) * MXU_DIM)
                 for c in (C_PU, C_PG, C_AG, C_MA, C_MP) for i in range(n_c)]
    filled = []

    def fill(upto):
        for cols in fill_cols[len(filled):upto]:
            filled.append(_dot(hb, win_ref[:, cols]))

    zqt = zq_s[...]
    zk = _dot(hb, win_ref[:, C_K])
    vt = _dot_nt(wt_s[ATTN_WIDTH:, :], hb)
    fill(PREFILL_K)
    k = _head_rmsnorm(zk, ones_ref, k_gain)
    fill(PREFILL)
    for h in range(N_HEADS):
        qt_s[h * HEAD_DIM:(h + 1) * HEAD_DIM, :] = _head_rmsnorm_fm(zqt, q_gain, h).astype(BF16)

    pk_ref[...] = k[TM - WINDOW:, :].T
    pv_ref[...] = vt[:, TM - WINDOW:]

    k_s[0, WINDOW:, :] = k[:, :LANES].astype(BF16)
    k_s[1, WINDOW:, :] = k[:, LANES:].astype(BF16)
    vt_s[:, WINDOW:] = vt.astype(BF16)

    zeros_half = jnp.zeros((HEAD_DIM, PAIR), BF16)
    q_lane = lax.broadcasted_iota(jnp.int32, (1, GROUP_REP * PAIR), 1)

    def scores(p, g):
        tok = slice(p * PAIR, (p + 1) * PAIR)
        keys = slice(p * PAIR, p * PAIR + PAIR_KEYS)
        cols = []
        for r in range(GROUP_REP):
            h = g * GROUP_REP + r
            qh = qt_s[h * HEAD_DIM:(h + 1) * HEAD_DIM, tok]
            cols.append(jnp.concatenate([qh, zeros_half] if g % 2 == 0 else [zeros_half, qh], axis=0))
        rhs = jnp.concatenate(cols, axis=1)
        s = _dot(k_s[g // 2, keys, :], rhs) + bias_ref[g]
        if p == 0:
            kk = lax.broadcasted_iota(jnp.int32, s.shape, 0)
            s = jnp.where((kk >= WINDOW) | (t > 0), s, NEG_INF)
        return s

    def finish(p, g, s):
        tok = slice(p * PAIR, (p + 1) * PAIR)
        keys = slice(p * PAIR, p * PAIR + PAIR_KEYS)
        sink = sinks_ref[g * GROUP_REP]
        for r in range(1, GROUP_REP):
            sink = jnp.where(q_lane < r * PAIR, sink, sinks_ref[g * GROUP_REP + r])
        m = jnp.maximum(jnp.max(s, axis=0, keepdims=True), sink)
        e = jnp.exp(s - m)
        denom = jnp.sum(e, axis=0, keepdims=True) + jnp.exp(sink - m)
        vg = vt_s[g * HEAD_DIM:(g + 1) * HEAD_DIM, keys]
        o_t = _dot(vg, e.astype(BF16)) * (1.0 / denom)
        c = g * MXU_DIM
        attn_s[tok, c:c + LANES] = jnp.concatenate(
            [o_t[:, 0:PAIR], o_t[:, PAIR:2 * PAIR]], axis=0).T
        attn_s[tok, c + LANES:c + MXU_DIM] = jnp.concatenate(
            [o_t[:, 2 * PAIR:3 * PAIR], o_t[:, 3 * PAIR:]], axis=0).T

    pu = jnp.concatenate(filled[:n_c], axis=1)
    pbuf[HIST_ROWS:, :] = pu
    hist_s[b] = pu[TM - HIST_ROWS:, :]
    sums = _window_sums(pbuf[...], _delay_rows)
    pos1 = (t * TM + 1 + lax.broadcasted_iota(jnp.int32, (TM, 1), 0)).astype(F32)
    mixed = []
    for gi, w in enumerate(POOL_WINDOWS):
        inv_cnt = 1.0 / jnp.minimum(float(w), pos1)
        mixed.append(sums[gi][HIST_ROWS:, :] * inv_cnt - pu[:, gi * POOL_GROUP:(gi + 1) * POOL_GROUP])

    blocks = [(p, g) for p in range(TM // PAIR) for g in range(N_KV_HEADS)]
    s_next = scores(*blocks[0])
    for i, blk in enumerate(blocks):
        s_cur = s_next
        if i + 1 < len(blocks):
            s_next = scores(*blocks[i + 1])
        fill(PREFILL + (len(fill_cols) - PREFILL) * (i + 1) // len(blocks))
        finish(*blk, s_cur)
    pg, ag, ma, mp = (jnp.concatenate(filled[j * n_c:(j + 1) * n_c], axis=1) for j in range(1, 5))

    merged = _merge_branches(attn_s[...], mixed, (ag, pg, ma, mp), pw_ref, ps_ref, wabr_ref, wpbr_ref)

    h_s[...] = _rmsnorm_rows(xn_ref[...], ng_ref[...]).astype(BF16)
    zq_s[...] = _dot_nt(wt_s[0:ATTN_WIDTH, :], h_s[...])

    y_ref[...] = x_ref[...] + _dot(merged, wout_ref[...])

    k_s[:, 0:WINDOW, :] = k_s[:, TM:TM + WINDOW, :]
    vt_s[:, 0:WINDOW] = vt_s[:, TM:TM + WINDOW]
    pbuf[0:HIST_ROWS, :] = pbuf[TM:TM + HIST_ROWS, :]

    @pl.when(first)
    def _():
        for i in range(len(weights)):
            export_copy(i).wait()

    @pl.when((b == pl.num_programs(0) - 1) & (t == pl.num_programs(1) - 1))
    def _():
        pp_ref[...] = pltpu.einshape("bic->ibc", hist_s[...])[HIST_ROWS - POOL_HIST:]


def _resident(shape):
    nd = len(shape)
    return pl.BlockSpec(shape, lambda b, t: (0,) * nd, pipeline_mode=pl.Buffered(1))


def _prompt_layer(x, sinks, ng, gain_cols, gain_rows, ones, bias_p, ps, weights):
    B, S, _ = x.shape
    assert S % TM == 0 and TM % PAIR == 0 and TM >= WINDOW
    assert all(w.shape[0] % STAGE_ROWS == 0 for w in weights)
    consts = (ng, gain_cols, gain_rows, ones, bias_p, ps)
    hbm = pl.BlockSpec(memory_space=pl.ANY)
    n_t = S // TM

    def next_tile(b, t):
        i = jnp.minimum(b * n_t + t + 1, B * n_t - 1)
        return i // n_t, i % n_t, 0

    return pl.pallas_call(
        _prompt_kernel,
        grid=(B, S // TM),
        in_specs=[pl.BlockSpec(memory_space=pltpu.SMEM),
                  pl.BlockSpec((None, TM, D_MODEL), lambda b, t: (b, t, 0)),
                  pl.BlockSpec((None, TM, D_MODEL), next_tile)]
                 + [_resident(c.shape) for c in consts] + [hbm] * len(weights),
        out_specs=[
            pl.BlockSpec((None, TM, D_MODEL), lambda b, t: (b, t, 0)),
            pl.BlockSpec((None, KV_WIDTH, WINDOW), lambda b, t: (b, 0, 0)),
            pl.BlockSpec((None, KV_WIDTH, WINDOW), lambda b, t: (b, 0, 0)),
            pl.BlockSpec((POOL_HIST, B, POOL_WIDTH), lambda b, t: (0, 0, 0)),
        ] + [hbm] * len(weights),
        out_shape=[
            jax.ShapeDtypeStruct((B, S, D_MODEL), F32),
            jax.ShapeDtypeStruct((B, KV_WIDTH, WINDOW), F32),
            jax.ShapeDtypeStruct((B, KV_WIDTH, WINDOW), F32),
            jax.ShapeDtypeStruct((POOL_HIST, B, POOL_WIDTH), F32),
        ] + [jax.ShapeDtypeStruct(w.shape, BF16) for w in weights],
        scratch_shapes=[pltpu.VMEM(w.shape, BF16) for w in weights] + [
            pltpu.VMEM((N_STAGE, STAGE_ROWS, STAGE_COLS), F32),
            pltpu.SemaphoreType.DMA((N_STAGE,)),
            pltpu.SemaphoreType.DMA((len(weights),)),
            pltpu.VMEM((ATTN_WIDTH + KV_WIDTH, D_MODEL), BF16),
            pltpu.VMEM((TM, D_MODEL), BF16),
            pltpu.VMEM((ATTN_WIDTH, TM), BF16),
            pltpu.VMEM((2, WINDOW + TM, LANES), BF16),
            pltpu.VMEM((KV_WIDTH, WINDOW + TM), BF16),
            pltpu.VMEM((TM, ATTN_WIDTH), F32),
            pltpu.VMEM((HIST_ROWS + TM, POOL_WIDTH), F32),
            pltpu.VMEM((B, HIST_ROWS, POOL_WIDTH), F32),
        ],
        compiler_params=pltpu.CompilerParams(
            dimension_semantics=("arbitrary", "arbitrary"),
            vmem_limit_bytes=VMEM_LIMIT,
        ),
        name="prompt_layer",
    )(sinks, x, x, *consts, *weights)


def _sample_kernel(n_b, n_s, start,
                   sinks_ref, x_ref, skt_ref, svt_ref, sp_ref, ng_ref, win_ref,
                   gcol_ref, grow_ref, ones_ref, ts_ref, tn_ref, pw_ref, ps_ref, wabr_ref, wpbr_ref, wout_ref,
                   y_ref, ok_ref, ov_ref, op_ref):
    n_tok = n_b * n_s
    x = x_ref[...]
    hb = _rmsnorm_rows(x, ng_ref[...]).astype(BF16)

    q = _head_rmsnorm(_dot(hb, win_ref[:, C_Q]), ones_ref, grow_ref[:, :ATTN_WIDTH]).astype(BF16)
    q3 = q.reshape(n_b, n_s, ATTN_WIDTH)
    zkt = _dot_nt(win_ref[:, C_K].T, hb)
    k_gain = gcol_ref[:, 1:2]
    knt = jnp.concatenate([_head_rmsnorm_fm(zkt, k_gain, g) for g in range(N_KV_HEADS)], axis=0)
    vnt = _dot_nt(win_ref[:, C_V].T, hb)

    lane = lax.broadcasted_iota(jnp.int32, (1, WINDOW), 1)
    per_tile = LANES // n_s
    for s in range(n_b):
        tile = slice((s // per_tile) * LANES, (s // per_tile + 1) * LANES)
        shift = (WINDOW - n_s - (s % per_tile) * n_s) % LANES
        for new, st_ref, o_ref in ((knt, skt_ref, ok_ref), (vnt, svt_ref, ov_ref)):
            fresh = new[:, tile] if shift == 0 else pltpu.roll(new[:, tile], shift, axis=1)
            o_ref[s] = jnp.where(lane < WINDOW - n_s, pltpu.roll(st_ref[s], WINDOW - n_s, axis=1), fresh)

    knb = knt.astype(BF16)
    vnb = vnt.astype(BF16)
    key_stream = (lax.broadcasted_iota(jnp.int32, (n_b, 1, 2 * n_tok), 2) & (n_tok - 1)) >> (n_s.bit_length() - 1)
    own = key_stream == lax.broadcasted_iota(jnp.int32, (n_b, 1, 2 * n_tok), 0)
    top = lax.broadcasted_iota(jnp.int32, (1, 2 * n_s, 1), 1) < n_s
    low = lax.broadcasted_iota(jnp.int32, (1, 1, LANES), 2) < HEAD_DIM

    def scores(g):
        hd = slice(g * HEAD_DIM, (g + 1) * HEAD_DIM)
        c = g * MXU_DIM
        lhs = jnp.concatenate([q3[:, :, c:c + LANES], q3[:, :, c + LANES:c + MXU_DIM]], axis=1)
        kst = _diag2(skt_ref[:, hd, :].astype(BF16), 1, 2)
        s_st = jnp.stack([_dot(lhs[s], kst[s]) for s in range(n_b)]) + ts_ref[g][None]
        s_nw = _dot(lhs.reshape(2 * n_tok, LANES), _diag2(knb[hd, :], 0, 1))
        s_nw = jnp.where(own, s_nw.reshape(n_b, 2 * n_s, 2 * n_tok) + tn_ref[g][None], NEG_INF)
        return s_st, s_nw

    def finish(g, s_st, s_nw):
        hd = slice(g * HEAD_DIM, (g + 1) * HEAD_DIM)
        p_st, p_nw, inv = [], [], []
        for j in range(2):
            a = s_st[:, :, j * WINDOW:(j + 1) * WINDOW]
            b = s_nw[:, :, j * n_tok:(j + 1) * n_tok]
            sink = jnp.where(top, sinks_ref[4 * g + j], sinks_ref[4 * g + 2 + j])
            m = jnp.maximum(jnp.maximum(jnp.max(a, axis=-1, keepdims=True),
                                        jnp.max(b, axis=-1, keepdims=True)), sink)
            ea = jnp.exp(a - m)
            eb = jnp.exp(b - m)
            inv.append(1.0 / (jnp.sum(ea, axis=-1, keepdims=True) + jnp.sum(eb, axis=-1, keepdims=True)
                              + jnp.exp(sink - m)))
            p_st.append(ea.astype(BF16))
            p_nw.append(eb.astype(BF16))
        p_st = jnp.concatenate(p_st, axis=2)
        p_nw = jnp.concatenate(p_nw, axis=2).reshape(2 * n_tok, 2 * n_tok)
        vst = _diag2(svt_ref[:, hd, :].astype(BF16), 1, 2)
        o = jnp.stack([_dot_nt(p_st[s], vst[s]) for s in range(n_b)])
        o = o + _dot_nt(p_nw, _diag2(vnb[hd, :], 0, 1)).reshape(n_b, 2 * n_s, LANES)
        o = o * jnp.where(low, inv[0], inv[1])
        return jnp.concatenate([o[:, :n_s, :], o[:, n_s:, :]], axis=2)

    outs = []
    nxt = scores(0)
    for g in range(N_KV_HEADS):
        cur = nxt
        if g + 1 < N_KV_HEADS:
            nxt = scores(g + 1)
        outs.append(finish(g, *cur))
    attn_o = jnp.concatenate(outs, axis=2).reshape(n_tok, ATTN_WIDTH)

    pu = _dot(hb, win_ref[:, C_PU])
    pu_t = pltpu.einshape("sic->isc", pu.reshape(n_b, n_s, POOL_WIDTH))
    full = jnp.concatenate([jnp.zeros((1, n_b, POOL_WIDTH), F32), sp_ref[...], pu_t], axis=0)
    op_ref[...] = full[HIST_ROWS + n_s - POOL_HIST:]
    sums = _window_sums(full, _delay_leading)
    pos1 = (start + 1 + lax.broadcasted_iota(jnp.int32, (n_s, 1, 1), 0)).astype(F32)
    mixed = []
    for gi, w in enumerate(POOL_WINDOWS):
        inv_cnt = 1.0 / jnp.minimum(float(w), pos1)
        sw = sums[gi][HIST_ROWS:] * inv_cnt - pu_t[..., gi * POOL_GROUP:(gi + 1) * POOL_GROUP]
        mixed.append(sw.reshape(n_tok, POOL_GROUP))

    def stream_major(po):
        return pltpu.einshape("isc->sic", po.reshape(n_s, n_b, POOL_WIDTH)).reshape(n_tok, POOL_WIDTH)

    gates = tuple(_dot(hb, win_ref[:, c]) for c in (C_AG, C_PG, C_MA, C_MP))
    merged = _merge_branches(attn_o, mixed, gates, pw_ref, ps_ref, wabr_ref, wpbr_ref, stream_major)
    y_ref[...] = x + _dot(merged, wout_ref[...])


def _sample_layer(x, skt, svt, sp, start, sinks, ng, w_in, gain_cols, gain_rows, ones, ts, tn,
                  pw, ps, wabr, wpbr, wout):
    n_b, n_s, _ = x.shape
    n_tok = n_b * n_s
    assert n_s % 16 == 0 and LANES % n_s == 0 and n_s >= POOL_HIST and n_tok % LANES == 0
    assert n_s & (n_s - 1) == 0 and n_tok & (n_tok - 1) == 0
    vmem = pl.BlockSpec(memory_space=pltpu.VMEM)
    y, ok, ov, op = pl.pallas_call(
        functools.partial(_sample_kernel, n_b, n_s, start),
        in_specs=[pl.BlockSpec(memory_space=pltpu.SMEM)] + [vmem] * 16,
        out_specs=[vmem] * 4,
        out_shape=[
            jax.ShapeDtypeStruct((n_tok, D_MODEL), F32),
            jax.ShapeDtypeStruct((n_b, KV_WIDTH, WINDOW), F32),
            jax.ShapeDtypeStruct((n_b, KV_WIDTH, WINDOW), F32),
            jax.ShapeDtypeStruct((POOL_HIST, n_b, POOL_WIDTH), F32),
        ],
        compiler_params=pltpu.CompilerParams(vmem_limit_bytes=VMEM_LIMIT),
        name="sample_layer",
    )(sinks, x.reshape(n_tok, D_MODEL), skt, svt, sp, ng, w_in, gain_cols, gain_rows, ones, ts, tn,
      pw, ps, wabr, wpbr, wout)
    return y.reshape(n_b, n_s, D_MODEL), ok, ov, op


def _kv_to_feature_major(a):
    return jnp.transpose(a, (0, 2, 3, 1)).reshape(a.shape[0], KV_WIDTH, WINDOW)


def _kv_from_feature_major(a):
    return jnp.transpose(a.reshape(a.shape[0], N_KV_HEADS, HEAD_DIM, WINDOW), (0, 3, 1, 2))


def kernel(x_prompt, x_sample, state_attn_k, state_attn_v, state_pool, norm_gain, w_in, q_norm_gain, k_norm_gain, attn_sinks, rel_bias, pool_w, pool_scale, w_attn_br, w_pool_br, w_out):
    depth = w_in.shape[0]
    n_b, n_s, _ = x_sample.shape
    past_len = 4096
    bias_p, tab_state, tab_new = _bias_tables(rel_bias, n_b, n_s)
    seg = np.arange(MXU_DIM) // HEAD_DIM
    ones = jnp.asarray(seg[:, None] == seg[None, :], BF16)

    xp, xs = x_prompt, x_sample
    pk, pv, pp, sk, sv, sp = [], [], [], [], [], []
    for l in range(depth):
        ng = norm_gain[l].reshape(1, D_MODEL)
        qg = q_norm_gain[l] * (HEAD_DIM ** -0.5)
        kg = k_norm_gain[l]
        gain_cols = jnp.stack([qg, kg], axis=1)
        gain_rows = jnp.concatenate([jnp.tile(qg, N_HEADS), jnp.tile(kg, N_KV_HEADS)]).reshape(1, -1)
        ps = pool_scale[l].reshape(1, POOL_WIDTH)
        xp, k_l, v_l, p_l, w_in_b, wabr_b, wpbr_b, wout_b, pw_b = _prompt_layer(
            xp, attn_sinks[l], ng, gain_cols, gain_rows, ones, bias_p, ps,
            (w_in[l], w_attn_br[l], w_pool_br[l], w_out[l], pool_w[l].reshape(POOL_WIDTH, POOL_GROUP)))
        tail = (pw_b, ps, wabr_b, wpbr_b, wout_b)
        pk.append(_kv_from_feature_major(k_l))
        pv.append(_kv_from_feature_major(v_l))
        pp.append(jnp.transpose(p_l, (1, 0, 2)))

        xs, k_l, v_l, p_l = _sample_layer(
            xs,
            _kv_to_feature_major(state_attn_k[l]),
            _kv_to_feature_major(state_attn_v[l]),
            jnp.transpose(state_pool[l], (1, 0, 2)),
            past_len, attn_sinks[l], ng, w_in_b, gain_cols, gain_rows, ones, tab_state, tab_new, *tail)
        sk.append(_kv_from_feature_major(k_l))
        sv.append(_kv_from_feature_major(v_l))
        sp.append(jnp.transpose(p_l, (1, 0, 2)))
    return (xp, xs, jnp.stack(pk), jnp.stack(pv), jnp.stack(pp),
            jnp.stack(sk), jnp.stack(sv), jnp.stack(sp))
```

```python
import functools
import numpy as np

import jax
import jax.numpy as jnp
from jax import lax
from jax.experimental import pallas as pl
from jax.experimental.pallas import tpu as pltpu

D_MODEL = 1024
CHUNK = 64
WINDOW = 128
N_HEADS = 16
N_KV_HEADS = 4
GROUP_REP = N_HEADS // N_KV_HEADS
HEAD_DIM = 64
ATTN_WIDTH = N_HEADS * HEAD_DIM
KV_WIDTH = N_KV_HEADS * HEAD_DIM
POOL_WINDOWS = (2, 4, 8, 16)
POOL_WIDTH = D_MODEL
POOL_GROUP = POOL_WIDTH // len(POOL_WINDOWS)
POOL_HIST = max(POOL_WINDOWS) - 1
HIST_ROWS = POOL_HIST + 1
N_BUCKETS = 32
MAX_DISTANCE = 128
EPS = 1e-6
NEG_INF = -1e30

_SPLITS = (ATTN_WIDTH, KV_WIDTH, KV_WIDTH, ATTN_WIDTH, POOL_WIDTH, POOL_WIDTH, D_MODEL, D_MODEL)
_OFFS = tuple(sum(_SPLITS[:i]) for i in range(len(_SPLITS) + 1))
C_Q, C_K, C_V, C_AG, C_PU, C_PG, C_MA, C_MP = (slice(_OFFS[i], _OFFS[i + 1]) for i in range(8))

LANES = 128
MXU_DIM = 256
PAIR = 2 * CHUNK
PAIR_KEYS = PAIR + WINDOW
TM = 512
PREFILL_K, PREFILL = 2, 8
STAGE_ROWS, STAGE_COLS, N_STAGE = ATTN_WIDTH, TM, 3
VMEM_LIMIT = 60 * 1024 * 1024

BF16 = jnp.bfloat16
F32 = jnp.float32


def _dot(a, b):
    return jnp.dot(a, b, preferred_element_type=F32)


def _dot_nt(a, b):
    return lax.dot_general(a, b, (((1,), (1,)), ((), ())), preferred_element_type=F32)


def _rmsnorm_rows(x, gain):
    y = x * lax.rsqrt(jnp.mean(x * x, axis=-1, keepdims=True) + EPS)
    return y * gain


def _head_rmsnorm(z, ones_ref, gain):
    sq = z * z
    hi = sq.astype(BF16)
    lo = (sq - hi.astype(F32)).astype(BF16)
    ones = ones_ref[...]
    parts = []
    for c in range(z.shape[1] // MXU_DIM):
        sl = slice(c * MXU_DIM, (c + 1) * MXU_DIM)
        parts.append(_dot(hi[:, sl], ones) + _dot(lo[:, sl], ones))
    ss = parts[0] if len(parts) == 1 else jnp.concatenate(parts, axis=1)
    return (z * lax.rsqrt(ss * (1.0 / HEAD_DIM) + EPS)) * gain


def _head_rmsnorm_fm(zt, gain_col, h):
    zh = zt[h * HEAD_DIM:(h + 1) * HEAD_DIM, :]
    ms = jnp.sum(zh * zh, axis=0, keepdims=True) * (1.0 / HEAD_DIM)
    return (zh * lax.rsqrt(ms + EPS)) * gain_col


def _window_sums(full, delay):
    outs = []
    for gi, w in enumerate(POOL_WINDOWS):
        s = full[..., gi * POOL_GROUP:(gi + 1) * POOL_GROUP]
        sh = 1
        while sh < w:
            s = s + delay(s, sh)
            sh *= 2
        outs.append(s)
    return outs


def _delay_rows(a, n):
    return pltpu.roll(a, n, axis=0)


def _delay_leading(a, n):
    return jnp.concatenate([a[-n:], a[:-n]], axis=0)


def _merge_branches(attn_o, pool_mixed, gates, pw_ref, ps_ref, wabr_ref, wpbr_ref, pool_rows=None):
    ag, pg, ma, mp = gates
    po = []
    for gi in range(len(POOL_WINDOWS)):
        po.append(_dot(pool_mixed[gi].astype(BF16), pw_ref[gi * POOL_GROUP:(gi + 1) * POOL_GROUP, :]))
    po = jnp.concatenate(po, axis=1)
    if pool_rows is not None:
        po = pool_rows(po)
    po = po * ps_ref[...]
    p = _dot((po * (pg * jax.nn.sigmoid(pg))).astype(BF16), wpbr_ref[...])
    a = _dot((attn_o * (ag * jax.nn.sigmoid(ag))).astype(BF16), wabr_ref[...])
    acc = jax.nn.sigmoid(ma) * a + jax.nn.sigmoid(mp) * p
    return acc.astype(BF16)


def _diag2(a, axis_r, axis_c):
    z = jnp.zeros_like(a)
    return jnp.concatenate([jnp.concatenate([a, z], axis=axis_c),
                            jnp.concatenate([z, a], axis=axis_c)], axis=axis_r)


def _bias_kernel(n_s, tbl_ref, bkp_ref, bks_ref, bkn_ref, bp_ref, ts_ref, tn_ref):
    g = pl.program_id(0)

    def gather(bucket, h):
        acc = jnp.zeros(bucket.shape, F32)
        for b in range(N_BUCKETS):
            acc = jnp.where(bucket == b, tbl_ref[h, b], acc)
        return acc

    bkp = bkp_ref[...]
    for r in range(GROUP_REP):
        bp_ref[:, r * PAIR:(r + 1) * PAIR] = jnp.where(bkp < 0, NEG_INF, gather(bkp, g * GROUP_REP + r))
    bks = bks_ref[...]
    bkn = bkn_ref[...]
    for c in range(2):
        for j in range(2):
            h = g * GROUP_REP + 2 * c + j
            rows = slice(c * n_s, (c + 1) * n_s)
            ts_ref[rows, j * WINDOW:(j + 1) * WINDOW] = gather(bks, h)
            tn_ref[rows, j * bkn.shape[1]:(j + 1) * bkn.shape[1]] = gather(bkn, h)


def _t5_bucket(rel):
    assert (N_BUCKETS, MAX_DISTANCE) == (32, 128)
    nb = N_BUCKETS // 2
    max_exact = nb // 2
    n = np.abs(rel)
    log2_sq = np.vectorize(lambda v: max(int(v), 1).bit_length() - 1)(n * n)
    large = np.minimum(max_exact + log2_sq - 6, nb - 1)
    return (np.where(rel > 0, nb, 0) + np.where(n < max_exact, n, large)).astype(np.int32)


def _bias_tables(rel_bias, n_b, n_s):
    kk = np.arange(PAIR_KEYS)[:, None]
    qq = np.arange(PAIR)[None, :]
    c0 = (qq // CHUNK) * CHUNK
    bkp = np.where((kk >= c0) & (kk < c0 + CHUNK + WINDOW), _t5_bucket(kk - WINDOW - qq), -1).astype(np.int32)
    fr = np.arange(n_s)[:, None]
    bks = _t5_bucket(np.arange(WINDOW)[None, :] - WINDOW - fr)
    bkn = _t5_bucket(np.arange(n_b * n_s)[None, :] % n_s - fr)
    n_new = n_b * n_s
    whole = lambda g: (0, 0)
    per_g = lambda g: (g, 0, 0)
    return pl.pallas_call(
        functools.partial(_bias_kernel, n_s),
        grid=(N_KV_HEADS,),
        in_specs=[
            pl.BlockSpec(memory_space=pltpu.SMEM),
            pl.BlockSpec((PAIR_KEYS, PAIR), whole),
            pl.BlockSpec((n_s, WINDOW), whole),
            pl.BlockSpec((n_s, n_new), whole),
        ],
        out_specs=[
            pl.BlockSpec((None, PAIR_KEYS, GROUP_REP * PAIR), per_g),
            pl.BlockSpec((None, 2 * n_s, 2 * WINDOW), per_g),
            pl.BlockSpec((None, 2 * n_s, 2 * n_new), per_g),
        ],
        out_shape=[
            jax.ShapeDtypeStruct((N_KV_HEADS, PAIR_KEYS, GROUP_REP * PAIR), F32),
            jax.ShapeDtypeStruct((N_KV_HEADS, 2 * n_s, 2 * WINDOW), F32),
            jax.ShapeDtypeStruct((N_KV_HEADS, 2 * n_s, 2 * n_new), F32),
        ],
        name="bias_tables",
    )(rel_bias.T, bkp, bks, bkn)


def _weight_jobs(pairs):
    jobs = []
    for src, dst in pairs:
        n_rows, n_cols = dst.shape
        for r0 in range(0, n_rows, STAGE_ROWS):
            for c0 in range(0, n_cols, STAGE_COLS):
                jobs.append((src, dst, r0, c0, min(STAGE_COLS, n_cols - c0)))
    return jobs


def _prompt_kernel(sinks_ref, x_ref, xn_ref, ng_ref, gcol_ref, grow_ref, ones_ref, bias_ref, ps_ref,
                   win_hbm, wabr_hbm, wpbr_hbm, wout_hbm, pw_hbm,
                   y_ref, pk_ref, pv_ref, pp_ref, winb_hbm, wabrb_hbm, wpbrb_hbm, woutb_hbm, pwb_hbm,
                   win_ref, wabr_ref, wpbr_ref, wout_ref, pw_ref, stage, sem_in, sem_out,
                   wt_s, h_s, qt_s, k_s, vt_s, attn_s, pbuf, hist_s):
    b = pl.program_id(0)
    t = pl.program_id(1)
    first = (b == 0) & (t == 0)
    q_gain = gcol_ref[:, 0:1]
    k_gain = grow_ref[:, ATTN_WIDTH:]
    zq_s = stage.at[0]
    weights = ((win_hbm, win_ref), (wabr_hbm, wabr_ref), (wpbr_hbm, wpbr_ref), (wout_hbm, wout_ref),
               (pw_hbm, pw_ref))
    exports = (winb_hbm, wabrb_hbm, wpbrb_hbm, woutb_hbm, pwb_hbm)

    def export_copy(i):
        return pltpu.make_async_copy(weights[i][1], exports[i], sem_out.at[i])

    @pl.when(first)
    def _():
        jobs = _weight_jobs(weights)

        def load(i):
            src, _, r0, c0, w = jobs[i]
            return pltpu.make_async_copy(src.at[pl.ds(r0, STAGE_ROWS), pl.ds(c0, w)],
                                         stage.at[i % N_STAGE, :, pl.ds(0, w)], sem_in.at[i % N_STAGE])

        for i in range(min(N_STAGE - 1, len(jobs))):
            load(i).start()
        for i in range(len(jobs)):
            if i + N_STAGE - 1 < len(jobs):
                load(i + N_STAGE - 1).start()
            load(i).wait()
            _, dst, r0, c0, w = jobs[i]
            dst[r0:r0 + STAGE_ROWS, c0:c0 + w] = stage[i % N_STAGE, :, 0:w].astype(BF16)
        for i in range(len(weights)):
            export_copy(i).start()
        wt_s[0:ATTN_WIDTH, :] = win_ref[:, C_Q].T
        wt_s[ATTN_WIDTH:, :] = win_ref[:, C_V].T
        h_s[...] = _rmsnorm_rows(x_ref[...], ng_ref[...]).astype(BF16)
        zq_s[...] = _dot_nt(wt_s[0:ATTN_WIDTH, :], h_s[...])

    @pl.when(t == 0)
    def _():
        k_s[:, 0:WINDOW, :] = jnp.zeros((2, WINDOW, LANES), BF16)
        vt_s[:, 0:WINDOW] = jnp.zeros((KV_WIDTH, WINDOW), BF16)
        pbuf[0:HIST_ROWS, :] = jnp.zeros((HIST_ROWS, POOL_WIDTH), F32)

    hb = h_s[...]

    n_c = D_MODEL // MXU_DIM
    fill_cols = [slice(c.start + i * MXU_DIM, c.start + (i + 1) * MXU_DIM)
                 for c in (C_PU, C_PG, C_AG, C_MA, C_MP) for i in range(n_c)]
    filled = []

    def fill(upto):
        for cols in fill_cols[len(filled):upto]:
            filled.append(_dot(hb, win_ref[:, cols]))

    zqt = zq_s[...]
    zk = _dot(hb, win_ref[:, C_K])
    vt = _dot_nt(wt_s[ATTN_WIDTH:, :], hb)
    fill(PREFILL_K)
    k = _head_rmsnorm(zk, ones_ref, k_gain)
    fill(PREFILL)
    for h in range(N_HEADS):
        qt_s[h * HEAD_DIM:(h + 1) * HEAD_DIM, :] = _head_rmsnorm_fm(zqt, q_gain, h).astype(BF16)

    pk_ref[...] = k[TM - WINDOW:, :].T
    pv_ref[...] = vt[:, TM - WINDOW:]

    k_s[0, WINDOW:, :] = k[:, :LANES].astype(BF16)
    k_s[1, WINDOW:, :] = k[:, LANES:].astype(BF16)
    vt_s[:, WINDOW:] = vt.astype(BF16)

    zeros_half = jnp.zeros((HEAD_DIM, PAIR), BF16)
    q_lane = lax.broadcasted_iota(jnp.int32, (1, GROUP_REP * PAIR), 1)

    def scores(p, g):
        tok = slice(p * PAIR, (p + 1) * PAIR)
        keys = slice(p * PAIR, p * PAIR + PAIR_KEYS)
        cols = []
        for r in range(GROUP_REP):
            h = g * GROUP_REP + r
            qh = qt_s[h * HEAD_DIM:(h + 1) * HEAD_DIM, tok]
            cols.append(jnp.concatenate([qh, zeros_half] if g % 2 == 0 else [zeros_half, qh], axis=0))
        rhs = jnp.concatenate(cols, axis=1)
        s = _dot(k_s[g // 2, keys, :], rhs) + bias_ref[g]
        if p == 0:
            kk = lax.broadcasted_iota(jnp.int32, s.shape, 0)
            s = jnp.where((kk >= WINDOW) | (t > 0), s, NEG_INF)
        return s

    def finish(p, g, s):
        tok = slice(p * PAIR, (p + 1) * PAIR)
        keys = slice(p * PAIR, p * PAIR + PAIR_KEYS)
        sink = sinks_ref[g * GROUP_REP]
        for r in range(1, GROUP_REP):
            sink = jnp.where(q_lane < r * PAIR, sink, sinks_ref[g * GROUP_REP + r])
        m = jnp.maximum(jnp.max(s, axis=0, keepdims=True), sink)
        e = jnp.exp(s - m)
        denom = jnp.sum(e, axis=0, keepdims=True) + jnp.exp(sink - m)
        vg = vt_s[g * HEAD_DIM:(g + 1) * HEAD_DIM, keys]
        o_t = _dot(vg, e.astype(BF16)) * (1.0 / denom)
        c = g * MXU_DIM
        attn_s[tok, c:c + LANES] = jnp.concatenate(
            [o_t[:, 0:PAIR], o_t[:, PAIR:2 * PAIR]], axis=0).T
        attn_s[tok, c + LANES:c + MXU_DIM] = jnp.concatenate(
            [o_t[:, 2 * PAIR:3 * PAIR], o_t[:, 3 * PAIR:]], axis=0).T

    pu = jnp.concatenate(filled[:n_c], axis=1)
    pbuf[HIST_ROWS:, :] = pu
    hist_s[b] = pu[TM - HIST_ROWS:, :]
    sums = _window_sums(pbuf[...], _delay_rows)
    pos1 = (t * TM + 1 + lax.broadcasted_iota(jnp.int32, (TM, 1), 0)).astype(F32)
    mixed = []
    for gi, w in enumerate(POOL_WINDOWS):
        inv_cnt = 1.0 / jnp.minimum(float(w), pos1)
        mixed.append(sums[gi][HIST_ROWS:, :] * inv_cnt - pu[:, gi * POOL_GROUP:(gi + 1) * POOL_GROUP])

    blocks = [(p, g) for p in range(TM // PAIR) for g in range(N_KV_HEADS)]
    s_next = scores(*blocks[0])
    for i, blk in enumerate(blocks):
        s_cur = s_next
        if i + 1 < len(blocks):
            s_next = scores(*blocks[i + 1])
        fill(PREFILL + (len(fill_cols) - PREFILL) * (i + 1) // len(blocks))
        finish(*blk, s_cur)
    pg, ag, ma, mp = (jnp.concatenate(filled[j * n_c:(j + 1) * n_c], axis=1) for j in range(1, 5))

    merged = _merge_branches(attn_s[...], mixed, (ag, pg, ma, mp), pw_ref, ps_ref, wabr_ref, wpbr_ref)

    h_s[...] = _rmsnorm_rows(xn_ref[...], ng_ref[...]).astype(BF16)
    zq_s[...] = _dot_nt(wt_s[0:ATTN_WIDTH, :], h_s[...])

    y_ref[...] = x_ref[...] + _dot(merged, wout_ref[...])

    k_s[:, 0:WINDOW, :] = k_s[:, TM:TM + WINDOW, :]
    vt_s[:, 0:WINDOW] = vt_s[:, TM:TM + WINDOW]
    pbuf[0:HIST_ROWS, :] = pbuf[TM:TM + HIST_ROWS, :]

    @pl.when(first)
    def _():
        for i in range(len(weights)):
            export_copy(i).wait()

    @pl.when((b == pl.num_programs(0) - 1) & (t == pl.num_programs(1) - 1))
    def _():
        pp_ref[...] = jnp.swapaxes(hist_s[...], 0, 1)[HIST_ROWS - POOL_HIST:]


def _resident(shape):
    nd = len(shape)
    return pl.BlockSpec(shape, lambda b, t: (0,) * nd, pipeline_mode=pl.Buffered(1))


def _prompt_layer(x, sinks, ng, gain_cols, gain_rows, ones, bias_p, ps, weights):
    B, S, _ = x.shape
    assert S % TM == 0 and TM % PAIR == 0 and TM >= WINDOW
    assert all(w.shape[0] % STAGE_ROWS == 0 for w in weights)
    consts = (ng, gain_cols, gain_rows, ones, bias_p, ps)
    hbm = pl.BlockSpec(memory_space=pl.ANY)
    n_t = S // TM

    def next_tile(b, t):
        i = jnp.minimum(b * n_t + t + 1, B * n_t - 1)
        return i // n_t, i % n_t, 0

    return pl.pallas_call(
        _prompt_kernel,
        grid=(B, S // TM),
        in_specs=[pl.BlockSpec(memory_space=pltpu.SMEM),
                  pl.BlockSpec((None, TM, D_MODEL), lambda b, t: (b, t, 0)),
                  pl.BlockSpec((None, TM, D_MODEL), next_tile)]
                 + [_resident(c.shape) for c in consts] + [hbm] * len(weights),
        out_specs=[
            pl.BlockSpec((None, TM, D_MODEL), lambda b, t: (b, t, 0)),
            pl.BlockSpec((None, KV_WIDTH, WINDOW), lambda b, t: (b, 0, 0)),
            pl.BlockSpec((None, KV_WIDTH, WINDOW), lambda b, t: (b, 0, 0)),
            pl.BlockSpec((POOL_HIST, B, POOL_WIDTH), lambda b, t: (0, 0, 0)),
        ] + [hbm] * len(weights),
        out_shape=[
            jax.ShapeDtypeStruct((B, S, D_MODEL), F32),
            jax.ShapeDtypeStruct((B, KV_WIDTH, WINDOW), F32),
            jax.ShapeDtypeStruct((B, KV_WIDTH, WINDOW), F32),
            jax.ShapeDtypeStruct((POOL_HIST, B, POOL_WIDTH), F32),
        ] + [jax.ShapeDtypeStruct(w.shape, BF16) for w in weights],
        scratch_shapes=[pltpu.VMEM(w.shape, BF16) for w in weights] + [
            pltpu.VMEM((N_STAGE, STAGE_ROWS, STAGE_COLS), F32),
            pltpu.SemaphoreType.DMA((N_STAGE,)),
            pltpu.SemaphoreType.DMA((len(weights),)),
            pltpu.VMEM((ATTN_WIDTH + KV_WIDTH, D_MODEL), BF16),
            pltpu.VMEM((TM, D_MODEL), BF16),
            pltpu.VMEM((ATTN_WIDTH, TM), BF16),
            pltpu.VMEM((2, WINDOW + TM, LANES), BF16),
            pltpu.VMEM((KV_WIDTH, WINDOW + TM), BF16),
            pltpu.VMEM((TM, ATTN_WIDTH), F32),
            pltpu.VMEM((HIST_ROWS + TM, POOL_WIDTH), F32),
            pltpu.VMEM((B, HIST_ROWS, POOL_WIDTH), F32),
        ],
        compiler_params=pltpu.CompilerParams(
            dimension_semantics=("arbitrary", "arbitrary"),
            vmem_limit_bytes=VMEM_LIMIT,
        ),
        name="prompt_layer",
    )(sinks, x, x, *consts, *weights)


def _sample_kernel(n_b, n_s, start,
                   sinks_ref, x_ref, skt_ref, svt_ref, sp_ref, ng_ref, win_ref,
                   gcol_ref, grow_ref, ones_ref, ts_ref, tn_ref, pw_ref, ps_ref, wabr_ref, wpbr_ref, wout_ref,
                   y_ref, ok_ref, ov_ref, op_ref):
    n_tok = n_b * n_s
    x = x_ref[...]
    hb = _rmsnorm_rows(x, ng_ref[...]).astype(BF16)

    q = _head_rmsnorm(_dot(hb, win_ref[:, C_Q]), ones_ref, grow_ref[:, :ATTN_WIDTH]).astype(BF16)
    q3 = q.reshape(n_b, n_s, ATTN_WIDTH)
    zkt = _dot_nt(win_ref[:, C_K].T, hb)
    k_gain = gcol_ref[:, 1:2]
    knt = jnp.concatenate([_head_rmsnorm_fm(zkt, k_gain, g) for g in range(N_KV_HEADS)], axis=0)
    vnt = _dot_nt(win_ref[:, C_V].T, hb)

    lane = lax.broadcasted_iota(jnp.int32, (1, WINDOW), 1)
    per_tile = LANES // n_s
    for s in range(n_b):
        tile = slice((s // per_tile) * LANES, (s // per_tile + 1) * LANES)
        shift = (WINDOW - n_s - (s % per_tile) * n_s) % LANES
        for new, st_ref, o_ref in ((knt, skt_ref, ok_ref), (vnt, svt_ref, ov_ref)):
            fresh = new[:, tile] if shift == 0 else pltpu.roll(new[:, tile], shift, axis=1)
            o_ref[s] = jnp.where(lane < WINDOW - n_s, pltpu.roll(st_ref[s], WINDOW - n_s, axis=1), fresh)

    knb = knt.astype(BF16)
    vnb = vnt.astype(BF16)
    key_stream = (lax.broadcasted_iota(jnp.int32, (n_b, 1, 2 * n_tok), 2) & (n_tok - 1)) >> (n_s.bit_length() - 1)
    own = key_stream == lax.broadcasted_iota(jnp.int32, (n_b, 1, 2 * n_tok), 0)
    top = lax.broadcasted_iota(jnp.int32, (1, 2 * n_s, 1), 1) < n_s
    low = lax.broadcasted_iota(jnp.int32, (1, 1, LANES), 2) < HEAD_DIM

    def scores(g):
        hd = slice(g * HEAD_DIM, (g + 1) * HEAD_DIM)
        c = g * MXU_DIM
        lhs = jnp.concatenate([q3[:, :, c:c + LANES], q3[:, :, c + LANES:c + MXU_DIM]], axis=1)
        kst = _diag2(skt_ref[:, hd, :].astype(BF16), 1, 2)
        s_st = jnp.stack([_dot(lhs[s], kst[s]) for s in range(n_b)]) + ts_ref[g][None]
        s_nw = _dot(lhs.reshape(2 * n_tok, LANES), _diag2(knb[hd, :], 0, 1))
        s_nw = jnp.where(own, s_nw.reshape(n_b, 2 * n_s, 2 * n_tok) + tn_ref[g][None], NEG_INF)
        return s_st, s_nw

    def finish(g, s_st, s_nw):
        hd = slice(g * HEAD_DIM, (g + 1) * HEAD_DIM)
        p_st, p_nw, inv = [], [], []
        for j in range(2):
            a = s_st[:, :, j * WINDOW:(j + 1) * WINDOW]
            b = s_nw[:, :, j * n_tok:(j + 1) * n_tok]
            sink = jnp.where(top, sinks_ref[4 * g + j], sinks_ref[4 * g + 2 + j])
            m = jnp.maximum(jnp.maximum(jnp.max(a, axis=-1, keepdims=True),
                                        jnp.max(b, axis=-1, keepdims=True)), sink)
            ea = jnp.exp(a - m)
            eb = jnp.exp(b - m)
            inv.append(1.0 / (jnp.sum(ea, axis=-1, keepdims=True) + jnp.sum(eb, axis=-1, keepdims=True)
                              + jnp.exp(sink - m)))
            p_st.append(ea.astype(BF16))
            p_nw.append(eb.astype(BF16))
        p_st = jnp.concatenate(p_st, axis=2)
        p_nw = jnp.concatenate(p_nw, axis=2).reshape(2 * n_tok, 2 * n_tok)
        vst = _diag2(svt_ref[:, hd, :].astype(BF16), 1, 2)
        o = jnp.stack([_dot_nt(p_st[s], vst[s]) for s in range(n_b)])
        o = o + _dot_nt(p_nw, _diag2(vnb[hd, :], 0, 1)).reshape(n_b, 2 * n_s, LANES)
        o = o * jnp.where(low, inv[0], inv[1])
        return jnp.concatenate([o[:, :n_s, :], o[:, n_s:, :]], axis=2)

    outs = []
    nxt = scores(0)
    for g in range(N_KV_HEADS):
        cur = nxt
        if g + 1 < N_KV_HEADS:
            nxt = scores(g + 1)
        outs.append(finish(g, *cur))
    attn_o = jnp.concatenate(outs, axis=2).reshape(n_tok, ATTN_WIDTH)

    pu = _dot(hb, win_ref[:, C_PU])
    pu_t = jnp.swapaxes(pu.reshape(n_b, n_s, POOL_WIDTH), 0, 1)
    full = jnp.concatenate([jnp.zeros((1, n_b, POOL_WIDTH), F32), sp_ref[...], pu_t], axis=0)
    op_ref[...] = full[HIST_ROWS + n_s - POOL_HIST:]
    sums = _window_sums(full, _delay_leading)
    pos1 = (start + 1 + lax.broadcasted_iota(jnp.int32, (n_s, 1, 1), 0)).astype(F32)
    mixed = []
    for gi, w in enumerate(POOL_WINDOWS):
        inv_cnt = 1.0 / jnp.minimum(float(w), pos1)
        sw = sums[gi][HIST_ROWS:] * inv_cnt - pu_t[..., gi * POOL_GROUP:(gi + 1) * POOL_GROUP]
        mixed.append(sw.reshape(n_tok, POOL_GROUP))

    def stream_major(po):
        return jnp.swapaxes(po.reshape(n_s, n_b, POOL_WIDTH), 0, 1).reshape(n_tok, POOL_WIDTH)

    gates = tuple(_dot(hb, win_ref[:, c]) for c in (C_AG, C_PG, C_MA, C_MP))
    merged = _merge_branches(attn_o, mixed, gates, pw_ref, ps_ref, wabr_ref, wpbr_ref, stream_major)
    y_ref[...] = x + _dot(merged, wout_ref[...])


def _sample_layer(x, skt, svt, sp, start, sinks, ng, w_in, gain_cols, gain_rows, ones, ts, tn,
                  pw, ps, wabr, wpbr, wout):
    n_b, n_s, _ = x.shape
    n_tok = n_b * n_s
    assert n_s % 16 == 0 and LANES % n_s == 0 and n_s >= POOL_HIST and n_tok % LANES == 0
    assert n_s & (n_s - 1) == 0 and n_tok & (n_tok - 1) == 0
    vmem = pl.BlockSpec(memory_space=pltpu.VMEM)
    y, ok, ov, op = pl.pallas_call(
        functools.partial(_sample_kernel, n_b, n_s, start),
        in_specs=[pl.BlockSpec(memory_space=pltpu.SMEM)] + [vmem] * 16,
        out_specs=[vmem] * 4,
        out_shape=[
            jax.ShapeDtypeStruct((n_tok, D_MODEL), F32),
            jax.ShapeDtypeStruct((n_b, KV_WIDTH, WINDOW), F32),
            jax.ShapeDtypeStruct((n_b, KV_WIDTH, WINDOW), F32),
            jax.ShapeDtypeStruct((POOL_HIST, n_b, POOL_WIDTH), F32),
        ],
        compiler_params=pltpu.CompilerParams(vmem_limit_bytes=VMEM_LIMIT),
        name="sample_layer",
    )(sinks, x.reshape(n_tok, D_MODEL), skt, svt, sp, ng, w_in, gain_cols, gain_rows, ones, ts, tn,
      pw, ps, wabr, wpbr, wout)
    return y.reshape(n_b, n_s, D_MODEL), ok, ov, op


def _kv_to_feature_major(a):
    return jnp.transpose(a, (0, 2, 3, 1)).reshape(a.shape[0], KV_WIDTH, WINDOW)


def _kv_from_feature_major(a):
    return jnp.transpose(a.reshape(a.shape[0], N_KV_HEADS, HEAD_DIM, WINDOW), (0, 3, 1, 2))


def kernel(x_prompt, x_sample, state_attn_k, state_attn_v, state_pool, norm_gain, w_in, q_norm_gain, k_norm_gain, attn_sinks, rel_bias, pool_w, pool_scale, w_attn_br, w_pool_br, w_out):
    depth = w_in.shape[0]
    n_b, n_s, _ = x_sample.shape
    past_len = 4096
    bias_p, tab_state, tab_new = _bias_tables(rel_bias, n_b, n_s)
    seg = np.arange(MXU_DIM) // HEAD_DIM
    ones = jnp.asarray(seg[:, None] == seg[None, :], BF16)

    xp, xs = x_prompt, x_sample
    pk, pv, pp, sk, sv, sp = [], [], [], [], [], []
    for l in range(depth):
        ng = norm_gain[l].reshape(1, D_MODEL)
        qg = q_norm_gain[l] * (HEAD_DIM ** -0.5)
        kg = k_norm_gain[l]
        gain_cols = jnp.stack([qg, kg], axis=1)
        gain_rows = jnp.concatenate([jnp.tile(qg, N_HEADS), jnp.tile(kg, N_KV_HEADS)]).reshape(1, -1)
        ps = pool_scale[l].reshape(1, POOL_WIDTH)
        xp, k_l, v_l, p_l, w_in_b, wabr_b, wpbr_b, wout_b, pw_b = _prompt_layer(
            xp, attn_sinks[l], ng, gain_cols, gain_rows, ones, bias_p, ps,
            (w_in[l], w_attn_br[l], w_pool_br[l], w_out[l], pool_w[l].reshape(POOL_WIDTH, POOL_GROUP)))
        tail = (pw_b, ps, wabr_b, wpbr_b, wout_b)
        pk.append(_kv_from_feature_major(k_l))
        pv.append(_kv_from_feature_major(v_l))
        pp.append(jnp.transpose(p_l, (1, 0, 2)))

        xs, k_l, v_l, p_l = _sample_layer(
            xs,
            _kv_to_feature_major(state_attn_k[l]),
            _kv_to_feature_major(state_attn_v[l]),
            jnp.transpose(state_pool[l], (1, 0, 2)),
            past_len, attn_sinks[l], ng, w_in_b, gain_cols, gain_rows, ones, tab_state, tab_new, *tail)
        sk.append(_kv_from_feature_major(k_l))
        sv.append(_kv_from_feature_major(v_l))
        sp.append(jnp.transpose(p_l, (1, 0, 2)))
    return (xp, xs, jnp.stack(pk), jnp.stack(pv), jnp.stack(pp),
            jnp.stack(sk), jnp.stack(sv), jnp.stack(sp))
```

```python
import functools
import numpy as np

import jax
import jax.numpy as jnp
from jax import lax
from jax.experimental import pallas as pl
from jax.experimental.pallas import tpu as pltpu

D_MODEL = 1024
CHUNK = 64
WINDOW = 128
N_HEADS = 16
N_KV_HEADS = 4
GROUP_REP = N_HEADS // N_KV_HEADS
HEAD_DIM = 64
ATTN_WIDTH = N_HEADS * HEAD_DIM
KV_WIDTH = N_KV_HEADS * HEAD_DIM
POOL_WINDOWS = (2, 4, 8, 16)
POOL_WIDTH = D_MODEL
POOL_GROUP = POOL_WIDTH // len(POOL_WINDOWS)
POOL_HIST = max(POOL_WINDOWS) - 1
HIST_ROWS = POOL_HIST + 1
N_BUCKETS = 32
MAX_DISTANCE = 128
EPS = 1e-6
NEG_INF = -1e30

_SPLITS = (ATTN_WIDTH, KV_WIDTH, KV_WIDTH, ATTN_WIDTH, POOL_WIDTH, POOL_WIDTH, D_MODEL, D_MODEL)
_OFFS = tuple(sum(_SPLITS[:i]) for i in range(len(_SPLITS) + 1))
C_Q, C_K, C_V, C_AG, C_PU, C_PG, C_MA, C_MP = (slice(_OFFS[i], _OFFS[i + 1]) for i in range(8))

LANES = 128
MXU_DIM = 256
PAIR = 2 * CHUNK
PAIR_KEYS = PAIR + WINDOW
TM = 512
PREFILL_K, PREFILL = 2, 8
STAGE_ROWS, STAGE_COLS, N_STAGE = ATTN_WIDTH, TM, 3
VMEM_LIMIT = 60 * 1024 * 1024

BF16 = jnp.bfloat16
F32 = jnp.float32


def _dot(a, b):
    return jnp.dot(a, b, preferred_element_type=F32)


def _dot_nt(a, b):
    return lax.dot_general(a, b, (((1,), (1,)), ((), ())), preferred_element_type=F32)


def _rmsnorm_rows(x, gain):
    y = x * lax.rsqrt(jnp.mean(x * x, axis=-1, keepdims=True) + EPS)
    return y * gain


def _head_rmsnorm(z, ones_ref, gain):
    sq = z * z
    hi = sq.astype(BF16)
    lo = (sq - hi.astype(F32)).astype(BF16)
    ones = ones_ref[...]
    parts = []
    for c in range(z.shape[1] // MXU_DIM):
        sl = slice(c * MXU_DIM, (c + 1) * MXU_DIM)
        parts.append(_dot(hi[:, sl], ones) + _dot(lo[:, sl], ones))
    ss = parts[0] if len(parts) == 1 else jnp.concatenate(parts, axis=1)
    return (z * lax.rsqrt(ss * (1.0 / HEAD_DIM) + EPS)) * gain


def _head_rmsnorm_fm(zt, gain_col, h):
    zh = zt[h * HEAD_DIM:(h + 1) * HEAD_DIM, :]
    ms = jnp.sum(zh * zh, axis=0, keepdims=True) * (1.0 / HEAD_DIM)
    return (zh * lax.rsqrt(ms + EPS)) * gain_col


def _window_sums(full, delay):
    outs = []
    for gi, w in enumerate(POOL_WINDOWS):
        s = full[..., gi * POOL_GROUP:(gi + 1) * POOL_GROUP]
        sh = 1
        while sh < w:
            s = s + delay(s, sh)
            sh *= 2
        outs.append(s)
    return outs


def _delay_rows(a, n):
    return pltpu.roll(a, n, axis=0)


def _delay_leading(a, n):
    return jnp.concatenate([a[-n:], a[:-n]], axis=0)


def _merge_branches(attn_o, pool_mixed, gates, pw_ref, ps_ref, wabr_ref, wpbr_ref, pool_rows=None):
    ag, pg, ma, mp = gates
    po = []
    for gi in range(len(POOL_WINDOWS)):
        po.append(_dot(pool_mixed[gi].astype(BF16), pw_ref[gi * POOL_GROUP:(gi + 1) * POOL_GROUP, :]))
    po = jnp.concatenate(po, axis=1)
    if pool_rows is not None:
        po = pool_rows(po)
    po = po * ps_ref[...]
    p = _dot((po * (pg * jax.nn.sigmoid(pg))).astype(BF16), wpbr_ref[...])
    a = _dot((attn_o * (ag * jax.nn.sigmoid(ag))).astype(BF16), wabr_ref[...])
    acc = jax.nn.sigmoid(ma) * a + jax.nn.sigmoid(mp) * p
    return acc.astype(BF16)


def _diag2(a, axis_r, axis_c):
    z = jnp.zeros_like(a)
    return jnp.concatenate([jnp.concatenate([a, z], axis=axis_c),
                            jnp.concatenate([z, a], axis=axis_c)], axis=axis_r)


def _gather_bias(bucket, tbl_ref, h):
    acc = jnp.zeros(bucket.shape, F32)
    for b in range(N_BUCKETS):
        acc = jnp.where(bucket == b, tbl_ref[h, b], acc)
    return acc


def _t5_bucket(rel):
    assert (N_BUCKETS, MAX_DISTANCE) == (32, 128)
    nb = N_BUCKETS // 2
    max_exact = nb // 2
    n = np.abs(rel)
    log2_sq = np.vectorize(lambda v: max(int(v), 1).bit_length() - 1)(n * n)
    large = np.minimum(max_exact + log2_sq - 6, nb - 1)
    return (np.where(rel > 0, nb, 0) + np.where(n < max_exact, n, large)).astype(np.int32)


def _bucket_tables(n_b, n_s):
    kk = np.arange(PAIR_KEYS)[:, None]
    qq = np.arange(PAIR)[None, :]
    c0 = (qq // CHUNK) * CHUNK
    bkp = np.where((kk >= c0) & (kk < c0 + CHUNK + WINDOW), _t5_bucket(kk - WINDOW - qq), -1).astype(np.int32)
    fr = np.arange(n_s)[:, None]
    bks = _t5_bucket(np.arange(WINDOW)[None, :] - WINDOW - fr)
    bkn = _t5_bucket(np.arange(n_b * n_s)[None, :] % n_s - fr)
    return bkp, bks, bkn


def _weight_jobs(pairs):
    jobs = []
    for src, dst in pairs:
        n_rows, n_cols = dst.shape
        for r0 in range(0, n_rows, STAGE_ROWS):
            for c0 in range(0, n_cols, STAGE_COLS):
                jobs.append((src, dst, r0, c0, min(STAGE_COLS, n_cols - c0)))
    return jobs


def _prompt_kernel(tbl_ref, sinks_ref, x_ref, xn_ref, ng_ref, gcol_ref, grow_ref, ones_ref, bkp_ref, ps_ref,
                   win_hbm, wabr_hbm, wpbr_hbm, wout_hbm, pw_hbm,
                   y_ref, pk_ref, pv_ref, pp_ref, winb_hbm, wabrb_hbm, wpbrb_hbm, woutb_hbm, pwb_hbm,
                   win_ref, wabr_ref, wpbr_ref, wout_ref, pw_ref, stage, sem_in, sem_out,
                   wt_s, h_s, qt_s, k_s, vt_s, attn_s, pbuf, hist_s, bias_s):
    b = pl.program_id(0)
    t = pl.program_id(1)
    first = (b == 0) & (t == 0)
    q_gain = gcol_ref[:, 0:1]
    k_gain = grow_ref[:, ATTN_WIDTH:]
    zq_s = stage.at[0]
    weights = ((win_hbm, win_ref), (wabr_hbm, wabr_ref), (wpbr_hbm, wpbr_ref), (wout_hbm, wout_ref),
               (pw_hbm, pw_ref))
    exports = (winb_hbm, wabrb_hbm, wpbrb_hbm, woutb_hbm, pwb_hbm)

    def export_copy(i):
        return pltpu.make_async_copy(weights[i][1], exports[i], sem_out.at[i])

    @pl.when(first)
    def _():
        jobs = _weight_jobs(weights)

        def load(i):
            src, _, r0, c0, w = jobs[i]
            return pltpu.make_async_copy(src.at[pl.ds(r0, STAGE_ROWS), pl.ds(c0, w)],
                                         stage.at[i % N_STAGE, :, pl.ds(0, w)], sem_in.at[i % N_STAGE])

        for i in range(min(N_STAGE - 1, len(jobs))):
            load(i).start()
        for i in range(len(jobs)):
            if i + N_STAGE - 1 < len(jobs):
                load(i + N_STAGE - 1).start()
            load(i).wait()
            _, dst, r0, c0, w = jobs[i]
            dst[r0:r0 + STAGE_ROWS, c0:c0 + w] = stage[i % N_STAGE, :, 0:w].astype(BF16)
            if i < N_HEADS:
                bkp = bkp_ref[...]
                bias_s[i // GROUP_REP, :, (i % GROUP_REP) * PAIR:(i % GROUP_REP + 1) * PAIR] = jnp.where(
                    bkp < 0, NEG_INF, _gather_bias(bkp, tbl_ref, i))
            if jobs[i][1] is win_ref and c0 + w == C_V.stop:
                wt_s[0:ATTN_WIDTH, :] = win_ref[:, C_Q].T
                wt_s[ATTN_WIDTH:, :] = win_ref[:, C_V].T
                h_s[...] = _rmsnorm_rows(x_ref[...], ng_ref[...]).astype(BF16)
        assert len(jobs) >= N_HEADS and C_V.stop % STAGE_COLS == 0
        zq_s[...] = _dot_nt(wt_s[0:ATTN_WIDTH, :], h_s[...])
        for i in range(len(weights)):
            export_copy(i).start()

    @pl.when(t == 0)
    def _():
        k_s[:, 0:WINDOW, :] = jnp.zeros((2, WINDOW, LANES), BF16)
        vt_s[:, 0:WINDOW] = jnp.zeros((KV_WIDTH, WINDOW), BF16)
        pbuf[0:HIST_ROWS, :] = jnp.zeros((HIST_ROWS, POOL_WIDTH), F32)

    hb = h_s[...]

    n_c = D_MODEL // MXU_DIM
    fill_cols = [slice(c.start + i * MXU_DIM, c.start + (i + 1) * MXU_DIM)
                 for c in (C_PU, C_PG, C_AG, C_MA, C_MP) for i in range(n_c)]
    filled = []

    def fill(upto):
        for cols in fill_cols[len(filled):upto]:
            filled.append(_dot(hb, win_ref[:, cols]))

    zqt = zq_s[...]
    zk = _dot(hb, win_ref[:, C_K])
    vt = _dot_nt(wt_s[ATTN_WIDTH:, :], hb)
    fill(PREFILL_K)
    k = _head_rmsnorm(zk, ones_ref, k_gain)
    fill(PREFILL)
    for h in range(N_HEADS):
        qt_s[h * HEAD_DIM:(h + 1) * HEAD_DIM, :] = _head_rmsnorm_fm(zqt, q_gain, h).astype(BF16)

    pk_ref[...] = k[TM - WINDOW:, :].T
    pv_ref[...] = vt[:, TM - WINDOW:]

    k_s[0, WINDOW:, :] = k[:, :LANES].astype(BF16)
    k_s[1, WINDOW:, :] = k[:, LANES:].astype(BF16)
    vt_s[:, WINDOW:] = vt.astype(BF16)

    zeros_half = jnp.zeros((HEAD_DIM, PAIR), BF16)
    q_lane = lax.broadcasted_iota(jnp.int32, (1, GROUP_REP * PAIR), 1)

    def scores(p, g):
        tok = slice(p * PAIR, (p + 1) * PAIR)
        keys = slice(p * PAIR, p * PAIR + PAIR_KEYS)
        cols = []
        for r in range(GROUP_REP):
            h = g * GROUP_REP + r
            qh = qt_s[h * HEAD_DIM:(h + 1) * HEAD_DIM, tok]
            cols.append(jnp.concatenate([qh, zeros_half] if g % 2 == 0 else [zeros_half, qh], axis=0))
        rhs = jnp.concatenate(cols, axis=1)
        s = _dot(k_s[g // 2, keys, :], rhs) + bias_s[g]
        if p == 0:
            kk = lax.broadcasted_iota(jnp.int32, s.shape, 0)
            s = jnp.where((kk >= WINDOW) | (t > 0), s, NEG_INF)
        return s

    def finish(p, g, s):
        tok = slice(p * PAIR, (p + 1) * PAIR)
        keys = slice(p * PAIR, p * PAIR + PAIR_KEYS)
        sink = sinks_ref[g * GROUP_REP]
        for r in range(1, GROUP_REP):
            sink = jnp.where(q_lane < r * PAIR, sink, sinks_ref[g * GROUP_REP + r])
        m = jnp.maximum(jnp.max(s, axis=0, keepdims=True), sink)
        e = jnp.exp(s - m)
        denom = jnp.sum(e, axis=0, keepdims=True) + jnp.exp(sink - m)
        vg = vt_s[g * HEAD_DIM:(g + 1) * HEAD_DIM, keys]
        o_t = _dot(vg, e.astype(BF16)) * (1.0 / denom)
        c = g * MXU_DIM
        attn_s[tok, c:c + LANES] = jnp.concatenate(
            [o_t[:, 0:PAIR], o_t[:, PAIR:2 * PAIR]], axis=0).T
        attn_s[tok, c + LANES:c + MXU_DIM] = jnp.concatenate(
            [o_t[:, 2 * PAIR:3 * PAIR], o_t[:, 3 * PAIR:]], axis=0).T

    pu = jnp.concatenate(filled[:n_c], axis=1)
    pbuf[HIST_ROWS:, :] = pu
    hist_s[b] = pu[TM - HIST_ROWS:, :]
    sums = _window_sums(pbuf[...], _delay_rows)
    pos1 = (t * TM + 1 + lax.broadcasted_iota(jnp.int32, (TM, 1), 0)).astype(F32)
    mixed = []
    for gi, w in enumerate(POOL_WINDOWS):
        inv_cnt = 1.0 / jnp.minimum(float(w), pos1)
        mixed.append(sums[gi][HIST_ROWS:, :] * inv_cnt - pu[:, gi * POOL_GROUP:(gi + 1) * POOL_GROUP])

    blocks = [(p, g) for p in range(TM // PAIR) for g in range(N_KV_HEADS)]
    s_next = scores(*blocks[0])
    for i, blk in enumerate(blocks):
        s_cur = s_next
        if i + 1 < len(blocks):
            s_next = scores(*blocks[i + 1])
        fill(PREFILL + (len(fill_cols) - PREFILL) * (i + 1) // len(blocks))
        finish(*blk, s_cur)
    pg, ag, ma, mp = (jnp.concatenate(filled[j * n_c:(j + 1) * n_c], axis=1) for j in range(1, 5))

    merged = _merge_branches(attn_s[...], mixed, (ag, pg, ma, mp), pw_ref, ps_ref, wabr_ref, wpbr_ref)

    h_s[...] = _rmsnorm_rows(xn_ref[...], ng_ref[...]).astype(BF16)
    zq_s[...] = _dot_nt(wt_s[0:ATTN_WIDTH, :], h_s[...])

    y_ref[...] = x_ref[...] + _dot(merged, wout_ref[...])

    k_s[:, 0:WINDOW, :] = k_s[:, TM:TM + WINDOW, :]
    vt_s[:, 0:WINDOW] = vt_s[:, TM:TM + WINDOW]
    pbuf[0:HIST_ROWS, :] = pbuf[TM:TM + HIST_ROWS, :]

    @pl.when(first)
    def _():
        for i in range(len(weights)):
            export_copy(i).wait()

    @pl.when((b == pl.num_programs(0) - 1) & (t == pl.num_programs(1) - 1))
    def _():
        pp_ref[...] = jnp.swapaxes(hist_s[...], 0, 1)[HIST_ROWS - POOL_HIST:]


def _resident(shape):
    nd = len(shape)
    return pl.BlockSpec(shape, lambda b, t: (0,) * nd, pipeline_mode=pl.Buffered(1))


def _prompt_layer(x, tbl, sinks, ng, gain_cols, gain_rows, ones, bkp, ps, weights):
    B, S, _ = x.shape
    assert S % TM == 0 and TM % PAIR == 0 and TM >= WINDOW
    assert all(w.shape[0] % STAGE_ROWS == 0 for w in weights)
    consts = (ng, gain_cols, gain_rows, ones, bkp, ps)
    hbm = pl.BlockSpec(memory_space=pl.ANY)
    n_t = S // TM

    def next_tile(b, t):
        i = jnp.minimum(b * n_t + t + 1, B * n_t - 1)
        return i // n_t, i % n_t, 0

    return pl.pallas_call(
        _prompt_kernel,
        grid=(B, S // TM),
        in_specs=[pl.BlockSpec(memory_space=pltpu.SMEM),
                  pl.BlockSpec(memory_space=pltpu.SMEM),
                  pl.BlockSpec((None, TM, D_MODEL), lambda b, t: (b, t, 0)),
                  pl.BlockSpec((None, TM, D_MODEL), next_tile)]
                 + [_resident(c.shape) for c in consts] + [hbm] * len(weights),
        out_specs=[
            pl.BlockSpec((None, TM, D_MODEL), lambda b, t: (b, t, 0)),
            pl.BlockSpec((None, KV_WIDTH, WINDOW), lambda b, t: (b, 0, 0)),
            pl.BlockSpec((None, KV_WIDTH, WINDOW), lambda b, t: (b, 0, 0)),
            pl.BlockSpec((POOL_HIST, B, POOL_WIDTH), lambda b, t: (0, 0, 0)),
        ] + [hbm] * len(weights),
        out_shape=[
            jax.ShapeDtypeStruct((B, S, D_MODEL), F32),
            jax.ShapeDtypeStruct((B, KV_WIDTH, WINDOW), F32),
            jax.ShapeDtypeStruct((B, KV_WIDTH, WINDOW), F32),
            jax.ShapeDtypeStruct((POOL_HIST, B, POOL_WIDTH), F32),
        ] + [jax.ShapeDtypeStruct(w.shape, BF16) for w in weights],
        scratch_shapes=[pltpu.VMEM(w.shape, BF16) for w in weights] + [
            pltpu.VMEM((N_STAGE, STAGE_ROWS, STAGE_COLS), F32),
            pltpu.SemaphoreType.DMA((N_STAGE,)),
            pltpu.SemaphoreType.DMA((len(weights),)),
            pltpu.VMEM((ATTN_WIDTH + KV_WIDTH, D_MODEL), BF16),
            pltpu.VMEM((TM, D_MODEL), BF16),
            pltpu.VMEM((ATTN_WIDTH, TM), BF16),
            pltpu.VMEM((2, WINDOW + TM, LANES), BF16),
            pltpu.VMEM((KV_WIDTH, WINDOW + TM), BF16),
            pltpu.VMEM((TM, ATTN_WIDTH), F32),
            pltpu.VMEM((HIST_ROWS + TM, POOL_WIDTH), F32),
            pltpu.VMEM((B, HIST_ROWS, POOL_WIDTH), F32),
            pltpu.VMEM((N_KV_HEADS, PAIR_KEYS, GROUP_REP * PAIR), F32),
        ],
        compiler_params=pltpu.CompilerParams(
            dimension_semantics=("arbitrary", "arbitrary"),
            vmem_limit_bytes=VMEM_LIMIT,
        ),
        name="prompt_layer",
    )(tbl, sinks, x, x, *consts, *weights)


def _sample_kernel(n_b, n_s, start,
                   tbl_ref, sinks_ref, x_ref, skt_ref, svt_ref, sp_ref, ng_ref, win_ref,
                   gcol_ref, grow_ref, ones_ref, bks_ref, bkn_ref, pw_ref, ps_ref, wabr_ref, wpbr_ref, wout_ref,
                   y_ref, ok_ref, ov_ref, op_ref):
    n_tok = n_b * n_s
    x = x_ref[...]
    hb = _rmsnorm_rows(x, ng_ref[...]).astype(BF16)

    q = _head_rmsnorm(_dot(hb, win_ref[:, C_Q]), ones_ref, grow_ref[:, :ATTN_WIDTH]).astype(BF16)
    q3 = q.reshape(n_b, n_s, ATTN_WIDTH)
    zkt = _dot_nt(win_ref[:, C_K].T, hb)
    k_gain = gcol_ref[:, 1:2]
    knt = jnp.concatenate([_head_rmsnorm_fm(zkt, k_gain, g) for g in range(N_KV_HEADS)], axis=0)
    vnt = _dot_nt(win_ref[:, C_V].T, hb)

    lane = lax.broadcasted_iota(jnp.int32, (1, WINDOW), 1)
    per_tile = LANES // n_s
    for s in range(n_b):
        tile = slice((s // per_tile) * LANES, (s // per_tile + 1) * LANES)
        shift = (WINDOW - n_s - (s % per_tile) * n_s) % LANES
        for new, st_ref, o_ref in ((knt, skt_ref, ok_ref), (vnt, svt_ref, ov_ref)):
            fresh = new[:, tile] if shift == 0 else pltpu.roll(new[:, tile], shift, axis=1)
            o_ref[s] = jnp.where(lane < WINDOW - n_s, pltpu.roll(st_ref[s], WINDOW - n_s, axis=1), fresh)

    knb = knt.astype(BF16)
    vnb = vnt.astype(BF16)
    key_stream = (lax.broadcasted_iota(jnp.int32, (n_b, 1, 2 * n_tok), 2) & (n_tok - 1)) >> (n_s.bit_length() - 1)
    own = key_stream == lax.broadcasted_iota(jnp.int32, (n_b, 1, 2 * n_tok), 0)
    top = lax.broadcasted_iota(jnp.int32, (1, 2 * n_s, 1), 1) < n_s
    low = lax.broadcasted_iota(jnp.int32, (1, 1, LANES), 2) < HEAD_DIM

    def bias_rows(bucket, g):
        return jnp.concatenate([jnp.concatenate(
            [_gather_bias(bucket, tbl_ref, g * GROUP_REP + 2 * c + j) for j in range(2)], axis=1)
            for c in range(2)], axis=0)

    def scores(g):
        hd = slice(g * HEAD_DIM, (g + 1) * HEAD_DIM)
        c = g * MXU_DIM
        lhs = jnp.concatenate([q3[:, :, c:c + LANES], q3[:, :, c + LANES:c + MXU_DIM]], axis=1)
        kst = _diag2(skt_ref[:, hd, :].astype(BF16), 1, 2)
        s_st = jnp.stack([_dot(lhs[s], kst[s]) for s in range(n_b)]) + bias_rows(bks_ref[...], g)[None]
        s_nw = _dot(lhs.reshape(2 * n_tok, LANES), _diag2(knb[hd, :], 0, 1))
        s_nw = jnp.where(own, s_nw.reshape(n_b, 2 * n_s, 2 * n_tok) + bias_rows(bkn_ref[...], g)[None], NEG_INF)
        return s_st, s_nw

    def finish(g, s_st, s_nw):
        hd = slice(g * HEAD_DIM, (g + 1) * HEAD_DIM)
        p_st, p_nw, inv = [], [], []
        for j in range(2):
            a = s_st[:, :, j * WINDOW:(j + 1) * WINDOW]
            b = s_nw[:, :, j * n_tok:(j + 1) * n_tok]
            sink = jnp.where(top, sinks_ref[4 * g + j], sinks_ref[4 * g + 2 + j])
            m = jnp.maximum(jnp.maximum(jnp.max(a, axis=-1, keepdims=True),
                                        jnp.max(b, axis=-1, keepdims=True)), sink)
            ea = jnp.exp(a - m)
            eb = jnp.exp(b - m)
            inv.append(1.0 / (jnp.sum(ea, axis=-1, keepdims=True) + jnp.sum(eb, axis=-1, keepdims=True)
                              + jnp.exp(sink - m)))
            p_st.append(ea.astype(BF16))
            p_nw.append(eb.astype(BF16))
        p_st = jnp.concatenate(p_st, axis=2)
        p_nw = jnp.concatenate(p_nw, axis=2).reshape(2 * n_tok, 2 * n_tok)
        vst = _diag2(svt_ref[:, hd, :].astype(BF16), 1, 2)
        o = jnp.stack([_dot_nt(p_st[s], vst[s]) for s in range(n_b)])
        o = o + _dot_nt(p_nw, _diag2(vnb[hd, :], 0, 1)).reshape(n_b, 2 * n_s, LANES)
        o = o * jnp.where(low, inv[0], inv[1])
        return jnp.concatenate([o[:, :n_s, :], o[:, n_s:, :]], axis=2)

    outs = []
    nxt = scores(0)
    for g in range(N_KV_HEADS):
        cur = nxt
        if g + 1 < N_KV_HEADS:
            nxt = scores(g + 1)
        outs.append(finish(g, *cur))
    attn_o = jnp.concatenate(outs, axis=2).reshape(n_tok, ATTN_WIDTH)

    pu = _dot(hb, win_ref[:, C_PU])
    pu_t = jnp.swapaxes(pu.reshape(n_b, n_s, POOL_WIDTH), 0, 1)
    full = jnp.concatenate([jnp.zeros((1, n_b, POOL_WIDTH), F32), sp_ref[...], pu_t], axis=0)
    op_ref[...] = full[HIST_ROWS + n_s - POOL_HIST:]
    sums = _window_sums(full, _delay_leading)
    pos1 = (start + 1 + lax.broadcasted_iota(jnp.int32, (n_s, 1, 1), 0)).astype(F32)
    mixed = []
    for gi, w in enumerate(POOL_WINDOWS):
        inv_cnt = 1.0 / jnp.minimum(float(w), pos1)
        sw = sums[gi][HIST_ROWS:] * inv_cnt - pu_t[..., gi * POOL_GROUP:(gi + 1) * POOL_GROUP]
        mixed.append(sw.reshape(n_tok, POOL_GROUP))

    def stream_major(po):
        return jnp.swapaxes(po.reshape(n_s, n_b, POOL_WIDTH), 0, 1).reshape(n_tok, POOL_WIDTH)

    gates = tuple(_dot(hb, win_ref[:, c]) for c in (C_AG, C_PG, C_MA, C_MP))
    merged = _merge_branches(attn_o, mixed, gates, pw_ref, ps_ref, wabr_ref, wpbr_ref, stream_major)
    y_ref[...] = x + _dot(merged, wout_ref[...])


def _sample_layer(x, skt, svt, sp, start, tbl, sinks, ng, w_in, gain_cols, gain_rows, ones, bks, bkn,
                  pw, ps, wabr, wpbr, wout):
    n_b, n_s, _ = x.shape
    n_tok = n_b * n_s
    assert n_s % 16 == 0 and LANES % n_s == 0 and n_s >= POOL_HIST and n_tok % LANES == 0
    assert n_s & (n_s - 1) == 0 and n_tok & (n_tok - 1) == 0
    vmem = pl.BlockSpec(memory_space=pltpu.VMEM)
    y, ok, ov, op = pl.pallas_call(
        functools.partial(_sample_kernel, n_b, n_s, start),
        in_specs=[pl.BlockSpec(memory_space=pltpu.SMEM)] * 2 + [vmem] * 16,
        out_specs=[vmem] * 4,
        out_shape=[
            jax.ShapeDtypeStruct((n_tok, D_MODEL), F32),
            jax.ShapeDtypeStruct((n_b, KV_WIDTH, WINDOW), F32),
            jax.ShapeDtypeStruct((n_b, KV_WIDTH, WINDOW), F32),
            jax.ShapeDtypeStruct((POOL_HIST, n_b, POOL_WIDTH), F32),
        ],
        compiler_params=pltpu.CompilerParams(vmem_limit_bytes=VMEM_LIMIT),
        name="sample_layer",
    )(tbl, sinks, x.reshape(n_tok, D_MODEL), skt, svt, sp, ng, w_in, gain_cols, gain_rows, ones, bks, bkn,
      pw, ps, wabr, wpbr, wout)
    return y.reshape(n_b, n_s, D_MODEL), ok, ov, op


def _kv_to_feature_major(a):
    return jnp.transpose(a, (0, 2, 3, 1)).reshape(a.shape[0], KV_WIDTH, WINDOW)


def _kv_from_feature_major(a):
    return jnp.transpose(a.reshape(a.shape[0], N_KV_HEADS, HEAD_DIM, WINDOW), (0, 3, 1, 2))


def kernel(x_prompt, x_sample, state_attn_k, state_attn_v, state_pool, norm_gain, w_in, q_norm_gain, k_norm_gain, attn_sinks, rel_bias, pool_w, pool_scale, w_attn_br, w_pool_br, w_out):
    depth = w_in.shape[0]
    n_b, n_s, _ = x_sample.shape
    past_len = 4096
    bkp, bks, bkn = _bucket_tables(n_b, n_s)
    tbl = rel_bias.T
    seg = np.arange(MXU_DIM) // HEAD_DIM
    ones = jnp.asarray(seg[:, None] == seg[None, :], BF16)

    xp, xs = x_prompt, x_sample
    pk, pv, pp, sk, sv, sp = [], [], [], [], [], []
    for l in range(depth):
        ng = norm_gain[l].reshape(1, D_MODEL)
        qg = q_norm_gain[l] * (HEAD_DIM ** -0.5)
        kg = k_norm_gain[l]
        gain_cols = jnp.stack([qg, kg], axis=1)
        gain_rows = jnp.concatenate([jnp.tile(qg, N_HEADS), jnp.tile(kg, N_KV_HEADS)]).reshape(1, -1)
        ps = pool_scale[l].reshape(1, POOL_WIDTH)
        xp, k_l, v_l, p_l, w_in_b, wabr_b, wpbr_b, wout_b, pw_b = _prompt_layer(
            xp, tbl, attn_sinks[l], ng, gain_cols, gain_rows, ones, bkp, ps,
            (w_in[l], w_attn_br[l], w_pool_br[l], w_out[l], pool_w[l].reshape(POOL_WIDTH, POOL_GROUP)))
        tail = (pw_b, ps, wabr_b, wpbr_b, wout_b)
        pk.append(_kv_from_feature_major(k_l))
        pv.append(_kv_from_feature_major(v_l))
        pp.append(jnp.transpose(p_l, (1, 0, 2)))

        xs, k_l, v_l, p_l = _sample_layer(
            xs,
            _kv_to_feature_major(state_attn_k[l]),
            _kv_to_feature_major(state_attn_v[l]),
            jnp.transpose(state_pool[l], (1, 0, 2)),
            past_len, tbl, attn_sinks[l], ng, w_in_b, gain_cols, gain_rows, ones, bks, bkn, *tail)
        sk.append(_kv_from_feature_major(k_l))
        sv.append(_kv_from_feature_major(v_l))
        sp.append(jnp.transpose(p_l, (1, 0, 2)))
    return (xp, xs, jnp.stack(pk), jnp.stack(pv), jnp.stack(pp),
            jnp.stack(sk), jnp.stack(sv), jnp.stack(sp))
```

```python
import functools
import numpy as np

import jax
import jax.numpy as jnp
from jax import lax
from jax.experimental import pallas as pl
from jax.experimental.pallas import tpu as pltpu

D_MODEL = 1024
CHUNK = 64
WINDOW = 128
N_HEADS = 16
N_KV_HEADS = 4
GROUP_REP = N_HEADS // N_KV_HEADS
HEAD_DIM = 64
ATTN_WIDTH = N_HEADS * HEAD_DIM
KV_WIDTH = N_KV_HEADS * HEAD_DIM
POOL_WINDOWS = (2, 4, 8, 16)
POOL_WIDTH = D_MODEL
POOL_GROUP = POOL_WIDTH // len(POOL_WINDOWS)
POOL_HIST = max(POOL_WINDOWS) - 1
HIST_ROWS = POOL_HIST + 1
N_BUCKETS = 32
MAX_DISTANCE = 128
EPS = 1e-6
NEG_INF = -1e30

_SPLITS = (ATTN_WIDTH, KV_WIDTH, KV_WIDTH, ATTN_WIDTH, POOL_WIDTH, POOL_WIDTH, D_MODEL, D_MODEL)
_OFFS = tuple(sum(_SPLITS[:i]) for i in range(len(_SPLITS) + 1))
C_Q, C_K, C_V, C_AG, C_PU, C_PG, C_MA, C_MP = (slice(_OFFS[i], _OFFS[i + 1]) for i in range(8))

LANES = 128
MXU_DIM = 256
PAIR = 2 * CHUNK
PAIR_KEYS = PAIR + WINDOW
TM = 512
PREFILL_K, PREFILL = 2, 8
STAGE_ROWS, STAGE_COLS, N_STAGE = ATTN_WIDTH, TM, 3
VMEM_LIMIT = 60 * 1024 * 1024

BF16 = jnp.bfloat16
F32 = jnp.float32


def _dot(a, b):
    return jnp.dot(a, b, preferred_element_type=F32)


def _dot_nt(a, b):
    return lax.dot_general(a, b, (((1,), (1,)), ((), ())), preferred_element_type=F32)


def _rmsnorm_rows(x, gain):
    y = x * lax.rsqrt(jnp.mean(x * x, axis=-1, keepdims=True) + EPS)
    return y * gain


def _head_rmsnorm(z, ones_ref, gain):
    sq = z * z
    hi = sq.astype(BF16)
    lo = (sq - hi.astype(F32)).astype(BF16)
    ones = ones_ref[...]
    parts = []
    for c in range(z.shape[1] // MXU_DIM):
        sl = slice(c * MXU_DIM, (c + 1) * MXU_DIM)
        parts.append(_dot(hi[:, sl], ones) + _dot(lo[:, sl], ones))
    ss = parts[0] if len(parts) == 1 else jnp.concatenate(parts, axis=1)
    return (z * lax.rsqrt(ss * (1.0 / HEAD_DIM) + EPS)) * gain


def _head_rmsnorm_fm(zt, gain_col, h):
    zh = zt[h * HEAD_DIM:(h + 1) * HEAD_DIM, :]
    ms = jnp.sum(zh * zh, axis=0, keepdims=True) * (1.0 / HEAD_DIM)
    return (zh * lax.rsqrt(ms + EPS)) * gain_col


def _window_sums(full, delay):
    outs = []
    for gi, w in enumerate(POOL_WINDOWS):
        s = full[..., gi * POOL_GROUP:(gi + 1) * POOL_GROUP]
        sh = 1
        while sh < w:
            s = s + delay(s, sh)
            sh *= 2
        outs.append(s)
    return outs


def _delay_rows(a, n):
    return pltpu.roll(a, n, axis=0)


def _delay_leading(a, n):
    return jnp.concatenate([a[-n:], a[:-n]], axis=0)


def _merge_branches(attn_o, pool_mixed, gates, pw_ref, ps_ref, wabr_ref, wpbr_ref, pool_rows=None):
    ag, pg, ma, mp = gates
    po = []
    for gi in range(len(POOL_WINDOWS)):
        po.append(_dot(pool_mixed[gi].astype(BF16), pw_ref[gi * POOL_GROUP:(gi + 1) * POOL_GROUP, :]))
    po = jnp.concatenate(po, axis=1)
    if pool_rows is not None:
        po = pool_rows(po)
    po = po * ps_ref[...]
    p = _dot((po * (pg * jax.nn.sigmoid(pg))).astype(BF16), wpbr_ref[...])
    a = _dot((attn_o * (ag * jax.nn.sigmoid(ag))).astype(BF16), wabr_ref[...])
    acc = jax.nn.sigmoid(ma) * a + jax.nn.sigmoid(mp) * p
    return acc.astype(BF16)


def _diag2(a, axis_r, axis_c):
    z = jnp.zeros_like(a)
    return jnp.concatenate([jnp.concatenate([a, z], axis=axis_c),
                            jnp.concatenate([z, a], axis=axis_c)], axis=axis_r)


def _gather_bias(bucket, tbl_ref, h):
    acc = jnp.zeros(bucket.shape, F32)
    for b in range(N_BUCKETS):
        acc = jnp.where(bucket == b, tbl_ref[h, b], acc)
    return acc


def _t5_bucket(rel):
    assert (N_BUCKETS, MAX_DISTANCE) == (32, 128)
    nb = N_BUCKETS // 2
    max_exact = nb // 2
    n = np.abs(rel)
    log2_sq = np.vectorize(lambda v: max(int(v), 1).bit_length() - 1)(n * n)
    large = np.minimum(max_exact + log2_sq - 6, nb - 1)
    return (np.where(rel > 0, nb, 0) + np.where(n < max_exact, n, large)).astype(np.int32)


def _bucket_tables(n_b, n_s):
    kk = np.arange(PAIR_KEYS)[:, None]
    qq = np.arange(PAIR)[None, :]
    c0 = (qq // CHUNK) * CHUNK
    bkp = np.where((kk >= c0) & (kk < c0 + CHUNK + WINDOW), _t5_bucket(kk - WINDOW - qq), -1).astype(np.int32)
    fr = np.arange(n_s)[:, None]
    bks = _t5_bucket(np.arange(WINDOW)[None, :] - WINDOW - fr)
    bkn = _t5_bucket(np.arange(n_b * n_s)[None, :] % n_s - fr)
    return bkp, bks, bkn


def _weight_jobs(pairs):
    jobs = []
    for src, dst in pairs:
        n_rows, n_cols = dst.shape
        for r0 in range(0, n_rows, STAGE_ROWS):
            for c0 in range(0, n_cols, STAGE_COLS):
                jobs.append((src, dst, r0, c0, min(STAGE_COLS, n_cols - c0)))
    return jobs


def _prompt_kernel(tbl_ref, sinks_ref, x_ref, xn_ref, ng_ref, gcol_ref, grow_ref, ones_ref, bkp_ref, ps_ref,
                   win_hbm, wabr_hbm, wpbr_hbm, wout_hbm, pw_hbm,
                   y_ref, pk_ref, pv_ref, pp_ref, winb_hbm, wabrb_hbm, wpbrb_hbm, woutb_hbm, pwb_hbm,
                   win_ref, wabr_ref, wpbr_ref, wout_ref, pw_ref, stage, sem_in, sem_out,
                   wt_s, h_s, qt_s, k_s, vt_s, attn_s, pbuf, hist_s, bias_s):
    b = pl.program_id(0)
    t = pl.program_id(1)
    first = (b == 0) & (t == 0)
    q_gain = gcol_ref[:, 0:1]
    k_gain = grow_ref[:, ATTN_WIDTH:]
    zq_s = stage.at[0]
    weights = ((win_hbm, win_ref), (wabr_hbm, wabr_ref), (wpbr_hbm, wpbr_ref), (wout_hbm, wout_ref),
               (pw_hbm, pw_ref))
    exports = (winb_hbm, wabrb_hbm, wpbrb_hbm, woutb_hbm, pwb_hbm)

    def export_copy(i):
        return pltpu.make_async_copy(weights[i][1], exports[i], sem_out.at[i])

    @pl.when(first)
    def _():
        jobs = _weight_jobs(weights)

        def load(i):
            src, _, r0, c0, w = jobs[i]
            return pltpu.make_async_copy(src.at[pl.ds(r0, STAGE_ROWS), pl.ds(c0, w)],
                                         stage.at[i % N_STAGE, :, pl.ds(0, w)], sem_in.at[i % N_STAGE])

        for i in range(min(N_STAGE - 1, len(jobs))):
            load(i).start()
        for i in range(len(jobs)):
            if i + N_STAGE - 1 < len(jobs):
                load(i + N_STAGE - 1).start()
            load(i).wait()
            _, dst, r0, c0, w = jobs[i]
            dst[r0:r0 + STAGE_ROWS, c0:c0 + w] = stage[i % N_STAGE, :, 0:w].astype(BF16)
            if i < N_HEADS:
                bkp = bkp_ref[...]
                bias_s[i // GROUP_REP, :, (i % GROUP_REP) * PAIR:(i % GROUP_REP + 1) * PAIR] = jnp.where(
                    bkp < 0, NEG_INF, _gather_bias(bkp, tbl_ref, i))
            if jobs[i][1] is win_ref and c0 + w == C_V.stop:
                wt_s[0:ATTN_WIDTH, :] = win_ref[:, C_Q].T
                wt_s[ATTN_WIDTH:, :] = win_ref[:, C_V].T
                h_s[...] = _rmsnorm_rows(x_ref[...], ng_ref[...]).astype(BF16)
        assert len(jobs) >= N_HEADS and C_V.stop % STAGE_COLS == 0
        zq_s[...] = _dot_nt(wt_s[0:ATTN_WIDTH, :], h_s[...])
        for i in range(len(weights)):
            export_copy(i).start()

    @pl.when(t == 0)
    def _():
        k_s[:, 0:WINDOW, :] = jnp.zeros((2, WINDOW, LANES), BF16)
        vt_s[:, 0:WINDOW] = jnp.zeros((KV_WIDTH, WINDOW), BF16)
        pbuf[0:HIST_ROWS, :] = jnp.zeros((HIST_ROWS, POOL_WIDTH), F32)

    hb = h_s[...]

    n_c = D_MODEL // MXU_DIM
    fill_cols = [slice(c.start + i * MXU_DIM, c.start + (i + 1) * MXU_DIM)
                 for c in (C_PU, C_PG, C_AG, C_MA, C_MP) for i in range(n_c)]
    filled = []

    def fill(upto):
        for cols in fill_cols[len(filled):upto]:
            filled.append(_dot(hb, win_ref[:, cols]))

    zqt = zq_s[...]
    zk = _dot(hb, win_ref[:, C_K])
    vt = _dot_nt(wt_s[ATTN_WIDTH:, :], hb)
    fill(PREFILL_K)
    k = _head_rmsnorm(zk, ones_ref, k_gain)
    fill(PREFILL)
    for h in range(N_HEADS):
        qt_s[h * HEAD_DIM:(h + 1) * HEAD_DIM, :] = _head_rmsnorm_fm(zqt, q_gain, h).astype(BF16)

    pk_ref[...] = k[TM - WINDOW:, :].T
    pv_ref[...] = vt[:, TM - WINDOW:]

    k_s[0, WINDOW:, :] = k[:, :LANES].astype(BF16)
    k_s[1, WINDOW:, :] = k[:, LANES:].astype(BF16)
    vt_s[:, WINDOW:] = vt.astype(BF16)

    zeros_half = jnp.zeros((HEAD_DIM, PAIR), BF16)
    q_lane = lax.broadcasted_iota(jnp.int32, (1, GROUP_REP * PAIR), 1)

    def scores(p, g):
        tok = slice(p * PAIR, (p + 1) * PAIR)
        keys = slice(p * PAIR, p * PAIR + PAIR_KEYS)
        cols = []
        for r in range(GROUP_REP):
            h = g * GROUP_REP + r
            qh = qt_s[h * HEAD_DIM:(h + 1) * HEAD_DIM, tok]
            cols.append(jnp.concatenate([qh, zeros_half] if g % 2 == 0 else [zeros_half, qh], axis=0))
        rhs = jnp.concatenate(cols, axis=1)
        s = _dot(k_s[g // 2, keys, :], rhs) + bias_s[g]
        if p == 0:
            kk = lax.broadcasted_iota(jnp.int32, s.shape, 0)
            s = jnp.where((kk >= WINDOW) | (t > 0), s, NEG_INF)
        return s

    def finish(p, g, s):
        tok = slice(p * PAIR, (p + 1) * PAIR)
        keys = slice(p * PAIR, p * PAIR + PAIR_KEYS)
        sink = sinks_ref[g * GROUP_REP]
        for r in range(1, GROUP_REP):
            sink = jnp.where(q_lane < r * PAIR, sink, sinks_ref[g * GROUP_REP + r])
        m = jnp.maximum(jnp.max(s, axis=0, keepdims=True), sink)
        e = jnp.exp(s - m)
        denom = jnp.sum(e, axis=0, keepdims=True) + jnp.exp(sink - m)
        vg = vt_s[g * HEAD_DIM:(g + 1) * HEAD_DIM, keys]
        o_t = _dot(vg, e.astype(BF16)) * (1.0 / denom)
        c = g * MXU_DIM
        attn_s[tok, c:c + LANES] = jnp.concatenate(
            [o_t[:, 0:PAIR], o_t[:, PAIR:2 * PAIR]], axis=0).T
        attn_s[tok, c + LANES:c + MXU_DIM] = jnp.concatenate(
            [o_t[:, 2 * PAIR:3 * PAIR], o_t[:, 3 * PAIR:]], axis=0).T

    pu = jnp.concatenate(filled[:n_c], axis=1)
    pbuf[HIST_ROWS:, :] = pu
    hist_s[b] = pu[TM - HIST_ROWS:, :]
    sums = _window_sums(pbuf[...], _delay_rows)
    pos1 = (t * TM + 1 + lax.broadcasted_iota(jnp.int32, (TM, 1), 0)).astype(F32)
    mixed = []
    for gi, w in enumerate(POOL_WINDOWS):
        inv_cnt = 1.0 / jnp.minimum(float(w), pos1)
        mixed.append(sums[gi][HIST_ROWS:, :] * inv_cnt - pu[:, gi * POOL_GROUP:(gi + 1) * POOL_GROUP])

    blocks = [(p, g) for p in range(TM // PAIR) for g in range(N_KV_HEADS)]
    s_next = scores(*blocks[0])
    for i, blk in enumerate(blocks):
        s_cur = s_next
        if i + 1 < len(blocks):
            s_next = scores(*blocks[i + 1])
        fill(PREFILL + (len(fill_cols) - PREFILL) * (i + 1) // len(blocks))
        finish(*blk, s_cur)
    pg, ag, ma, mp = (jnp.concatenate(filled[j * n_c:(j + 1) * n_c], axis=1) for j in range(1, 5))

    merged = _merge_branches(attn_s[...], mixed, (ag, pg, ma, mp), pw_ref, ps_ref, wabr_ref, wpbr_ref)

    h_s[...] = _rmsnorm_rows(xn_ref[...], ng_ref[...]).astype(BF16)
    zq_s[...] = _dot_nt(wt_s[0:ATTN_WIDTH, :], h_s[...])

    y_ref[...] = x_ref[...] + _dot(merged, wout_ref[...])

    k_s[:, 0:WINDOW, :] = k_s[:, TM:TM + WINDOW, :]
    vt_s[:, 0:WINDOW] = vt_s[:, TM:TM + WINDOW]
    pbuf[0:HIST_ROWS, :] = pbuf[TM:TM + HIST_ROWS, :]

    @pl.when(first)
    def _():
        for i in range(len(weights)):
            export_copy(i).wait()

    @pl.when((b == pl.num_programs(0) - 1) & (t == pl.num_programs(1) - 1))
    def _():
        pp_ref[...] = jnp.swapaxes(hist_s[...], 0, 1)[HIST_ROWS - POOL_HIST:]


def _resident(shape):
    nd = len(shape)
    return pl.BlockSpec(shape, lambda b, t: (0,) * nd, pipeline_mode=pl.Buffered(1))


def _prompt_layer(x, tbl, sinks, ng, gain_cols, gain_rows, ones, bkp, ps, weights):
    B, S, _ = x.shape
    assert S % TM == 0 and TM % PAIR == 0 and TM >= WINDOW
    assert all(w.shape[0] % STAGE_ROWS == 0 for w in weights)
    consts = (ng, gain_cols, gain_rows, ones, bkp, ps)
    hbm = pl.BlockSpec(memory_space=pl.ANY)
    n_t = S // TM

    def next_tile(b, t):
        i = jnp.minimum(b * n_t + t + 1, B * n_t - 1)
        return i // n_t, i % n_t, 0

    return pl.pallas_call(
        _prompt_kernel,
        grid=(B, S // TM),
        in_specs=[pl.BlockSpec(memory_space=pltpu.SMEM),
                  pl.BlockSpec(memory_space=pltpu.SMEM),
                  pl.BlockSpec((None, TM, D_MODEL), lambda b, t: (b, t, 0)),
                  pl.BlockSpec((None, TM, D_MODEL), next_tile)]
                 + [_resident(c.shape) for c in consts] + [hbm] * len(weights),
        out_specs=[
            pl.BlockSpec((None, TM, D_MODEL), lambda b, t: (b, t, 0)),
            pl.BlockSpec((None, KV_WIDTH, WINDOW), lambda b, t: (b, 0, 0)),
            pl.BlockSpec((None, KV_WIDTH, WINDOW), lambda b, t: (b, 0, 0)),
            pl.BlockSpec((POOL_HIST, B, POOL_WIDTH), lambda b, t: (0, 0, 0)),
        ] + [hbm] * len(weights),
        out_shape=[
            jax.ShapeDtypeStruct((B, S, D_MODEL), F32),
            jax.ShapeDtypeStruct((B, KV_WIDTH, WINDOW), F32),
            jax.ShapeDtypeStruct((B, KV_WIDTH, WINDOW), F32),
            jax.ShapeDtypeStruct((POOL_HIST, B, POOL_WIDTH), F32),
        ] + [jax.ShapeDtypeStruct(w.shape, BF16) for w in weights],
        scratch_shapes=[pltpu.VMEM(w.shape, BF16) for w in weights] + [
            pltpu.VMEM((N_STAGE, STAGE_ROWS, STAGE_COLS), F32),
            pltpu.SemaphoreType.DMA((N_STAGE,)),
            pltpu.SemaphoreType.DMA((len(weights),)),
            pltpu.VMEM((ATTN_WIDTH + KV_WIDTH, D_MODEL), BF16),
            pltpu.VMEM((TM, D_MODEL), BF16),
            pltpu.VMEM((ATTN_WIDTH, TM), BF16),
            pltpu.VMEM((2, WINDOW + TM, LANES), BF16),
            pltpu.VMEM((KV_WIDTH, WINDOW + TM), BF16),
            pltpu.VMEM((TM, ATTN_WIDTH), F32),
            pltpu.VMEM((HIST_ROWS + TM, POOL_WIDTH), F32),
            pltpu.VMEM((B, HIST_ROWS, POOL_WIDTH), F32),
            pltpu.VMEM((N_KV_HEADS, PAIR_KEYS, GROUP_REP * PAIR), F32),
        ],
        compiler_params=pltpu.CompilerParams(
            dimension_semantics=("arbitrary", "arbitrary"),
            vmem_limit_bytes=VMEM_LIMIT,
        ),
        name="prompt_layer",
    )(tbl, sinks, x, x, *consts, *weights)


def _sample_kernel(n_b, n_s, start,
                   tbl_ref, sinks_ref, x_ref, skt_ref, svt_ref, sp_ref, ng_ref,
                   gcol_ref, grow_ref, ones_ref, bks_ref, bkn_ref, ps_ref,
                   win_hbm, pw_hbm, wabr_hbm, wpbr_hbm, wout_hbm,
                   y_ref, ok_ref, ov_ref, op_ref,
                   win_ref, pw_ref, wabr_ref, wpbr_ref, wout_ref, sem):
    n_tok = n_b * n_s
    qkv = slice(0, C_V.stop)
    rest = slice(C_V.stop, win_ref.shape[1])
    loads = [pltpu.make_async_copy(win_hbm.at[:, qkv], win_ref.at[:, qkv], sem.at[0]),
             pltpu.make_async_copy(win_hbm.at[:, rest], win_ref.at[:, rest], sem.at[1])]
    loads += [pltpu.make_async_copy(src, dst, sem.at[2 + i]) for i, (src, dst) in enumerate(
        ((pw_hbm, pw_ref), (wabr_hbm, wabr_ref), (wpbr_hbm, wpbr_ref), (wout_hbm, wout_ref)))]
    for c in loads:
        c.start()

    x = x_ref[...]
    hb = _rmsnorm_rows(x, ng_ref[...]).astype(BF16)

    loads[0].wait()
    q = _head_rmsnorm(_dot(hb, win_ref[:, C_Q]), ones_ref, grow_ref[:, :ATTN_WIDTH]).astype(BF16)
    q3 = q.reshape(n_b, n_s, ATTN_WIDTH)
    zkt = _dot_nt(win_ref[:, C_K].T, hb)
    k_gain = gcol_ref[:, 1:2]
    knt = jnp.concatenate([_head_rmsnorm_fm(zkt, k_gain, g) for g in range(N_KV_HEADS)], axis=0)
    vnt = _dot_nt(win_ref[:, C_V].T, hb)

    lane = lax.broadcasted_iota(jnp.int32, (1, WINDOW), 1)
    per_tile = LANES // n_s
    for s in range(n_b):
        tile = slice((s // per_tile) * LANES, (s // per_tile + 1) * LANES)
        shift = (WINDOW - n_s - (s % per_tile) * n_s) % LANES
        for new, st_ref, o_ref in ((knt, skt_ref, ok_ref), (vnt, svt_ref, ov_ref)):
            fresh = new[:, tile] if shift == 0 else pltpu.roll(new[:, tile], shift, axis=1)
            o_ref[s] = jnp.where(lane < WINDOW - n_s, pltpu.roll(st_ref[s], WINDOW - n_s, axis=1), fresh)

    knb = knt.astype(BF16)
    vnb = vnt.astype(BF16)
    key_stream = (lax.broadcasted_iota(jnp.int32, (n_b, 1, 2 * n_tok), 2) & (n_tok - 1)) >> (n_s.bit_length() - 1)
    own = key_stream == lax.broadcasted_iota(jnp.int32, (n_b, 1, 2 * n_tok), 0)
    top = lax.broadcasted_iota(jnp.int32, (1, 2 * n_s, 1), 1) < n_s
    low = lax.broadcasted_iota(jnp.int32, (1, 1, LANES), 2) < HEAD_DIM

    def bias_rows(bucket, g):
        return jnp.concatenate([jnp.concatenate(
            [_gather_bias(bucket, tbl_ref, g * GROUP_REP + 2 * c + j) for j in range(2)], axis=1)
            for c in range(2)], axis=0)

    def scores(g):
        hd = slice(g * HEAD_DIM, (g + 1) * HEAD_DIM)
        c = g * MXU_DIM
        lhs = jnp.concatenate([q3[:, :, c:c + LANES], q3[:, :, c + LANES:c + MXU_DIM]], axis=1)
        kst = _diag2(skt_ref[:, hd, :].astype(BF16), 1, 2)
        s_st = jnp.stack([_dot(lhs[s], kst[s]) for s in range(n_b)]) + bias_rows(bks_ref[...], g)[None]
        s_nw = _dot(lhs.reshape(2 * n_tok, LANES), _diag2(knb[hd, :], 0, 1))
        s_nw = jnp.where(own, s_nw.reshape(n_b, 2 * n_s, 2 * n_tok) + bias_rows(bkn_ref[...], g)[None], NEG_INF)
        return s_st, s_nw

    def finish(g, s_st, s_nw):
        hd = slice(g * HEAD_DIM, (g + 1) * HEAD_DIM)
        p_st, p_nw, inv = [], [], []
        for j in range(2):
            a = s_st[:, :, j * WINDOW:(j + 1) * WINDOW]
            b = s_nw[:, :, j * n_tok:(j + 1) * n_tok]
            sink = jnp.where(top, sinks_ref[4 * g + j], sinks_ref[4 * g + 2 + j])
            m = jnp.maximum(jnp.maximum(jnp.max(a, axis=-1, keepdims=True),
                                        jnp.max(b, axis=-1, keepdims=True)), sink)
            ea = jnp.exp(a - m)
            eb = jnp.exp(b - m)
            inv.append(1.0 / (jnp.sum(ea, axis=-1, keepdims=True) + jnp.sum(eb, axis=-1, keepdims=True)
                              + jnp.exp(sink - m)))
            p_st.append(ea.astype(BF16))
            p_nw.append(eb.astype(BF16))
        p_st = jnp.concatenate(p_st, axis=2)
        p_nw = jnp.concatenate(p_nw, axis=2).reshape(2 * n_tok, 2 * n_tok)
        vst = _diag2(svt_ref[:, hd, :].astype(BF16), 1, 2)
        o = jnp.stack([_dot_nt(p_st[s], vst[s]) for s in range(n_b)])
        o = o + _dot_nt(p_nw, _diag2(vnb[hd, :], 0, 1)).reshape(n_b, 2 * n_s, LANES)
        o = o * jnp.where(low, inv[0], inv[1])
        return jnp.concatenate([o[:, :n_s, :], o[:, n_s:, :]], axis=2)

    outs = []
    nxt = scores(0)
    for g in range(N_KV_HEADS):
        cur = nxt
        if g + 1 < N_KV_HEADS:
            nxt = scores(g + 1)
        outs.append(finish(g, *cur))
    attn_o = jnp.concatenate(outs, axis=2).reshape(n_tok, ATTN_WIDTH)

    for c in loads[1:]:
        c.wait()
    pu = _dot(hb, win_ref[:, C_PU])
    pu_t = jnp.swapaxes(pu.reshape(n_b, n_s, POOL_WIDTH), 0, 1)
    full = jnp.concatenate([jnp.zeros((1, n_b, POOL_WIDTH), F32), sp_ref[...], pu_t], axis=0)
    op_ref[...] = full[HIST_ROWS + n_s - POOL_HIST:]
    sums = _window_sums(full, _delay_leading)
    pos1 = (start + 1 + lax.broadcasted_iota(jnp.int32, (n_s, 1, 1), 0)).astype(F32)
    mixed = []
    for gi, w in enumerate(POOL_WINDOWS):
        inv_cnt = 1.0 / jnp.minimum(float(w), pos1)
        sw = sums[gi][HIST_ROWS:] * inv_cnt - pu_t[..., gi * POOL_GROUP:(gi + 1) * POOL_GROUP]
        mixed.append(sw.reshape(n_tok, POOL_GROUP))

    def stream_major(po):
        return jnp.swapaxes(po.reshape(n_s, n_b, POOL_WIDTH), 0, 1).reshape(n_tok, POOL_WIDTH)

    gates = tuple(_dot(hb, win_ref[:, c]) for c in (C_AG, C_PG, C_MA, C_MP))
    merged = _merge_branches(attn_o, mixed, gates, pw_ref, ps_ref, wabr_ref, wpbr_ref, stream_major)
    y_ref[...] = x + _dot(merged, wout_ref[...])


def _sample_layer(x, skt, svt, sp, start, tbl, sinks, ng, w_in, gain_cols, gain_rows, ones, bks, bkn,
                  pw, ps, wabr, wpbr, wout):
    n_b, n_s, _ = x.shape
    n_tok = n_b * n_s
    assert n_s % 16 == 0 and LANES % n_s == 0 and n_s >= POOL_HIST and n_tok % LANES == 0
    assert n_s & (n_s - 1) == 0 and n_tok & (n_tok - 1) == 0
    vmem = pl.BlockSpec(memory_space=pltpu.VMEM)
    weights = (w_in, pw, wabr, wpbr, wout)
    y, ok, ov, op = pl.pallas_call(
        functools.partial(_sample_kernel, n_b, n_s, start),
        in_specs=[pl.BlockSpec(memory_space=pltpu.SMEM)] * 2 + [vmem] * 11
                 + [pl.BlockSpec(memory_space=pl.ANY)] * len(weights),
        out_specs=[vmem] * 4,
        out_shape=[
            jax.ShapeDtypeStruct((n_tok, D_MODEL), F32),
            jax.ShapeDtypeStruct((n_b, KV_WIDTH, WINDOW), F32),
            jax.ShapeDtypeStruct((n_b, KV_WIDTH, WINDOW), F32),
            jax.ShapeDtypeStruct((POOL_HIST, n_b, POOL_WIDTH), F32),
        ],
        scratch_shapes=[pltpu.VMEM(w.shape, w.dtype) for w in weights]
                       + [pltpu.SemaphoreType.DMA((len(weights) + 1,))],
        compiler_params=pltpu.CompilerParams(vmem_limit_bytes=VMEM_LIMIT),
        name="sample_layer",
    )(tbl, sinks, x.reshape(n_tok, D_MODEL), skt, svt, sp, ng, gain_cols, gain_rows, ones, bks, bkn, ps,
      *weights)
    return y.reshape(n_b, n_s, D_MODEL), ok, ov, op


def _kv_to_feature_major(a):
    return jnp.transpose(a, (0, 2, 3, 1)).reshape(a.shape[0], KV_WIDTH, WINDOW)


def _kv_from_feature_major(a):
    return jnp.transpose(a.reshape(a.shape[0], N_KV_HEADS, HEAD_DIM, WINDOW), (0, 3, 1, 2))


def kernel(x_prompt, x_sample, state_attn_k, state_attn_v, state_pool, norm_gain, w_in, q_norm_gain, k_norm_gain, attn_sinks, rel_bias, pool_w, pool_scale, w_attn_br, w_pool_br, w_out):
    depth = w_in.shape[0]
    n_b, n_s, _ = x_sample.shape
    past_len = 4096
    bkp, bks, bkn = _bucket_tables(n_b, n_s)
    tbl = rel_bias.T
    seg = np.arange(MXU_DIM) // HEAD_DIM
    ones = jnp.asarray(seg[:, None] == seg[None, :], BF16)

    xp, xs = x_prompt, x_sample
    pk, pv, pp, sk, sv, sp = [], [], [], [], [], []
    for l in range(depth):
        ng = norm_gain[l].reshape(1, D_MODEL)
        qg = q_norm_gain[l] * (HEAD_DIM ** -0.5)
        kg = k_norm_gain[l]
        gain_cols = jnp.stack([qg, kg], axis=1)
        gain_rows = jnp.concatenate([jnp.tile(qg, N_HEADS), jnp.tile(kg, N_KV_HEADS)]).reshape(1, -1)
        ps = pool_scale[l].reshape(1, POOL_WIDTH)
        xp, k_l, v_l, p_l, w_in_b, wabr_b, wpbr_b, wout_b, pw_b = _prompt_layer(
            xp, tbl, attn_sinks[l], ng, gain_cols, gain_rows, ones, bkp, ps,
            (w_in[l], w_attn_br[l], w_pool_br[l], w_out[l], pool_w[l].reshape(POOL_WIDTH, POOL_GROUP)))
        tail = (pw_b, ps, wabr_b, wpbr_b, wout_b)
        pk.append(_kv_from_feature_major(k_l))
        pv.append(_kv_from_feature_major(v_l))
        pp.append(jnp.transpose(p_l, (1, 0, 2)))

        xs, k_l, v_l, p_l = _sample_layer(
            xs,
            _kv_to_feature_major(state_attn_k[l]),
            _kv_to_feature_major(state_attn_v[l]),
            jnp.transpose(state_pool[l], (1, 0, 2)),
            past_len, tbl, attn_sinks[l], ng, w_in_b, gain_cols, gain_rows, ones, bks, bkn, *tail)
        sk.append(_kv_from_feature_major(k_l))
        sv.append(_kv_from_feature_major(v_l))
        sp.append(jnp.transpose(p_l, (1, 0, 2)))
    return (xp, xs, jnp.stack(pk), jnp.stack(pv), jnp.stack(pp),
            jnp.stack(sk), jnp.stack(sv), jnp.stack(sp))
```

```python
import functools
import numpy as np

import jax
import jax.numpy as jnp
from jax import lax
from jax.experimental import pallas as pl
from jax.experimental.pallas import tpu as pltpu

D_MODEL = 1024
CHUNK = 64
WINDOW = 128
N_HEADS = 16
N_KV_HEADS = 4
GROUP_REP = N_HEADS // N_KV_HEADS
HEAD_DIM = 64
ATTN_WIDTH = N_HEADS * HEAD_DIM
KV_WIDTH = N_KV_HEADS * HEAD_DIM
POOL_WINDOWS = (2, 4, 8, 16)
POOL_WIDTH = D_MODEL
POOL_GROUP = POOL_WIDTH // len(POOL_WINDOWS)
POOL_HIST = max(POOL_WINDOWS) - 1
HIST_ROWS = POOL_HIST + 1
N_BUCKETS = 32
MAX_DISTANCE = 128
EPS = 1e-6
NEG_INF = -1e30

_SPLITS = (ATTN_WIDTH, KV_WIDTH, KV_WIDTH, ATTN_WIDTH, POOL_WIDTH, POOL_WIDTH, D_MODEL, D_MODEL)
_OFFS = tuple(sum(_SPLITS[:i]) for i in range(len(_SPLITS) + 1))
C_Q, C_K, C_V, C_AG, C_PU, C_PG, C_MA, C_MP = (slice(_OFFS[i], _OFFS[i + 1]) for i in range(8))

LANES = 128
MXU_DIM = 256
PAIR = 2 * CHUNK
PAIR_KEYS = PAIR + WINDOW
TM = 512
PREFILL = 8
STAGE_ROWS, STAGE_COLS, N_STAGE = ATTN_WIDTH, TM, 3
VMEM_LIMIT = 60 * 1024 * 1024

BF16 = jnp.bfloat16
F32 = jnp.float32


def _dot(a, b):
    return jnp.dot(a, b, preferred_element_type=F32)


def _dot_nt(a, b):
    return lax.dot_general(a, b, (((1,), (1,)), ((), ())), preferred_element_type=F32)


def _rmsnorm_rows(x, gain):
    y = x * lax.rsqrt(jnp.mean(x * x, axis=-1, keepdims=True) + EPS)
    return y * gain


def _head_rmsnorm(z, ones_ref, gain):
    sq = z * z
    hi = sq.astype(BF16)
    lo = (sq - hi.astype(F32)).astype(BF16)
    ones = ones_ref[...]
    parts = []
    for c in range(z.shape[1] // MXU_DIM):
        sl = slice(c * MXU_DIM, (c + 1) * MXU_DIM)
        parts.append(_dot(hi[:, sl], ones) + _dot(lo[:, sl], ones))
    ss = parts[0] if len(parts) == 1 else jnp.concatenate(parts, axis=1)
    return (z * lax.rsqrt(ss * (1.0 / HEAD_DIM) + EPS)) * gain


def _head_rmsnorm_fm(zt, gain_col, h):
    zh = zt[h * HEAD_DIM:(h + 1) * HEAD_DIM, :]
    ms = jnp.sum(zh * zh, axis=0, keepdims=True) * (1.0 / HEAD_DIM)
    return (zh * lax.rsqrt(ms + EPS)) * gain_col


def _window_sums(full, delay):
    outs = []
    for gi, w in enumerate(POOL_WINDOWS):
        s = full[..., gi * POOL_GROUP:(gi + 1) * POOL_GROUP]
        sh = 1
        while sh < w:
            s = s + delay(s, sh)
            sh *= 2
        outs.append(s)
    return outs


def _delay_rows(a, n):
    return pltpu.roll(a, n, axis=0)


def _delay_leading(a, n):
    return jnp.concatenate([a[-n:], a[:-n]], axis=0)


def _merge_branches(attn_o, pool_mixed, gates, pw_ref, ps_ref, wabr_ref, wpbr_ref, pool_rows=None):
    ag, pg, ma, mp = gates
    po = []
    for gi in range(len(POOL_WINDOWS)):
        po.append(_dot(pool_mixed[gi].astype(BF16), pw_ref[gi * POOL_GROUP:(gi + 1) * POOL_GROUP, :]))
    po = jnp.concatenate(po, axis=1)
    if pool_rows is not None:
        po = pool_rows(po)
    po = po * ps_ref[...]
    p = _dot((po * (pg * jax.nn.sigmoid(pg))).astype(BF16), wpbr_ref[...])
    a = _dot((attn_o * (ag * jax.nn.sigmoid(ag))).astype(BF16), wabr_ref[...])
    acc = jax.nn.sigmoid(ma) * a + jax.nn.sigmoid(mp) * p
    return acc.astype(BF16)


def _diag2(a, axis_r, axis_c):
    z = jnp.zeros_like(a)
    return jnp.concatenate([jnp.concatenate([a, z], axis=axis_c),
                            jnp.concatenate([z, a], axis=axis_c)], axis=axis_r)


def _gather_bias(bucket, tbl_ref, h):
    acc = jnp.zeros(bucket.shape, F32)
    for b in range(N_BUCKETS):
        acc = jnp.where(bucket == b, tbl_ref[h, b], acc)
    return acc


def _t5_bucket(rel):
    assert (N_BUCKETS, MAX_DISTANCE) == (32, 128)
    nb = N_BUCKETS // 2
    max_exact = nb // 2
    n = np.abs(rel)
    log2_sq = np.vectorize(lambda v: max(int(v), 1).bit_length() - 1)(n * n)
    large = np.minimum(max_exact + log2_sq - 6, nb - 1)
    return (np.where(rel > 0, nb, 0) + np.where(n < max_exact, n, large)).astype(np.int32)


def _bucket_tables(n_b, n_s):
    kk = np.arange(PAIR_KEYS)[:, None]
    qq = np.arange(PAIR)[None, :]
    c0 = (qq // CHUNK) * CHUNK
    bkp = np.where((kk >= c0) & (kk < c0 + CHUNK + WINDOW), _t5_bucket(kk - WINDOW - qq), -1).astype(np.int32)
    fr = np.arange(n_s)[:, None]
    bks = _t5_bucket(np.arange(WINDOW)[None, :] - WINDOW - fr)
    bkn = _t5_bucket(np.arange(n_b * n_s)[None, :] % n_s - fr)
    return bkp, bks, bkn


def _weight_jobs(pairs):
    jobs = []
    for src, dst in pairs:
        n_rows, n_cols = dst.shape
        for r0 in range(0, n_rows, STAGE_ROWS):
            for c0 in range(0, n_cols, STAGE_COLS):
                jobs.append((src, dst, r0, c0, min(STAGE_COLS, n_cols - c0)))
    return jobs


def _prompt_kernel(tbl_ref, sinks_ref, x_ref, xn_ref, ng_ref, gcol_ref, bkp_ref, ps_ref,
                   win_hbm, wabr_hbm, wpbr_hbm, wout_hbm, pw_hbm,
                   y_ref, pk_ref, pv_ref, pp_ref, winb_hbm, wabrb_hbm, wpbrb_hbm, woutb_hbm, pwb_hbm,
                   win_ref, wabr_ref, wpbr_ref, wout_ref, pw_ref, stage, sem_in, sem_out,
                   wt_s, h_s, qt_s, k_s, vt_s, attn_s, pbuf, hist_s, bias_s):
    b = pl.program_id(0)
    t = pl.program_id(1)
    first = (b == 0) & (t == 0)
    q_gain = gcol_ref[:, 0:1]
    k_gain = gcol_ref[:, 1:2]
    zq_s = stage.at[0]
    weights = ((win_hbm, win_ref), (wabr_hbm, wabr_ref), (wpbr_hbm, wpbr_ref), (wout_hbm, wout_ref),
               (pw_hbm, pw_ref))
    exports = (winb_hbm, wabrb_hbm, wpbrb_hbm, woutb_hbm, pwb_hbm)

    def export_copy(i):
        return pltpu.make_async_copy(weights[i][1], exports[i], sem_out.at[i])

    @pl.when(first)
    def _():
        jobs = _weight_jobs(weights)

        def load(i):
            src, _, r0, c0, w = jobs[i]
            return pltpu.make_async_copy(src.at[pl.ds(r0, STAGE_ROWS), pl.ds(c0, w)],
                                         stage.at[i % N_STAGE, :, pl.ds(0, w)], sem_in.at[i % N_STAGE])

        for i in range(min(N_STAGE - 1, len(jobs))):
            load(i).start()
        for i in range(len(jobs)):
            if i + N_STAGE - 1 < len(jobs):
                load(i + N_STAGE - 1).start()
            load(i).wait()
            _, dst, r0, c0, w = jobs[i]
            dst[r0:r0 + STAGE_ROWS, c0:c0 + w] = stage[i % N_STAGE, :, 0:w].astype(BF16)
            if i < N_HEADS:
                bkp = bkp_ref[...]
                bias_s[i // GROUP_REP, :, (i % GROUP_REP) * PAIR:(i % GROUP_REP + 1) * PAIR] = jnp.where(
                    bkp < 0, NEG_INF, _gather_bias(bkp, tbl_ref, i))
            if jobs[i][1] is win_ref and c0 + w == C_V.stop:
                wt_s[...] = win_ref[:, 0:C_V.stop].T
                h_s[...] = _rmsnorm_rows(x_ref[...], ng_ref[...]).astype(BF16)
        assert len(jobs) >= N_HEADS and C_V.stop % STAGE_COLS == 0
        zq_s[...] = _dot_nt(wt_s[C_Q, :], h_s[...])
        for i in range(len(weights)):
            export_copy(i).start()

    @pl.when(t == 0)
    def _():
        k_s[:, 0:WINDOW, :] = jnp.zeros((2, WINDOW, LANES), BF16)
        vt_s[:, 0:WINDOW] = jnp.zeros((KV_WIDTH, WINDOW), BF16)
        pbuf[0:HIST_ROWS, :] = jnp.zeros((HIST_ROWS, POOL_WIDTH), F32)

    hb = h_s[...]

    n_c = D_MODEL // MXU_DIM
    fill_cols = [slice(c.start + i * MXU_DIM, c.start + (i + 1) * MXU_DIM)
                 for c in (C_PU, C_PG, C_AG, C_MA, C_MP) for i in range(n_c)]
    filled = []

    def fill(upto):
        for cols in fill_cols[len(filled):upto]:
            filled.append(_dot(hb, win_ref[:, cols]))

    zqt = zq_s[...]
    zkt = _dot_nt(wt_s[C_K, :], hb)
    vt = _dot_nt(wt_s[C_V, :], hb)
    fill(PREFILL)
    kt = jnp.concatenate([_head_rmsnorm_fm(zkt, k_gain, g) for g in range(N_KV_HEADS)], axis=0)
    for h in range(N_HEADS):
        qt_s[h * HEAD_DIM:(h + 1) * HEAD_DIM, :] = _head_rmsnorm_fm(zqt, q_gain, h).astype(BF16)

    pk_ref[...] = kt[:, TM - WINDOW:]
    pv_ref[...] = vt[:, TM - WINDOW:]

    k_s[0, WINDOW:, :] = kt[:LANES, :].T.astype(BF16)
    k_s[1, WINDOW:, :] = kt[LANES:, :].T.astype(BF16)
    vt_s[:, WINDOW:] = vt.astype(BF16)

    zeros_half = jnp.zeros((HEAD_DIM, PAIR), BF16)
    q_lane = lax.broadcasted_iota(jnp.int32, (1, GROUP_REP * PAIR), 1)

    def scores(p, g):
        tok = slice(p * PAIR, (p + 1) * PAIR)
        keys = slice(p * PAIR, p * PAIR + PAIR_KEYS)
        cols = []
        for r in range(GROUP_REP):
            h = g * GROUP_REP + r
            qh = qt_s[h * HEAD_DIM:(h + 1) * HEAD_DIM, tok]
            cols.append(jnp.concatenate([qh, zeros_half] if g % 2 == 0 else [zeros_half, qh], axis=0))
        rhs = jnp.concatenate(cols, axis=1)
        s = _dot(k_s[g // 2, keys, :], rhs) + bias_s[g]
        if p == 0:
            kk = lax.broadcasted_iota(jnp.int32, s.shape, 0)
            s = jnp.where((kk >= WINDOW) | (t > 0), s, NEG_INF)
        return s

    def finish(p, g, s):
        tok = slice(p * PAIR, (p + 1) * PAIR)
        keys = slice(p * PAIR, p * PAIR + PAIR_KEYS)
        sink = sinks_ref[g * GROUP_REP]
        for r in range(1, GROUP_REP):
            sink = jnp.where(q_lane < r * PAIR, sink, sinks_ref[g * GROUP_REP + r])
        m = jnp.maximum(jnp.max(s, axis=0, keepdims=True), sink)
        e = jnp.exp(s - m)
        denom = jnp.sum(e, axis=0, keepdims=True) + jnp.exp(sink - m)
        vg = vt_s[g * HEAD_DIM:(g + 1) * HEAD_DIM, keys]
        o_t = _dot(vg, e.astype(BF16)) * (1.0 / denom)
        c = g * MXU_DIM
        attn_s[tok, c:c + LANES] = jnp.concatenate(
            [o_t[:, 0:PAIR], o_t[:, PAIR:2 * PAIR]], axis=0).T
        attn_s[tok, c + LANES:c + MXU_DIM] = jnp.concatenate(
            [o_t[:, 2 * PAIR:3 * PAIR], o_t[:, 3 * PAIR:]], axis=0).T

    pu = jnp.concatenate(filled[:n_c], axis=1)
    pbuf[HIST_ROWS:, :] = pu
    hist_s[b] = pu[TM - HIST_ROWS:, :]
    sums = _window_sums(pbuf[...], _delay_rows)
    pos1 = (t * TM + 1 + lax.broadcasted_iota(jnp.int32, (TM, 1), 0)).astype(F32)
    mixed = []
    for gi, w in enumerate(POOL_WINDOWS):
        inv_cnt = 1.0 / jnp.minimum(float(w), pos1)
        mixed.append(sums[gi][HIST_ROWS:, :] * inv_cnt - pu[:, gi * POOL_GROUP:(gi + 1) * POOL_GROUP])

    blocks = [(p, g) for p in range(TM // PAIR) for g in range(N_KV_HEADS)]
    s_next = scores(*blocks[0])
    for i, blk in enumerate(blocks):
        s_cur = s_next
        if i + 1 < len(blocks):
            s_next = scores(*blocks[i + 1])
        fill(PREFILL + (len(fill_cols) - PREFILL) * (i + 1) // len(blocks))
        finish(*blk, s_cur)
    pg, ag, ma, mp = (jnp.concatenate(filled[j * n_c:(j + 1) * n_c], axis=1) for j in range(1, 5))

    merged = _merge_branches(attn_s[...], mixed, (ag, pg, ma, mp), pw_ref, ps_ref, wabr_ref, wpbr_ref)

    h_s[...] = _rmsnorm_rows(xn_ref[...], ng_ref[...]).astype(BF16)
    zq_s[...] = _dot_nt(wt_s[C_Q, :], h_s[...])

    y_ref[...] = x_ref[...] + _dot(merged, wout_ref[...])

    k_s[:, 0:WINDOW, :] = k_s[:, TM:TM + WINDOW, :]
    vt_s[:, 0:WINDOW] = vt_s[:, TM:TM + WINDOW]
    pbuf[0:HIST_ROWS, :] = pbuf[TM:TM + HIST_ROWS, :]

    @pl.when(first)
    def _():
        for i in range(len(weights)):
            export_copy(i).wait()

    @pl.when((b == pl.num_programs(0) - 1) & (t == pl.num_programs(1) - 1))
    def _():
        pp_ref[...] = jnp.swapaxes(hist_s[...], 0, 1)[HIST_ROWS - POOL_HIST:]


def _resident(shape):
    nd = len(shape)
    return pl.BlockSpec(shape, lambda b, t: (0,) * nd, pipeline_mode=pl.Buffered(1))


def _prompt_layer(x, tbl, sinks, ng, gain_cols, bkp, ps, weights):
    B, S, _ = x.shape
    assert S % TM == 0 and TM % PAIR == 0 and TM >= WINDOW
    assert all(w.shape[0] % STAGE_ROWS == 0 for w in weights)
    consts = (ng, gain_cols, bkp, ps)
    hbm = pl.BlockSpec(memory_space=pl.ANY)
    n_t = S // TM

    def next_tile(b, t):
        i = jnp.minimum(b * n_t + t + 1, B * n_t - 1)
        return i // n_t, i % n_t, 0

    return pl.pallas_call(
        _prompt_kernel,
        grid=(B, S // TM),
        in_specs=[pl.BlockSpec(memory_space=pltpu.SMEM),
                  pl.BlockSpec(memory_space=pltpu.SMEM),
                  pl.BlockSpec((None, TM, D_MODEL), lambda b, t: (b, t, 0)),
                  pl.BlockSpec((None, TM, D_MODEL), next_tile)]
                 + [_resident(c.shape) for c in consts] + [hbm] * len(weights),
        out_specs=[
            pl.BlockSpec((None, TM, D_MODEL), lambda b, t: (b, t, 0)),
            pl.BlockSpec((None, KV_WIDTH, WINDOW), lambda b, t: (b, 0, 0)),
            pl.BlockSpec((None, KV_WIDTH, WINDOW), lambda b, t: (b, 0, 0)),
            pl.BlockSpec((POOL_HIST, B, POOL_WIDTH), lambda b, t: (0, 0, 0)),
        ] + [hbm] * len(weights),
        out_shape=[
            jax.ShapeDtypeStruct((B, S, D_MODEL), F32),
            jax.ShapeDtypeStruct((B, KV_WIDTH, WINDOW), F32),
            jax.ShapeDtypeStruct((B, KV_WIDTH, WINDOW), F32),
            jax.ShapeDtypeStruct((POOL_HIST, B, POOL_WIDTH), F32),
        ] + [jax.ShapeDtypeStruct(w.shape, BF16) for w in weights],
        scratch_shapes=[pltpu.VMEM(w.shape, BF16) for w in weights] + [
            pltpu.VMEM((N_STAGE, STAGE_ROWS, STAGE_COLS), F32),
            pltpu.SemaphoreType.DMA((N_STAGE,)),
            pltpu.SemaphoreType.DMA((len(weights),)),
            pltpu.VMEM((C_V.stop, D_MODEL), BF16),
            pltpu.VMEM((TM, D_MODEL), BF16),
            pltpu.VMEM((ATTN_WIDTH, TM), BF16),
            pltpu.VMEM((2, WINDOW + TM, LANES), BF16),
            pltpu.VMEM((KV_WIDTH, WINDOW + TM), BF16),
            pltpu.VMEM((TM, ATTN_WIDTH), F32),
            pltpu.VMEM((HIST_ROWS + TM, POOL_WIDTH), F32),
            pltpu.VMEM((B, HIST_ROWS, POOL_WIDTH), F32),
            pltpu.VMEM((N_KV_HEADS, PAIR_KEYS, GROUP_REP * PAIR), F32),
        ],
        compiler_params=pltpu.CompilerParams(
            dimension_semantics=("arbitrary", "arbitrary"),
            vmem_limit_bytes=VMEM_LIMIT,
        ),
        name="prompt_layer",
    )(tbl, sinks, x, x, *consts, *weights)


def _sample_kernel(n_b, n_s, start,
                   tbl_ref, sinks_ref, x_ref, skt_ref, svt_ref, sp_ref, ng_ref,
                   gcol_ref, grow_ref, ones_ref, bks_ref, bkn_ref, ps_ref,
                   win_hbm, pw_hbm, wabr_hbm, wpbr_hbm, wout_hbm,
                   y_ref, ok_ref, ov_ref, op_ref,
                   win_ref, pw_ref, wabr_ref, wpbr_ref, wout_ref, sem):
    n_tok = n_b * n_s
    qkv = slice(0, C_V.stop)
    rest = slice(C_V.stop, win_ref.shape[1])
    loads = [pltpu.make_async_copy(win_hbm.at[:, qkv], win_ref.at[:, qkv], sem.at[0]),
             pltpu.make_async_copy(win_hbm.at[:, rest], win_ref.at[:, rest], sem.at[1])]
    loads += [pltpu.make_async_copy(src, dst, sem.at[2 + i]) for i, (src, dst) in enumerate(
        ((pw_hbm, pw_ref), (wabr_hbm, wabr_ref), (wpbr_hbm, wpbr_ref), (wout_hbm, wout_ref)))]
    for c in loads:
        c.start()

    x = x_ref[...]
    hb = _rmsnorm_rows(x, ng_ref[...]).astype(BF16)

    loads[0].wait()
    q = _head_rmsnorm(_dot(hb, win_ref[:, C_Q]), ones_ref, grow_ref[:, :ATTN_WIDTH]).astype(BF16)
    q3 = q.reshape(n_b, n_s, ATTN_WIDTH)
    zkt = _dot_nt(win_ref[:, C_K].T, hb)
    k_gain = gcol_ref[:, 1:2]
    knt = jnp.concatenate([_head_rmsnorm_fm(zkt, k_gain, g) for g in range(N_KV_HEADS)], axis=0)
    vnt = _dot_nt(win_ref[:, C_V].T, hb)

    lane = lax.broadcasted_iota(jnp.int32, (1, WINDOW), 1)
    per_tile = LANES // n_s
    for s in range(n_b):
        tile = slice((s // per_tile) * LANES, (s // per_tile + 1) * LANES)
        shift = (WINDOW - n_s - (s % per_tile) * n_s) % LANES
        for new, st_ref, o_ref in ((knt, skt_ref, ok_ref), (vnt, svt_ref, ov_ref)):
            fresh = new[:, tile] if shift == 0 else pltpu.roll(new[:, tile], shift, axis=1)
            o_ref[s] = jnp.where(lane < WINDOW - n_s, pltpu.roll(st_ref[s], WINDOW - n_s, axis=1), fresh)

    knb = knt.astype(BF16)
    vnb = vnt.astype(BF16)
    key_stream = (lax.broadcasted_iota(jnp.int32, (n_b, 1, 2 * n_tok), 2) & (n_tok - 1)) >> (n_s.bit_length() - 1)
    own = key_stream == lax.broadcasted_iota(jnp.int32, (n_b, 1, 2 * n_tok), 0)
    top = lax.broadcasted_iota(jnp.int32, (1, 2 * n_s, 1), 1) < n_s
    low = lax.broadcasted_iota(jnp.int32, (1, 1, LANES), 2) < HEAD_DIM

    def bias_rows(bucket, g):
        return jnp.concatenate([jnp.concatenate(
            [_gather_bias(bucket, tbl_ref, g * GROUP_REP + 2 * c + j) for j in range(2)], axis=1)
            for c in range(2)], axis=0)

    def scores(g):
        hd = slice(g * HEAD_DIM, (g + 1) * HEAD_DIM)
        c = g * MXU_DIM
        lhs = jnp.concatenate([q3[:, :, c:c + LANES], q3[:, :, c + LANES:c + MXU_DIM]], axis=1)
        kst = _diag2(skt_ref[:, hd, :].astype(BF16), 1, 2)
        s_st = jnp.stack([_dot(lhs[s], kst[s]) for s in range(n_b)]) + bias_rows(bks_ref[...], g)[None]
        s_nw = _dot(lhs.reshape(2 * n_tok, LANES), _diag2(knb[hd, :], 0, 1))
        s_nw = jnp.where(own, s_nw.reshape(n_b, 2 * n_s, 2 * n_tok) + bias_rows(bkn_ref[...], g)[None], NEG_INF)
        return s_st, s_nw

    def finish(g, s_st, s_nw):
        hd = slice(g * HEAD_DIM, (g + 1) * HEAD_DIM)
        p_st, p_nw, inv = [], [], []
        for j in range(2):
            a = s_st[:, :, j * WINDOW:(j + 1) * WINDOW]
            b = s_nw[:, :, j * n_tok:(j + 1) * n_tok]
            sink = jnp.where(top, sinks_ref[4 * g + j], sinks_ref[4 * g + 2 + j])
            m = jnp.maximum(jnp.maximum(jnp.max(a, axis=-1, keepdims=True),
                                        jnp.max(b, axis=-1, keepdims=True)), sink)
            ea = jnp.exp(a - m)
            eb = jnp.exp(b - m)
            inv.append(1.0 / (jnp.sum(ea, axis=-1, keepdims=True) + jnp.sum(eb, axis=-1, keepdims=True)
                              + jnp.exp(sink - m)))
            p_st.append(ea.astype(BF16))
            p_nw.append(eb.astype(BF16))
        p_st = jnp.concatenate(p_st, axis=2)
        p_nw = jnp.concatenate(p_nw, axis=2).reshape(2 * n_tok, 2 * n_tok)
        vst = _diag2(svt_ref[:, hd, :].astype(BF16), 1, 2)
        o = jnp.stack([_dot_nt(p_st[s], vst[s]) for s in range(n_b)])
        o = o + _dot_nt(p_nw, _diag2(vnb[hd, :], 0, 1)).reshape(n_b, 2 * n_s, LANES)
        o = o * jnp.where(low, inv[0], inv[1])
        return jnp.concatenate([o[:, :n_s, :], o[:, n_s:, :]], axis=2)

    outs = []
    nxt = scores(0)
    for g in range(N_KV_HEADS):
        cur = nxt
        if g + 1 < N_KV_HEADS:
            nxt = scores(g + 1)
        outs.append(finish(g, *cur))
    attn_o = jnp.concatenate(outs, axis=2).reshape(n_tok, ATTN_WIDTH)

    for c in loads[1:]:
        c.wait()
    pu = _dot(hb, win_ref[:, C_PU])
    pu_t = jnp.swapaxes(pu.reshape(n_b, n_s, POOL_WIDTH), 0, 1)
    full = jnp.concatenate([jnp.zeros((1, n_b, POOL_WIDTH), F32), sp_ref[...], pu_t], axis=0)
    op_ref[...] = full[HIST_ROWS + n_s - POOL_HIST:]
    sums = _window_sums(full, _delay_leading)
    pos1 = (start + 1 + lax.broadcasted_iota(jnp.int32, (n_s, 1, 1), 0)).astype(F32)
    mixed = []
    for gi, w in enumerate(POOL_WINDOWS):
        inv_cnt = 1.0 / jnp.minimum(float(w), pos1)
        sw = sums[gi][HIST_ROWS:] * inv_cnt - pu_t[..., gi * POOL_GROUP:(gi + 1) * POOL_GROUP]
        mixed.append(sw.reshape(n_tok, POOL_GROUP))

    def stream_major(po):
        return jnp.swapaxes(po.reshape(n_s, n_b, POOL_WIDTH), 0, 1).reshape(n_tok, POOL_WIDTH)

    gates = tuple(_dot(hb, win_ref[:, c]) for c in (C_AG, C_PG, C_MA, C_MP))
    merged = _merge_branches(attn_o, mixed, gates, pw_ref, ps_ref, wabr_ref, wpbr_ref, stream_major)
    y_ref[...] = x + _dot(merged, wout_ref[...])


def _sample_layer(x, skt, svt, sp, start, tbl, sinks, ng, w_in, gain_cols, gain_rows, ones, bks, bkn,
                  pw, ps, wabr, wpbr, wout):
    n_b, n_s, _ = x.shape
    n_tok = n_b * n_s
    assert n_s % 16 == 0 and LANES % n_s == 0 and n_s >= POOL_HIST and n_tok % LANES == 0
    assert n_s & (n_s - 1) == 0 and n_tok & (n_tok - 1) == 0
    vmem = pl.BlockSpec(memory_space=pltpu.VMEM)
    weights = (w_in, pw, wabr, wpbr, wout)
    y, ok, ov, op = pl.pallas_call(
        functools.partial(_sample_kernel, n_b, n_s, start),
        in_specs=[pl.BlockSpec(memory_space=pltpu.SMEM)] * 2 + [vmem] * 11
                 + [pl.BlockSpec(memory_space=pl.ANY)] * len(weights),
        out_specs=[vmem] * 4,
        out_shape=[
            jax.ShapeDtypeStruct((n_tok, D_MODEL), F32),
            jax.ShapeDtypeStruct((n_b, KV_WIDTH, WINDOW), F32),
            jax.ShapeDtypeStruct((n_b, KV_WIDTH, WINDOW), F32),
            jax.ShapeDtypeStruct((POOL_HIST, n_b, POOL_WIDTH), F32),
        ],
        scratch_shapes=[pltpu.VMEM(w.shape, w.dtype) for w in weights]
                       + [pltpu.SemaphoreType.DMA((len(weights) + 1,))],
        compiler_params=pltpu.CompilerParams(vmem_limit_bytes=VMEM_LIMIT),
        name="sample_layer",
    )(tbl, sinks, x.reshape(n_tok, D_MODEL), skt, svt, sp, ng, gain_cols, gain_rows, ones, bks, bkn, ps,
      *weights)
    return y.reshape(n_b, n_s, D_MODEL), ok, ov, op


def _kv_to_feature_major(a):
    return jnp.transpose(a, (0, 2, 3, 1)).reshape(a.shape[0], KV_WIDTH, WINDOW)


def _kv_from_feature_major(a):
    return jnp.transpose(a.reshape(a.shape[0], N_KV_HEADS, HEAD_DIM, WINDOW), (0, 3, 1, 2))


def kernel(x_prompt, x_sample, state_attn_k, state_attn_v, state_pool, norm_gain, w_in, q_norm_gain, k_norm_gain, attn_sinks, rel_bias, pool_w, pool_scale, w_attn_br, w_pool_br, w_out):
    depth = w_in.shape[0]
    n_b, n_s, _ = x_sample.shape
    past_len = 4096
    bkp, bks, bkn = _bucket_tables(n_b, n_s)
    tbl = rel_bias.T
    seg = np.arange(MXU_DIM) // HEAD_DIM
    ones = jnp.asarray(seg[:, None] == seg[None, :], BF16)

    xp, xs = x_prompt, x_sample
    pk, pv, pp, sk, sv, sp = [], [], [], [], [], []
    for l in range(depth):
        ng = norm_gain[l].reshape(1, D_MODEL)
        qg = q_norm_gain[l] * (HEAD_DIM ** -0.5)
        kg = k_norm_gain[l]
        gain_cols = jnp.stack([qg, kg], axis=1)
        gain_rows = jnp.concatenate([jnp.tile(qg, N_HEADS), jnp.tile(kg, N_KV_HEADS)]).reshape(1, -1)
        ps = pool_scale[l].reshape(1, POOL_WIDTH)
        xp, k_l, v_l, p_l, w_in_b, wabr_b, wpbr_b, wout_b, pw_b = _prompt_layer(
            xp, tbl, attn_sinks[l], ng, gain_cols, bkp, ps,
            (w_in[l], w_attn_br[l], w_pool_br[l], w_out[l], pool_w[l].reshape(POOL_WIDTH, POOL_GROUP)))
        tail = (pw_b, ps, wabr_b, wpbr_b, wout_b)
        pk.append(_kv_from_feature_major(k_l))
        pv.append(_kv_from_feature_major(v_l))
        pp.append(jnp.transpose(p_l, (1, 0, 2)))

        xs, k_l, v_l, p_l = _sample_layer(
            xs,
            _kv_to_feature_major(state_attn_k[l]),
            _kv_to_feature_major(state_attn_v[l]),
            jnp.transpose(state_pool[l], (1, 0, 2)),
            past_len, tbl, attn_sinks[l], ng, w_in_b, gain_cols, gain_rows, ones, bks, bkn, *tail)
        sk.append(_kv_from_feature_major(k_l))
        sv.append(_kv_from_feature_major(v_l))
        sp.append(jnp.transpose(p_l, (1, 0, 2)))
    return (xp, xs, jnp.stack(pk), jnp.stack(pv), jnp.stack(pp),
            jnp.stack(sk), jnp.stack(sv), jnp.stack(sp))
```

```python
import functools
import numpy as np

import jax
import jax.numpy as jnp
from jax import lax
from jax.experimental import pallas as pl
from jax.experimental.pallas import tpu as pltpu

D_MODEL = 1024
CHUNK = 64
WINDOW = 128
N_HEADS = 16
N_KV_HEADS = 4
GROUP_REP = N_HEADS // N_KV_HEADS
HEAD_DIM = 64
ATTN_WIDTH = N_HEADS * HEAD_DIM
KV_WIDTH = N_KV_HEADS * HEAD_DIM
POOL_WINDOWS = (2, 4, 8, 16)
POOL_WIDTH = D_MODEL
POOL_GROUP = POOL_WIDTH // len(POOL_WINDOWS)
POOL_HIST = max(POOL_WINDOWS) - 1
HIST_ROWS = POOL_HIST + 1
N_BUCKETS = 32
MAX_DISTANCE = 128
EPS = 1e-6
NEG_INF = -1e30

_SPLITS = (ATTN_WIDTH, KV_WIDTH, KV_WIDTH, ATTN_WIDTH, POOL_WIDTH, POOL_WIDTH, D_MODEL, D_MODEL)
_OFFS = tuple(sum(_SPLITS[:i]) for i in range(len(_SPLITS) + 1))
C_Q, C_K, C_V, C_AG, C_PU, C_PG, C_MA, C_MP = (slice(_OFFS[i], _OFFS[i + 1]) for i in range(8))

LANES = 128
MXU_DIM = 256
PAIR = 2 * CHUNK
PAIR_KEYS = PAIR + WINDOW
CHUNK_KEYS = CHUNK + WINDOW
TM = 512
PREFILL = 8
STAGE_ROWS, STAGE_COLS, N_STAGE = ATTN_WIDTH, TM, 3
VMEM_LIMIT = 60 * 1024 * 1024

BF16 = jnp.bfloat16
F32 = jnp.float32


def _dot(a, b):
    return jnp.dot(a, b, preferred_element_type=F32)


def _dot_nt(a, b):
    return lax.dot_general(a, b, (((1,), (1,)), ((), ())), preferred_element_type=F32)


def _rmsnorm_rows(x, gain):
    y = x * lax.rsqrt(jnp.mean(x * x, axis=-1, keepdims=True) + EPS)
    return y * gain


def _head_rmsnorm(z, ones_ref, gain):
    sq = z * z
    hi = sq.astype(BF16)
    lo = (sq - hi.astype(F32)).astype(BF16)
    ones = ones_ref[...]
    parts = []
    for c in range(z.shape[1] // MXU_DIM):
        sl = slice(c * MXU_DIM, (c + 1) * MXU_DIM)
        parts.append(_dot(hi[:, sl], ones) + _dot(lo[:, sl], ones))
    ss = parts[0] if len(parts) == 1 else jnp.concatenate(parts, axis=1)
    return (z * lax.rsqrt(ss * (1.0 / HEAD_DIM) + EPS)) * gain


def _head_rmsnorm_fm(zt, gain_col, h):
    zh = zt[h * HEAD_DIM:(h + 1) * HEAD_DIM, :]
    ms = jnp.sum(zh * zh, axis=0, keepdims=True) * (1.0 / HEAD_DIM)
    return (zh * lax.rsqrt(ms + EPS)) * gain_col


def _window_sums(full, delay):
    outs = []
    for gi, w in enumerate(POOL_WINDOWS):
        s = full[..., gi * POOL_GROUP:(gi + 1) * POOL_GROUP]
        sh = 1
        while sh < w:
            s = s + delay(s, sh)
            sh *= 2
        outs.append(s)
    return outs


def _delay_rows(a, n):
    return pltpu.roll(a, n, axis=0)


def _delay_leading(a, n):
    return jnp.concatenate([a[-n:], a[:-n]], axis=0)


def _merge_branches(attn_o, pool_mixed, gates, pw_ref, ps_ref, wabr_ref, wpbr_ref, pool_rows=None):
    ag, pg, ma, mp = gates
    po = []
    for gi in range(len(POOL_WINDOWS)):
        po.append(_dot(pool_mixed[gi].astype(BF16), pw_ref[gi * POOL_GROUP:(gi + 1) * POOL_GROUP, :]))
    po = jnp.concatenate(po, axis=1)
    if pool_rows is not None:
        po = pool_rows(po)
    po = po * ps_ref[...]
    p = _dot((po * (pg * jax.nn.sigmoid(pg))).astype(BF16), wpbr_ref[...])
    a = _dot((attn_o * (ag * jax.nn.sigmoid(ag))).astype(BF16), wabr_ref[...])
    acc = jax.nn.sigmoid(ma) * a + jax.nn.sigmoid(mp) * p
    return acc.astype(BF16)


def _diag2(a, axis_r, axis_c):
    z = jnp.zeros_like(a)
    return jnp.concatenate([jnp.concatenate([a, z], axis=axis_c),
                            jnp.concatenate([z, a], axis=axis_c)], axis=axis_r)


def _gather_bias(bucket, tbl_ref, h):
    acc = jnp.zeros(bucket.shape, F32)
    for b in range(N_BUCKETS):
        acc = jnp.where(bucket == b, tbl_ref[h, b], acc)
    return acc


def _t5_bucket(rel):
    assert (N_BUCKETS, MAX_DISTANCE) == (32, 128)
    nb = N_BUCKETS // 2
    max_exact = nb // 2
    n = np.abs(rel)
    log2_sq = np.vectorize(lambda v: max(int(v), 1).bit_length() - 1)(n * n)
    large = np.minimum(max_exact + log2_sq - 6, nb - 1)
    return (np.where(rel > 0, nb, 0) + np.where(n < max_exact, n, large)).astype(np.int32)


def _bucket_tables(n_b, n_s):
    bkp = _t5_bucket(np.arange(CHUNK_KEYS)[:, None] - WINDOW - np.arange(LANES)[None, :] % CHUNK)
    fr = np.arange(n_s)[:, None]
    bks = _t5_bucket(np.arange(WINDOW)[None, :] - WINDOW - fr)
    bkn = _t5_bucket(np.arange(n_b * n_s)[None, :] % n_s - fr)
    return bkp, bks, bkn


def _weight_jobs(pairs):
    jobs = []
    for src, dst in pairs:
        n_rows, n_cols = dst.shape
        for r0 in range(0, n_rows, STAGE_ROWS):
            for c0 in range(0, n_cols, STAGE_COLS):
                jobs.append((src, dst, r0, c0, min(STAGE_COLS, n_cols - c0)))
    return jobs


def _prompt_kernel(tbl_ref, sinks_ref, x_ref, xn_ref, ng_ref, gcol_ref, bkp_ref, ps_ref,
                   win_hbm, wabr_hbm, wpbr_hbm, wout_hbm, pw_hbm,
                   y_ref, pk_ref, pv_ref, pp_ref, winb_hbm, wabrb_hbm, wpbrb_hbm, woutb_hbm, pwb_hbm,
                   win_ref, wabr_ref, wpbr_ref, wout_ref, pw_ref, stage, sem_in, sem_out,
                   wt_s, h_s, qt_s, k_s, vt_s, attn_s, pbuf, hist_s, bias_s):
    b = pl.program_id(0)
    t = pl.program_id(1)
    first = (b == 0) & (t == 0)
    q_gain = gcol_ref[:, 0:1]
    k_gain = gcol_ref[:, 1:2]
    zq_s = stage.at[0]
    weights = ((win_hbm, win_ref), (wabr_hbm, wabr_ref), (wpbr_hbm, wpbr_ref), (wout_hbm, wout_ref),
               (pw_hbm, pw_ref))
    exports = (winb_hbm, wabrb_hbm, wpbrb_hbm, woutb_hbm, pwb_hbm)

    def export_copy(i):
        return pltpu.make_async_copy(weights[i][1], exports[i], sem_out.at[i])

    @pl.when(first)
    def _():
        jobs = _weight_jobs(weights)

        def load(i):
            src, _, r0, c0, w = jobs[i]
            return pltpu.make_async_copy(src.at[pl.ds(r0, STAGE_ROWS), pl.ds(c0, w)],
                                         stage.at[i % N_STAGE, :, pl.ds(0, w)], sem_in.at[i % N_STAGE])

        for i in range(min(N_STAGE - 1, len(jobs))):
            load(i).start()
        for i in range(len(jobs)):
            if i + N_STAGE - 1 < len(jobs):
                load(i + N_STAGE - 1).start()
            load(i).wait()
            _, dst, r0, c0, w = jobs[i]
            dst[r0:r0 + STAGE_ROWS, c0:c0 + w] = stage[i % N_STAGE, :, 0:w].astype(BF16)
            if i < N_HEADS // 2:
                bkp = bkp_ref[...]
                pair = jnp.where(lax.broadcasted_iota(jnp.int32, bkp.shape, 1) < CHUNK,
                                 _gather_bias(bkp, tbl_ref, 2 * i), _gather_bias(bkp, tbl_ref, 2 * i + 1))
                bias_s[2 * i // GROUP_REP, :, (2 * i % GROUP_REP) * CHUNK:(2 * i % GROUP_REP + 2) * CHUNK] = pair
            if jobs[i][1] is win_ref and c0 + w == C_V.stop:
                wt_s[...] = win_ref[:, 0:C_V.stop].T
                h_s[...] = _rmsnorm_rows(x_ref[...], ng_ref[...]).astype(BF16)
        assert len(jobs) >= N_HEADS // 2 and C_V.stop % STAGE_COLS == 0
        zq_s[...] = _dot_nt(wt_s[C_Q, :], h_s[...])
        for i in range(len(weights)):
            export_copy(i).start()

    @pl.when(t == 0)
    def _():
        k_s[:, 0:WINDOW, :] = jnp.zeros((2, WINDOW, LANES), BF16)
        vt_s[:, 0:WINDOW] = jnp.zeros((KV_WIDTH, WINDOW), BF16)
        pbuf[0:HIST_ROWS, :] = jnp.zeros((HIST_ROWS, POOL_WIDTH), F32)

    hb = h_s[...]

    n_c = D_MODEL // MXU_DIM
    fill_cols = [slice(c.start + i * MXU_DIM, c.start + (i + 1) * MXU_DIM)
                 for c in (C_PU, C_PG, C_AG, C_MA, C_MP) for i in range(n_c)]
    filled = []

    def fill(upto):
        for cols in fill_cols[len(filled):upto]:
            filled.append(_dot(hb, win_ref[:, cols]))

    zqt = zq_s[...]
    zkt = _dot_nt(wt_s[C_K, :], hb)
    vt = _dot_nt(wt_s[C_V, :], hb)
    fill(PREFILL)
    kt = jnp.concatenate([_head_rmsnorm_fm(zkt, k_gain, g) for g in range(N_KV_HEADS)], axis=0)
    for h in range(N_HEADS):
        qt_s[h * HEAD_DIM:(h + 1) * HEAD_DIM, :] = _head_rmsnorm_fm(zqt, q_gain, h).astype(BF16)

    pk_ref[...] = kt[:, TM - WINDOW:]
    pv_ref[...] = vt[:, TM - WINDOW:]

    k_s[0, WINDOW:, :] = kt[:LANES, :].T.astype(BF16)
    k_s[1, WINDOW:, :] = kt[LANES:, :].T.astype(BF16)
    vt_s[:, WINDOW:] = vt.astype(BF16)

    zeros_half = jnp.zeros((HEAD_DIM, CHUNK), BF16)
    q_lane = lax.broadcasted_iota(jnp.int32, (1, GROUP_REP * CHUNK), 1)

    def scores(p, g):
        out = []
        for c in range(2):
            tok = slice(p * PAIR + c * CHUNK, p * PAIR + (c + 1) * CHUNK)
            keys = slice(p * PAIR + c * CHUNK, p * PAIR + c * CHUNK + CHUNK_KEYS)
            cols = []
            for r in range(GROUP_REP):
                h = g * GROUP_REP + r
                qh = qt_s[h * HEAD_DIM:(h + 1) * HEAD_DIM, tok]
                cols.append(jnp.concatenate([qh, zeros_half] if g % 2 == 0 else [zeros_half, qh], axis=0))
            rhs = jnp.concatenate(cols, axis=1)
            s = _dot(k_s[g // 2, keys, :], rhs) + bias_s[g]
            if p == 0:
                kk = lax.broadcasted_iota(jnp.int32, s.shape, 0)
                s = jnp.where((kk >= WINDOW - c * CHUNK) | (t > 0), s, NEG_INF)
            out.append(s)
        return out

    def finish(p, g, s_ab):
        tok = slice(p * PAIR, (p + 1) * PAIR)
        keys = slice(p * PAIR, p * PAIR + PAIR_KEYS)
        sink = sinks_ref[g * GROUP_REP]
        for r in range(1, GROUP_REP):
            sink = jnp.where(q_lane < r * CHUNK, sink, sinks_ref[g * GROUP_REP + r])
        pads = jnp.zeros((PAIR_KEYS - CHUNK_KEYS, GROUP_REP * CHUNK), BF16)
        e_ab, inv = [], []
        for c, s in enumerate(s_ab):
            m = jnp.maximum(jnp.max(s, axis=0, keepdims=True), sink)
            e = jnp.exp(s - m)
            inv.append(1.0 / (jnp.sum(e, axis=0, keepdims=True) + jnp.exp(sink - m)))
            e_ab.append(jnp.concatenate([e.astype(BF16), pads] if c == 0 else [pads, e.astype(BF16)], axis=0))
        vg = vt_s[g * HEAD_DIM:(g + 1) * HEAD_DIM, keys]
        o_t = _dot(vg, jnp.concatenate(e_ab, axis=1)) * jnp.concatenate(inv, axis=1)
        n_q = GROUP_REP * CHUNK
        for j in range(GROUP_REP // 2):
            rows = [jnp.concatenate([o_t[:, r * CHUNK:(r + 1) * CHUNK], o_t[:, n_q + r * CHUNK:n_q + (r + 1) * CHUNK]],
                                    axis=1) for r in (2 * j, 2 * j + 1)]
            lanes = slice(g * MXU_DIM + j * LANES, g * MXU_DIM + (j + 1) * LANES)
            attn_s[tok, lanes] = jnp.concatenate(rows, axis=0).T

    pu = jnp.concatenate(filled[:n_c], axis=1)
    pbuf[HIST_ROWS:, :] = pu
    hist_s[b] = pu[TM - HIST_ROWS:, :]
    sums = _window_sums(pbuf[...], _delay_rows)
    pos1 = (t * TM + 1 + lax.broadcasted_iota(jnp.int32, (TM, 1), 0)).astype(F32)
    mixed = []
    for gi, w in enumerate(POOL_WINDOWS):
        inv_cnt = 1.0 / jnp.minimum(float(w), pos1)
        mixed.append(sums[gi][HIST_ROWS:, :] * inv_cnt - pu[:, gi * POOL_GROUP:(gi + 1) * POOL_GROUP])

    blocks = [(p, g) for p in range(TM // PAIR) for g in range(N_KV_HEADS)]
    s_next = scores(*blocks[0])
    for i, blk in enumerate(blocks):
        s_cur = s_next
        if i + 1 < len(blocks):
            s_next = scores(*blocks[i + 1])
        fill(PREFILL + (len(fill_cols) - PREFILL) * (i + 1) // len(blocks))
        finish(*blk, s_cur)
    pg, ag, ma, mp = (jnp.concatenate(filled[j * n_c:(j + 1) * n_c], axis=1) for j in range(1, 5))

    merged = _merge_branches(attn_s[...], mixed, (ag, pg, ma, mp), pw_ref, ps_ref, wabr_ref, wpbr_ref)

    h_s[...] = _rmsnorm_rows(xn_ref[...], ng_ref[...]).astype(BF16)
    zq_s[...] = _dot_nt(wt_s[C_Q, :], h_s[...])

    y_ref[...] = x_ref[...] + _dot(merged, wout_ref[...])

    k_s[:, 0:WINDOW, :] = k_s[:, TM:TM + WINDOW, :]
    vt_s[:, 0:WINDOW] = vt_s[:, TM:TM + WINDOW]
    pbuf[0:HIST_ROWS, :] = pbuf[TM:TM + HIST_ROWS, :]

    @pl.when(first)
    def _():
        for i in range(len(weights)):
            export_copy(i).wait()

    @pl.when((b == pl.num_programs(0) - 1) & (t == pl.num_programs(1) - 1))
    def _():
        pp_ref[...] = jnp.swapaxes(hist_s[...], 0, 1)[HIST_ROWS - POOL_HIST:]


def _resident(shape):
    nd = len(shape)
    return pl.BlockSpec(shape, lambda b, t: (0,) * nd, pipeline_mode=pl.Buffered(1))


def _prompt_layer(x, tbl, sinks, ng, gain_cols, bkp, ps, weights):
    B, S, _ = x.shape
    assert S % TM == 0 and TM % PAIR == 0 and TM >= WINDOW
    assert all(w.shape[0] % STAGE_ROWS == 0 for w in weights)
    consts = (ng, gain_cols, bkp, ps)
    hbm = pl.BlockSpec(memory_space=pl.ANY)
    n_t = S // TM

    def next_tile(b, t):
        i = jnp.minimum(b * n_t + t + 1, B * n_t - 1)
        return i // n_t, i % n_t, 0

    return pl.pallas_call(
        _prompt_kernel,
        grid=(B, S // TM),
        in_specs=[pl.BlockSpec(memory_space=pltpu.SMEM),
                  pl.BlockSpec(memory_space=pltpu.SMEM),
                  pl.BlockSpec((None, TM, D_MODEL), lambda b, t: (b, t, 0)),
                  pl.BlockSpec((None, TM, D_MODEL), next_tile)]
                 + [_resident(c.shape) for c in consts] + [hbm] * len(weights),
        out_specs=[
            pl.BlockSpec((None, TM, D_MODEL), lambda b, t: (b, t, 0)),
            pl.BlockSpec((None, KV_WIDTH, WINDOW), lambda b, t: (b, 0, 0)),
            pl.BlockSpec((None, KV_WIDTH, WINDOW), lambda b, t: (b, 0, 0)),
            pl.BlockSpec((POOL_HIST, B, POOL_WIDTH), lambda b, t: (0, 0, 0)),
        ] + [hbm] * len(weights),
        out_shape=[
            jax.ShapeDtypeStruct((B, S, D_MODEL), F32),
            jax.ShapeDtypeStruct((B, KV_WIDTH, WINDOW), F32),
            jax.ShapeDtypeStruct((B, KV_WIDTH, WINDOW), F32),
            jax.ShapeDtypeStruct((POOL_HIST, B, POOL_WIDTH), F32),
        ] + [jax.ShapeDtypeStruct(w.shape, BF16) for w in weights],
        scratch_shapes=[pltpu.VMEM(w.shape, BF16) for w in weights] + [
            pltpu.VMEM((N_STAGE, STAGE_ROWS, STAGE_COLS), F32),
            pltpu.SemaphoreType.DMA((N_STAGE,)),
            pltpu.SemaphoreType.DMA((len(weights),)),
            pltpu.VMEM((C_V.stop, D_MODEL), BF16),
            pltpu.VMEM((TM, D_MODEL), BF16),
            pltpu.VMEM((ATTN_WIDTH, TM), BF16),
            pltpu.VMEM((2, WINDOW + TM, LANES), BF16),
            pltpu.VMEM((KV_WIDTH, WINDOW + TM), BF16),
            pltpu.VMEM((TM, ATTN_WIDTH), F32),
            pltpu.VMEM((HIST_ROWS + TM, POOL_WIDTH), F32),
            pltpu.VMEM((B, HIST_ROWS, POOL_WIDTH), F32),
            pltpu.VMEM((N_KV_HEADS, CHUNK_KEYS, GROUP_REP * CHUNK), F32),
        ],
        compiler_params=pltpu.CompilerParams(
            dimension_semantics=("arbitrary", "arbitrary"),
            vmem_limit_bytes=VMEM_LIMIT,
        ),
        name="prompt_layer",
    )(tbl, sinks, x, x, *consts, *weights)


def _sample_kernel(n_b, n_s, start,
                   tbl_ref, sinks_ref, x_ref, skt_ref, svt_ref, sp_ref, ng_ref,
                   gcol_ref, grow_ref, ones_ref, bks_ref, bkn_ref, ps_ref,
                   win_hbm, pw_hbm, wabr_hbm, wpbr_hbm, wout_hbm,
                   y_ref, ok_ref, ov_ref, op_ref,
                   win_ref, pw_ref, wabr_ref, wpbr_ref, wout_ref, sem):
    n_tok = n_b * n_s
    qkv = slice(0, C_V.stop)
    rest = slice(C_V.stop, win_ref.shape[1])
    loads = [pltpu.make_async_copy(win_hbm.at[:, qkv], win_ref.at[:, qkv], sem.at[0]),
             pltpu.make_async_copy(win_hbm.at[:, rest], win_ref.at[:, rest], sem.at[1])]
    loads += [pltpu.make_async_copy(src, dst, sem.at[2 + i]) for i, (src, dst) in enumerate(
        ((pw_hbm, pw_ref), (wabr_hbm, wabr_ref), (wpbr_hbm, wpbr_ref), (wout_hbm, wout_ref)))]
    for c in loads:
        c.start()

    x = x_ref[...]
    hb = _rmsnorm_rows(x, ng_ref[...]).astype(BF16)

    loads[0].wait()
    q = _head_rmsnorm(_dot(hb, win_ref[:, C_Q]), ones_ref, grow_ref[:, :ATTN_WIDTH]).astype(BF16)
    q3 = q.reshape(n_b, n_s, ATTN_WIDTH)
    zkt = _dot_nt(win_ref[:, C_K].T, hb)
    k_gain = gcol_ref[:, 1:2]
    knt = jnp.concatenate([_head_rmsnorm_fm(zkt, k_gain, g) for g in range(N_KV_HEADS)], axis=0)
    vnt = _dot_nt(win_ref[:, C_V].T, hb)

    lane = lax.broadcasted_iota(jnp.int32, (1, WINDOW), 1)
    per_tile = LANES // n_s
    for s in range(n_b):
        tile = slice((s // per_tile) * LANES, (s // per_tile + 1) * LANES)
        shift = (WINDOW - n_s - (s % per_tile) * n_s) % LANES
        for new, st_ref, o_ref in ((knt, skt_ref, ok_ref), (vnt, svt_ref, ov_ref)):
            fresh = new[:, tile] if shift == 0 else pltpu.roll(new[:, tile], shift, axis=1)
            o_ref[s] = jnp.where(lane < WINDOW - n_s, pltpu.roll(st_ref[s], WINDOW - n_s, axis=1), fresh)

    knb = knt.astype(BF16)
    vnb = vnt.astype(BF16)
    key_stream = (lax.broadcasted_iota(jnp.int32, (n_b, 1, 2 * n_tok), 2) & (n_tok - 1)) >> (n_s.bit_length() - 1)
    own = key_stream == lax.broadcasted_iota(jnp.int32, (n_b, 1, 2 * n_tok), 0)
    top = lax.broadcasted_iota(jnp.int32, (1, 2 * n_s, 1), 1) < n_s
    low = lax.broadcasted_iota(jnp.int32, (1, 1, LANES), 2) < HEAD_DIM

    def bias_rows(bucket, g):
        return jnp.concatenate([jnp.concatenate(
            [_gather_bias(bucket, tbl_ref, g * GROUP_REP + 2 * c + j) for j in range(2)], axis=1)
            for c in range(2)], axis=0)

    def scores(g):
        hd = slice(g * HEAD_DIM, (g + 1) * HEAD_DIM)
        c = g * MXU_DIM
        lhs = jnp.concatenate([q3[:, :, c:c + LANES], q3[:, :, c + LANES:c + MXU_DIM]], axis=1)
        kst = _diag2(skt_ref[:, hd, :].astype(BF16), 1, 2)
        s_st = jnp.stack([_dot(lhs[s], kst[s]) for s in range(n_b)]) + bias_rows(bks_ref[...], g)[None]
        s_nw = _dot(lhs.reshape(2 * n_tok, LANES), _diag2(knb[hd, :], 0, 1))
        s_nw = jnp.where(own, s_nw.reshape(n_b, 2 * n_s, 2 * n_tok) + bias_rows(bkn_ref[...], g)[None], NEG_INF)
        return s_st, s_nw

    def finish(g, s_st, s_nw):
        hd = slice(g * HEAD_DIM, (g + 1) * HEAD_DIM)
        p_st, p_nw, inv = [], [], []
        for j in range(2):
            a = s_st[:, :, j * WINDOW:(j + 1) * WINDOW]
            b = s_nw[:, :, j * n_tok:(j + 1) * n_tok]
            sink = jnp.where(top, sinks_ref[4 * g + j], sinks_ref[4 * g + 2 + j])
            m = jnp.maximum(jnp.maximum(jnp.max(a, axis=-1, keepdims=True),
                                        jnp.max(b, axis=-1, keepdims=True)), sink)
            ea = jnp.exp(a - m)
            eb = jnp.exp(b - m)
            inv.append(1.0 / (jnp.sum(ea, axis=-1, keepdims=True) + jnp.sum(eb, axis=-1, keepdims=True)
                              + jnp.exp(sink - m)))
            p_st.append(ea.astype(BF16))
            p_nw.append(eb.astype(BF16))
        p_st = jnp.concatenate(p_st, axis=2)
        p_nw = jnp.concatenate(p_nw, axis=2).reshape(2 * n_tok, 2 * n_tok)
        vst = _diag2(svt_ref[:, hd, :].astype(BF16), 1, 2)
        o = jnp.stack([_dot_nt(p_st[s], vst[s]) for s in range(n_b)])
        o = o + _dot_nt(p_nw, _diag2(vnb[hd, :], 0, 1)).reshape(n_b, 2 * n_s, LANES)
        o = o * jnp.where(low, inv[0], inv[1])
        return jnp.concatenate([o[:, :n_s, :], o[:, n_s:, :]], axis=2)

    outs = []
    nxt = scores(0)
    for g in range(N_KV_HEADS):
        cur = nxt
        if g + 1 < N_KV_HEADS:
            nxt = scores(g + 1)
        outs.append(finish(g, *cur))
    attn_o = jnp.concatenate(outs, axis=2).reshape(n_tok, ATTN_WIDTH)

    for c in loads[1:]:
        c.wait()
    pu = _dot(hb, win_ref[:, C_PU])
    pu_t = jnp.swapaxes(pu.reshape(n_b, n_s, POOL_WIDTH), 0, 1)
    full = jnp.concatenate([jnp.zeros((1, n_b, POOL_WIDTH), F32), sp_ref[...], pu_t], axis=0)
    op_ref[...] = full[HIST_ROWS + n_s - POOL_HIST:]
    sums = _window_sums(full, _delay_leading)
    pos1 = (start + 1 + lax.broadcasted_iota(jnp.int32, (n_s, 1, 1), 0)).astype(F32)
    mixed = []
    for gi, w in enumerate(POOL_WINDOWS):
        inv_cnt = 1.0 / jnp.minimum(float(w), pos1)
        sw = sums[gi][HIST_ROWS:] * inv_cnt - pu_t[..., gi * POOL_GROUP:(gi + 1) * POOL_GROUP]
        mixed.append(sw.reshape(n_tok, POOL_GROUP))

    def stream_major(po):
        return jnp.swapaxes(po.reshape(n_s, n_b, POOL_WIDTH), 0, 1).reshape(n_tok, POOL_WIDTH)

    gates = tuple(_dot(hb, win_ref[:, c]) for c in (C_AG, C_PG, C_MA, C_MP))
    merged = _merge_branches(attn_o, mixed, gates, pw_ref, ps_ref, wabr_ref, wpbr_ref, stream_major)
    y_ref[...] = x + _dot(merged, wout_ref[...])


def _sample_layer(x, skt, svt, sp, start, tbl, sinks, ng, w_in, gain_cols, gain_rows, ones, bks, bkn,
                  pw, ps, wabr, wpbr, wout):
    n_b, n_s, _ = x.shape
    n_tok = n_b * n_s
    assert n_s % 16 == 0 and LANES % n_s == 0 and n_s >= POOL_HIST and n_tok % LANES == 0
    assert n_s & (n_s - 1) == 0 and n_tok & (n_tok - 1) == 0
    vmem = pl.BlockSpec(memory_space=pltpu.VMEM)
    weights = (w_in, pw, wabr, wpbr, wout)
    y, ok, ov, op = pl.pallas_call(
        functools.partial(_sample_kernel, n_b, n_s, start),
        in_specs=[pl.BlockSpec(memory_space=pltpu.SMEM)] * 2 + [vmem] * 11
                 + [pl.BlockSpec(memory_space=pl.ANY)] * len(weights),
        out_specs=[vmem] * 4,
        out_shape=[
            jax.ShapeDtypeStruct((n_tok, D_MODEL), F32),
            jax.ShapeDtypeStruct((n_b, KV_WIDTH, WINDOW), F32),
            jax.ShapeDtypeStruct((n_b, KV_WIDTH, WINDOW), F32),
            jax.ShapeDtypeStruct((POOL_HIST, n_b, POOL_WIDTH), F32),
        ],
        scratch_shapes=[pltpu.VMEM(w.shape, w.dtype) for w in weights]
                       + [pltpu.SemaphoreType.DMA((len(weights) + 1,))],
        compiler_params=pltpu.CompilerParams(vmem_limit_bytes=VMEM_LIMIT),
        name="sample_layer",
    )(tbl, sinks, x.reshape(n_tok, D_MODEL), skt, svt, sp, ng, gain_cols, gain_rows, ones, bks, bkn, ps,
      *weights)
    return y.reshape(n_b, n_s, D_MODEL), ok, ov, op


def _kv_to_feature_major(a):
    return jnp.transpose(a, (0, 2, 3, 1)).reshape(a.shape[0], KV_WIDTH, WINDOW)


def _kv_from_feature_major(a):
    return jnp.transpose(a.reshape(a.shape[0], N_KV_HEADS, HEAD_DIM, WINDOW), (0, 3, 1, 2))


def kernel(x_prompt, x_sample, state_attn_k, state_attn_v, state_pool, norm_gain, w_in, q_norm_gain, k_norm_gain, attn_sinks, rel_bias, pool_w, pool_scale, w_attn_br, w_pool_br, w_out):
    depth = w_in.shape[0]
    n_b, n_s, _ = x_sample.shape
    past_len = 4096
    bkp, bks, bkn = _bucket_tables(n_b, n_s)
    tbl = rel_bias.T
    seg = np.arange(MXU_DIM) // HEAD_DIM
    ones = jnp.asarray(seg[:, None] == seg[None, :], BF16)

    xp, xs = x_prompt, x_sample
    pk, pv, pp, sk, sv, sp = [], [], [], [], [], []
    for l in range(depth):
        ng = norm_gain[l].reshape(1, D_MODEL)
        qg = q_norm_gain[l] * (HEAD_DIM ** -0.5)
        kg = k_norm_gain[l]
        gain_cols = jnp.stack([qg, kg], axis=1)
        gain_rows = jnp.concatenate([jnp.tile(qg, N_HEADS), jnp.tile(kg, N_KV_HEADS)]).reshape(1, -1)
        ps = pool_scale[l].reshape(1, POOL_WIDTH)
        xp, k_l, v_l, p_l, w_in_b, wabr_b, wpbr_b, wout_b, pw_b = _prompt_layer(
            xp, tbl, attn_sinks[l], ng, gain_cols, bkp, ps,
            (w_in[l], w_attn_br[l], w_pool_br[l], w_out[l], pool_w[l].reshape(POOL_WIDTH, POOL_GROUP)))
        tail = (pw_b, ps, wabr_b, wpbr_b, wout_b)
        pk.append(_kv_from_feature_major(k_l))
        pv.append(_kv_from_feature_major(v_l))
        pp.append(jnp.transpose(p_l, (1, 0, 2)))

        xs, k_l, v_l, p_l = _sample_layer(
            xs,
            _kv_to_feature_major(state_attn_k[l]),
            _kv_to_feature_major(state_attn_v[l]),
            jnp.transpose(state_pool[l], (1, 0, 2)),
            past_len, tbl, attn_sinks[l], ng, w_in_b, gain_cols, gain_rows, ones, bks, bkn, *tail)
        sk.append(_kv_from_feature_major(k_l))
        sv.append(_kv_from_feature_major(v_l))
        sp.append(jnp.transpose(p_l, (1, 0, 2)))
    return (xp, xs, jnp.stack(pk), jnp.stack(pv), jnp.stack(pp),
            jnp.stack(sk), jnp.stack(sv), jnp.stack(sp))
```

```python
import functools
import numpy as np

import jax
import jax.numpy as jnp
from jax import lax
from jax.experimental import pallas as pl
from jax.experimental.pallas import tpu as pltpu

D_MODEL = 1024
CHUNK = 64
WINDOW = 128
N_HEADS = 16
N_KV_HEADS = 4
GROUP_REP = N_HEADS // N_KV_HEADS
HEAD_DIM = 64
ATTN_WIDTH = N_HEADS * HEAD_DIM
KV_WIDTH = N_KV_HEADS * HEAD_DIM
POOL_WINDOWS = (2, 4, 8, 16)
POOL_WIDTH = D_MODEL
POOL_GROUP = POOL_WIDTH // len(POOL_WINDOWS)
POOL_HIST = max(POOL_WINDOWS) - 1
HIST_ROWS = POOL_HIST + 1
N_BUCKETS = 32
MAX_DISTANCE = 128
EPS = 1e-6
NEG_INF = -1e30

_SPLITS = (ATTN_WIDTH, KV_WIDTH, KV_WIDTH, ATTN_WIDTH, POOL_WIDTH, POOL_WIDTH, D_MODEL, D_MODEL)
_OFFS = tuple(sum(_SPLITS[:i]) for i in range(len(_SPLITS) + 1))
C_Q, C_K, C_V, C_AG, C_PU, C_PG, C_MA, C_MP = (slice(_OFFS[i], _OFFS[i + 1]) for i in range(8))

LANES = 128
MXU_DIM = 256
PAIR = 2 * CHUNK
PAIR_KEYS = PAIR + WINDOW
CHUNK_KEYS = CHUNK + WINDOW
TM = 512
PREFILL = 8
STAGE_ROWS, STAGE_COLS, N_STAGE = ATTN_WIDTH, TM, 3
N_SAMPLE_MOVES = 12
VMEM_LIMIT = 60 * 1024 * 1024

BF16 = jnp.bfloat16
F32 = jnp.float32


def _dot(a, b):
    return jnp.dot(a, b, preferred_element_type=F32)


def _dot_nt(a, b):
    return lax.dot_general(a, b, (((1,), (1,)), ((), ())), preferred_element_type=F32)


def _rmsnorm_rows(x, gain):
    y = x * lax.rsqrt(jnp.mean(x * x, axis=-1, keepdims=True) + EPS)
    return y * gain


def _head_rmsnorm(z, ones_ref, gain):
    sq = z * z
    hi = sq.astype(BF16)
    lo = (sq - hi.astype(F32)).astype(BF16)
    ones = ones_ref[...]
    parts = []
    for c in range(z.shape[1] // MXU_DIM):
        sl = slice(c * MXU_DIM, (c + 1) * MXU_DIM)
        parts.append(_dot(hi[:, sl], ones) + _dot(lo[:, sl], ones))
    ss = parts[0] if len(parts) == 1 else jnp.concatenate(parts, axis=1)
    return (z * lax.rsqrt(ss * (1.0 / HEAD_DIM) + EPS)) * gain


def _head_rmsnorm_fm(zt, gain_col, h):
    zh = zt[h * HEAD_DIM:(h + 1) * HEAD_DIM, :]
    ms = jnp.sum(zh * zh, axis=0, keepdims=True) * (1.0 / HEAD_DIM)
    return (zh * lax.rsqrt(ms + EPS)) * gain_col


def _window_sums(full, delay):
    outs = []
    for gi, w in enumerate(POOL_WINDOWS):
        s = full[..., gi * POOL_GROUP:(gi + 1) * POOL_GROUP]
        sh = 1
        while sh < w:
            s = s + delay(s, sh)
            sh *= 2
        outs.append(s)
    return outs


def _delay_rows(a, n):
    return pltpu.roll(a, n, axis=0)


def _delay_leading(a, n):
    return jnp.concatenate([a[-n:], a[:-n]], axis=0)


def _merge_branches(attn_o, pool_mixed, gates, pw_ref, ps_ref, wabr_ref, wpbr_ref, pool_rows=None):
    ag, pg, ma, mp = gates
    po = []
    for gi in range(len(POOL_WINDOWS)):
        po.append(_dot(pool_mixed[gi].astype(BF16), pw_ref[gi * POOL_GROUP:(gi + 1) * POOL_GROUP, :]))
    po = jnp.concatenate(po, axis=1)
    if pool_rows is not None:
        po = pool_rows(po)
    po = po * ps_ref[...]
    p = _dot((po * (pg * jax.nn.sigmoid(pg))).astype(BF16), wpbr_ref[...])
    a = _dot((attn_o * (ag * jax.nn.sigmoid(ag))).astype(BF16), wabr_ref[...])
    acc = jax.nn.sigmoid(ma) * a + jax.nn.sigmoid(mp) * p
    return acc.astype(BF16)


def _diag2(a, axis_r, axis_c):
    z = jnp.zeros_like(a)
    return jnp.concatenate([jnp.concatenate([a, z], axis=axis_c),
                            jnp.concatenate([z, a], axis=axis_c)], axis=axis_r)


def _gather_bias(bucket, tbl_ref, h):
    acc = jnp.zeros(bucket.shape, F32)
    for b in range(N_BUCKETS):
        acc = jnp.where(bucket == b, tbl_ref[h, b], acc)
    return acc


def _t5_bucket(rel):
    assert (N_BUCKETS, MAX_DISTANCE) == (32, 128)
    nb = N_BUCKETS // 2
    max_exact = nb // 2
    n = np.abs(rel)
    log2_sq = np.vectorize(lambda v: max(int(v), 1).bit_length() - 1)(n * n)
    large = np.minimum(max_exact + log2_sq - 6, nb - 1)
    return (np.where(rel > 0, nb, 0) + np.where(n < max_exact, n, large)).astype(np.int32)


def _bucket_tables(n_b, n_s):
    bkp = _t5_bucket(np.arange(CHUNK_KEYS)[:, None] - WINDOW - np.arange(LANES)[None, :] % CHUNK)
    fr = np.arange(n_s)[:, None]
    bks = _t5_bucket(np.arange(WINDOW)[None, :] - WINDOW - fr)
    bkn = _t5_bucket(np.arange(n_b * n_s)[None, :] % n_s - fr)
    return bkp, bks, bkn


def _weight_jobs(pairs):
    jobs = []
    for src, dst in pairs:
        n_rows, n_cols = dst.shape
        for r0 in range(0, n_rows, STAGE_ROWS):
            for c0 in range(0, n_cols, STAGE_COLS):
                jobs.append((src, dst, r0, c0, min(STAGE_COLS, n_cols - c0)))
    return jobs


def _prompt_kernel(tbl_ref, sinks_ref, x_ref, xn_ref, ng_ref, gcol_ref, bkp_ref, ps_ref,
                   win_hbm, wabr_hbm, wpbr_hbm, wout_hbm, pw_hbm,
                   y_ref, pk_ref, pv_ref, pp_ref, winb_hbm, wabrb_hbm, wpbrb_hbm, woutb_hbm, pwb_hbm,
                   win_ref, wabr_ref, wpbr_ref, wout_ref, pw_ref, stage, sem_in, sem_out,
                   wt_s, h_s, qt_s, k_s, vt_s, attn_s, pbuf, hist_s, bias_s):
    b = pl.program_id(0)
    t = pl.program_id(1)
    first = (b == 0) & (t == 0)
    q_gain = gcol_ref[:, 0:1]
    k_gain = gcol_ref[:, 1:2]
    zq_s = stage.at[0]
    weights = ((win_hbm, win_ref), (wabr_hbm, wabr_ref), (wpbr_hbm, wpbr_ref), (wout_hbm, wout_ref),
               (pw_hbm, pw_ref))
    exports = (winb_hbm, wabrb_hbm, wpbrb_hbm, woutb_hbm, pwb_hbm)

    def export_copy(i):
        return pltpu.make_async_copy(weights[i][1], exports[i], sem_out.at[i])

    @pl.when(first)
    def _():
        jobs = _weight_jobs(weights)

        def load(i):
            src, _, r0, c0, w = jobs[i]
            return pltpu.make_async_copy(src.at[pl.ds(r0, STAGE_ROWS), pl.ds(c0, w)],
                                         stage.at[i % N_STAGE, :, pl.ds(0, w)], sem_in.at[i % N_STAGE])

        for i in range(min(N_STAGE - 1, len(jobs))):
            load(i).start()
        for i in range(len(jobs)):
            if i + N_STAGE - 1 < len(jobs):
                load(i + N_STAGE - 1).start()
            load(i).wait()
            _, dst, r0, c0, w = jobs[i]
            dst[r0:r0 + STAGE_ROWS, c0:c0 + w] = stage[i % N_STAGE, :, 0:w].astype(BF16)
            if i < N_HEADS // 2:
                bkp = bkp_ref[...]
                pair = jnp.where(lax.broadcasted_iota(jnp.int32, bkp.shape, 1) < CHUNK,
                                 _gather_bias(bkp, tbl_ref, 2 * i), _gather_bias(bkp, tbl_ref, 2 * i + 1))
                bias_s[2 * i // GROUP_REP, :, (2 * i % GROUP_REP) * CHUNK:(2 * i % GROUP_REP + 2) * CHUNK] = pair
            if jobs[i][1] is win_ref and c0 + w == C_V.stop:
                wt_s[...] = win_ref[:, 0:C_V.stop].T
                h_s[...] = _rmsnorm_rows(x_ref[...], ng_ref[...]).astype(BF16)
        assert len(jobs) >= N_HEADS // 2 and C_V.stop % STAGE_COLS == 0
        zq_s[...] = _dot_nt(wt_s[C_Q, :], h_s[...])
        for i in range(len(weights)):
            export_copy(i).start()

    @pl.when(t == 0)
    def _():
        k_s[:, 0:WINDOW, :] = jnp.zeros((2, WINDOW, LANES), BF16)
        vt_s[:, 0:WINDOW] = jnp.zeros((KV_WIDTH, WINDOW), BF16)
        pbuf[0:HIST_ROWS, :] = jnp.zeros((HIST_ROWS, POOL_WIDTH), F32)

    hb = h_s[...]

    n_c = D_MODEL // MXU_DIM
    fill_cols = [slice(c.start + i * MXU_DIM, c.start + (i + 1) * MXU_DIM)
                 for c in (C_PU, C_PG, C_AG, C_MA, C_MP) for i in range(n_c)]
    filled = []

    def fill(upto):
        for cols in fill_cols[len(filled):upto]:
            filled.append(_dot(hb, win_ref[:, cols]))

    zqt = zq_s[...]
    zkt = _dot_nt(wt_s[C_K, :], hb)
    vt = _dot_nt(wt_s[C_V, :], hb)
    fill(PREFILL)
    kt = jnp.concatenate([_head_rmsnorm_fm(zkt, k_gain, g) for g in range(N_KV_HEADS)], axis=0)
    for h in range(N_HEADS):
        qt_s[h * HEAD_DIM:(h + 1) * HEAD_DIM, :] = _head_rmsnorm_fm(zqt, q_gain, h).astype(BF16)

    pk_ref[...] = kt[:, TM - WINDOW:]
    pv_ref[...] = vt[:, TM - WINDOW:]

    k_s[0, WINDOW:, :] = kt[:LANES, :].T.astype(BF16)
    k_s[1, WINDOW:, :] = kt[LANES:, :].T.astype(BF16)
    vt_s[:, WINDOW:] = vt.astype(BF16)

    zeros_half = jnp.zeros((HEAD_DIM, CHUNK), BF16)
    q_lane = lax.broadcasted_iota(jnp.int32, (1, GROUP_REP * CHUNK), 1)

    def scores(p, g):
        out = []
        for c in range(2):
            tok = slice(p * PAIR + c * CHUNK, p * PAIR + (c + 1) * CHUNK)
            keys = slice(p * PAIR + c * CHUNK, p * PAIR + c * CHUNK + CHUNK_KEYS)
            cols = []
            for r in range(GROUP_REP):
                h = g * GROUP_REP + r
                qh = qt_s[h * HEAD_DIM:(h + 1) * HEAD_DIM, tok]
                cols.append(jnp.concatenate([qh, zeros_half] if g % 2 == 0 else [zeros_half, qh], axis=0))
            rhs = jnp.concatenate(cols, axis=1)
            s = _dot(k_s[g // 2, keys, :], rhs) + bias_s[g]
            if p == 0:
                kk = lax.broadcasted_iota(jnp.int32, s.shape, 0)
                s = jnp.where((kk >= WINDOW - c * CHUNK) | (t > 0), s, NEG_INF)
            out.append(s)
        return out

    def finish(p, g, s_ab):
        tok = slice(p * PAIR, (p + 1) * PAIR)
        keys = slice(p * PAIR, p * PAIR + PAIR_KEYS)
        sink = sinks_ref[g * GROUP_REP]
        for r in range(1, GROUP_REP):
            sink = jnp.where(q_lane < r * CHUNK, sink, sinks_ref[g * GROUP_REP + r])
        pads = jnp.zeros((PAIR_KEYS - CHUNK_KEYS, GROUP_REP * CHUNK), BF16)
        e_ab, inv = [], []
        for c, s in enumerate(s_ab):
            m = jnp.maximum(jnp.max(s, axis=0, keepdims=True), sink)
            e = jnp.exp(s - m)
            inv.append(1.0 / (jnp.sum(e, axis=0, keepdims=True) + jnp.exp(sink - m)))
            e_ab.append(jnp.concatenate([e.astype(BF16), pads] if c == 0 else [pads, e.astype(BF16)], axis=0))
        vg = vt_s[g * HEAD_DIM:(g + 1) * HEAD_DIM, keys]
        o_t = _dot(vg, jnp.concatenate(e_ab, axis=1)) * jnp.concatenate(inv, axis=1)
        n_q = GROUP_REP * CHUNK
        for j in range(GROUP_REP // 2):
            rows = [jnp.concatenate([o_t[:, r * CHUNK:(r + 1) * CHUNK], o_t[:, n_q + r * CHUNK:n_q + (r + 1) * CHUNK]],
                                    axis=1) for r in (2 * j, 2 * j + 1)]
            lanes = slice(g * MXU_DIM + j * LANES, g * MXU_DIM + (j + 1) * LANES)
            attn_s[tok, lanes] = jnp.concatenate(rows, axis=0).T

    pu = jnp.concatenate(filled[:n_c], axis=1)
    pbuf[HIST_ROWS:, :] = pu
    hist_s[b] = pu[TM - HIST_ROWS:, :]
    sums = _window_sums(pbuf[...], _delay_rows)
    pos1 = (t * TM + 1 + lax.broadcasted_iota(jnp.int32, (TM, 1), 0)).astype(F32)
    mixed = []
    for gi, w in enumerate(POOL_WINDOWS):
        inv_cnt = 1.0 / jnp.minimum(float(w), pos1)
        mixed.append(sums[gi][HIST_ROWS:, :] * inv_cnt - pu[:, gi * POOL_GROUP:(gi + 1) * POOL_GROUP])

    blocks = [(p, g) for p in range(TM // PAIR) for g in range(N_KV_HEADS)]
    s_next = scores(*blocks[0])
    for i, blk in enumerate(blocks):
        s_cur = s_next
        if i + 1 < len(blocks):
            s_next = scores(*blocks[i + 1])
        fill(PREFILL + (len(fill_cols) - PREFILL) * (i + 1) // len(blocks))
        finish(*blk, s_cur)
    pg, ag, ma, mp = (jnp.concatenate(filled[j * n_c:(j + 1) * n_c], axis=1) for j in range(1, 5))

    merged = _merge_branches(attn_s[...], mixed, (ag, pg, ma, mp), pw_ref, ps_ref, wabr_ref, wpbr_ref)

    h_s[...] = _rmsnorm_rows(xn_ref[...], ng_ref[...]).astype(BF16)
    zq_s[...] = _dot_nt(wt_s[C_Q, :], h_s[...])

    y_ref[...] = x_ref[...] + _dot(merged, wout_ref[...])

    k_s[:, 0:WINDOW, :] = k_s[:, TM:TM + WINDOW, :]
    vt_s[:, 0:WINDOW] = vt_s[:, TM:TM + WINDOW]
    pbuf[0:HIST_ROWS, :] = pbuf[TM:TM + HIST_ROWS, :]

    @pl.when(first)
    def _():
        for i in range(len(weights)):
            export_copy(i).wait()

    @pl.when((b == pl.num_programs(0) - 1) & (t == pl.num_programs(1) - 1))
    def _():
        pp_ref[...] = jnp.swapaxes(hist_s[...], 0, 1)[HIST_ROWS - POOL_HIST:]


def _resident(shape):
    nd = len(shape)
    return pl.BlockSpec(shape, lambda b, t: (0,) * nd, pipeline_mode=pl.Buffered(1))


def _prompt_layer(x, tbl, sinks, ng, gain_cols, bkp, ps, weights):
    B, S, _ = x.shape
    assert S % TM == 0 and TM % PAIR == 0 and TM >= WINDOW
    assert all(w.shape[0] % STAGE_ROWS == 0 for w in weights)
    consts = (ng, gain_cols, bkp, ps)
    hbm = pl.BlockSpec(memory_space=pl.ANY)
    n_t = S // TM

    def next_tile(b, t):
        i = jnp.minimum(b * n_t + t + 1, B * n_t - 1)
        return i // n_t, i % n_t, 0

    return pl.pallas_call(
        _prompt_kernel,
        grid=(B, S // TM),
        in_specs=[pl.BlockSpec(memory_space=pltpu.SMEM),
                  pl.BlockSpec(memory_space=pltpu.SMEM),
                  pl.BlockSpec((None, TM, D_MODEL), lambda b, t: (b, t, 0)),
                  pl.BlockSpec((None, TM, D_MODEL), next_tile)]
                 + [_resident(c.shape) for c in consts] + [hbm] * len(weights),
        out_specs=[
            pl.BlockSpec((None, TM, D_MODEL), lambda b, t: (b, t, 0)),
            pl.BlockSpec((None, KV_WIDTH, WINDOW), lambda b, t: (b, 0, 0)),
            pl.BlockSpec((None, KV_WIDTH, WINDOW), lambda b, t: (b, 0, 0)),
            pl.BlockSpec((POOL_HIST, B, POOL_WIDTH), lambda b, t: (0, 0, 0)),
        ] + [hbm] * len(weights),
        out_shape=[
            jax.ShapeDtypeStruct((B, S, D_MODEL), F32),
            jax.ShapeDtypeStruct((B, KV_WIDTH, WINDOW), F32),
            jax.ShapeDtypeStruct((B, KV_WIDTH, WINDOW), F32),
            jax.ShapeDtypeStruct((POOL_HIST, B, POOL_WIDTH), F32),
        ] + [jax.ShapeDtypeStruct(w.shape, BF16) for w in weights],
        scratch_shapes=[pltpu.VMEM(w.shape, BF16) for w in weights] + [
            pltpu.VMEM((N_STAGE, STAGE_ROWS, STAGE_COLS), F32),
            pltpu.SemaphoreType.DMA((N_STAGE,)),
            pltpu.SemaphoreType.DMA((len(weights),)),
            pltpu.VMEM((C_V.stop, D_MODEL), BF16),
            pltpu.VMEM((TM, D_MODEL), BF16),
            pltpu.VMEM((ATTN_WIDTH, TM), BF16),
            pltpu.VMEM((2, WINDOW + TM, LANES), BF16),
            pltpu.VMEM((KV_WIDTH, WINDOW + TM), BF16),
            pltpu.VMEM((TM, ATTN_WIDTH), F32),
            pltpu.VMEM((HIST_ROWS + TM, POOL_WIDTH), F32),
            pltpu.VMEM((B, HIST_ROWS, POOL_WIDTH), F32),
            pltpu.VMEM((N_KV_HEADS, CHUNK_KEYS, GROUP_REP * CHUNK), F32),
        ],
        compiler_params=pltpu.CompilerParams(
            dimension_semantics=("arbitrary", "arbitrary"),
            vmem_limit_bytes=VMEM_LIMIT,
        ),
        name="prompt_layer",
    )(tbl, sinks, x, x, *consts, *weights)


def _sample_kernel(n_b, n_s, start,
                   tbl_ref, sinks_ref, x_ref, ng_ref,
                   gcol_ref, grow_ref, ones_ref, bks_ref, bkn_ref, ps_ref,
                   skt_hbm, svt_hbm, sp_hbm, win_hbm, pw_hbm, wabr_hbm, wpbr_hbm, wout_hbm,
                   y_ref, ok_hbm, ov_hbm, op_hbm,
                   skt_ref, svt_ref, sp_ref, ok_ref, ov_ref, op_ref,
                   win_ref, pw_ref, wabr_ref, wpbr_ref, wout_ref, sem):
    n_tok = n_b * n_s
    qkv = slice(0, C_V.stop)
    rest = slice(C_V.stop, win_ref.shape[1])
    moves = ((win_hbm.at[:, qkv], win_ref.at[:, qkv]), (skt_hbm, skt_ref), (svt_hbm, svt_ref),
             (win_hbm.at[:, rest], win_ref.at[:, rest]), (sp_hbm, sp_ref), (pw_hbm, pw_ref),
             (wabr_hbm, wabr_ref), (wpbr_hbm, wpbr_ref), (wout_hbm, wout_ref),
             (ok_ref, ok_hbm), (ov_ref, ov_hbm), (op_ref, op_hbm))
    assert len(moves) == N_SAMPLE_MOVES
    moves = [pltpu.make_async_copy(src, dst, sem.at[i]) for i, (src, dst) in enumerate(moves)]
    (qkv_in, k_in, v_in), late_in, (k_out, v_out, p_out) = moves[:3], moves[3:9], moves[9:]
    for c in moves[:9]:
        c.start()

    x = x_ref[...]
    hb = _rmsnorm_rows(x, ng_ref[...]).astype(BF16)

    qkv_in.wait()
    q = _head_rmsnorm(_dot(hb, win_ref[:, C_Q]), ones_ref, grow_ref[:, :ATTN_WIDTH]).astype(BF16)
    q3 = q.reshape(n_b, n_s, ATTN_WIDTH)
    zkt = _dot_nt(win_ref[:, C_K].T, hb)
    k_gain = gcol_ref[:, 1:2]
    knt = jnp.concatenate([_head_rmsnorm_fm(zkt, k_gain, g) for g in range(N_KV_HEADS)], axis=0)
    vnt = _dot_nt(win_ref[:, C_V].T, hb)

    k_in.wait()
    v_in.wait()
    lane = lax.broadcasted_iota(jnp.int32, (1, WINDOW), 1)
    per_tile = LANES // n_s
    for s in range(n_b):
        tile = slice((s // per_tile) * LANES, (s // per_tile + 1) * LANES)
        shift = (WINDOW - n_s - (s % per_tile) * n_s) % LANES
        for new, st_ref, o_ref in ((knt, skt_ref, ok_ref), (vnt, svt_ref, ov_ref)):
            fresh = new[:, tile] if shift == 0 else pltpu.roll(new[:, tile], shift, axis=1)
            o_ref[s] = jnp.where(lane < WINDOW - n_s, pltpu.roll(st_ref[s], WINDOW - n_s, axis=1), fresh)
    k_out.start()
    v_out.start()

    knb = knt.astype(BF16)
    vnb = vnt.astype(BF16)
    key_stream = (lax.broadcasted_iota(jnp.int32, (n_b, 1, 2 * n_tok), 2) & (n_tok - 1)) >> (n_s.bit_length() - 1)
    own = key_stream == lax.broadcasted_iota(jnp.int32, (n_b, 1, 2 * n_tok), 0)
    top = lax.broadcasted_iota(jnp.int32, (1, 2 * n_s, 1), 1) < n_s
    low = lax.broadcasted_iota(jnp.int32, (1, 1, LANES), 2) < HEAD_DIM

    def bias_rows(bucket, g):
        return jnp.concatenate([jnp.concatenate(
            [_gather_bias(bucket, tbl_ref, g * GROUP_REP + 2 * c + j) for j in range(2)], axis=1)
            for c in range(2)], axis=0)

    def scores(g):
        hd = slice(g * HEAD_DIM, (g + 1) * HEAD_DIM)
        c = g * MXU_DIM
        lhs = jnp.concatenate([q3[:, :, c:c + LANES], q3[:, :, c + LANES:c + MXU_DIM]], axis=1)
        kst = _diag2(skt_ref[:, hd, :].astype(BF16), 1, 2)
        s_st = jnp.stack([_dot(lhs[s], kst[s]) for s in range(n_b)]) + bias_rows(bks_ref[...], g)[None]
        s_nw = _dot(lhs.reshape(2 * n_tok, LANES), _diag2(knb[hd, :], 0, 1))
        s_nw = jnp.where(own, s_nw.reshape(n_b, 2 * n_s, 2 * n_tok) + bias_rows(bkn_ref[...], g)[None], NEG_INF)
        return s_st, s_nw

    def finish(g, s_st, s_nw):
        hd = slice(g * HEAD_DIM, (g + 1) * HEAD_DIM)
        p_st, p_nw, inv = [], [], []
        for j in range(2):
            a = s_st[:, :, j * WINDOW:(j + 1) * WINDOW]
            b = s_nw[:, :, j * n_tok:(j + 1) * n_tok]
            sink = jnp.where(top, sinks_ref[4 * g + j], sinks_ref[4 * g + 2 + j])
            m = jnp.maximum(jnp.maximum(jnp.max(a, axis=-1, keepdims=True),
                                        jnp.max(b, axis=-1, keepdims=True)), sink)
            ea = jnp.exp(a - m)
            eb = jnp.exp(b - m)
            inv.append(1.0 / (jnp.sum(ea, axis=-1, keepdims=True) + jnp.sum(eb, axis=-1, keepdims=True)
                              + jnp.exp(sink - m)))
            p_st.append(ea.astype(BF16))
            p_nw.append(eb.astype(BF16))
        p_st = jnp.concatenate(p_st, axis=2)
        p_nw = jnp.concatenate(p_nw, axis=2).reshape(2 * n_tok, 2 * n_tok)
        vst = _diag2(svt_ref[:, hd, :].astype(BF16), 1, 2)
        o = jnp.stack([_dot_nt(p_st[s], vst[s]) for s in range(n_b)])
        o = o + _dot_nt(p_nw, _diag2(vnb[hd, :], 0, 1)).reshape(n_b, 2 * n_s, LANES)
        o = o * jnp.where(low, inv[0], inv[1])
        return jnp.concatenate([o[:, :n_s, :], o[:, n_s:, :]], axis=2)

    outs = []
    nxt = scores(0)
    for g in range(N_KV_HEADS):
        cur = nxt
        if g + 1 < N_KV_HEADS:
            nxt = scores(g + 1)
        outs.append(finish(g, *cur))
    attn_o = jnp.concatenate(outs, axis=2).reshape(n_tok, ATTN_WIDTH)

    for c in late_in:
        c.wait()
    pu = _dot(hb, win_ref[:, C_PU])
    pu_t = jnp.swapaxes(pu.reshape(n_b, n_s, POOL_WIDTH), 0, 1)
    full = jnp.concatenate([jnp.zeros((1, n_b, POOL_WIDTH), F32), sp_ref[...], pu_t], axis=0)
    op_ref[...] = full[HIST_ROWS + n_s - POOL_HIST:]
    p_out.start()
    sums = _window_sums(full, _delay_leading)
    pos1 = (start + 1 + lax.broadcasted_iota(jnp.int32, (n_s, 1, 1), 0)).astype(F32)
    mixed = []
    for gi, w in enumerate(POOL_WINDOWS):
        inv_cnt = 1.0 / jnp.minimum(float(w), pos1)
        sw = sums[gi][HIST_ROWS:] * inv_cnt - pu_t[..., gi * POOL_GROUP:(gi + 1) * POOL_GROUP]
        mixed.append(sw.reshape(n_tok, POOL_GROUP))

    def stream_major(po):
        return jnp.swapaxes(po.reshape(n_s, n_b, POOL_WIDTH), 0, 1).reshape(n_tok, POOL_WIDTH)

    gates = tuple(_dot(hb, win_ref[:, c]) for c in (C_AG, C_PG, C_MA, C_MP))
    merged = _merge_branches(attn_o, mixed, gates, pw_ref, ps_ref, wabr_ref, wpbr_ref, stream_major)
    y_ref[...] = x + _dot(merged, wout_ref[...])
    for c in (k_out, v_out, p_out):
        c.wait()


def _sample_layer(x, skt, svt, sp, start, tbl, sinks, ng, w_in, gain_cols, gain_rows, ones, bks, bkn,
                  pw, ps, wabr, wpbr, wout):
    n_b, n_s, _ = x.shape
    n_tok = n_b * n_s
    assert n_s % 16 == 0 and LANES % n_s == 0 and n_s >= POOL_HIST and n_tok % LANES == 0
    assert n_s & (n_s - 1) == 0 and n_tok & (n_tok - 1) == 0
    vmem = pl.BlockSpec(memory_space=pltpu.VMEM)
    anywhere = pl.BlockSpec(memory_space=pl.ANY)
    weights = (w_in, pw, wabr, wpbr, wout)
    state = (skt, svt, sp)
    new_state = [jax.ShapeDtypeStruct(a.shape, a.dtype) for a in state]
    y, ok, ov, op = pl.pallas_call(
        functools.partial(_sample_kernel, n_b, n_s, start),
        in_specs=[pl.BlockSpec(memory_space=pltpu.SMEM)] * 2 + [vmem] * 8
                 + [anywhere] * (len(state) + len(weights)),
        out_specs=[vmem] + [anywhere] * len(state),
        out_shape=[jax.ShapeDtypeStruct((n_tok, D_MODEL), F32)] + new_state,
        scratch_shapes=[pltpu.VMEM(a.shape, a.dtype) for a in state + tuple(new_state) + weights]
                       + [pltpu.SemaphoreType.DMA((N_SAMPLE_MOVES,))],
        compiler_params=pltpu.CompilerParams(vmem_limit_bytes=VMEM_LIMIT),
        name="sample_layer",
    )(tbl, sinks, x.reshape(n_tok, D_MODEL), ng, gain_cols, gain_rows, ones, bks, bkn, ps,
      *state, *weights)
    return y.reshape(n_b, n_s, D_MODEL), ok, ov, op


def _kv_to_feature_major(a):
    return jnp.transpose(a, (0, 2, 3, 1)).reshape(a.shape[0], KV_WIDTH, WINDOW)


def _kv_from_feature_major(a):
    return jnp.transpose(a.reshape(a.shape[0], N_KV_HEADS, HEAD_DIM, WINDOW), (0, 3, 1, 2))


def kernel(x_prompt, x_sample, state_attn_k, state_attn_v, state_pool, norm_gain, w_in, q_norm_gain, k_norm_gain, attn_sinks, rel_bias, pool_w, pool_scale, w_attn_br, w_pool_br, w_out):
    depth = w_in.shape[0]
    n_b, n_s, _ = x_sample.shape
    past_len = 4096
    bkp, bks, bkn = _bucket_tables(n_b, n_s)
    tbl = rel_bias.T
    seg = np.arange(MXU_DIM) // HEAD_DIM
    ones = jnp.asarray(seg[:, None] == seg[None, :], BF16)

    xp, xs = x_prompt, x_sample
    pk, pv, pp, sk, sv, sp = [], [], [], [], [], []
    for l in range(depth):
        ng = norm_gain[l].reshape(1, D_MODEL)
        qg = q_norm_gain[l] * (HEAD_DIM ** -0.5)
        kg = k_norm_gain[l]
        gain_cols = jnp.stack([qg, kg], axis=1)
        gain_rows = jnp.concatenate([jnp.tile(qg, N_HEADS), jnp.tile(kg, N_KV_HEADS)]).reshape(1, -1)
        ps = pool_scale[l].reshape(1, POOL_WIDTH)
        xp, k_l, v_l, p_l, w_in_b, wabr_b, wpbr_b, wout_b, pw_b = _prompt_layer(
            xp, tbl, attn_sinks[l], ng, gain_cols, bkp, ps,
            (w_in[l], w_attn_br[l], w_pool_br[l], w_out[l], pool_w[l].reshape(POOL_WIDTH, POOL_GROUP)))
        tail = (pw_b, ps, wabr_b, wpbr_b, wout_b)
        pk.append(_kv_from_feature_major(k_l))
        pv.append(_kv_from_feature_major(v_l))
        pp.append(jnp.transpose(p_l, (1, 0, 2)))

        xs, k_l, v_l, p_l = _sample_layer(
            xs,
            _kv_to_feature_major(state_attn_k[l]),
            _kv_to_feature_major(state_attn_v[l]),
            jnp.transpose(state_pool[l], (1, 0, 2)),
            past_len, tbl, attn_sinks[l], ng, w_in_b, gain_cols, gain_rows, ones, bks, bkn, *tail)
        sk.append(_kv_from_feature_major(k_l))
        sv.append(_kv_from_feature_major(v_l))
        sp.append(jnp.transpose(p_l, (1, 0, 2)))
    return (xp, xs, jnp.stack(pk), jnp.stack(pv), jnp.stack(pp),
            jnp.stack(sk), jnp.stack(sv), jnp.stack(sp))
```

```python
import functools
import numpy as np

import jax
import jax.numpy as jnp
from jax import lax
from jax.experimental import pallas as pl
from jax.experimental.pallas import tpu as pltpu

D_MODEL = 1024
CHUNK = 64
WINDOW = 128
N_HEADS = 16
N_KV_HEADS = 4
GROUP_REP = N_HEADS // N_KV_HEADS
HEAD_DIM = 64
ATTN_WIDTH = N_HEADS * HEAD_DIM
KV_WIDTH = N_KV_HEADS * HEAD_DIM
POOL_WINDOWS = (2, 4, 8, 16)
POOL_WIDTH = D_MODEL
POOL_GROUP = POOL_WIDTH // len(POOL_WINDOWS)
POOL_HIST = max(POOL_WINDOWS) - 1
HIST_ROWS = POOL_HIST + 1
N_BUCKETS = 32
MAX_DISTANCE = 128
EPS = 1e-6
NEG_INF = -1e30

_SPLITS = (ATTN_WIDTH, KV_WIDTH, KV_WIDTH, ATTN_WIDTH, POOL_WIDTH, POOL_WIDTH, D_MODEL, D_MODEL)
_OFFS = tuple(sum(_SPLITS[:i]) for i in range(len(_SPLITS) + 1))
C_Q, C_K, C_V, C_AG, C_PU, C_PG, C_MA, C_MP = (slice(_OFFS[i], _OFFS[i + 1]) for i in range(8))

LANES = 128
MXU_DIM = 256
PAIR = 2 * CHUNK
PAIR_KEYS = PAIR + WINDOW
CHUNK_KEYS = CHUNK + WINDOW
TM = 512
PREFILL = 8
STAGE_ROWS, STAGE_COLS, N_STAGE = ATTN_WIDTH, TM, 3
N_SAMPLE_MOVES = 12
VMEM_LIMIT = 60 * 1024 * 1024

BF16 = jnp.bfloat16
F32 = jnp.float32


def _dot(a, b):
    return jnp.dot(a, b, preferred_element_type=F32)


def _dot_nt(a, b):
    return lax.dot_general(a, b, (((1,), (1,)), ((), ())), preferred_element_type=F32)


def _rmsnorm_rows(x, gain):
    y = x * lax.rsqrt(jnp.mean(x * x, axis=-1, keepdims=True) + EPS)
    return y * gain


def _head_rmsnorm(z, ones_ref, gain):
    sq = z * z
    hi = sq.astype(BF16)
    lo = (sq - hi.astype(F32)).astype(BF16)
    ones = ones_ref[...]
    parts = []
    for c in range(z.shape[1] // MXU_DIM):
        sl = slice(c * MXU_DIM, (c + 1) * MXU_DIM)
        parts.append(_dot(hi[:, sl], ones) + _dot(lo[:, sl], ones))
    ss = parts[0] if len(parts) == 1 else jnp.concatenate(parts, axis=1)
    return (z * lax.rsqrt(ss * (1.0 / HEAD_DIM) + EPS)) * gain


def _head_rmsnorm_fm(zt, gain_col, h):
    zh = zt[h * HEAD_DIM:(h + 1) * HEAD_DIM, :]
    ms = jnp.sum(zh * zh, axis=0, keepdims=True) * (1.0 / HEAD_DIM)
    return (zh * lax.rsqrt(ms + EPS)) * gain_col


def _window_sums(full, delay):
    outs = []
    for gi, w in enumerate(POOL_WINDOWS):
        s = full[..., gi * POOL_GROUP:(gi + 1) * POOL_GROUP]
        sh = 1
        while sh < w:
            s = s + delay(s, sh)
            sh *= 2
        outs.append(s)
    return outs


def _delay_rows(a, n):
    return pltpu.roll(a, n, axis=0)


def _delay_leading(a, n):
    return jnp.concatenate([a[-n:], a[:-n]], axis=0)


def _merge_branches(attn_o, pool_mixed, gates, pw_ref, ps_ref, wabr_ref, wpbr_ref, pool_rows=None):
    ag, pg, ma, mp = gates
    po = []
    for gi in range(len(POOL_WINDOWS)):
        po.append(_dot(pool_mixed[gi].astype(BF16), pw_ref[gi * POOL_GROUP:(gi + 1) * POOL_GROUP, :]))
    po = jnp.concatenate(po, axis=1)
    if pool_rows is not None:
        po = pool_rows(po)
    po = po * ps_ref[...]
    p = _dot((po * (pg * jax.nn.sigmoid(pg))).astype(BF16), wpbr_ref[...])
    a = _dot((attn_o * (ag * jax.nn.sigmoid(ag))).astype(BF16), wabr_ref[...])
    acc = jax.nn.sigmoid(ma) * a + jax.nn.sigmoid(mp) * p
    return acc.astype(BF16)


def _diag2(a, axis_r, axis_c):
    z = jnp.zeros_like(a)
    return jnp.concatenate([jnp.concatenate([a, z], axis=axis_c),
                            jnp.concatenate([z, a], axis=axis_c)], axis=axis_r)


def _gather_bias(bucket, tbl_ref, h):
    acc = jnp.zeros(bucket.shape, F32)
    for b in range(N_BUCKETS):
        acc = jnp.where(bucket == b, tbl_ref[h, b], acc)
    return acc


def _t5_bucket(rel):
    assert (N_BUCKETS, MAX_DISTANCE) == (32, 128)
    nb = N_BUCKETS // 2
    max_exact = nb // 2
    n = np.abs(rel)
    log2_sq = np.vectorize(lambda v: max(int(v), 1).bit_length() - 1)(n * n)
    large = np.minimum(max_exact + log2_sq - 6, nb - 1)
    return (np.where(rel > 0, nb, 0) + np.where(n < max_exact, n, large)).astype(np.int32)


def _bucket_tables(n_b, n_s):
    bkp = _t5_bucket(np.arange(CHUNK_KEYS)[:, None] - WINDOW - np.arange(LANES)[None, :] % CHUNK)
    fr = np.arange(n_s)[:, None]
    bks = _t5_bucket(np.arange(WINDOW)[None, :] - WINDOW - fr)
    bkn = _t5_bucket(np.arange(n_b * n_s)[None, :] % n_s - fr)
    return bkp, bks, bkn


def _weight_jobs(pairs):
    jobs = []
    for src, dst in pairs:
        n_rows, n_cols = dst.shape
        for r0 in range(0, n_rows, STAGE_ROWS):
            for c0 in range(0, n_cols, STAGE_COLS):
                jobs.append((src, dst, r0, c0, min(STAGE_COLS, n_cols - c0)))
    return jobs


def _prompt_kernel(tbl_ref, sinks_ref, x_ref, xn_ref, ng_ref, gcol_ref, bkp_ref, ps_ref,
                   win_hbm, wabr_hbm, wpbr_hbm, wout_hbm, pw_hbm,
                   y_ref, pk_ref, pv_ref, pp_ref, winb_hbm, wabrb_hbm, wpbrb_hbm, woutb_hbm, pwb_hbm,
                   win_ref, wabr_ref, wpbr_ref, wout_ref, pw_ref, stage, sem_in, sem_out,
                   wt_s, h_s, qt_s, k_s, vt_s, attn_s, pbuf, hist_s, bias_s):
    b = pl.program_id(0)
    t = pl.program_id(1)
    first = (b == 0) & (t == 0)
    q_gain = gcol_ref[:, 0:1]
    k_gain = gcol_ref[:, 1:2]
    zq_s = stage.at[0]
    weights = ((win_hbm, win_ref), (wabr_hbm, wabr_ref), (wpbr_hbm, wpbr_ref), (wout_hbm, wout_ref),
               (pw_hbm, pw_ref))
    exports = (winb_hbm, wabrb_hbm, wpbrb_hbm, woutb_hbm, pwb_hbm)

    def export_copy(i):
        return pltpu.make_async_copy(weights[i][1], exports[i], sem_out.at[i])

    @pl.when(first)
    def _():
        jobs = _weight_jobs(weights)

        def load(i):
            src, _, r0, c0, w = jobs[i]
            return pltpu.make_async_copy(src.at[pl.ds(r0, STAGE_ROWS), pl.ds(c0, w)],
                                         stage.at[i % N_STAGE, :, pl.ds(0, w)], sem_in.at[i % N_STAGE])

        for i in range(min(N_STAGE - 1, len(jobs))):
            load(i).start()
        for i in range(len(jobs)):
            if i + N_STAGE - 1 < len(jobs):
                load(i + N_STAGE - 1).start()
            load(i).wait()
            _, dst, r0, c0, w = jobs[i]
            dst[r0:r0 + STAGE_ROWS, c0:c0 + w] = stage[i % N_STAGE, :, 0:w].astype(BF16)
            if i < N_HEADS // 2:
                bkp = bkp_ref[...]
                pair = jnp.where(lax.broadcasted_iota(jnp.int32, bkp.shape, 1) < CHUNK,
                                 _gather_bias(bkp, tbl_ref, 2 * i), _gather_bias(bkp, tbl_ref, 2 * i + 1))
                bias_s[2 * i // GROUP_REP, :, (2 * i % GROUP_REP) * CHUNK:(2 * i % GROUP_REP + 2) * CHUNK] = pair
            if jobs[i][1] is win_ref and c0 + w == C_V.stop:
                wt_s[...] = win_ref[:, 0:C_V.stop].T
                h_s[...] = _rmsnorm_rows(x_ref[...], ng_ref[...]).astype(BF16)
        assert len(jobs) >= N_HEADS // 2 and C_V.stop % STAGE_COLS == 0
        zq_s[...] = _dot_nt(wt_s[C_Q, :], h_s[...])
        for i in range(len(weights)):
            export_copy(i).start()

    @pl.when(t == 0)
    def _():
        k_s[:, 0:WINDOW, :] = jnp.zeros((2, WINDOW, LANES), BF16)
        vt_s[:, 0:WINDOW] = jnp.zeros((KV_WIDTH, WINDOW), BF16)
        pbuf[0:HIST_ROWS, :] = jnp.zeros((HIST_ROWS, POOL_WIDTH), F32)

    hb = h_s[...]

    n_c = D_MODEL // MXU_DIM
    fill_cols = [slice(c.start + i * MXU_DIM, c.start + (i + 1) * MXU_DIM)
                 for c in (C_PU, C_PG, C_AG, C_MA, C_MP) for i in range(n_c)]
    filled = []

    def fill(upto):
        for cols in fill_cols[len(filled):upto]:
            filled.append(_dot(hb, win_ref[:, cols]))

    zqt = zq_s[...]
    zkt = _dot_nt(wt_s[C_K, :], hb)
    vt = _dot_nt(wt_s[C_V, :], hb)
    fill(PREFILL)
    kt = jnp.concatenate([_head_rmsnorm_fm(zkt, k_gain, g) for g in range(N_KV_HEADS)], axis=0)
    for h in range(N_HEADS):
        qt_s[h * HEAD_DIM:(h + 1) * HEAD_DIM, :] = _head_rmsnorm_fm(zqt, q_gain, h).astype(BF16)

    pk_ref[...] = kt[:, TM - WINDOW:]
    pv_ref[...] = vt[:, TM - WINDOW:]

    k_s[0, WINDOW:, :] = kt[:LANES, :].T.astype(BF16)
    k_s[1, WINDOW:, :] = kt[LANES:, :].T.astype(BF16)
    vt_s[:, WINDOW:] = vt.astype(BF16)

    zeros_half = jnp.zeros((HEAD_DIM, CHUNK), BF16)
    q_lane = lax.broadcasted_iota(jnp.int32, (1, GROUP_REP * CHUNK), 1)

    def scores(p, g):
        out = []
        for c in range(2):
            tok = slice(p * PAIR + c * CHUNK, p * PAIR + (c + 1) * CHUNK)
            keys = slice(p * PAIR + c * CHUNK, p * PAIR + c * CHUNK + CHUNK_KEYS)
            cols = []
            for r in range(GROUP_REP):
                h = g * GROUP_REP + r
                qh = qt_s[h * HEAD_DIM:(h + 1) * HEAD_DIM, tok]
                cols.append(jnp.concatenate([qh, zeros_half] if g % 2 == 0 else [zeros_half, qh], axis=0))
            rhs = jnp.concatenate(cols, axis=1)
            s = _dot(k_s[g // 2, keys, :], rhs) + bias_s[g]
            if p == 0:
                kk = lax.broadcasted_iota(jnp.int32, s.shape, 0)
                s = jnp.where((kk >= WINDOW - c * CHUNK) | (t > 0), s, NEG_INF)
            out.append(s)
        return out

    def finish(p, g, s_ab):
        tok = slice(p * PAIR, (p + 1) * PAIR)
        keys = slice(p * PAIR, p * PAIR + PAIR_KEYS)
        sink = sinks_ref[g * GROUP_REP]
        for r in range(1, GROUP_REP):
            sink = jnp.where(q_lane < r * CHUNK, sink, sinks_ref[g * GROUP_REP + r])
        pads = jnp.zeros((PAIR_KEYS - CHUNK_KEYS, GROUP_REP * CHUNK), BF16)
        e_ab, inv = [], []
        for c, s in enumerate(s_ab):
            m = jnp.maximum(jnp.max(s, axis=0, keepdims=True), sink)
            e = jnp.exp(s - m)
            inv.append(1.0 / (jnp.sum(e, axis=0, keepdims=True) + jnp.exp(sink - m)))
            e_ab.append(jnp.concatenate([e.astype(BF16), pads] if c == 0 else [pads, e.astype(BF16)], axis=0))
        vg = vt_s[g * HEAD_DIM:(g + 1) * HEAD_DIM, keys]
        o_t = _dot(vg, jnp.concatenate(e_ab, axis=1)) * jnp.concatenate(inv, axis=1)
        n_q = GROUP_REP * CHUNK
        for j in range(GROUP_REP // 2):
            rows = [jnp.concatenate([o_t[:, r * CHUNK:(r + 1) * CHUNK], o_t[:, n_q + r * CHUNK:n_q + (r + 1) * CHUNK]],
                                    axis=1) for r in (2 * j, 2 * j + 1)]
            lanes = slice(g * MXU_DIM + j * LANES, g * MXU_DIM + (j + 1) * LANES)
            attn_s[tok, lanes] = jnp.concatenate(rows, axis=0).T

    pu = jnp.concatenate(filled[:n_c], axis=1)
    pbuf[HIST_ROWS:, :] = pu
    hist_s[b] = pu[TM - HIST_ROWS:, :]
    sums = _window_sums(pbuf[...], _delay_rows)
    pos1 = (t * TM + 1 + lax.broadcasted_iota(jnp.int32, (TM, 1), 0)).astype(F32)
    mixed = []
    for gi, w in enumerate(POOL_WINDOWS):
        inv_cnt = 1.0 / jnp.minimum(float(w), pos1)
        mixed.append(sums[gi][HIST_ROWS:, :] * inv_cnt - pu[:, gi * POOL_GROUP:(gi + 1) * POOL_GROUP])

    blocks = [(p, g) for p in range(TM // PAIR) for g in range(N_KV_HEADS)]
    s_next = scores(*blocks[0])
    for i, blk in enumerate(blocks):
        s_cur = s_next
        if i + 1 < len(blocks):
            s_next = scores(*blocks[i + 1])
        fill(PREFILL + (len(fill_cols) - PREFILL) * (i + 1) // len(blocks))
        finish(*blk, s_cur)
    pg, ag, ma, mp = (jnp.concatenate(filled[j * n_c:(j + 1) * n_c], axis=1) for j in range(1, 5))

    merged = _merge_branches(attn_s[...], mixed, (ag, pg, ma, mp), pw_ref, ps_ref, wabr_ref, wpbr_ref)

    h_s[...] = _rmsnorm_rows(xn_ref[...], ng_ref[...]).astype(BF16)
    zq_s[...] = _dot_nt(wt_s[C_Q, :], h_s[...])

    y_ref[...] = x_ref[...] + _dot(merged, wout_ref[...])

    k_s[:, 0:WINDOW, :] = k_s[:, TM:TM + WINDOW, :]
    vt_s[:, 0:WINDOW] = vt_s[:, TM:TM + WINDOW]
    pbuf[0:HIST_ROWS, :] = pbuf[TM:TM + HIST_ROWS, :]

    @pl.when(first)
    def _():
        for i in range(len(weights)):
            export_copy(i).wait()

    @pl.when((b == pl.num_programs(0) - 1) & (t == pl.num_programs(1) - 1))
    def _():
        pp_ref[...] = jnp.swapaxes(hist_s[...], 0, 1)[HIST_ROWS - POOL_HIST:]


def _resident(shape):
    nd = len(shape)
    return pl.BlockSpec(shape, lambda b, t: (0,) * nd, pipeline_mode=pl.Buffered(1))


def _prompt_layer(x, tbl, sinks, ng, gain_cols, bkp, ps, weights):
    B, S, _ = x.shape
    assert S % TM == 0 and TM % PAIR == 0 and TM >= WINDOW
    assert all(w.shape[0] % STAGE_ROWS == 0 for w in weights)
    consts = (ng, gain_cols, bkp, ps)
    hbm = pl.BlockSpec(memory_space=pl.ANY)
    n_t = S // TM

    def next_tile(b, t):
        i = jnp.minimum(b * n_t + t + 1, B * n_t - 1)
        return i // n_t, i % n_t, 0

    return pl.pallas_call(
        _prompt_kernel,
        grid=(B, S // TM),
        in_specs=[pl.BlockSpec(memory_space=pltpu.SMEM),
                  pl.BlockSpec(memory_space=pltpu.SMEM),
                  pl.BlockSpec((None, TM, D_MODEL), lambda b, t: (b, t, 0)),
                  pl.BlockSpec((None, TM, D_MODEL), next_tile)]
                 + [_resident(c.shape) for c in consts] + [hbm] * len(weights),
        out_specs=[
            pl.BlockSpec((None, TM, D_MODEL), lambda b, t: (b, t, 0)),
            pl.BlockSpec((None, KV_WIDTH, WINDOW), lambda b, t: (b, 0, 0)),
            pl.BlockSpec((None, KV_WIDTH, WINDOW), lambda b, t: (b, 0, 0)),
            pl.BlockSpec((POOL_HIST, B, POOL_WIDTH), lambda b, t: (0, 0, 0)),
        ] + [hbm] * len(weights),
        out_shape=[
            jax.ShapeDtypeStruct((B, S, D_MODEL), F32),
            jax.ShapeDtypeStruct((B, KV_WIDTH, WINDOW), F32),
            jax.ShapeDtypeStruct((B, KV_WIDTH, WINDOW), F32),
            jax.ShapeDtypeStruct((POOL_HIST, B, POOL_WIDTH), F32),
        ] + [jax.ShapeDtypeStruct(w.shape, BF16) for w in weights],
        scratch_shapes=[pltpu.VMEM(w.shape, BF16) for w in weights] + [
            pltpu.VMEM((N_STAGE, STAGE_ROWS, STAGE_COLS), F32),
            pltpu.SemaphoreType.DMA((N_STAGE,)),
            pltpu.SemaphoreType.DMA((len(weights),)),
            pltpu.VMEM((C_V.stop, D_MODEL), BF16),
            pltpu.VMEM((TM, D_MODEL), BF16),
            pltpu.VMEM((ATTN_WIDTH, TM), BF16),
            pltpu.VMEM((2, WINDOW + TM, LANES), BF16),
            pltpu.VMEM((KV_WIDTH, WINDOW + TM), BF16),
            pltpu.VMEM((TM, ATTN_WIDTH), F32),
            pltpu.VMEM((HIST_ROWS + TM, POOL_WIDTH), F32),
            pltpu.VMEM((B, HIST_ROWS, POOL_WIDTH), F32),
            pltpu.VMEM((N_KV_HEADS, CHUNK_KEYS, GROUP_REP * CHUNK), F32),
        ],
        compiler_params=pltpu.CompilerParams(
            dimension_semantics=("arbitrary", "arbitrary"),
            vmem_limit_bytes=VMEM_LIMIT,
        ),
        name="prompt_layer",
    )(tbl, sinks, x, x, *consts, *weights)


def _sample_kernel(n_b, n_s, start,
                   tbl_ref, sinks_ref, x_ref, ng_ref,
                   gcol_ref, grow_ref, ones_ref, bks_ref, bkn_ref, ps_ref,
                   skt_hbm, svt_hbm, sp_hbm, win_hbm, pw_hbm, wabr_hbm, wpbr_hbm, wout_hbm,
                   y_ref, ok_hbm, ov_hbm, op_hbm,
                   skt_ref, svt_ref, sp_ref, ok_ref, ov_ref, op_ref,
                   win_ref, pw_ref, wabr_ref, wpbr_ref, wout_ref, sem):
    n_tok = n_b * n_s
    qkv = slice(0, C_V.stop)
    rest = slice(C_V.stop, win_ref.shape[1])
    moves = ((win_hbm.at[:, qkv], win_ref.at[:, qkv]), (skt_hbm, skt_ref), (svt_hbm, svt_ref),
             (win_hbm.at[:, rest], win_ref.at[:, rest]), (sp_hbm, sp_ref), (pw_hbm, pw_ref),
             (wabr_hbm, wabr_ref), (wpbr_hbm, wpbr_ref), (wout_hbm, wout_ref),
             (ok_ref, ok_hbm), (ov_ref, ov_hbm), (op_ref, op_hbm))
    assert len(moves) == N_SAMPLE_MOVES
    moves = [pltpu.make_async_copy(src, dst, sem.at[i]) for i, (src, dst) in enumerate(moves)]
    (qkv_in, k_in, v_in), late_in, (k_out, v_out, p_out) = moves[:3], moves[3:9], moves[9:]
    for c in moves[:9]:
        c.start()

    x = x_ref[...]
    hb = _rmsnorm_rows(x, ng_ref[...]).astype(BF16)

    qkv_in.wait()
    q = _head_rmsnorm(_dot(hb, win_ref[:, C_Q]), ones_ref, grow_ref[:, :ATTN_WIDTH]).astype(BF16)
    q3 = q.reshape(n_b, n_s, ATTN_WIDTH)
    zkt = _dot_nt(win_ref[:, C_K].T, hb)
    k_gain = gcol_ref[:, 1:2]
    knt = jnp.concatenate([_head_rmsnorm_fm(zkt, k_gain, g) for g in range(N_KV_HEADS)], axis=0)
    vnt = _dot_nt(win_ref[:, C_V].T, hb)

    k_in.wait()
    v_in.wait()
    lane = lax.broadcasted_iota(jnp.int32, (1, WINDOW), 1)
    per_tile = LANES // n_s
    for s in range(n_b):
        tile = slice((s // per_tile) * LANES, (s // per_tile + 1) * LANES)
        shift = (WINDOW - n_s - (s % per_tile) * n_s) % LANES
        for new, st_ref, o_ref in ((knt, skt_ref, ok_ref), (vnt, svt_ref, ov_ref)):
            fresh = new[:, tile] if shift == 0 else pltpu.roll(new[:, tile], shift, axis=1)
            o_ref[s] = jnp.where(lane < WINDOW - n_s, pltpu.roll(st_ref[s], WINDOW - n_s, axis=1), fresh)

    knb = knt.astype(BF16)
    vnb = vnt.astype(BF16)
    key_stream = (lax.broadcasted_iota(jnp.int32, (n_b, 1, 2 * n_tok), 2) & (n_tok - 1)) >> (n_s.bit_length() - 1)
    own = key_stream == lax.broadcasted_iota(jnp.int32, (n_b, 1, 2 * n_tok), 0)
    top = lax.broadcasted_iota(jnp.int32, (1, 2 * n_s, 1), 1) < n_s
    low = lax.broadcasted_iota(jnp.int32, (1, 1, LANES), 2) < HEAD_DIM

    def bias_rows(bucket, g):
        return jnp.concatenate([jnp.concatenate(
            [_gather_bias(bucket, tbl_ref, g * GROUP_REP + 2 * c + j) for j in range(2)], axis=1)
            for c in range(2)], axis=0)

    def scores(g):
        hd = slice(g * HEAD_DIM, (g + 1) * HEAD_DIM)
        c = g * MXU_DIM
        lhs = jnp.concatenate([q3[:, :, c:c + LANES], q3[:, :, c + LANES:c + MXU_DIM]], axis=1)
        kst = _diag2(skt_ref[:, hd, :].astype(BF16), 1, 2)
        s_st = jnp.stack([_dot(lhs[s], kst[s]) for s in range(n_b)]) + bias_rows(bks_ref[...], g)[None]
        s_nw = _dot(lhs.reshape(2 * n_tok, LANES), _diag2(knb[hd, :], 0, 1))
        s_nw = jnp.where(own, s_nw.reshape(n_b, 2 * n_s, 2 * n_tok) + bias_rows(bkn_ref[...], g)[None], NEG_INF)
        return s_st, s_nw

    def finish(g, s_st, s_nw):
        hd = slice(g * HEAD_DIM, (g + 1) * HEAD_DIM)
        p_st, p_nw, inv = [], [], []
        for j in range(2):
            a = s_st[:, :, j * WINDOW:(j + 1) * WINDOW]
            b = s_nw[:, :, j * n_tok:(j + 1) * n_tok]
            sink = jnp.where(top, sinks_ref[4 * g + j], sinks_ref[4 * g + 2 + j])
            m = jnp.maximum(jnp.maximum(jnp.max(a, axis=-1, keepdims=True),
                                        jnp.max(b, axis=-1, keepdims=True)), sink)
            ea = jnp.exp(a - m)
            eb = jnp.exp(b - m)
            inv.append(1.0 / (jnp.sum(ea, axis=-1, keepdims=True) + jnp.sum(eb, axis=-1, keepdims=True)
                              + jnp.exp(sink - m)))
            p_st.append(ea.astype(BF16))
            p_nw.append(eb.astype(BF16))
        p_st = jnp.concatenate(p_st, axis=2)
        p_nw = jnp.concatenate(p_nw, axis=2).reshape(2 * n_tok, 2 * n_tok)
        vst = _diag2(svt_ref[:, hd, :].astype(BF16), 1, 2)
        o = jnp.stack([_dot_nt(p_st[s], vst[s]) for s in range(n_b)])
        o = o + _dot_nt(p_nw, _diag2(vnb[hd, :], 0, 1)).reshape(n_b, 2 * n_s, LANES)
        o = o * jnp.where(low, inv[0], inv[1])
        return jnp.concatenate([o[:, :n_s, :], o[:, n_s:, :]], axis=2)

    outs = []
    nxt = scores(0)
    for g in range(N_KV_HEADS):
        cur = nxt
        if g + 1 < N_KV_HEADS:
            nxt = scores(g + 1)
        outs.append(finish(g, *cur))
    attn_o = jnp.concatenate(outs, axis=2).reshape(n_tok, ATTN_WIDTH)

    for c in late_in:
        c.wait()
    k_out.start()
    v_out.start()
    pu = _dot(hb, win_ref[:, C_PU])
    pu_t = jnp.swapaxes(pu.reshape(n_b, n_s, POOL_WIDTH), 0, 1)
    full = jnp.concatenate([jnp.zeros((1, n_b, POOL_WIDTH), F32), sp_ref[...], pu_t], axis=0)
    op_ref[...] = full[HIST_ROWS + n_s - POOL_HIST:]
    p_out.start()
    sums = _window_sums(full, _delay_leading)
    pos1 = (start + 1 + lax.broadcasted_iota(jnp.int32, (n_s, 1, 1), 0)).astype(F32)
    mixed = []
    for gi, w in enumerate(POOL_WINDOWS):
        inv_cnt = 1.0 / jnp.minimum(float(w), pos1)
        sw = sums[gi][HIST_ROWS:] * inv_cnt - pu_t[..., gi * POOL_GROUP:(gi + 1) * POOL_GROUP]
        mixed.append(sw.reshape(n_tok, POOL_GROUP))

    def stream_major(po):
        return jnp.swapaxes(po.reshape(n_s, n_b, POOL_WIDTH), 0, 1).reshape(n_tok, POOL_WIDTH)

    gates = tuple(_dot(hb, win_ref[:, c]) for c in (C_AG, C_PG, C_MA, C_MP))
    merged = _merge_branches(attn_o, mixed, gates, pw_ref, ps_ref, wabr_ref, wpbr_ref, stream_major)
    y_ref[...] = x + _dot(merged, wout_ref[...])
    for c in (k_out, v_out, p_out):
        c.wait()


def _sample_layer(x, skt, svt, sp, start, tbl, sinks, ng, w_in, gain_cols, gain_rows, ones, bks, bkn,
                  pw, ps, wabr, wpbr, wout):
    n_b, n_s, _ = x.shape
    n_tok = n_b * n_s
    assert n_s % 16 == 0 and LANES % n_s == 0 and n_s >= POOL_HIST and n_tok % LANES == 0
    assert n_s & (n_s - 1) == 0 and n_tok & (n_tok - 1) == 0
    vmem = pl.BlockSpec(memory_space=pltpu.VMEM)
    anywhere = pl.BlockSpec(memory_space=pl.ANY)
    weights = (w_in, pw, wabr, wpbr, wout)
    state = (skt, svt, sp)
    new_state = [jax.ShapeDtypeStruct(a.shape, a.dtype) for a in state]
    y, ok, ov, op = pl.pallas_call(
        functools.partial(_sample_kernel, n_b, n_s, start),
        in_specs=[pl.BlockSpec(memory_space=pltpu.SMEM)] * 2 + [vmem] * 8
                 + [anywhere] * (len(state) + len(weights)),
        out_specs=[vmem] + [anywhere] * len(state),
        out_shape=[jax.ShapeDtypeStruct((n_tok, D_MODEL), F32)] + new_state,
        scratch_shapes=[pltpu.VMEM(a.shape, a.dtype) for a in state + tuple(new_state) + weights]
                       + [pltpu.SemaphoreType.DMA((N_SAMPLE_MOVES,))],
        compiler_params=pltpu.CompilerParams(vmem_limit_bytes=VMEM_LIMIT),
        name="sample_layer",
    )(tbl, sinks, x.reshape(n_tok, D_MODEL), ng, gain_cols, gain_rows, ones, bks, bkn, ps,
      *state, *weights)
    return y.reshape(n_b, n_s, D_MODEL), ok, ov, op


def _kv_to_feature_major(a):
    return jnp.transpose(a, (0, 2, 3, 1)).reshape(a.shape[0], KV_WIDTH, WINDOW)


def _kv_from_feature_major(a):
    return jnp.transpose(a.reshape(a.shape[0], N_KV_HEADS, HEAD_DIM, WINDOW), (0, 3, 1, 2))


def kernel(x_prompt, x_sample, state_attn_k, state_attn_v, state_pool, norm_gain, w_in, q_norm_gain, k_norm_gain, attn_sinks, rel_bias, pool_w, pool_scale, w_attn_br, w_pool_br, w_out):
    depth = w_in.shape[0]
    n_b, n_s, _ = x_sample.shape
    past_len = 4096
    bkp, bks, bkn = _bucket_tables(n_b, n_s)
    tbl = rel_bias.T
    seg = np.arange(MXU_DIM) // HEAD_DIM
    ones = jnp.asarray(seg[:, None] == seg[None, :], BF16)

    xp, xs = x_prompt, x_sample
    pk, pv, pp, sk, sv, sp = [], [], [], [], [], []
    for l in range(depth):
        ng = norm_gain[l].reshape(1, D_MODEL)
        qg = q_norm_gain[l] * (HEAD_DIM ** -0.5)
        kg = k_norm_gain[l]
        gain_cols = jnp.stack([qg, kg], axis=1)
        gain_rows = jnp.concatenate([jnp.tile(qg, N_HEADS), jnp.tile(kg, N_KV_HEADS)]).reshape(1, -1)
        ps = pool_scale[l].reshape(1, POOL_WIDTH)
        xp, k_l, v_l, p_l, w_in_b, wabr_b, wpbr_b, wout_b, pw_b = _prompt_layer(
            xp, tbl, attn_sinks[l], ng, gain_cols, bkp, ps,
            (w_in[l], w_attn_br[l], w_pool_br[l], w_out[l], pool_w[l].reshape(POOL_WIDTH, POOL_GROUP)))
        tail = (pw_b, ps, wabr_b, wpbr_b, wout_b)
        pk.append(_kv_from_feature_major(k_l))
        pv.append(_kv_from_feature_major(v_l))
        pp.append(jnp.transpose(p_l, (1, 0, 2)))

        xs, k_l, v_l, p_l = _sample_layer(
            xs,
            _kv_to_feature_major(state_attn_k[l]),
            _kv_to_feature_major(state_attn_v[l]),
            jnp.transpose(state_pool[l], (1, 0, 2)),
            past_len, tbl, attn_sinks[l], ng, w_in_b, gain_cols, gain_rows, ones, bks, bkn, *tail)
        sk.append(_kv_from_feature_major(k_l))
        sv.append(_kv_from_feature_major(v_l))
        sp.append(jnp.transpose(p_l, (1, 0, 2)))
    return (xp, xs, jnp.stack(pk), jnp.stack(pv), jnp.stack(pp),
            jnp.stack(sk), jnp.stack(sv), jnp.stack(sp))
```

```python
import functools
import numpy as np

import jax
import jax.numpy as jnp
from jax import lax
from jax.experimental import pallas as pl
from jax.experimental.pallas import tpu as pltpu

D_MODEL = 1024
CHUNK = 64
WINDOW = 128
N_HEADS = 16
N_KV_HEADS = 4
GROUP_REP = N_HEADS // N_KV_HEADS
HEAD_DIM = 64
ATTN_WIDTH = N_HEADS * HEAD_DIM
KV_WIDTH = N_KV_HEADS * HEAD_DIM
POOL_WINDOWS = (2, 4, 8, 16)
POOL_WIDTH = D_MODEL
POOL_GROUP = POOL_WIDTH // len(POOL_WINDOWS)
POOL_HIST = max(POOL_WINDOWS) - 1
HIST_ROWS = POOL_HIST + 1
N_BUCKETS = 32
MAX_DISTANCE = 128
EPS = 1e-6
NEG_INF = -1e30

_SPLITS = (ATTN_WIDTH, KV_WIDTH, KV_WIDTH, ATTN_WIDTH, POOL_WIDTH, POOL_WIDTH, D_MODEL, D_MODEL)
_OFFS = tuple(sum(_SPLITS[:i]) for i in range(len(_SPLITS) + 1))
C_Q, C_K, C_V, C_AG, C_PU, C_PG, C_MA, C_MP = (slice(_OFFS[i], _OFFS[i + 1]) for i in range(8))

LANES = 128
MXU_DIM = 256
PAIR = 2 * CHUNK
PAIR_KEYS = PAIR + WINDOW
CHUNK_KEYS = CHUNK + WINDOW
TM = 512
PREFILL = 8
STAGE_ROWS, STAGE_COLS, N_STAGE = ATTN_WIDTH, TM, 3
N_SAMPLE_MOVES = 12
VMEM_LIMIT = 60 * 1024 * 1024

BF16 = jnp.bfloat16
F32 = jnp.float32


def _dot(a, b):
    return jnp.dot(a, b, preferred_element_type=F32)


def _dot_nt(a, b):
    return lax.dot_general(a, b, (((1,), (1,)), ((), ())), preferred_element_type=F32)


def _rmsnorm_rows(x, gain):
    y = x * lax.rsqrt(jnp.mean(x * x, axis=-1, keepdims=True) + EPS)
    return y * gain


def _head_rmsnorm(z, ones_ref, gain):
    sq = z * z
    hi = sq.astype(BF16)
    lo = (sq - hi.astype(F32)).astype(BF16)
    ones = ones_ref[...]
    parts = []
    for c in range(z.shape[1] // MXU_DIM):
        sl = slice(c * MXU_DIM, (c + 1) * MXU_DIM)
        parts.append(_dot(hi[:, sl], ones) + _dot(lo[:, sl], ones))
    ss = parts[0] if len(parts) == 1 else jnp.concatenate(parts, axis=1)
    return (z * lax.rsqrt(ss * (1.0 / HEAD_DIM) + EPS)) * gain


def _head_rmsnorm_fm(zt, gain_col, h):
    zh = zt[h * HEAD_DIM:(h + 1) * HEAD_DIM, :]
    ms = jnp.sum(zh * zh, axis=0, keepdims=True) * (1.0 / HEAD_DIM)
    return (zh * lax.rsqrt(ms + EPS)) * gain_col


def _window_sums(full, delay):
    outs = []
    for gi, w in enumerate(POOL_WINDOWS):
        s = full[..., gi * POOL_GROUP:(gi + 1) * POOL_GROUP]
        sh = 1
        while sh < w:
            s = s + delay(s, sh)
            sh *= 2
        outs.append(s)
    return outs


def _delay_rows(a, n):
    return pltpu.roll(a, n, axis=0)


def _delay_leading(a, n):
    return jnp.concatenate([a[-n:], a[:-n]], axis=0)


def _merge_branches(attn_o, pool_mixed, gates, pw_ref, ps_ref, wabr_ref, wpbr_ref, pool_rows=None):
    ag, pg, ma, mp = gates
    po = []
    for gi in range(len(POOL_WINDOWS)):
        po.append(_dot(pool_mixed[gi].astype(BF16), pw_ref[gi * POOL_GROUP:(gi + 1) * POOL_GROUP, :]))
    po = jnp.concatenate(po, axis=1)
    if pool_rows is not None:
        po = pool_rows(po)
    po = po * ps_ref[...]
    p = _dot((po * (pg * jax.nn.sigmoid(pg))).astype(BF16), wpbr_ref[...])
    a = _dot((attn_o * (ag * jax.nn.sigmoid(ag))).astype(BF16), wabr_ref[...])
    acc = jax.nn.sigmoid(ma) * a + jax.nn.sigmoid(mp) * p
    return acc.astype(BF16)


def _diag2(a, axis_r, axis_c):
    z = jnp.zeros_like(a)
    return jnp.concatenate([jnp.concatenate([a, z], axis=axis_c),
                            jnp.concatenate([z, a], axis=axis_c)], axis=axis_r)


def _gather_bias(bucket, tbl_ref, h):
    acc = jnp.zeros(bucket.shape, F32)
    for b in range(N_BUCKETS):
        acc = jnp.where(bucket == b, tbl_ref[h, b], acc)
    return acc


def _t5_bucket(rel):
    assert (N_BUCKETS, MAX_DISTANCE) == (32, 128)
    nb = N_BUCKETS // 2
    max_exact = nb // 2
    n = np.abs(rel)
    log2_sq = np.vectorize(lambda v: max(int(v), 1).bit_length() - 1)(n * n)
    large = np.minimum(max_exact + log2_sq - 6, nb - 1)
    return (np.where(rel > 0, nb, 0) + np.where(n < max_exact, n, large)).astype(np.int32)


def _bucket_tables(n_b, n_s):
    bkp = _t5_bucket(np.arange(CHUNK_KEYS)[:, None] - WINDOW - np.arange(LANES)[None, :] % CHUNK)
    fr = np.arange(n_s)[:, None]
    bks = _t5_bucket(np.arange(WINDOW)[None, :] - WINDOW - fr)
    bkn = _t5_bucket(np.arange(n_b * n_s)[None, :] % n_s - fr)
    return bkp, bks, bkn


def _weight_jobs(pairs):
    jobs = []
    for src, dst in pairs:
        n_rows, n_cols = dst.shape
        for r0 in range(0, n_rows, STAGE_ROWS):
            for c0 in range(0, n_cols, STAGE_COLS):
                jobs.append((src, dst, r0, c0, min(STAGE_COLS, n_cols - c0)))
    return jobs


def _prompt_kernel(tbl_ref, sinks_ref, x_ref, xn_ref, ng_ref, gcol_ref, bkp_ref, ps_ref,
                   win_hbm, wabr_hbm, wpbr_hbm, wout_hbm, pw_hbm,
                   y_ref, pk_ref, pv_ref, pp_ref, winb_hbm, wabrb_hbm, wpbrb_hbm, woutb_hbm, pwb_hbm,
                   win_ref, wabr_ref, wpbr_ref, wout_ref, pw_ref, stage, sem_in, sem_out,
                   wt_s, h_s, qt_s, k_s, vt_s, attn_s, pbuf, hist_s, bias_s):
    b = pl.program_id(0)
    t = pl.program_id(1)
    first = (b == 0) & (t == 0)
    q_gain = gcol_ref[:, 0:1]
    k_gain = gcol_ref[:, 1:2]
    zq_s = stage.at[0]
    weights = ((win_hbm, win_ref), (wabr_hbm, wabr_ref), (wpbr_hbm, wpbr_ref), (wout_hbm, wout_ref),
               (pw_hbm, pw_ref))
    exports = (winb_hbm, wabrb_hbm, wpbrb_hbm, woutb_hbm, pwb_hbm)

    def export_copy(i):
        return pltpu.make_async_copy(weights[i][1], exports[i], sem_out.at[i])

    @pl.when(first)
    def _():
        jobs = _weight_jobs(weights)

        def load(i):
            src, _, r0, c0, w = jobs[i]
            return pltpu.make_async_copy(src.at[pl.ds(r0, STAGE_ROWS), pl.ds(c0, w)],
                                         stage.at[i % N_STAGE, :, pl.ds(0, w)], sem_in.at[i % N_STAGE])

        for i in range(min(N_STAGE - 1, len(jobs))):
            load(i).start()
        for i in range(len(jobs)):
            if i + N_STAGE - 1 < len(jobs):
                load(i + N_STAGE - 1).start()
            load(i).wait()
            _, dst, r0, c0, w = jobs[i]
            dst[r0:r0 + STAGE_ROWS, c0:c0 + w] = stage[i % N_STAGE, :, 0:w].astype(BF16)
            if i < N_HEADS // 2:
                bkp = bkp_ref[...]
                pair = jnp.where(lax.broadcasted_iota(jnp.int32, bkp.shape, 1) < CHUNK,
                                 _gather_bias(bkp, tbl_ref, 2 * i), _gather_bias(bkp, tbl_ref, 2 * i + 1))
                bias_s[2 * i // GROUP_REP, :, (2 * i % GROUP_REP) * CHUNK:(2 * i % GROUP_REP + 2) * CHUNK] = pair
            if jobs[i][1] is win_ref and c0 + w == C_V.stop:
                wt_s[...] = win_ref[:, 0:C_V.stop].T
                h_s[...] = _rmsnorm_rows(x_ref[...], ng_ref[...]).astype(BF16)
        assert len(jobs) >= N_HEADS // 2 and C_V.stop % STAGE_COLS == 0
        zq_s[...] = _dot_nt(wt_s[C_Q, :], h_s[...])
        for i in range(len(weights)):
            export_copy(i).start()

    @pl.when(t == 0)
    def _():
        k_s[:, 0:WINDOW, :] = jnp.zeros((2, WINDOW, LANES), BF16)
        vt_s[:, 0:WINDOW] = jnp.zeros((KV_WIDTH, WINDOW), BF16)
        pbuf[0:HIST_ROWS, :] = jnp.zeros((HIST_ROWS, POOL_WIDTH), F32)

    hb = h_s[...]

    n_c = D_MODEL // MXU_DIM
    fill_cols = [slice(c.start + i * MXU_DIM, c.start + (i + 1) * MXU_DIM)
                 for c in (C_PU, C_PG, C_AG, C_MA, C_MP) for i in range(n_c)]
    filled = []

    def fill(upto):
        for cols in fill_cols[len(filled):upto]:
            filled.append(_dot(hb, win_ref[:, cols]))

    zqt = zq_s[...]
    zkt = _dot_nt(wt_s[C_K, :], hb)
    vt = _dot_nt(wt_s[C_V, :], hb)
    fill(PREFILL)
    kt = jnp.concatenate([_head_rmsnorm_fm(zkt, k_gain, g) for g in range(N_KV_HEADS)], axis=0)
    for h in range(N_HEADS):
        qt_s[h * HEAD_DIM:(h + 1) * HEAD_DIM, :] = _head_rmsnorm_fm(zqt, q_gain, h).astype(BF16)

    pk_ref[...] = kt[:, TM - WINDOW:]
    pv_ref[...] = vt[:, TM - WINDOW:]

    k_s[0, WINDOW:, :] = kt[:LANES, :].T.astype(BF16)
    k_s[1, WINDOW:, :] = kt[LANES:, :].T.astype(BF16)
    vt_s[:, WINDOW:] = vt.astype(BF16)

    zeros_half = jnp.zeros((HEAD_DIM, CHUNK), BF16)
    q_lane = lax.broadcasted_iota(jnp.int32, (1, GROUP_REP * CHUNK), 1)

    def scores(p, g):
        out = []
        for c in range(2):
            tok = slice(p * PAIR + c * CHUNK, p * PAIR + (c + 1) * CHUNK)
            keys = slice(p * PAIR + c * CHUNK, p * PAIR + c * CHUNK + CHUNK_KEYS)
            cols = []
            for r in range(GROUP_REP):
                h = g * GROUP_REP + r
                qh = qt_s[h * HEAD_DIM:(h + 1) * HEAD_DIM, tok]
                cols.append(jnp.concatenate([qh, zeros_half] if g % 2 == 0 else [zeros_half, qh], axis=0))
            rhs = jnp.concatenate(cols, axis=1)
            s = _dot(k_s[g // 2, keys, :], rhs) + bias_s[g]
            if p == 0:
                kk = lax.broadcasted_iota(jnp.int32, s.shape, 0)
                s = jnp.where((kk >= WINDOW - c * CHUNK) | (t > 0), s, NEG_INF)
            out.append(s)
        return out

    def finish(p, g, s_ab):
        tok = slice(p * PAIR, (p + 1) * PAIR)
        keys = slice(p * PAIR, p * PAIR + PAIR_KEYS)
        sink = sinks_ref[g * GROUP_REP]
        for r in range(1, GROUP_REP):
            sink = jnp.where(q_lane < r * CHUNK, sink, sinks_ref[g * GROUP_REP + r])
        pads = jnp.zeros((PAIR_KEYS - CHUNK_KEYS, GROUP_REP * CHUNK), BF16)
        e_ab, inv = [], []
        for c, s in enumerate(s_ab):
            m = jnp.maximum(jnp.max(s, axis=0, keepdims=True), sink)
            e = jnp.exp(s - m)
            inv.append(1.0 / (jnp.sum(e, axis=0, keepdims=True) + jnp.exp(sink - m)))
            e_ab.append(jnp.concatenate([e.astype(BF16), pads] if c == 0 else [pads, e.astype(BF16)], axis=0))
        vg = vt_s[g * HEAD_DIM:(g + 1) * HEAD_DIM, keys]
        o_t = _dot(vg, jnp.concatenate(e_ab, axis=1)) * jnp.concatenate(inv, axis=1)
        n_q = GROUP_REP * CHUNK
        for j in range(GROUP_REP // 2):
            rows = [jnp.concatenate([o_t[:, r * CHUNK:(r + 1) * CHUNK], o_t[:, n_q + r * CHUNK:n_q + (r + 1) * CHUNK]],
                                    axis=1) for r in (2 * j, 2 * j + 1)]
            lanes = slice(g * MXU_DIM + j * LANES, g * MXU_DIM + (j + 1) * LANES)
            attn_s[tok, lanes] = jnp.concatenate(rows, axis=0).T

    pu = jnp.concatenate(filled[:n_c], axis=1)
    pbuf[HIST_ROWS:, :] = pu
    hist_s[b] = pu[TM - HIST_ROWS:, :]
    sums = _window_sums(pbuf[...], _delay_rows)
    pos1 = (t * TM + 1 + lax.broadcasted_iota(jnp.int32, (TM, 1), 0)).astype(F32)
    mixed = []
    for gi, w in enumerate(POOL_WINDOWS):
        inv_cnt = 1.0 / jnp.minimum(float(w), pos1)
        mixed.append(sums[gi][HIST_ROWS:, :] * inv_cnt - pu[:, gi * POOL_GROUP:(gi + 1) * POOL_GROUP])

    blocks = [(p, g) for p in range(TM // PAIR) for g in range(N_KV_HEADS)]
    s_next = scores(*blocks[0])
    for i, blk in enumerate(blocks):
        s_cur = s_next
        if i + 1 < len(blocks):
            s_next = scores(*blocks[i + 1])
        fill(PREFILL + (len(fill_cols) - PREFILL) * (i + 1) // len(blocks))
        finish(*blk, s_cur)
    pg, ag, ma, mp = (jnp.concatenate(filled[j * n_c:(j + 1) * n_c], axis=1) for j in range(1, 5))

    merged = _merge_branches(attn_s[...], mixed, (ag, pg, ma, mp), pw_ref, ps_ref, wabr_ref, wpbr_ref)

    h_s[...] = _rmsnorm_rows(xn_ref[...], ng_ref[...]).astype(BF16)
    zq_s[...] = _dot_nt(wt_s[C_Q, :], h_s[...])

    y_ref[...] = x_ref[...] + _dot(merged, wout_ref[...])

    k_s[:, 0:WINDOW, :] = k_s[:, TM:TM + WINDOW, :]
    vt_s[:, 0:WINDOW] = vt_s[:, TM:TM + WINDOW]
    pbuf[0:HIST_ROWS, :] = pbuf[TM:TM + HIST_ROWS, :]

    @pl.when(first)
    def _():
        for i in range(len(weights)):
            export_copy(i).wait()

    @pl.when((b == pl.num_programs(0) - 1) & (t == pl.num_programs(1) - 1))
    def _():
        pp_ref[...] = jnp.swapaxes(hist_s[...], 0, 1)[HIST_ROWS - POOL_HIST:]


def _resident(shape):
    nd = len(shape)
    return pl.BlockSpec(shape, lambda b, t: (0,) * nd, pipeline_mode=pl.Buffered(1))


def _prompt_layer(x, tbl, sinks, ng, gain_cols, bkp, ps, weights):
    B, S, _ = x.shape
    assert S % TM == 0 and TM % PAIR == 0 and TM >= WINDOW
    assert all(w.shape[0] % STAGE_ROWS == 0 for w in weights)
    consts = (ng, gain_cols, bkp, ps)
    hbm = pl.BlockSpec(memory_space=pl.ANY)
    n_t = S // TM

    def next_tile(b, t):
        i = jnp.minimum(b * n_t + t + 1, B * n_t - 1)
        return i // n_t, i % n_t, 0

    return pl.pallas_call(
        _prompt_kernel,
        grid=(B, S // TM),
        in_specs=[pl.BlockSpec(memory_space=pltpu.SMEM),
                  pl.BlockSpec(memory_space=pltpu.SMEM),
                  pl.BlockSpec((None, TM, D_MODEL), lambda b, t: (b, t, 0)),
                  pl.BlockSpec((None, TM, D_MODEL), next_tile)]
                 + [_resident(c.shape) for c in consts] + [hbm] * len(weights),
        out_specs=[
            pl.BlockSpec((None, TM, D_MODEL), lambda b, t: (b, t, 0)),
            pl.BlockSpec((None, KV_WIDTH, WINDOW), lambda b, t: (b, 0, 0)),
            pl.BlockSpec((None, KV_WIDTH, WINDOW), lambda b, t: (b, 0, 0)),
            pl.BlockSpec((POOL_HIST, B, POOL_WIDTH), lambda b, t: (0, 0, 0)),
        ] + [hbm] * len(weights),
        out_shape=[
            jax.ShapeDtypeStruct((B, S, D_MODEL), F32),
            jax.ShapeDtypeStruct((B, KV_WIDTH, WINDOW), F32),
            jax.ShapeDtypeStruct((B, KV_WIDTH, WINDOW), F32),
            jax.ShapeDtypeStruct((POOL_HIST, B, POOL_WIDTH), F32),
        ] + [jax.ShapeDtypeStruct(w.shape, BF16) for w in weights],
        scratch_shapes=[pltpu.VMEM(w.shape, BF16) for w in weights] + [
            pltpu.VMEM((N_STAGE, STAGE_ROWS, STAGE_COLS), F32),
            pltpu.SemaphoreType.DMA((N_STAGE,)),
            pltpu.SemaphoreType.DMA((len(weights),)),
            pltpu.VMEM((C_V.stop, D_MODEL), BF16),
            pltpu.VMEM((TM, D_MODEL), BF16),
            pltpu.VMEM((ATTN_WIDTH, TM), BF16),
            pltpu.VMEM((2, WINDOW + TM, LANES), BF16),
            pltpu.VMEM((KV_WIDTH, WINDOW + TM), BF16),
            pltpu.VMEM((TM, ATTN_WIDTH), F32),
            pltpu.VMEM((HIST_ROWS + TM, POOL_WIDTH), F32),
            pltpu.VMEM((B, HIST_ROWS, POOL_WIDTH), F32),
            pltpu.VMEM((N_KV_HEADS, CHUNK_KEYS, GROUP_REP * CHUNK), F32),
        ],
        compiler_params=pltpu.CompilerParams(
            dimension_semantics=("arbitrary", "arbitrary"),
            vmem_limit_bytes=VMEM_LIMIT,
        ),
        name="prompt_layer",
    )(tbl, sinks, x, x, *consts, *weights)


def _sample_kernel(n_b, n_s, start,
                   tbl_ref, sinks_ref, x_ref, ng_ref,
                   gcol_ref, grow_ref, ones_ref, bks_ref, bkn_ref, ps_ref,
                   skt_hbm, svt_hbm, sp_hbm, win_hbm, pw_hbm, wabr_hbm, wpbr_hbm, wout_hbm,
                   y_ref, ok_hbm, ov_hbm, op_hbm,
                   skt_ref, svt_ref, sp_ref, ok_ref, ov_ref, op_ref,
                   win_ref, pw_ref, wabr_ref, wpbr_ref, wout_ref, sem):
    n_tok = n_b * n_s
    qkv = slice(0, C_V.stop)
    rest = slice(C_V.stop, win_ref.shape[1])
    moves = ((win_hbm.at[:, qkv], win_ref.at[:, qkv]), (skt_hbm, skt_ref), (svt_hbm, svt_ref),
             (win_hbm.at[:, rest], win_ref.at[:, rest]), (sp_hbm, sp_ref), (pw_hbm, pw_ref),
             (wpbr_hbm, wpbr_ref), (wabr_hbm, wabr_ref), (wout_hbm, wout_ref),
             (ok_ref, ok_hbm), (ov_ref, ov_hbm), (op_ref, op_hbm))
    assert len(moves) == N_SAMPLE_MOVES
    moves = [pltpu.make_async_copy(src, dst, sem.at[i]) for i, (src, dst) in enumerate(moves)]
    (qkv_in, k_in, v_in), gates_in, merge_in, wout_in = moves[:3], moves[3:5], moves[5:8], moves[8]
    k_out, v_out, p_out = moves[9:]
    for c in moves[:9]:
        c.start()

    x = x_ref[...]
    hb = _rmsnorm_rows(x, ng_ref[...]).astype(BF16)

    qkv_in.wait()
    q = _head_rmsnorm(_dot(hb, win_ref[:, C_Q]), ones_ref, grow_ref[:, :ATTN_WIDTH]).astype(BF16)
    q3 = q.reshape(n_b, n_s, ATTN_WIDTH)
    zkt = _dot_nt(win_ref[:, C_K].T, hb)
    k_gain = gcol_ref[:, 1:2]
    knt = jnp.concatenate([_head_rmsnorm_fm(zkt, k_gain, g) for g in range(N_KV_HEADS)], axis=0)
    vnt = _dot_nt(win_ref[:, C_V].T, hb)

    k_in.wait()
    v_in.wait()
    lane = lax.broadcasted_iota(jnp.int32, (1, WINDOW), 1)
    per_tile = LANES // n_s
    for s in range(n_b):
        tile = slice((s // per_tile) * LANES, (s // per_tile + 1) * LANES)
        shift = (WINDOW - n_s - (s % per_tile) * n_s) % LANES
        for new, st_ref, o_ref in ((knt, skt_ref, ok_ref), (vnt, svt_ref, ov_ref)):
            fresh = new[:, tile] if shift == 0 else pltpu.roll(new[:, tile], shift, axis=1)
            o_ref[s] = jnp.where(lane < WINDOW - n_s, pltpu.roll(st_ref[s], WINDOW - n_s, axis=1), fresh)

    knb = knt.astype(BF16)
    vnb = vnt.astype(BF16)
    key_stream = (lax.broadcasted_iota(jnp.int32, (n_b, 1, 2 * n_tok), 2) & (n_tok - 1)) >> (n_s.bit_length() - 1)
    own = key_stream == lax.broadcasted_iota(jnp.int32, (n_b, 1, 2 * n_tok), 0)
    top = lax.broadcasted_iota(jnp.int32, (1, 2 * n_s, 1), 1) < n_s
    low = lax.broadcasted_iota(jnp.int32, (1, 1, LANES), 2) < HEAD_DIM

    def bias_rows(bucket, g):
        return jnp.concatenate([jnp.concatenate(
            [_gather_bias(bucket, tbl_ref, g * GROUP_REP + 2 * c + j) for j in range(2)], axis=1)
            for c in range(2)], axis=0)

    def scores(g):
        hd = slice(g * HEAD_DIM, (g + 1) * HEAD_DIM)
        c = g * MXU_DIM
        lhs = jnp.concatenate([q3[:, :, c:c + LANES], q3[:, :, c + LANES:c + MXU_DIM]], axis=1)
        kst = _diag2(skt_ref[:, hd, :].astype(BF16), 1, 2)
        s_st = jnp.stack([_dot(lhs[s], kst[s]) for s in range(n_b)]) + bias_rows(bks_ref[...], g)[None]
        s_nw = _dot(lhs.reshape(2 * n_tok, LANES), _diag2(knb[hd, :], 0, 1))
        s_nw = jnp.where(own, s_nw.reshape(n_b, 2 * n_s, 2 * n_tok) + bias_rows(bkn_ref[...], g)[None], NEG_INF)
        return s_st, s_nw

    def finish(g, s_st, s_nw):
        hd = slice(g * HEAD_DIM, (g + 1) * HEAD_DIM)
        p_st, p_nw, inv = [], [], []
        for j in range(2):
            a = s_st[:, :, j * WINDOW:(j + 1) * WINDOW]
            b = s_nw[:, :, j * n_tok:(j + 1) * n_tok]
            sink = jnp.where(top, sinks_ref[4 * g + j], sinks_ref[4 * g + 2 + j])
            m = jnp.maximum(jnp.maximum(jnp.max(a, axis=-1, keepdims=True),
                                        jnp.max(b, axis=-1, keepdims=True)), sink)
            ea = jnp.exp(a - m)
            eb = jnp.exp(b - m)
            inv.append(1.0 / (jnp.sum(ea, axis=-1, keepdims=True) + jnp.sum(eb, axis=-1, keepdims=True)
                              + jnp.exp(sink - m)))
            p_st.append(ea.astype(BF16))
            p_nw.append(eb.astype(BF16))
        p_st = jnp.concatenate(p_st, axis=2)
        p_nw = jnp.concatenate(p_nw, axis=2).reshape(2 * n_tok, 2 * n_tok)
        vst = _diag2(svt_ref[:, hd, :].astype(BF16), 1, 2)
        o = jnp.stack([_dot_nt(p_st[s], vst[s]) for s in range(n_b)])
        o = o + _dot_nt(p_nw, _diag2(vnb[hd, :], 0, 1)).reshape(n_b, 2 * n_s, LANES)
        o = o * jnp.where(low, inv[0], inv[1])
        return jnp.concatenate([o[:, :n_s, :], o[:, n_s:, :]], axis=2)

    outs = []
    nxt = scores(0)
    for g in range(N_KV_HEADS):
        cur = nxt
        if g + 1 < N_KV_HEADS:
            nxt = scores(g + 1)
        outs.append(finish(g, *cur))
    attn_o = jnp.concatenate(outs, axis=2).reshape(n_tok, ATTN_WIDTH)

    for c in gates_in:
        c.wait()
    pu = _dot(hb, win_ref[:, C_PU])
    pu_t = jnp.swapaxes(pu.reshape(n_b, n_s, POOL_WIDTH), 0, 1)
    full = jnp.concatenate([jnp.zeros((1, n_b, POOL_WIDTH), F32), sp_ref[...], pu_t], axis=0)
    op_ref[...] = full[HIST_ROWS + n_s - POOL_HIST:]
    p_out.start()
    sums = _window_sums(full, _delay_leading)
    pos1 = (start + 1 + lax.broadcasted_iota(jnp.int32, (n_s, 1, 1), 0)).astype(F32)
    mixed = []
    for gi, w in enumerate(POOL_WINDOWS):
        inv_cnt = 1.0 / jnp.minimum(float(w), pos1)
        sw = sums[gi][HIST_ROWS:] * inv_cnt - pu_t[..., gi * POOL_GROUP:(gi + 1) * POOL_GROUP]
        mixed.append(sw.reshape(n_tok, POOL_GROUP))

    def stream_major(po):
        return jnp.swapaxes(po.reshape(n_s, n_b, POOL_WIDTH), 0, 1).reshape(n_tok, POOL_WIDTH)

    gates = tuple(_dot(hb, win_ref[:, c]) for c in (C_AG, C_PG, C_MA, C_MP))
    for c in merge_in:
        c.wait()
    k_out.start()
    v_out.start()
    merged = _merge_branches(attn_o, mixed, gates, pw_ref, ps_ref, wabr_ref, wpbr_ref, stream_major)
    wout_in.wait()
    y_ref[...] = x + _dot(merged, wout_ref[...])
    for c in (k_out, v_out, p_out):
        c.wait()


def _sample_layer(x, skt, svt, sp, start, tbl, sinks, ng, w_in, gain_cols, gain_rows, ones, bks, bkn,
                  pw, ps, wabr, wpbr, wout):
    n_b, n_s, _ = x.shape
    n_tok = n_b * n_s
    assert n_s % 16 == 0 and LANES % n_s == 0 and n_s >= POOL_HIST and n_tok % LANES == 0
    assert n_s & (n_s - 1) == 0 and n_tok & (n_tok - 1) == 0
    vmem = pl.BlockSpec(memory_space=pltpu.VMEM)
    anywhere = pl.BlockSpec(memory_space=pl.ANY)
    weights = (w_in, pw, wabr, wpbr, wout)
    state = (skt, svt, sp)
    new_state = [jax.ShapeDtypeStruct(a.shape, a.dtype) for a in state]
    y, ok, ov, op = pl.pallas_call(
        functools.partial(_sample_kernel, n_b, n_s, start),
        in_specs=[pl.BlockSpec(memory_space=pltpu.SMEM)] * 2 + [vmem] * 8
                 + [anywhere] * (len(state) + len(weights)),
        out_specs=[vmem] + [anywhere] * len(state),
        out_shape=[jax.ShapeDtypeStruct((n_tok, D_MODEL), F32)] + new_state,
        scratch_shapes=[pltpu.VMEM(a.shape, a.dtype) for a in state + tuple(new_state) + weights]
                       + [pltpu.SemaphoreType.DMA((N_SAMPLE_MOVES,))],
        compiler_params=pltpu.CompilerParams(vmem_limit_bytes=VMEM_LIMIT),
        name="sample_layer",
    )(tbl, sinks, x.reshape(n_tok, D_MODEL), ng, gain_cols, gain_rows, ones, bks, bkn, ps,
      *state, *weights)
    return y.reshape(n_b, n_s, D_MODEL), ok, ov, op


def _kv_to_feature_major(a):
    return jnp.transpose(a, (0, 2, 3, 1)).reshape(a.shape[0], KV_WIDTH, WINDOW)


def _kv_from_feature_major(a):
    return jnp.transpose(a.reshape(a.shape[0], N_KV_HEADS, HEAD_DIM, WINDOW), (0, 3, 1, 2))


def kernel(x_prompt, x_sample, state_attn_k, state_attn_v, state_pool, norm_gain, w_in, q_norm_gain, k_norm_gain, attn_sinks, rel_bias, pool_w, pool_scale, w_attn_br, w_pool_br, w_out):
    depth = w_in.shape[0]
    n_b, n_s, _ = x_sample.shape
    past_len = 4096
    bkp, bks, bkn = _bucket_tables(n_b, n_s)
    tbl = rel_bias.T
    seg = np.arange(MXU_DIM) // HEAD_DIM
    ones = jnp.asarray(seg[:, None] == seg[None, :], BF16)

    xp, xs = x_prompt, x_sample
    pk, pv, pp, sk, sv, sp = [], [], [], [], [], []
    for l in range(depth):
        ng = norm_gain[l].reshape(1, D_MODEL)
        qg = q_norm_gain[l] * (HEAD_DIM ** -0.5)
        kg = k_norm_gain[l]
        gain_cols = jnp.stack([qg, kg], axis=1)
        gain_rows = jnp.concatenate([jnp.tile(qg, N_HEADS), jnp.tile(kg, N_KV_HEADS)]).reshape(1, -1)
        ps = pool_scale[l].reshape(1, POOL_WIDTH)
        xp, k_l, v_l, p_l, w_in_b, wabr_b, wpbr_b, wout_b, pw_b = _prompt_layer(
            xp, tbl, attn_sinks[l], ng, gain_cols, bkp, ps,
            (w_in[l], w_attn_br[l], w_pool_br[l], w_out[l], pool_w[l].reshape(POOL_WIDTH, POOL_GROUP)))
        tail = (pw_b, ps, wabr_b, wpbr_b, wout_b)
        pk.append(_kv_from_feature_major(k_l))
        pv.append(_kv_from_feature_major(v_l))
        pp.append(jnp.transpose(p_l, (1, 0, 2)))

        xs, k_l, v_l, p_l = _sample_layer(
            xs,
            _kv_to_feature_major(state_attn_k[l]),
            _kv_to_feature_major(state_attn_v[l]),
            jnp.transpose(state_pool[l], (1, 0, 2)),
            past_len, tbl, attn_sinks[l], ng, w_in_b, gain_cols, gain_rows, ones, bks, bkn, *tail)
        sk.append(_kv_from_feature_major(k_l))
        sv.append(_kv_from_feature_major(v_l))
        sp.append(jnp.transpose(p_l, (1, 0, 2)))
    return (xp, xs, jnp.stack(pk), jnp.stack(pv), jnp.stack(pp),
            jnp.stack(sk), jnp.stack(sv), jnp.stack(sp))
```

```python
import functools
import numpy as np

import jax
import jax.numpy as jnp
from jax import lax
from jax.experimental import pallas as pl
from jax.experimental.pallas import tpu as pltpu

D_MODEL = 1024
CHUNK = 64
WINDOW = 128
N_HEADS = 16
N_KV_HEADS = 4
GROUP_REP = N_HEADS // N_KV_HEADS
HEAD_DIM = 64
ATTN_WIDTH = N_HEADS * HEAD_DIM
KV_WIDTH = N_KV_HEADS * HEAD_DIM
POOL_WINDOWS = (2, 4, 8, 16)
POOL_WIDTH = D_MODEL
POOL_GROUP = POOL_WIDTH // len(POOL_WINDOWS)
POOL_HIST = max(POOL_WINDOWS) - 1
HIST_ROWS = POOL_HIST + 1
N_BUCKETS = 32
MAX_DISTANCE = 128
EPS = 1e-6
NEG_INF = -1e30

_SPLITS = (ATTN_WIDTH, KV_WIDTH, KV_WIDTH, ATTN_WIDTH, POOL_WIDTH, POOL_WIDTH, D_MODEL, D_MODEL)
_OFFS = tuple(sum(_SPLITS[:i]) for i in range(len(_SPLITS) + 1))
C_Q, C_K, C_V, C_AG, C_PU, C_PG, C_MA, C_MP = (slice(_OFFS[i], _OFFS[i + 1]) for i in range(8))

LANES = 128
MXU_DIM = 256
PAIR = 2 * CHUNK
PAIR_KEYS = PAIR + WINDOW
CHUNK_KEYS = CHUNK + WINDOW
TM = 512
PREFILL = 8
STAGE_ROWS, STAGE_COLS, N_STAGE = ATTN_WIDTH, TM, 3
N_SAMPLE_MOVES = 12
VMEM_LIMIT = 60 * 1024 * 1024

BF16 = jnp.bfloat16
F32 = jnp.float32


def _dot(a, b):
    return jnp.dot(a, b, preferred_element_type=F32)


def _dot_nt(a, b):
    return lax.dot_general(a, b, (((1,), (1,)), ((), ())), preferred_element_type=F32)


def _rmsnorm_rows(x, gain):
    y = x * lax.rsqrt(jnp.mean(x * x, axis=-1, keepdims=True) + EPS)
    return y * gain


def _head_rmsnorm(z, ones_ref, gain):
    sq = z * z
    hi = sq.astype(BF16)
    lo = (sq - hi.astype(F32)).astype(BF16)
    ones = ones_ref[...]
    parts = []
    for c in range(z.shape[1] // MXU_DIM):
        sl = slice(c * MXU_DIM, (c + 1) * MXU_DIM)
        parts.append(_dot(hi[:, sl], ones) + _dot(lo[:, sl], ones))
    ss = parts[0] if len(parts) == 1 else jnp.concatenate(parts, axis=1)
    return (z * lax.rsqrt(ss * (1.0 / HEAD_DIM) + EPS)) * gain


def _head_rmsnorm_fm(zt, gain_col, h):
    zh = zt[h * HEAD_DIM:(h + 1) * HEAD_DIM, :]
    ms = jnp.sum(zh * zh, axis=0, keepdims=True) * (1.0 / HEAD_DIM)
    return (zh * lax.rsqrt(ms + EPS)) * gain_col


def _window_sums(full, delay):
    outs = []
    for gi, w in enumerate(POOL_WINDOWS):
        s = full[..., gi * POOL_GROUP:(gi + 1) * POOL_GROUP]
        sh = 1
        while sh < w:
            s = s + delay(s, sh)
            sh *= 2
        outs.append(s)
    return outs


def _delay_rows(a, n):
    return pltpu.roll(a, n, axis=0)


def _delay_leading(a, n):
    return jnp.concatenate([a[-n:], a[:-n]], axis=0)


def _merge_branches(attn_o, pool_mixed, gates, pw_ref, ps_ref, wabr_ref, wpbr_ref, pool_rows=None):
    ag, pg, ma, mp = gates
    po = []
    for gi in range(len(POOL_WINDOWS)):
        po.append(_dot(pool_mixed[gi].astype(BF16), pw_ref[gi * POOL_GROUP:(gi + 1) * POOL_GROUP, :]))
    po = jnp.concatenate(po, axis=1)
    if pool_rows is not None:
        po = pool_rows(po)
    po = po * ps_ref[...]
    p = _dot((po * (pg * jax.nn.sigmoid(pg))).astype(BF16), wpbr_ref[...])
    a = _dot((attn_o * (ag * jax.nn.sigmoid(ag))).astype(BF16), wabr_ref[...])
    acc = jax.nn.sigmoid(ma) * a + jax.nn.sigmoid(mp) * p
    return acc.astype(BF16)


def _diag2(a, axis_r, axis_c):
    z = jnp.zeros_like(a)
    return jnp.concatenate([jnp.concatenate([a, z], axis=axis_c),
                            jnp.concatenate([z, a], axis=axis_c)], axis=axis_r)


def _gather_bias(bucket, tbl_ref, h):
    acc = jnp.zeros(bucket.shape, F32)
    for b in range(N_BUCKETS):
        acc = jnp.where(bucket == b, tbl_ref[h, b], acc)
    return acc


def _t5_bucket(rel):
    assert (N_BUCKETS, MAX_DISTANCE) == (32, 128)
    nb = N_BUCKETS // 2
    max_exact = nb // 2
    n = np.abs(rel)
    log2_sq = np.vectorize(lambda v: max(int(v), 1).bit_length() - 1)(n * n)
    large = np.minimum(max_exact + log2_sq - 6, nb - 1)
    return (np.where(rel > 0, nb, 0) + np.where(n < max_exact, n, large)).astype(np.int32)


def _bucket_tables(n_b, n_s):
    bkp = _t5_bucket(np.arange(CHUNK_KEYS)[:, None] - WINDOW - np.arange(LANES)[None, :] % CHUNK)
    fr = np.arange(n_s)[:, None]
    bks = _t5_bucket(np.arange(WINDOW)[None, :] - WINDOW - fr)
    bkn = _t5_bucket(np.arange(n_b * n_s)[None, :] % n_s - fr)
    return bkp, bks, bkn


def _weight_jobs(pairs):
    jobs = []
    for src, dst in pairs:
        n_rows, n_cols = dst.shape
        for r0 in range(0, n_rows, STAGE_ROWS):
            for c0 in range(0, n_cols, STAGE_COLS):
                jobs.append((src, dst, r0, c0, min(STAGE_COLS, n_cols - c0)))
    return jobs


def _prompt_kernel(tbl_ref, sinks_ref, x_ref, xn_ref, ng_ref, gcol_ref, bkp_ref, ps_ref,
                   win_hbm, wabr_hbm, wpbr_hbm, wout_hbm, pw_hbm,
                   y_ref, pk_ref, pv_ref, pp_ref, winb_hbm, wabrb_hbm, wpbrb_hbm, woutb_hbm, pwb_hbm,
                   win_ref, wabr_ref, wpbr_ref, wout_ref, pw_ref, stage, sem_in, sem_out,
                   wt_s, h_s, qt_s, k_s, vt_s, attn_s, pbuf, hist_s, bias_s):
    b = pl.program_id(0)
    t = pl.program_id(1)
    first = (b == 0) & (t == 0)
    q_gain = gcol_ref[:, 0:1]
    k_gain = gcol_ref[:, 1:2]
    zq_s = stage.at[0]
    weights = ((win_hbm, win_ref), (wabr_hbm, wabr_ref), (wpbr_hbm, wpbr_ref), (wout_hbm, wout_ref),
               (pw_hbm, pw_ref))
    exports = (winb_hbm, wabrb_hbm, wpbrb_hbm, woutb_hbm, pwb_hbm)

    def export_copy(i):
        return pltpu.make_async_copy(weights[i][1], exports[i], sem_out.at[i])

    @pl.when(first)
    def _():
        jobs = _weight_jobs(weights)

        def load(i):
            src, _, r0, c0, w = jobs[i]
            return pltpu.make_async_copy(src.at[pl.ds(r0, STAGE_ROWS), pl.ds(c0, w)],
                                         stage.at[i % N_STAGE, :, pl.ds(0, w)], sem_in.at[i % N_STAGE])

        for i in range(min(N_STAGE - 1, len(jobs))):
            load(i).start()
        for i in range(len(jobs)):
            if i + N_STAGE - 1 < len(jobs):
                load(i + N_STAGE - 1).start()
            load(i).wait()
            _, dst, r0, c0, w = jobs[i]
            dst[r0:r0 + STAGE_ROWS, c0:c0 + w] = stage[i % N_STAGE, :, 0:w].astype(BF16)
            if i < N_HEADS // 2:
                bkp = bkp_ref[...]
                pair = jnp.where(lax.broadcasted_iota(jnp.int32, bkp.shape, 1) < CHUNK,
                                 _gather_bias(bkp, tbl_ref, 2 * i), _gather_bias(bkp, tbl_ref, 2 * i + 1))
                bias_s[2 * i // GROUP_REP, :, (2 * i % GROUP_REP) * CHUNK:(2 * i % GROUP_REP + 2) * CHUNK] = pair
            if jobs[i][1] is win_ref and c0 + w == C_V.stop:
                wt_s[...] = win_ref[:, 0:C_V.stop].T
                h_s[...] = _rmsnorm_rows(x_ref[...], ng_ref[...]).astype(BF16)
        assert len(jobs) >= N_HEADS // 2 and C_V.stop % STAGE_COLS == 0
        zq_s[...] = _dot_nt(wt_s[C_Q, :], h_s[...])
        for i in range(len(weights)):
            export_copy(i).start()

    @pl.when(t == 0)
    def _():
        k_s[:, 0:WINDOW, :] = jnp.zeros((2, WINDOW, LANES), BF16)
        vt_s[:, 0:WINDOW] = jnp.zeros((KV_WIDTH, WINDOW), BF16)
        pbuf[0:HIST_ROWS, :] = jnp.zeros((HIST_ROWS, POOL_WIDTH), F32)

    hb = h_s[...]

    n_c = D_MODEL // MXU_DIM
    fill_cols = [slice(c.start + i * MXU_DIM, c.start + (i + 1) * MXU_DIM)
                 for c in (C_PU, C_PG, C_AG, C_MA, C_MP) for i in range(n_c)]
    filled = []

    def fill(upto):
        for cols in fill_cols[len(filled):upto]:
            filled.append(_dot(hb, win_ref[:, cols]))

    zqt = zq_s[...]
    zkt = _dot_nt(wt_s[C_K, :], hb)
    vt = _dot_nt(wt_s[C_V, :], hb)
    fill(PREFILL)
    kt = jnp.concatenate([_head_rmsnorm_fm(zkt, k_gain, g) for g in range(N_KV_HEADS)], axis=0)
    for h in range(N_HEADS):
        qt_s[h * HEAD_DIM:(h + 1) * HEAD_DIM, :] = _head_rmsnorm_fm(zqt, q_gain, h).astype(BF16)

    pk_ref[...] = kt[:, TM - WINDOW:]
    pv_ref[...] = vt[:, TM - WINDOW:]

    k_s[0, WINDOW:, :] = kt[:LANES, :].T.astype(BF16)
    k_s[1, WINDOW:, :] = kt[LANES:, :].T.astype(BF16)
    vt_s[:, WINDOW:] = vt.astype(BF16)

    zeros_half = jnp.zeros((HEAD_DIM, CHUNK), BF16)
    q_lane = lax.broadcasted_iota(jnp.int32, (1, GROUP_REP * CHUNK), 1)

    def scores(p, g):
        out = []
        for c in range(2):
            tok = slice(p * PAIR + c * CHUNK, p * PAIR + (c + 1) * CHUNK)
            keys = slice(p * PAIR + c * CHUNK, p * PAIR + c * CHUNK + CHUNK_KEYS)
            cols = []
            for r in range(GROUP_REP):
                h = g * GROUP_REP + r
                qh = qt_s[h * HEAD_DIM:(h + 1) * HEAD_DIM, tok]
                cols.append(jnp.concatenate([qh, zeros_half] if g % 2 == 0 else [zeros_half, qh], axis=0))
            rhs = jnp.concatenate(cols, axis=1)
            s = _dot(k_s[g // 2, keys, :], rhs) + bias_s[g]
            if p == 0:
                kk = lax.broadcasted_iota(jnp.int32, s.shape, 0)
                s = jnp.where((kk >= WINDOW - c * CHUNK) | (t > 0), s, NEG_INF)
            out.append(s)
        return out

    def finish(p, g, s_ab):
        tok = slice(p * PAIR, (p + 1) * PAIR)
        keys = slice(p * PAIR, p * PAIR + PAIR_KEYS)
        sink = sinks_ref[g * GROUP_REP]
        for r in range(1, GROUP_REP):
            sink = jnp.where(q_lane < r * CHUNK, sink, sinks_ref[g * GROUP_REP + r])
        pads = jnp.zeros((PAIR_KEYS - CHUNK_KEYS, GROUP_REP * CHUNK), BF16)
        e_ab, inv = [], []
        for c, s in enumerate(s_ab):
            m = jnp.maximum(jnp.max(s, axis=0, keepdims=True), sink)
            e = jnp.exp(s - m)
            inv.append(1.0 / (jnp.sum(e, axis=0, keepdims=True) + jnp.exp(sink - m)))
            e_ab.append(jnp.concatenate([e.astype(BF16), pads] if c == 0 else [pads, e.astype(BF16)], axis=0))
        vg = vt_s[g * HEAD_DIM:(g + 1) * HEAD_DIM, keys]
        o_t = _dot(vg, jnp.concatenate(e_ab, axis=1)) * jnp.concatenate(inv, axis=1)
        n_q = GROUP_REP * CHUNK
        for j in range(GROUP_REP // 2):
            rows = [jnp.concatenate([o_t[:, r * CHUNK:(r + 1) * CHUNK], o_t[:, n_q + r * CHUNK:n_q + (r + 1) * CHUNK]],
                                    axis=1) for r in (2 * j, 2 * j + 1)]
            lanes = slice(g * MXU_DIM + j * LANES, g * MXU_DIM + (j + 1) * LANES)
            attn_s[tok, lanes] = jnp.concatenate(rows, axis=0).T

    pu = jnp.concatenate(filled[:n_c], axis=1)
    pbuf[HIST_ROWS:, :] = pu
    hist_s[b] = pu[TM - HIST_ROWS:, :]
    sums = _window_sums(pbuf[...], _delay_rows)
    pos1 = (t * TM + 1 + lax.broadcasted_iota(jnp.int32, (TM, 1), 0)).astype(F32)
    mixed = []
    for gi, w in enumerate(POOL_WINDOWS):
        inv_cnt = 1.0 / jnp.minimum(float(w), pos1)
        mixed.append(sums[gi][HIST_ROWS:, :] * inv_cnt - pu[:, gi * POOL_GROUP:(gi + 1) * POOL_GROUP])

    blocks = [(p, g) for p in range(TM // PAIR) for g in range(N_KV_HEADS)]
    s_next = scores(*blocks[0])
    for i, blk in enumerate(blocks):
        s_cur = s_next
        if i + 1 < len(blocks):
            s_next = scores(*blocks[i + 1])
        fill(PREFILL + (len(fill_cols) - PREFILL) * (i + 1) // len(blocks))
        finish(*blk, s_cur)
    pg, ag, ma, mp = (jnp.concatenate(filled[j * n_c:(j + 1) * n_c], axis=1) for j in range(1, 5))

    merged = _merge_branches(attn_s[...], mixed, (ag, pg, ma, mp), pw_ref, ps_ref, wabr_ref, wpbr_ref)

    h_s[...] = _rmsnorm_rows(xn_ref[...], ng_ref[...]).astype(BF16)
    zq_s[...] = _dot_nt(wt_s[C_Q, :], h_s[...])

    y_ref[...] = x_ref[...] + _dot(merged, wout_ref[...])

    k_s[:, 0:WINDOW, :] = k_s[:, TM:TM + WINDOW, :]
    vt_s[:, 0:WINDOW] = vt_s[:, TM:TM + WINDOW]
    pbuf[0:HIST_ROWS, :] = pbuf[TM:TM + HIST_ROWS, :]

    @pl.when(first)
    def _():
        for i in range(len(weights)):
            export_copy(i).wait()

    @pl.when((b == pl.num_programs(0) - 1) & (t == pl.num_programs(1) - 1))
    def _():
        pp_ref[...] = jnp.swapaxes(hist_s[...], 0, 1)[HIST_ROWS - POOL_HIST:]


def _resident(shape):
    nd = len(shape)
    return pl.BlockSpec(shape, lambda b, t: (0,) * nd, pipeline_mode=pl.Buffered(1))


def _prompt_layer(x, tbl, sinks, ng, gain_cols, bkp, ps, weights):
    B, S, _ = x.shape
    assert S % TM == 0 and TM % PAIR == 0 and TM >= WINDOW
    assert all(w.shape[0] % STAGE_ROWS == 0 for w in weights)
    consts = (ng, gain_cols, bkp, ps)
    hbm = pl.BlockSpec(memory_space=pl.ANY)
    n_t = S // TM

    def next_tile(b, t):
        i = jnp.minimum(b * n_t + t + 1, B * n_t - 1)
        return i // n_t, i % n_t, 0

    return pl.pallas_call(
        _prompt_kernel,
        grid=(B, S // TM),
        in_specs=[pl.BlockSpec(memory_space=pltpu.SMEM),
                  pl.BlockSpec(memory_space=pltpu.SMEM),
                  pl.BlockSpec((None, TM, D_MODEL), lambda b, t: (b, t, 0)),
                  pl.BlockSpec((None, TM, D_MODEL), next_tile)]
                 + [_resident(c.shape) for c in consts] + [hbm] * len(weights),
        out_specs=[
            pl.BlockSpec((None, TM, D_MODEL), lambda b, t: (b, t, 0)),
            pl.BlockSpec((None, KV_WIDTH, WINDOW), lambda b, t: (b, 0, 0)),
            pl.BlockSpec((None, KV_WIDTH, WINDOW), lambda b, t: (b, 0, 0)),
            pl.BlockSpec((POOL_HIST, B, POOL_WIDTH), lambda b, t: (0, 0, 0)),
        ] + [hbm] * len(weights),
        out_shape=[
            jax.ShapeDtypeStruct((B, S, D_MODEL), F32),
            jax.ShapeDtypeStruct((B, KV_WIDTH, WINDOW), F32),
            jax.ShapeDtypeStruct((B, KV_WIDTH, WINDOW), F32),
            jax.ShapeDtypeStruct((POOL_HIST, B, POOL_WIDTH), F32),
        ] + [jax.ShapeDtypeStruct(w.shape, BF16) for w in weights],
        scratch_shapes=[pltpu.VMEM(w.shape, BF16) for w in weights] + [
            pltpu.VMEM((N_STAGE, STAGE_ROWS, STAGE_COLS), F32),
            pltpu.SemaphoreType.DMA((N_STAGE,)),
            pltpu.SemaphoreType.DMA((len(weights),)),
            pltpu.VMEM((C_V.stop, D_MODEL), BF16),
            pltpu.VMEM((TM, D_MODEL), BF16),
            pltpu.VMEM((ATTN_WIDTH, TM), BF16),
            pltpu.VMEM((2, WINDOW + TM, LANES), BF16),
            pltpu.VMEM((KV_WIDTH, WINDOW + TM), BF16),
            pltpu.VMEM((TM, ATTN_WIDTH), F32),
            pltpu.VMEM((HIST_ROWS + TM, POOL_WIDTH), F32),
            pltpu.VMEM((B, HIST_ROWS, POOL_WIDTH), F32),
            pltpu.VMEM((N_KV_HEADS, CHUNK_KEYS, GROUP_REP * CHUNK), F32),
        ],
        compiler_params=pltpu.CompilerParams(
            dimension_semantics=("arbitrary", "arbitrary"),
            vmem_limit_bytes=VMEM_LIMIT,
        ),
        name="prompt_layer",
    )(tbl, sinks, x, x, *consts, *weights)


def _sample_kernel(n_b, n_s, start,
                   tbl_ref, sinks_ref, x_ref, ng_ref,
                   gcol_ref, grow_ref, ones_ref, bks_ref, bkn_ref, ps_ref,
                   skt_hbm, svt_hbm, sp_hbm, win_hbm, pw_hbm, wabr_hbm, wpbr_hbm, wout_hbm,
                   y_ref, ok_hbm, ov_hbm, op_hbm,
                   skt_ref, svt_ref, sp_ref, ok_ref, ov_ref, op_ref,
                   win_ref, pw_ref, wabr_ref, wpbr_ref, wout_ref, sem):
    n_tok = n_b * n_s
    qkv = slice(0, C_V.stop)
    rest = slice(C_V.stop, win_ref.shape[1])
    moves = ((win_hbm.at[:, qkv], win_ref.at[:, qkv]), (skt_hbm, skt_ref), (svt_hbm, svt_ref),
             (win_hbm.at[:, rest], win_ref.at[:, rest]), (sp_hbm, sp_ref), (pw_hbm, pw_ref),
             (wpbr_hbm, wpbr_ref), (wabr_hbm, wabr_ref), (wout_hbm, wout_ref),
             (ok_ref, ok_hbm), (ov_ref, ov_hbm), (op_ref, op_hbm))
    assert len(moves) == N_SAMPLE_MOVES
    moves = [pltpu.make_async_copy(src, dst, sem.at[i]) for i, (src, dst) in enumerate(moves)]
    (qkv_in, k_in, v_in), late_in, (k_out, v_out, p_out) = moves[:3], moves[3:9], moves[9:]
    qkv_in.start()

    x = x_ref[...]
    hb = _rmsnorm_rows(x, ng_ref[...]).astype(BF16)

    qkv_in.wait()
    k_in.start()
    v_in.start()
    q = _head_rmsnorm(_dot(hb, win_ref[:, C_Q]), ones_ref, grow_ref[:, :ATTN_WIDTH]).astype(BF16)
    q3 = q.reshape(n_b, n_s, ATTN_WIDTH)
    zkt = _dot_nt(win_ref[:, C_K].T, hb)
    k_gain = gcol_ref[:, 1:2]
    knt = jnp.concatenate([_head_rmsnorm_fm(zkt, k_gain, g) for g in range(N_KV_HEADS)], axis=0)
    vnt = _dot_nt(win_ref[:, C_V].T, hb)

    k_in.wait()
    v_in.wait()
    for c in late_in:
        c.start()
    lane = lax.broadcasted_iota(jnp.int32, (1, WINDOW), 1)
    per_tile = LANES // n_s
    for s in range(n_b):
        tile = slice((s // per_tile) * LANES, (s // per_tile + 1) * LANES)
        shift = (WINDOW - n_s - (s % per_tile) * n_s) % LANES
        for new, st_ref, o_ref in ((knt, skt_ref, ok_ref), (vnt, svt_ref, ov_ref)):
            fresh = new[:, tile] if shift == 0 else pltpu.roll(new[:, tile], shift, axis=1)
            o_ref[s] = jnp.where(lane < WINDOW - n_s, pltpu.roll(st_ref[s], WINDOW - n_s, axis=1), fresh)

    knb = knt.astype(BF16)
    vnb = vnt.astype(BF16)
    key_stream = (lax.broadcasted_iota(jnp.int32, (n_b, 1, 2 * n_tok), 2) & (n_tok - 1)) >> (n_s.bit_length() - 1)
    own = key_stream == lax.broadcasted_iota(jnp.int32, (n_b, 1, 2 * n_tok), 0)
    top = lax.broadcasted_iota(jnp.int32, (1, 2 * n_s, 1), 1) < n_s
    low = lax.broadcasted_iota(jnp.int32, (1, 1, LANES), 2) < HEAD_DIM

    def bias_rows(bucket, g):
        return jnp.concatenate([jnp.concatenate(
            [_gather_bias(bucket, tbl_ref, g * GROUP_REP + 2 * c + j) for j in range(2)], axis=1)
            for c in range(2)], axis=0)

    def scores(g):
        hd = slice(g * HEAD_DIM, (g + 1) * HEAD_DIM)
        c = g * MXU_DIM
        lhs = jnp.concatenate([q3[:, :, c:c + LANES], q3[:, :, c + LANES:c + MXU_DIM]], axis=1)
        kst = _diag2(skt_ref[:, hd, :].astype(BF16), 1, 2)
        s_st = jnp.stack([_dot(lhs[s], kst[s]) for s in range(n_b)]) + bias_rows(bks_ref[...], g)[None]
        s_nw = _dot(lhs.reshape(2 * n_tok, LANES), _diag2(knb[hd, :], 0, 1))
        s_nw = jnp.where(own, s_nw.reshape(n_b, 2 * n_s, 2 * n_tok) + bias_rows(bkn_ref[...], g)[None], NEG_INF)
        return s_st, s_nw

    def finish(g, s_st, s_nw):
        hd = slice(g * HEAD_DIM, (g + 1) * HEAD_DIM)
        p_st, p_nw, inv = [], [], []
        for j in range(2):
            a = s_st[:, :, j * WINDOW:(j + 1) * WINDOW]
            b = s_nw[:, :, j * n_tok:(j + 1) * n_tok]
            sink = jnp.where(top, sinks_ref[4 * g + j], sinks_ref[4 * g + 2 + j])
            m = jnp.maximum(jnp.maximum(jnp.max(a, axis=-1, keepdims=True),
                                        jnp.max(b, axis=-1, keepdims=True)), sink)
            ea = jnp.exp(a - m)
            eb = jnp.exp(b - m)
            inv.append(1.0 / (jnp.sum(ea, axis=-1, keepdims=True) + jnp.sum(eb, axis=-1, keepdims=True)
                              + jnp.exp(sink - m)))
            p_st.append(ea.astype(BF16))
            p_nw.append(eb.astype(BF16))
        p_st = jnp.concatenate(p_st, axis=2)
        p_nw = jnp.concatenate(p_nw, axis=2).reshape(2 * n_tok, 2 * n_tok)
        vst = _diag2(svt_ref[:, hd, :].astype(BF16), 1, 2)
        o = jnp.stack([_dot_nt(p_st[s], vst[s]) for s in range(n_b)])
        o = o + _dot_nt(p_nw, _diag2(vnb[hd, :], 0, 1)).reshape(n_b, 2 * n_s, LANES)
        o = o * jnp.where(low, inv[0], inv[1])
        return jnp.concatenate([o[:, :n_s, :], o[:, n_s:, :]], axis=2)

    outs = []
    nxt = scores(0)
    for g in range(N_KV_HEADS):
        cur = nxt
        if g + 1 < N_KV_HEADS:
            nxt = scores(g + 1)
        outs.append(finish(g, *cur))
    attn_o = jnp.concatenate(outs, axis=2).reshape(n_tok, ATTN_WIDTH)

    for c in late_in:
        c.wait()
    k_out.start()
    v_out.start()
    pu = _dot(hb, win_ref[:, C_PU])
    pu_t = jnp.swapaxes(pu.reshape(n_b, n_s, POOL_WIDTH), 0, 1)
    full = jnp.concatenate([jnp.zeros((1, n_b, POOL_WIDTH), F32), sp_ref[...], pu_t], axis=0)
    op_ref[...] = full[HIST_ROWS + n_s - POOL_HIST:]
    p_out.start()
    sums = _window_sums(full, _delay_leading)
    pos1 = (start + 1 + lax.broadcasted_iota(jnp.int32, (n_s, 1, 1), 0)).astype(F32)
    mixed = []
    for gi, w in enumerate(POOL_WINDOWS):
        inv_cnt = 1.0 / jnp.minimum(float(w), pos1)
        sw = sums[gi][HIST_ROWS:] * inv_cnt - pu_t[..., gi * POOL_GROUP:(gi + 1) * POOL_GROUP]
        mixed.append(sw.reshape(n_tok, POOL_GROUP))

    def stream_major(po):
        return jnp.swapaxes(po.reshape(n_s, n_b, POOL_WIDTH), 0, 1).reshape(n_tok, POOL_WIDTH)

    gates = tuple(_dot(hb, win_ref[:, c]) for c in (C_AG, C_PG, C_MA, C_MP))
    merged = _merge_branches(attn_o, mixed, gates, pw_ref, ps_ref, wabr_ref, wpbr_ref, stream_major)
    y_ref[...] = x + _dot(merged, wout_ref[...])
    for c in (k_out, v_out, p_out):
        c.wait()


def _sample_layer(x, skt, svt, sp, start, tbl, sinks, ng, w_in, gain_cols, gain_rows, ones, bks, bkn,
                  pw, ps, wabr, wpbr, wout):
    n_b, n_s, _ = x.shape
    n_tok = n_b * n_s
    assert n_s % 16 == 0 and LANES % n_s == 0 and n_s >= POOL_HIST and n_tok % LANES == 0
    assert n_s & (n_s - 1) == 0 and n_tok & (n_tok - 1) == 0
    vmem = pl.BlockSpec(memory_space=pltpu.VMEM)
    anywhere = pl.BlockSpec(memory_space=pl.ANY)
    weights = (w_in, pw, wabr, wpbr, wout)
    state = (skt, svt, sp)
    new_state = [jax.ShapeDtypeStruct(a.shape, a.dtype) for a in state]
    y, ok, ov, op = pl.pallas_call(
        functools.partial(_sample_kernel, n_b, n_s, start),
        in_specs=[pl.BlockSpec(memory_space=pltpu.SMEM)] * 2 + [vmem] * 8
                 + [anywhere] * (len(state) + len(weights)),
        out_specs=[vmem] + [anywhere] * len(state),
        out_shape=[jax.ShapeDtypeStruct((n_tok, D_MODEL), F32)] + new_state,
        scratch_shapes=[pltpu.VMEM(a.shape, a.dtype) for a in state + tuple(new_state) + weights]
                       + [pltpu.SemaphoreType.DMA((N_SAMPLE_MOVES,))],
        compiler_params=pltpu.CompilerParams(vmem_limit_bytes=VMEM_LIMIT),
        name="sample_layer",
    )(tbl, sinks, x.reshape(n_tok, D_MODEL), ng, gain_cols, gain_rows, ones, bks, bkn, ps,
      *state, *weights)
    return y.reshape(n_b, n_s, D_MODEL), ok, ov, op


def _kv_to_feature_major(a):
    return jnp.transpose(a, (0, 2, 3, 1)).reshape(a.shape[0], KV_WIDTH, WINDOW)


def _kv_from_feature_major(a):
    return jnp.transpose(a.reshape(a.shape[0], N_KV_HEADS, HEAD_DIM, WINDOW), (0, 3, 1, 2))


def kernel(x_prompt, x_sample, state_attn_k, state_attn_v, state_pool, norm_gain, w_in, q_norm_gain, k_norm_gain, attn_sinks, rel_bias, pool_w, pool_scale, w_attn_br, w_pool_br, w_out):
    depth = w_in.shape[0]
    n_b, n_s, _ = x_sample.shape
    past_len = 4096
    bkp, bks, bkn = _bucket_tables(n_b, n_s)
    tbl = rel_bias.T
    seg = np.arange(MXU_DIM) // HEAD_DIM
    ones = jnp.asarray(seg[:, None] == seg[None, :], BF16)

    xp, xs = x_prompt, x_sample
    pk, pv, pp, sk, sv, sp = [], [], [], [], [], []
    for l in range(depth):
        ng = norm_gain[l].reshape(1, D_MODEL)
        qg = q_norm_gain[l] * (HEAD_DIM ** -0.5)
        kg = k_norm_gain[l]
        gain_cols = jnp.stack([qg, kg], axis=1)
        gain_rows = jnp.concatenate([jnp.tile(qg, N_HEADS), jnp.tile(kg, N_KV_HEADS)]).reshape(1, -1)
        ps = pool_scale[l].reshape(1, POOL_WIDTH)
        xp, k_l, v_l, p_l, w_in_b, wabr_b, wpbr_b, wout_b, pw_b = _prompt_layer(
            xp, tbl, attn_sinks[l], ng, gain_cols, bkp, ps,
            (w_in[l], w_attn_br[l], w_pool_br[l], w_out[l], pool_w[l].reshape(POOL_WIDTH, POOL_GROUP)))
        tail = (pw_b, ps, wabr_b, wpbr_b, wout_b)
        pk.append(_kv_from_feature_major(k_l))
        pv.append(_kv_from_feature_major(v_l))
        pp.append(jnp.transpose(p_l, (1, 0, 2)))

        xs, k_l, v_l, p_l = _sample_layer(
            xs,
            _kv_to_feature_major(state_attn_k[l]),
            _kv_to_feature_major(state_attn_v[l]),
            jnp.transpose(state_pool[l], (1, 0, 2)),
            past_len, tbl, attn_sinks[l], ng, w_in_b, gain_cols, gain_rows, ones, bks, bkn, *tail)
        sk.append(_kv_from_feature_major(k_l))
        sv.append(_kv_from_feature_major(v_l))
        sp.append(jnp.transpose(p_l, (1, 0, 2)))
    return (xp, xs, jnp.stack(pk), jnp.stack(pv), jnp.stack(pp),
            jnp.stack(sk), jnp.stack(sv), jnp.stack(sp))
```

```python
import functools
import numpy as np

import jax
import jax.numpy as jnp
from jax import lax
from jax.experimental import pallas as pl
from jax.experimental.pallas import tpu as pltpu

D_MODEL = 1024
CHUNK = 64
WINDOW = 128
N_HEADS = 16
N_KV_HEADS = 4
GROUP_REP = N_HEADS // N_KV_HEADS
HEAD_DIM = 64
ATTN_WIDTH = N_HEADS * HEAD_DIM
KV_WIDTH = N_KV_HEADS * HEAD_DIM
POOL_WINDOWS = (2, 4, 8, 16)
POOL_WIDTH = D_MODEL
POOL_GROUP = POOL_WIDTH // len(POOL_WINDOWS)
POOL_HIST = max(POOL_WINDOWS) - 1
HIST_ROWS = POOL_HIST + 1
N_BUCKETS = 32
MAX_DISTANCE = 128
EPS = 1e-6
NEG_INF = -1e30

_SPLITS = (ATTN_WIDTH, KV_WIDTH, KV_WIDTH, ATTN_WIDTH, POOL_WIDTH, POOL_WIDTH, D_MODEL, D_MODEL)
_OFFS = tuple(sum(_SPLITS[:i]) for i in range(len(_SPLITS) + 1))
C_Q, C_K, C_V, C_AG, C_PU, C_PG, C_MA, C_MP = (slice(_OFFS[i], _OFFS[i + 1]) for i in range(8))

LANES = 128
MXU_DIM = 256
PAIR = 2 * CHUNK
PAIR_KEYS = PAIR + WINDOW
CHUNK_KEYS = CHUNK + WINDOW
TM = 512
PREFILL = 8
STAGE_ROWS, STAGE_COLS, N_STAGE = ATTN_WIDTH, TM, 3
SAMPLE_SPLIT = 4
N_SAMPLE_MOVES = 9 * SAMPLE_SPLIT + 3
VMEM_LIMIT = 60 * 1024 * 1024

BF16 = jnp.bfloat16
F32 = jnp.float32


def _dot(a, b):
    return jnp.dot(a, b, preferred_element_type=F32)


def _dot_nt(a, b):
    return lax.dot_general(a, b, (((1,), (1,)), ((), ())), preferred_element_type=F32)


def _rmsnorm_rows(x, gain):
    y = x * lax.rsqrt(jnp.mean(x * x, axis=-1, keepdims=True) + EPS)
    return y * gain


def _head_rmsnorm(z, ones_ref, gain):
    sq = z * z
    hi = sq.astype(BF16)
    lo = (sq - hi.astype(F32)).astype(BF16)
    ones = ones_ref[...]
    parts = []
    for c in range(z.shape[1] // MXU_DIM):
        sl = slice(c * MXU_DIM, (c + 1) * MXU_DIM)
        parts.append(_dot(hi[:, sl], ones) + _dot(lo[:, sl], ones))
    ss = parts[0] if len(parts) == 1 else jnp.concatenate(parts, axis=1)
    return (z * lax.rsqrt(ss * (1.0 / HEAD_DIM) + EPS)) * gain


def _head_rmsnorm_fm(zt, gain_col, h):
    zh = zt[h * HEAD_DIM:(h + 1) * HEAD_DIM, :]
    ms = jnp.sum(zh * zh, axis=0, keepdims=True) * (1.0 / HEAD_DIM)
    return (zh * lax.rsqrt(ms + EPS)) * gain_col


def _window_sums(full, delay):
    outs = []
    for gi, w in enumerate(POOL_WINDOWS):
        s = full[..., gi * POOL_GROUP:(gi + 1) * POOL_GROUP]
        sh = 1
        while sh < w:
            s = s + delay(s, sh)
            sh *= 2
        outs.append(s)
    return outs


def _delay_rows(a, n):
    return pltpu.roll(a, n, axis=0)


def _delay_leading(a, n):
    return jnp.concatenate([a[-n:], a[:-n]], axis=0)


def _merge_branches(attn_o, pool_mixed, gates, pw_ref, ps_ref, wabr_ref, wpbr_ref, pool_rows=None):
    ag, pg, ma, mp = gates
    po = []
    for gi in range(len(POOL_WINDOWS)):
        po.append(_dot(pool_mixed[gi].astype(BF16), pw_ref[gi * POOL_GROUP:(gi + 1) * POOL_GROUP, :]))
    po = jnp.concatenate(po, axis=1)
    if pool_rows is not None:
        po = pool_rows(po)
    po = po * ps_ref[...]
    p = _dot((po * (pg * jax.nn.sigmoid(pg))).astype(BF16), wpbr_ref[...])
    a = _dot((attn_o * (ag * jax.nn.sigmoid(ag))).astype(BF16), wabr_ref[...])
    acc = jax.nn.sigmoid(ma) * a + jax.nn.sigmoid(mp) * p
    return acc.astype(BF16)


def _diag2(a, axis_r, axis_c):
    z = jnp.zeros_like(a)
    return jnp.concatenate([jnp.concatenate([a, z], axis=axis_c),
                            jnp.concatenate([z, a], axis=axis_c)], axis=axis_r)


def _gather_bias(bucket, tbl_ref, h):
    acc = jnp.zeros(bucket.shape, F32)
    for b in range(N_BUCKETS):
        acc = jnp.where(bucket == b, tbl_ref[h, b], acc)
    return acc


def _t5_bucket(rel):
    assert (N_BUCKETS, MAX_DISTANCE) == (32, 128)
    nb = N_BUCKETS // 2
    max_exact = nb // 2
    n = np.abs(rel)
    log2_sq = np.vectorize(lambda v: max(int(v), 1).bit_length() - 1)(n * n)
    large = np.minimum(max_exact + log2_sq - 6, nb - 1)
    return (np.where(rel > 0, nb, 0) + np.where(n < max_exact, n, large)).astype(np.int32)


def _bucket_tables(n_b, n_s):
    bkp = _t5_bucket(np.arange(CHUNK_KEYS)[:, None] - WINDOW - np.arange(LANES)[None, :] % CHUNK)
    fr = np.arange(n_s)[:, None]
    bks = _t5_bucket(np.arange(WINDOW)[None, :] - WINDOW - fr)
    bkn = _t5_bucket(np.arange(n_b * n_s)[None, :] % n_s - fr)
    return bkp, bks, bkn


def _weight_jobs(pairs):
    jobs = []
    for src, dst in pairs:
        n_rows, n_cols = dst.shape
        for r0 in range(0, n_rows, STAGE_ROWS):
            for c0 in range(0, n_cols, STAGE_COLS):
                jobs.append((src, dst, r0, c0, min(STAGE_COLS, n_cols - c0)))
    return jobs


def _prompt_kernel(tbl_ref, sinks_ref, x_ref, xn_ref, ng_ref, gcol_ref, bkp_ref, ps_ref,
                   win_hbm, wabr_hbm, wpbr_hbm, wout_hbm, pw_hbm,
                   y_ref, pk_ref, pv_ref, pp_ref, winb_hbm, wabrb_hbm, wpbrb_hbm, woutb_hbm, pwb_hbm,
                   win_ref, wabr_ref, wpbr_ref, wout_ref, pw_ref, stage, sem_in, sem_out,
                   wt_s, h_s, qt_s, k_s, vt_s, attn_s, pbuf, hist_s, bias_s):
    b = pl.program_id(0)
    t = pl.program_id(1)
    first = (b == 0) & (t == 0)
    q_gain = gcol_ref[:, 0:1]
    k_gain = gcol_ref[:, 1:2]
    zq_s = stage.at[0]
    weights = ((win_hbm, win_ref), (wabr_hbm, wabr_ref), (wpbr_hbm, wpbr_ref), (wout_hbm, wout_ref),
               (pw_hbm, pw_ref))
    exports = (winb_hbm, wabrb_hbm, wpbrb_hbm, woutb_hbm, pwb_hbm)

    def export_copy(i):
        return pltpu.make_async_copy(weights[i][1], exports[i], sem_out.at[i])

    @pl.when(first)
    def _():
        jobs = _weight_jobs(weights)

        def load(i):
            src, _, r0, c0, w = jobs[i]
            return pltpu.make_async_copy(src.at[pl.ds(r0, STAGE_ROWS), pl.ds(c0, w)],
                                         stage.at[i % N_STAGE, :, pl.ds(0, w)], sem_in.at[i % N_STAGE])

        for i in range(min(N_STAGE - 1, len(jobs))):
            load(i).start()
        for i in range(len(jobs)):
            if i + N_STAGE - 1 < len(jobs):
                load(i + N_STAGE - 1).start()
            load(i).wait()
            _, dst, r0, c0, w = jobs[i]
            dst[r0:r0 + STAGE_ROWS, c0:c0 + w] = stage[i % N_STAGE, :, 0:w].astype(BF16)
            if i < N_HEADS // 2:
                bkp = bkp_ref[...]
                pair = jnp.where(lax.broadcasted_iota(jnp.int32, bkp.shape, 1) < CHUNK,
                                 _gather_bias(bkp, tbl_ref, 2 * i), _gather_bias(bkp, tbl_ref, 2 * i + 1))
                bias_s[2 * i // GROUP_REP, :, (2 * i % GROUP_REP) * CHUNK:(2 * i % GROUP_REP + 2) * CHUNK] = pair
            if jobs[i][1] is win_ref and c0 + w == C_V.stop:
                wt_s[...] = win_ref[:, 0:C_V.stop].T
                h_s[...] = _rmsnorm_rows(x_ref[...], ng_ref[...]).astype(BF16)
        assert len(jobs) >= N_HEADS // 2 and C_V.stop % STAGE_COLS == 0
        zq_s[...] = _dot_nt(wt_s[C_Q, :], h_s[...])
        for i in range(len(weights)):
            export_copy(i).start()

    @pl.when(t == 0)
    def _():
        k_s[:, 0:WINDOW, :] = jnp.zeros((2, WINDOW, LANES), BF16)
        vt_s[:, 0:WINDOW] = jnp.zeros((KV_WIDTH, WINDOW), BF16)
        pbuf[0:HIST_ROWS, :] = jnp.zeros((HIST_ROWS, POOL_WIDTH), F32)

    hb = h_s[...]

    n_c = D_MODEL // MXU_DIM
    fill_cols = [slice(c.start + i * MXU_DIM, c.start + (i + 1) * MXU_DIM)
                 for c in (C_PU, C_PG, C_AG, C_MA, C_MP) for i in range(n_c)]
    filled = []

    def fill(upto):
        for cols in fill_cols[len(filled):upto]:
            filled.append(_dot(hb, win_ref[:, cols]))

    zqt = zq_s[...]
    zkt = _dot_nt(wt_s[C_K, :], hb)
    vt = _dot_nt(wt_s[C_V, :], hb)
    fill(PREFILL)
    kt = jnp.concatenate([_head_rmsnorm_fm(zkt, k_gain, g) for g in range(N_KV_HEADS)], axis=0)
    for h in range(N_HEADS):
        qt_s[h * HEAD_DIM:(h + 1) * HEAD_DIM, :] = _head_rmsnorm_fm(zqt, q_gain, h).astype(BF16)

    pk_ref[...] = kt[:, TM - WINDOW:]
    pv_ref[...] = vt[:, TM - WINDOW:]

    k_s[0, WINDOW:, :] = kt[:LANES, :].T.astype(BF16)
    k_s[1, WINDOW:, :] = kt[LANES:, :].T.astype(BF16)
    vt_s[:, WINDOW:] = vt.astype(BF16)

    zeros_half = jnp.zeros((HEAD_DIM, CHUNK), BF16)
    q_lane = lax.broadcasted_iota(jnp.int32, (1, GROUP_REP * CHUNK), 1)

    def scores(p, g):
        out = []
        for c in range(2):
            tok = slice(p * PAIR + c * CHUNK, p * PAIR + (c + 1) * CHUNK)
            keys = slice(p * PAIR + c * CHUNK, p * PAIR + c * CHUNK + CHUNK_KEYS)
            cols = []
            for r in range(GROUP_REP):
                h = g * GROUP_REP + r
                qh = qt_s[h * HEAD_DIM:(h + 1) * HEAD_DIM, tok]
                cols.append(jnp.concatenate([qh, zeros_half] if g % 2 == 0 else [zeros_half, qh], axis=0))
            rhs = jnp.concatenate(cols, axis=1)
            s = _dot(k_s[g // 2, keys, :], rhs) + bias_s[g]
            if p == 0:
                kk = lax.broadcasted_iota(jnp.int32, s.shape, 0)
                s = jnp.where((kk >= WINDOW - c * CHUNK) | (t > 0), s, NEG_INF)
            out.append(s)
        return out

    def finish(p, g, s_ab):
        tok = slice(p * PAIR, (p + 1) * PAIR)
        keys = slice(p * PAIR, p * PAIR + PAIR_KEYS)
        sink = sinks_ref[g * GROUP_REP]
        for r in range(1, GROUP_REP):
            sink = jnp.where(q_lane < r * CHUNK, sink, sinks_ref[g * GROUP_REP + r])
        pads = jnp.zeros((PAIR_KEYS - CHUNK_KEYS, GROUP_REP * CHUNK), BF16)
        e_ab, inv = [], []
        for c, s in enumerate(s_ab):
            m = jnp.maximum(jnp.max(s, axis=0, keepdims=True), sink)
            e = jnp.exp(s - m)
            inv.append(1.0 / (jnp.sum(e, axis=0, keepdims=True) + jnp.exp(sink - m)))
            e_ab.append(jnp.concatenate([e.astype(BF16), pads] if c == 0 else [pads, e.astype(BF16)], axis=0))
        vg = vt_s[g * HEAD_DIM:(g + 1) * HEAD_DIM, keys]
        o_t = _dot(vg, jnp.concatenate(e_ab, axis=1)) * jnp.concatenate(inv, axis=1)
        n_q = GROUP_REP * CHUNK
        for j in range(GROUP_REP // 2):
            rows = [jnp.concatenate([o_t[:, r * CHUNK:(r + 1) * CHUNK], o_t[:, n_q + r * CHUNK:n_q + (r + 1) * CHUNK]],
                                    axis=1) for r in (2 * j, 2 * j + 1)]
            lanes = slice(g * MXU_DIM + j * LANES, g * MXU_DIM + (j + 1) * LANES)
            attn_s[tok, lanes] = jnp.concatenate(rows, axis=0).T

    pu = jnp.concatenate(filled[:n_c], axis=1)
    pbuf[HIST_ROWS:, :] = pu
    hist_s[b] = pu[TM - HIST_ROWS:, :]
    sums = _window_sums(pbuf[...], _delay_rows)
    pos1 = (t * TM + 1 + lax.broadcasted_iota(jnp.int32, (TM, 1), 0)).astype(F32)
    mixed = []
    for gi, w in enumerate(POOL_WINDOWS):
        inv_cnt = 1.0 / jnp.minimum(float(w), pos1)
        mixed.append(sums[gi][HIST_ROWS:, :] * inv_cnt - pu[:, gi * POOL_GROUP:(gi + 1) * POOL_GROUP])

    blocks = [(p, g) for p in range(TM // PAIR) for g in range(N_KV_HEADS)]
    s_next = scores(*blocks[0])
    for i, blk in enumerate(blocks):
        s_cur = s_next
        if i + 1 < len(blocks):
            s_next = scores(*blocks[i + 1])
        fill(PREFILL + (len(fill_cols) - PREFILL) * (i + 1) // len(blocks))
        finish(*blk, s_cur)
    pg, ag, ma, mp = (jnp.concatenate(filled[j * n_c:(j + 1) * n_c], axis=1) for j in range(1, 5))

    merged = _merge_branches(attn_s[...], mixed, (ag, pg, ma, mp), pw_ref, ps_ref, wabr_ref, wpbr_ref)

    h_s[...] = _rmsnorm_rows(xn_ref[...], ng_ref[...]).astype(BF16)
    zq_s[...] = _dot_nt(wt_s[C_Q, :], h_s[...])

    y_ref[...] = x_ref[...] + _dot(merged, wout_ref[...])

    k_s[:, 0:WINDOW, :] = k_s[:, TM:TM + WINDOW, :]
    vt_s[:, 0:WINDOW] = vt_s[:, TM:TM + WINDOW]
    pbuf[0:HIST_ROWS, :] = pbuf[TM:TM + HIST_ROWS, :]

    @pl.when(first)
    def _():
        for i in range(len(weights)):
            export_copy(i).wait()

    @pl.when((b == pl.num_programs(0) - 1) & (t == pl.num_programs(1) - 1))
    def _():
        pp_ref[...] = jnp.swapaxes(hist_s[...], 0, 1)[HIST_ROWS - POOL_HIST:]


def _resident(shape):
    nd = len(shape)
    return pl.BlockSpec(shape, lambda b, t: (0,) * nd, pipeline_mode=pl.Buffered(1))


def _prompt_layer(x, tbl, sinks, ng, gain_cols, bkp, ps, weights):
    B, S, _ = x.shape
    assert S % TM == 0 and TM % PAIR == 0 and TM >= WINDOW
    assert all(w.shape[0] % STAGE_ROWS == 0 for w in weights)
    consts = (ng, gain_cols, bkp, ps)
    hbm = pl.BlockSpec(memory_space=pl.ANY)
    n_t = S // TM

    def next_tile(b, t):
        i = jnp.minimum(b * n_t + t + 1, B * n_t - 1)
        return i // n_t, i % n_t, 0

    return pl.pallas_call(
        _prompt_kernel,
        grid=(B, S // TM),
        in_specs=[pl.BlockSpec(memory_space=pltpu.SMEM),
                  pl.BlockSpec(memory_space=pltpu.SMEM),
                  pl.BlockSpec((None, TM, D_MODEL), lambda b, t: (b, t, 0)),
                  pl.BlockSpec((None, TM, D_MODEL), next_tile)]
                 + [_resident(c.shape) for c in consts] + [hbm] * len(weights),
        out_specs=[
            pl.BlockSpec((None, TM, D_MODEL), lambda b, t: (b, t, 0)),
            pl.BlockSpec((None, KV_WIDTH, WINDOW), lambda b, t: (b, 0, 0)),
            pl.BlockSpec((None, KV_WIDTH, WINDOW), lambda b, t: (b, 0, 0)),
            pl.BlockSpec((POOL_HIST, B, POOL_WIDTH), lambda b, t: (0, 0, 0)),
        ] + [hbm] * len(weights),
        out_shape=[
            jax.ShapeDtypeStruct((B, S, D_MODEL), F32),
            jax.ShapeDtypeStruct((B, KV_WIDTH, WINDOW), F32),
            jax.ShapeDtypeStruct((B, KV_WIDTH, WINDOW), F32),
            jax.ShapeDtypeStruct((POOL_HIST, B, POOL_WIDTH), F32),
        ] + [jax.ShapeDtypeStruct(w.shape, BF16) for w in weights],
        scratch_shapes=[pltpu.VMEM(w.shape, BF16) for w in weights] + [
            pltpu.VMEM((N_STAGE, STAGE_ROWS, STAGE_COLS), F32),
            pltpu.SemaphoreType.DMA((N_STAGE,)),
            pltpu.SemaphoreType.DMA((len(weights),)),
            pltpu.VMEM((C_V.stop, D_MODEL), BF16),
            pltpu.VMEM((TM, D_MODEL), BF16),
            pltpu.VMEM((ATTN_WIDTH, TM), BF16),
            pltpu.VMEM((2, WINDOW + TM, LANES), BF16),
            pltpu.VMEM((KV_WIDTH, WINDOW + TM), BF16),
            pltpu.VMEM((TM, ATTN_WIDTH), F32),
            pltpu.VMEM((HIST_ROWS + TM, POOL_WIDTH), F32),
            pltpu.VMEM((B, HIST_ROWS, POOL_WIDTH), F32),
            pltpu.VMEM((N_KV_HEADS, CHUNK_KEYS, GROUP_REP * CHUNK), F32),
        ],
        compiler_params=pltpu.CompilerParams(
            dimension_semantics=("arbitrary", "arbitrary"),
            vmem_limit_bytes=VMEM_LIMIT,
        ),
        name="prompt_layer",
    )(tbl, sinks, x, x, *consts, *weights)


def _sample_kernel(n_b, n_s, start,
                   tbl_ref, sinks_ref, x_ref, ng_ref,
                   gcol_ref, grow_ref, ones_ref, bks_ref, bkn_ref, ps_ref,
                   skt_hbm, svt_hbm, sp_hbm, win_hbm, pw_hbm, wabr_hbm, wpbr_hbm, wout_hbm,
                   y_ref, ok_hbm, ov_hbm, op_hbm,
                   skt_ref, svt_ref, sp_ref, ok_ref, ov_ref, op_ref,
                   win_ref, pw_ref, wabr_ref, wpbr_ref, wout_ref, sem):
    n_tok = n_b * n_s
    sems = iter(range(N_SAMPLE_MOVES))

    def copies(src, dst, n=SAMPLE_SPLIT, cols=None):
        step, rem = divmod(src.shape[0], n)
        assert rem == 0
        blocks = [(slice(i * step, (i + 1) * step),) + (() if cols is None else (cols,)) for i in range(n)]
        return [pltpu.make_async_copy(src.at[b], dst.at[b], sem.at[next(sems)]) for b in blocks]

    def issue(group):
        for c in group:
            c.start()

    def wait_for(group):
        for c in group:
            c.wait()

    qkv_in = copies(win_hbm, win_ref, cols=slice(0, C_V.stop))
    state_in = copies(skt_hbm, skt_ref) + copies(svt_hbm, svt_ref)
    late_in = (copies(win_hbm, win_ref, cols=slice(C_V.stop, win_ref.shape[1]))
               + copies(sp_hbm, sp_ref, 1) + copies(pw_hbm, pw_ref, 1)
               + copies(wpbr_hbm, wpbr_ref) + copies(wabr_hbm, wabr_ref) + copies(wout_hbm, wout_ref))
    kv_out = copies(ok_ref, ok_hbm) + copies(ov_ref, ov_hbm)
    p_out = copies(op_ref, op_hbm, 1)
    assert next(sems, None) is None
    issue(qkv_in)

    x = x_ref[...]
    hb = _rmsnorm_rows(x, ng_ref[...]).astype(BF16)

    wait_for(qkv_in)
    issue(state_in)
    q = _head_rmsnorm(_dot(hb, win_ref[:, C_Q]), ones_ref, grow_ref[:, :ATTN_WIDTH]).astype(BF16)
    q3 = q.reshape(n_b, n_s, ATTN_WIDTH)
    zkt = _dot_nt(win_ref[:, C_K].T, hb)
    k_gain = gcol_ref[:, 1:2]
    knt = jnp.concatenate([_head_rmsnorm_fm(zkt, k_gain, g) for g in range(N_KV_HEADS)], axis=0)
    vnt = _dot_nt(win_ref[:, C_V].T, hb)

    wait_for(state_in)
    issue(late_in)
    lane = lax.broadcasted_iota(jnp.int32, (1, WINDOW), 1)
    per_tile = LANES // n_s
    for s in range(n_b):
        tile = slice((s // per_tile) * LANES, (s // per_tile + 1) * LANES)
        shift = (WINDOW - n_s - (s % per_tile) * n_s) % LANES
        for new, st_ref, o_ref in ((knt, skt_ref, ok_ref), (vnt, svt_ref, ov_ref)):
            fresh = new[:, tile] if shift == 0 else pltpu.roll(new[:, tile], shift, axis=1)
            o_ref[s] = jnp.where(lane < WINDOW - n_s, pltpu.roll(st_ref[s], WINDOW - n_s, axis=1), fresh)

    knb = knt.astype(BF16)
    vnb = vnt.astype(BF16)
    key_stream = (lax.broadcasted_iota(jnp.int32, (n_b, 1, 2 * n_tok), 2) & (n_tok - 1)) >> (n_s.bit_length() - 1)
    own = key_stream == lax.broadcasted_iota(jnp.int32, (n_b, 1, 2 * n_tok), 0)
    top = lax.broadcasted_iota(jnp.int32, (1, 2 * n_s, 1), 1) < n_s
    low = lax.broadcasted_iota(jnp.int32, (1, 1, LANES), 2) < HEAD_DIM

    def bias_rows(bucket, g):
        return jnp.concatenate([jnp.concatenate(
            [_gather_bias(bucket, tbl_ref, g * GROUP_REP + 2 * c + j) for j in range(2)], axis=1)
            for c in range(2)], axis=0)

    def scores(g):
        hd = slice(g * HEAD_DIM, (g + 1) * HEAD_DIM)
        c = g * MXU_DIM
        lhs = jnp.concatenate([q3[:, :, c:c + LANES], q3[:, :, c + LANES:c + MXU_DIM]], axis=1)
        kst = _diag2(skt_ref[:, hd, :].astype(BF16), 1, 2)
        s_st = jnp.stack([_dot(lhs[s], kst[s]) for s in range(n_b)]) + bias_rows(bks_ref[...], g)[None]
        s_nw = _dot(lhs.reshape(2 * n_tok, LANES), _diag2(knb[hd, :], 0, 1))
        s_nw = jnp.where(own, s_nw.reshape(n_b, 2 * n_s, 2 * n_tok) + bias_rows(bkn_ref[...], g)[None], NEG_INF)
        return s_st, s_nw

    def finish(g, s_st, s_nw):
        hd = slice(g * HEAD_DIM, (g + 1) * HEAD_DIM)
        p_st, p_nw, inv = [], [], []
        for j in range(2):
            a = s_st[:, :, j * WINDOW:(j + 1) * WINDOW]
            b = s_nw[:, :, j * n_tok:(j + 1) * n_tok]
            sink = jnp.where(top, sinks_ref[4 * g + j], sinks_ref[4 * g + 2 + j])
            m = jnp.maximum(jnp.maximum(jnp.max(a, axis=-1, keepdims=True),
                                        jnp.max(b, axis=-1, keepdims=True)), sink)
            ea = jnp.exp(a - m)
            eb = jnp.exp(b - m)
            inv.append(1.0 / (jnp.sum(ea, axis=-1, keepdims=True) + jnp.sum(eb, axis=-1, keepdims=True)
                              + jnp.exp(sink - m)))
            p_st.append(ea.astype(BF16))
            p_nw.append(eb.astype(BF16))
        p_st = jnp.concatenate(p_st, axis=2)
        p_nw = jnp.concatenate(p_nw, axis=2).reshape(2 * n_tok, 2 * n_tok)
        vst = _diag2(svt_ref[:, hd, :].astype(BF16), 1, 2)
        o = jnp.stack([_dot_nt(p_st[s], vst[s]) for s in range(n_b)])
        o = o + _dot_nt(p_nw, _diag2(vnb[hd, :], 0, 1)).reshape(n_b, 2 * n_s, LANES)
        o = o * jnp.where(low, inv[0], inv[1])
        return jnp.concatenate([o[:, :n_s, :], o[:, n_s:, :]], axis=2)

    outs = []
    nxt = scores(0)
    for g in range(N_KV_HEADS):
        cur = nxt
        if g + 1 < N_KV_HEADS:
            nxt = scores(g + 1)
        outs.append(finish(g, *cur))
    attn_o = jnp.concatenate(outs, axis=2).reshape(n_tok, ATTN_WIDTH)

    wait_for(late_in)
    issue(kv_out)
    pu = _dot(hb, win_ref[:, C_PU])
    pu_t = jnp.swapaxes(pu.reshape(n_b, n_s, POOL_WIDTH), 0, 1)
    full = jnp.concatenate([jnp.zeros((1, n_b, POOL_WIDTH), F32), sp_ref[...], pu_t], axis=0)
    op_ref[...] = full[HIST_ROWS + n_s - POOL_HIST:]
    issue(p_out)
    sums = _window_sums(full, _delay_leading)
    pos1 = (start + 1 + lax.broadcasted_iota(jnp.int32, (n_s, 1, 1), 0)).astype(F32)
    mixed = []
    for gi, w in enumerate(POOL_WINDOWS):
        inv_cnt = 1.0 / jnp.minimum(float(w), pos1)
        sw = sums[gi][HIST_ROWS:] * inv_cnt - pu_t[..., gi * POOL_GROUP:(gi + 1) * POOL_GROUP]
        mixed.append(sw.reshape(n_tok, POOL_GROUP))

    def stream_major(po):
        return jnp.swapaxes(po.reshape(n_s, n_b, POOL_WIDTH), 0, 1).reshape(n_tok, POOL_WIDTH)

    gates = tuple(_dot(hb, win_ref[:, c]) for c in (C_AG, C_PG, C_MA, C_MP))
    merged = _merge_branches(attn_o, mixed, gates, pw_ref, ps_ref, wabr_ref, wpbr_ref, stream_major)
    y_ref[...] = x + _dot(merged, wout_ref[...])
    wait_for(kv_out + p_out)


def _sample_layer(x, skt, svt, sp, start, tbl, sinks, ng, w_in, gain_cols, gain_rows, ones, bks, bkn,
                  pw, ps, wabr, wpbr, wout):
    n_b, n_s, _ = x.shape
    n_tok = n_b * n_s
    assert n_s % 16 == 0 and LANES % n_s == 0 and n_s >= POOL_HIST and n_tok % LANES == 0
    assert n_s & (n_s - 1) == 0 and n_tok & (n_tok - 1) == 0
    vmem = pl.BlockSpec(memory_space=pltpu.VMEM)
    anywhere = pl.BlockSpec(memory_space=pl.ANY)
    weights = (w_in, pw, wabr, wpbr, wout)
    state = (skt, svt, sp)
    new_state = [jax.ShapeDtypeStruct(a.shape, a.dtype) for a in state]
    y, ok, ov, op = pl.pallas_call(
        functools.partial(_sample_kernel, n_b, n_s, start),
        in_specs=[pl.BlockSpec(memory_space=pltpu.SMEM)] * 2 + [vmem] * 8
                 + [anywhere] * (len(state) + len(weights)),
        out_specs=[vmem] + [anywhere] * len(state),
        out_shape=[jax.ShapeDtypeStruct((n_tok, D_MODEL), F32)] + new_state,
        scratch_shapes=[pltpu.VMEM(a.shape, a.dtype) for a in state + tuple(new_state) + weights]
                       + [pltpu.SemaphoreType.DMA((N_SAMPLE_MOVES,))],
        compiler_params=pltpu.CompilerParams(vmem_limit_bytes=VMEM_LIMIT),
        name="sample_layer",
    )(tbl, sinks, x.reshape(n_tok, D_MODEL), ng, gain_cols, gain_rows, ones, bks, bkn, ps,
      *state, *weights)
    return y.reshape(n_b, n_s, D_MODEL), ok, ov, op


def _kv_to_feature_major(a):
    return jnp.transpose(a, (0, 2, 3, 1)).reshape(a.shape[0], KV_WIDTH, WINDOW)


def _kv_from_feature_major(a):
    return jnp.transpose(a.reshape(a.shape[0], N_KV_HEADS, HEAD_DIM, WINDOW), (0, 3, 1, 2))


def kernel(x_prompt, x_sample, state_attn_k, state_attn_v, state_pool, norm_gain, w_in, q_norm_gain, k_norm_gain, attn_sinks, rel_bias, pool_w, pool_scale, w_attn_br, w_pool_br, w_out):
    depth = w_in.shape[0]
    n_b, n_s, _ = x_sample.shape
    past_len = 4096
    bkp, bks, bkn = _bucket_tables(n_b, n_s)
    tbl = rel_bias.T
    seg = np.arange(MXU_DIM) // HEAD_DIM
    ones = jnp.asarray(seg[:, None] == seg[None, :], BF16)

    xp, xs = x_prompt, x_sample
    pk, pv, pp, sk, sv, sp = [], [], [], [], [], []
    for l in range(depth):
        ng = norm_gain[l].reshape(1, D_MODEL)
        qg = q_norm_gain[l] * (HEAD_DIM ** -0.5)
        kg = k_norm_gain[l]
        gain_cols = jnp.stack([qg, kg], axis=1)
        gain_rows = jnp.concatenate([jnp.tile(qg, N_HEADS), jnp.tile(kg, N_KV_HEADS)]).reshape(1, -1)
        ps = pool_scale[l].reshape(1, POOL_WIDTH)
        xp, k_l, v_l, p_l, w_in_b, wabr_b, wpbr_b, wout_b, pw_b = _prompt_layer(
            xp, tbl, attn_sinks[l], ng, gain_cols, bkp, ps,
            (w_in[l], w_attn_br[l], w_pool_br[l], w_out[l], pool_w[l].reshape(POOL_WIDTH, POOL_GROUP)))
        tail = (pw_b, ps, wabr_b, wpbr_b, wout_b)
        pk.append(_kv_from_feature_major(k_l))
        pv.append(_kv_from_feature_major(v_l))
        pp.append(jnp.transpose(p_l, (1, 0, 2)))

        xs, k_l, v_l, p_l = _sample_layer(
            xs,
            _kv_to_feature_major(state_attn_k[l]),
            _kv_to_feature_major(state_attn_v[l]),
            jnp.transpose(state_pool[l], (1, 0, 2)),
            past_len, tbl, attn_sinks[l], ng, w_in_b, gain_cols, gain_rows, ones, bks, bkn, *tail)
        sk.append(_kv_from_feature_major(k_l))
        sv.append(_kv_from_feature_major(v_l))
        sp.append(jnp.transpose(p_l, (1, 0, 2)))
    return (xp, xs, jnp.stack(pk), jnp.stack(pv), jnp.stack(pp),
            jnp.stack(sk), jnp.stack(sv), jnp.stack(sp))
```

```python
import functools
import numpy as np

import jax
import jax.numpy as jnp
from jax import lax
from jax.experimental import pallas as pl
from jax.experimental.pallas import tpu as pltpu

D_MODEL = 1024
CHUNK = 64
WINDOW = 128
N_HEADS = 16
N_KV_HEADS = 4
GROUP_REP = N_HEADS // N_KV_HEADS
HEAD_DIM = 64
ATTN_WIDTH = N_HEADS * HEAD_DIM
KV_WIDTH = N_KV_HEADS * HEAD_DIM
POOL_WINDOWS = (2, 4, 8, 16)
POOL_WIDTH = D_MODEL
POOL_GROUP = POOL_WIDTH // len(POOL_WINDOWS)
POOL_HIST = max(POOL_WINDOWS) - 1
HIST_ROWS = POOL_HIST + 1
N_BUCKETS = 32
MAX_DISTANCE = 128
EPS = 1e-6
NEG_INF = -1e30

_SPLITS = (ATTN_WIDTH, KV_WIDTH, KV_WIDTH, ATTN_WIDTH, POOL_WIDTH, POOL_WIDTH, D_MODEL, D_MODEL)
_OFFS = tuple(sum(_SPLITS[:i]) for i in range(len(_SPLITS) + 1))
C_Q, C_K, C_V, C_AG, C_PU, C_PG, C_MA, C_MP = (slice(_OFFS[i], _OFFS[i + 1]) for i in range(8))

LANES = 128
MXU_DIM = 256
PAIR = 2 * CHUNK
PAIR_KEYS = PAIR + WINDOW
CHUNK_KEYS = CHUNK + WINDOW
TM = 512
PREFILL = 8
STAGE_ROWS, STAGE_COLS, N_STAGE = ATTN_WIDTH, TM, 3
SAMPLE_SPLIT = 4
N_SAMPLE_MOVES = 9 * SAMPLE_SPLIT + 3
VMEM_LIMIT = 60 * 1024 * 1024

BF16 = jnp.bfloat16
F32 = jnp.float32


def _dot(a, b):
    return jnp.dot(a, b, preferred_element_type=F32)


def _dot_nt(a, b):
    return lax.dot_general(a, b, (((1,), (1,)), ((), ())), preferred_element_type=F32)


def _rmsnorm_rows(x, gain):
    y = x * lax.rsqrt(jnp.mean(x * x, axis=-1, keepdims=True) + EPS)
    return y * gain


def _head_rmsnorm(z, ones_ref, gain):
    sq = z * z
    hi = sq.astype(BF16)
    lo = (sq - hi.astype(F32)).astype(BF16)
    ones = ones_ref[...]
    parts = []
    for c in range(z.shape[1] // MXU_DIM):
        sl = slice(c * MXU_DIM, (c + 1) * MXU_DIM)
        parts.append(_dot(hi[:, sl], ones) + _dot(lo[:, sl], ones))
    ss = parts[0] if len(parts) == 1 else jnp.concatenate(parts, axis=1)
    return (z * lax.rsqrt(ss * (1.0 / HEAD_DIM) + EPS)) * gain


def _head_rmsnorm_fm(zt, gain_col, h):
    zh = zt[h * HEAD_DIM:(h + 1) * HEAD_DIM, :]
    ms = jnp.sum(zh * zh, axis=0, keepdims=True) * (1.0 / HEAD_DIM)
    return (zh * lax.rsqrt(ms + EPS)) * gain_col


def _window_sums(full, delay):
    outs = []
    for gi, w in enumerate(POOL_WINDOWS):
        s = full[..., gi * POOL_GROUP:(gi + 1) * POOL_GROUP]
        sh = 1
        while sh < w:
            s = s + delay(s, sh)
            sh *= 2
        outs.append(s)
    return outs


def _delay_rows(a, n):
    return pltpu.roll(a, n, axis=0)


def _delay_leading(a, n):
    return jnp.concatenate([a[-n:], a[:-n]], axis=0)


def _merge_branches(attn_o, pool_mixed, gates, pw_ref, ps_ref, wabr_ref, wpbr_ref, pool_rows=None):
    ag, pg, ma, mp = gates
    po = []
    for gi in range(len(POOL_WINDOWS)):
        po.append(_dot(pool_mixed[gi].astype(BF16), pw_ref[gi * POOL_GROUP:(gi + 1) * POOL_GROUP, :]))
    po = jnp.concatenate(po, axis=1)
    if pool_rows is not None:
        po = pool_rows(po)
    po = po * ps_ref[...]
    p = _dot((po * (pg * jax.nn.sigmoid(pg))).astype(BF16), wpbr_ref[...])
    a = _dot((attn_o * (ag * jax.nn.sigmoid(ag))).astype(BF16), wabr_ref[...])
    acc = jax.nn.sigmoid(ma) * a + jax.nn.sigmoid(mp) * p
    return acc.astype(BF16)


def _diag2(a, axis_r, axis_c):
    z = jnp.zeros_like(a)
    return jnp.concatenate([jnp.concatenate([a, z], axis=axis_c),
                            jnp.concatenate([z, a], axis=axis_c)], axis=axis_r)


def _gather_bias(bucket, tbl_ref, h):
    acc = jnp.zeros(bucket.shape, F32)
    for b in range(N_BUCKETS):
        acc = jnp.where(bucket == b, tbl_ref[h, b], acc)
    return acc


def _t5_bucket(rel):
    assert (N_BUCKETS, MAX_DISTANCE) == (32, 128)
    nb = N_BUCKETS // 2
    max_exact = nb // 2
    n = np.abs(rel)
    log2_sq = np.vectorize(lambda v: max(int(v), 1).bit_length() - 1)(n * n)
    large = np.minimum(max_exact + log2_sq - 6, nb - 1)
    return (np.where(rel > 0, nb, 0) + np.where(n < max_exact, n, large)).astype(np.int32)


def _bucket_tables(n_b, n_s):
    bkp = _t5_bucket(np.arange(CHUNK_KEYS)[:, None] - WINDOW - np.arange(LANES)[None, :] % CHUNK)
    fr = np.arange(n_s)[:, None]
    bks = _t5_bucket(np.arange(WINDOW)[None, :] - WINDOW - fr)
    bkn = _t5_bucket(np.arange(n_b * n_s)[None, :] % n_s - fr)
    return bkp, bks, bkn


def _weight_jobs(pairs):
    jobs = []
    for src, dst in pairs:
        n_rows, n_cols = dst.shape
        for r0 in range(0, n_rows, STAGE_ROWS):
            for c0 in range(0, n_cols, STAGE_COLS):
                jobs.append((src, dst, r0, c0, min(STAGE_COLS, n_cols - c0)))
    return jobs


def _prompt_kernel(tbl_ref, sinks_ref, x_ref, xn_ref, ng_ref, gcol_ref, bkp_ref, ps_ref,
                   win_hbm, wabr_hbm, wpbr_hbm, wout_hbm, pw_hbm,
                   y_ref, pk_ref, pv_ref, pp_ref, winb_hbm, wabrb_hbm, wpbrb_hbm, woutb_hbm, pwb_hbm,
                   win_ref, wabr_ref, wpbr_ref, wout_ref, pw_ref, stage, sem_in, sem_out,
                   wt_s, h_s, qt_s, k_s, vt_s, attn_s, pbuf, hist_s, bias_s):
    b = pl.program_id(0)
    t = pl.program_id(1)
    first = (b == 0) & (t == 0)
    q_gain = gcol_ref[:, 0:1]
    k_gain = gcol_ref[:, 1:2]
    zq_s = stage.at[0]
    weights = ((win_hbm, win_ref), (wabr_hbm, wabr_ref), (wpbr_hbm, wpbr_ref), (wout_hbm, wout_ref),
               (pw_hbm, pw_ref))
    exports = (winb_hbm, wabrb_hbm, wpbrb_hbm, woutb_hbm, pwb_hbm)

    def export_copy(i):
        return pltpu.make_async_copy(weights[i][1], exports[i], sem_out.at[i])

    @pl.when(first)
    def _():
        jobs = _weight_jobs(weights)

        def load(i):
            src, _, r0, c0, w = jobs[i]
            return pltpu.make_async_copy(src.at[pl.ds(r0, STAGE_ROWS), pl.ds(c0, w)],
                                         stage.at[i % N_STAGE, :, pl.ds(0, w)], sem_in.at[i % N_STAGE])

        for i in range(min(N_STAGE - 1, len(jobs))):
            load(i).start()
        for i in range(len(jobs)):
            if i + N_STAGE - 1 < len(jobs):
                load(i + N_STAGE - 1).start()
            load(i).wait()
            _, dst, r0, c0, w = jobs[i]
            dst[r0:r0 + STAGE_ROWS, c0:c0 + w] = stage[i % N_STAGE, :, 0:w].astype(BF16)
            if i < N_HEADS // 2:
                bkp = bkp_ref[...]
                pair = jnp.where(lax.broadcasted_iota(jnp.int32, bkp.shape, 1) < CHUNK,
                                 _gather_bias(bkp, tbl_ref, 2 * i), _gather_bias(bkp, tbl_ref, 2 * i + 1))
                bias_s[2 * i // GROUP_REP, :, (2 * i % GROUP_REP) * CHUNK:(2 * i % GROUP_REP + 2) * CHUNK] = pair
            if jobs[i][1] is win_ref and c0 + w == C_V.stop:
                wt_s[...] = win_ref[:, 0:C_V.stop].T
                h_s[...] = _rmsnorm_rows(x_ref[...], ng_ref[...]).astype(BF16)
        assert len(jobs) >= N_HEADS // 2 and C_V.stop % STAGE_COLS == 0
        zq_s[...] = _dot_nt(wt_s[C_Q, :], h_s[...])
        for i in range(len(weights)):
            export_copy(i).start()

    @pl.when(t == 0)
    def _():
        k_s[:, 0:WINDOW, :] = jnp.zeros((2, WINDOW, LANES), BF16)
        vt_s[:, 0:WINDOW] = jnp.zeros((KV_WIDTH, WINDOW), BF16)
        pbuf[0:HIST_ROWS, :] = jnp.zeros((HIST_ROWS, POOL_WIDTH), F32)

    hb = h_s[...]

    n_c = D_MODEL // MXU_DIM
    fill_cols = [slice(c.start + i * MXU_DIM, c.start + (i + 1) * MXU_DIM)
                 for c in (C_PU, C_PG, C_AG, C_MA, C_MP) for i in range(n_c)]
    filled = []

    def fill(upto):
        for cols in fill_cols[len(filled):upto]:
            filled.append(_dot(hb, win_ref[:, cols]))

    zqt = zq_s[...]
    zkt = _dot_nt(wt_s[C_K, :], hb)
    vt = _dot_nt(wt_s[C_V, :], hb)
    fill(PREFILL)
    kt = jnp.concatenate([_head_rmsnorm_fm(zkt, k_gain, g) for g in range(N_KV_HEADS)], axis=0)
    for h in range(N_HEADS):
        qt_s[h * HEAD_DIM:(h + 1) * HEAD_DIM, :] = _head_rmsnorm_fm(zqt, q_gain, h).astype(BF16)

    pk_ref[...] = kt[:, TM - WINDOW:]
    pv_ref[...] = vt[:, TM - WINDOW:]

    k_s[0, WINDOW:, :] = kt[:LANES, :].T.astype(BF16)
    k_s[1, WINDOW:, :] = kt[LANES:, :].T.astype(BF16)
    vt_s[:, WINDOW:] = vt.astype(BF16)

    zeros_half = jnp.zeros((HEAD_DIM, CHUNK), BF16)
    q_lane = lax.broadcasted_iota(jnp.int32, (1, GROUP_REP * CHUNK), 1)

    def scores(p, g):
        out = []
        for c in range(2):
            tok = slice(p * PAIR + c * CHUNK, p * PAIR + (c + 1) * CHUNK)
            keys = slice(p * PAIR + c * CHUNK, p * PAIR + c * CHUNK + CHUNK_KEYS)
            cols = []
            for r in range(GROUP_REP):
                h = g * GROUP_REP + r
                qh = qt_s[h * HEAD_DIM:(h + 1) * HEAD_DIM, tok]
                cols.append(jnp.concatenate([qh, zeros_half] if g % 2 == 0 else [zeros_half, qh], axis=0))
            rhs = jnp.concatenate(cols, axis=1)
            s = _dot(k_s[g // 2, keys, :], rhs) + bias_s[g]
            if p == 0:
                kk = lax.broadcasted_iota(jnp.int32, s.shape, 0)
                s = jnp.where((kk >= WINDOW - c * CHUNK) | (t > 0), s, NEG_INF)
            out.append(s)
        return out

    def finish(p, g, s_ab):
        tok = slice(p * PAIR, (p + 1) * PAIR)
        keys = slice(p * PAIR, p * PAIR + PAIR_KEYS)
        sink = sinks_ref[g * GROUP_REP]
        for r in range(1, GROUP_REP):
            sink = jnp.where(q_lane < r * CHUNK, sink, sinks_ref[g * GROUP_REP + r])
        pads = jnp.zeros((PAIR_KEYS - CHUNK_KEYS, GROUP_REP * CHUNK), BF16)
        e_ab, inv = [], []
        for c, s in enumerate(s_ab):
            m = jnp.maximum(jnp.max(s, axis=0, keepdims=True), sink)
            e = jnp.exp(s - m)
            inv.append(1.0 / (jnp.sum(e, axis=0, keepdims=True) + jnp.exp(sink - m)))
            e_ab.append(jnp.concatenate([e.astype(BF16), pads] if c == 0 else [pads, e.astype(BF16)], axis=0))
        vg = vt_s[g * HEAD_DIM:(g + 1) * HEAD_DIM, keys]
        o_t = _dot(vg, jnp.concatenate(e_ab, axis=1)) * jnp.concatenate(inv, axis=1)
        n_q = GROUP_REP * CHUNK
        for j in range(GROUP_REP // 2):
            rows = [jnp.concatenate([o_t[:, r * CHUNK:(r + 1) * CHUNK], o_t[:, n_q + r * CHUNK:n_q + (r + 1) * CHUNK]],
                                    axis=1) for r in (2 * j, 2 * j + 1)]
            lanes = slice(g * MXU_DIM + j * LANES, g * MXU_DIM + (j + 1) * LANES)
            attn_s[tok, lanes] = jnp.concatenate(rows, axis=0).T

    pu = jnp.concatenate(filled[:n_c], axis=1)
    pbuf[HIST_ROWS:, :] = pu
    hist_s[b] = pu[TM - HIST_ROWS:, :]
    sums = _window_sums(pbuf[...], _delay_rows)
    pos1 = (t * TM + 1 + lax.broadcasted_iota(jnp.int32, (TM, 1), 0)).astype(F32)
    mixed = []
    for gi, w in enumerate(POOL_WINDOWS):
        inv_cnt = 1.0 / jnp.minimum(float(w), pos1)
        mixed.append(sums[gi][HIST_ROWS:, :] * inv_cnt - pu[:, gi * POOL_GROUP:(gi + 1) * POOL_GROUP])

    blocks = [(p, g) for p in range(TM // PAIR) for g in range(N_KV_HEADS)]
    s_next = scores(*blocks[0])
    for i, blk in enumerate(blocks):
        s_cur = s_next
        if i + 1 < len(blocks):
            s_next = scores(*blocks[i + 1])
        fill(PREFILL + (len(fill_cols) - PREFILL) * (i + 1) // len(blocks))
        finish(*blk, s_cur)
    pg, ag, ma, mp = (jnp.concatenate(filled[j * n_c:(j + 1) * n_c], axis=1) for j in range(1, 5))

    merged = _merge_branches(attn_s[...], mixed, (ag, pg, ma, mp), pw_ref, ps_ref, wabr_ref, wpbr_ref)

    h_s[...] = _rmsnorm_rows(xn_ref[...], ng_ref[...]).astype(BF16)
    zq_s[...] = _dot_nt(wt_s[C_Q, :], h_s[...])

    y_ref[...] = x_ref[...] + _dot(merged, wout_ref[...])

    k_s[:, 0:WINDOW, :] = k_s[:, TM:TM + WINDOW, :]
    vt_s[:, 0:WINDOW] = vt_s[:, TM:TM + WINDOW]
    pbuf[0:HIST_ROWS, :] = pbuf[TM:TM + HIST_ROWS, :]

    @pl.when(first)
    def _():
        for i in range(len(weights)):
            export_copy(i).wait()

    @pl.when((b == pl.num_programs(0) - 1) & (t == pl.num_programs(1) - 1))
    def _():
        pp_ref[...] = jnp.swapaxes(hist_s[...], 0, 1)[HIST_ROWS - POOL_HIST:]


def _resident(shape):
    nd = len(shape)
    return pl.BlockSpec(shape, lambda b, t: (0,) * nd, pipeline_mode=pl.Buffered(1))


def _prompt_layer(x, tbl, sinks, ng, gain_cols, bkp, ps, weights):
    B, S, _ = x.shape
    assert S % TM == 0 and TM % PAIR == 0 and TM >= WINDOW
    assert all(w.shape[0] % STAGE_ROWS == 0 for w in weights)
    consts = (ng, gain_cols, bkp, ps)
    hbm = pl.BlockSpec(memory_space=pl.ANY)
    n_t = S // TM

    def next_tile(b, t):
        i = jnp.minimum(b * n_t + t + 1, B * n_t - 1)
        return i // n_t, i % n_t, 0

    return pl.pallas_call(
        _prompt_kernel,
        grid=(B, S // TM),
        in_specs=[pl.BlockSpec(memory_space=pltpu.SMEM),
                  pl.BlockSpec(memory_space=pltpu.SMEM),
                  pl.BlockSpec((None, TM, D_MODEL), lambda b, t: (b, t, 0)),
                  pl.BlockSpec((None, TM, D_MODEL), next_tile)]
                 + [_resident(c.shape) for c in consts] + [hbm] * len(weights),
        out_specs=[
            pl.BlockSpec((None, TM, D_MODEL), lambda b, t: (b, t, 0)),
            pl.BlockSpec((None, KV_WIDTH, WINDOW), lambda b, t: (b, 0, 0)),
            pl.BlockSpec((None, KV_WIDTH, WINDOW), lambda b, t: (b, 0, 0)),
            pl.BlockSpec((POOL_HIST, B, POOL_WIDTH), lambda b, t: (0, 0, 0)),
        ] + [hbm] * len(weights),
        out_shape=[
            jax.ShapeDtypeStruct((B, S, D_MODEL), F32),
            jax.ShapeDtypeStruct((B, KV_WIDTH, WINDOW), F32),
            jax.ShapeDtypeStruct((B, KV_WIDTH, WINDOW), F32),
            jax.ShapeDtypeStruct((POOL_HIST, B, POOL_WIDTH), F32),
        ] + [jax.ShapeDtypeStruct(w.shape, BF16) for w in weights],
        scratch_shapes=[pltpu.VMEM(w.shape, BF16) for w in weights] + [
            pltpu.VMEM((N_STAGE, STAGE_ROWS, STAGE_COLS), F32),
            pltpu.SemaphoreType.DMA((N_STAGE,)),
            pltpu.SemaphoreType.DMA((len(weights),)),
            pltpu.VMEM((C_V.stop, D_MODEL), BF16),
            pltpu.VMEM((TM, D_MODEL), BF16),
            pltpu.VMEM((ATTN_WIDTH, TM), BF16),
            pltpu.VMEM((2, WINDOW + TM, LANES), BF16),
            pltpu.VMEM((KV_WIDTH, WINDOW + TM), BF16),
            pltpu.VMEM((TM, ATTN_WIDTH), F32),
            pltpu.VMEM((HIST_ROWS + TM, POOL_WIDTH), F32),
            pltpu.VMEM((B, HIST_ROWS, POOL_WIDTH), F32),
            pltpu.VMEM((N_KV_HEADS, CHUNK_KEYS, GROUP_REP * CHUNK), F32),
        ],
        compiler_params=pltpu.CompilerParams(
            dimension_semantics=("arbitrary", "arbitrary"),
            vmem_limit_bytes=VMEM_LIMIT,
        ),
        name="prompt_layer",
    )(tbl, sinks, x, x, *consts, *weights)


def _sample_kernel(n_b, n_s, start,
                   tbl_ref, sinks_ref, x_ref, ng_ref,
                   gcol_ref, grow_ref, ones_ref, bks_ref, bkn_ref, ps_ref,
                   skt_hbm, svt_hbm, sp_hbm, win_hbm, pw_hbm, wabr_hbm, wpbr_hbm, wout_hbm,
                   y_ref, ok_hbm, ov_hbm, op_hbm,
                   skt_ref, svt_ref, sp_ref, ok_ref, ov_ref, op_ref,
                   win_ref, pw_ref, wabr_ref, wpbr_ref, wout_ref, sem):
    n_tok = n_b * n_s
    sems = iter(range(N_SAMPLE_MOVES))

    def copies(src, dst, n=SAMPLE_SPLIT, cols=None):
        step, rem = divmod(src.shape[0], n)
        assert rem == 0
        blocks = [(slice(i * step, (i + 1) * step),) + (() if cols is None else (cols,)) for i in range(n)]
        return [pltpu.make_async_copy(src.at[b], dst.at[b], sem.at[next(sems)]) for b in blocks]

    def issue(group):
        for c in group:
            c.start()

    def wait_for(group):
        for c in group:
            c.wait()

    qkv_in = copies(win_hbm, win_ref, cols=slice(0, C_V.stop))
    state_in = copies(skt_hbm, skt_ref) + copies(svt_hbm, svt_ref)
    late_in = (copies(win_hbm, win_ref, cols=slice(C_V.stop, win_ref.shape[1]))
               + copies(sp_hbm, sp_ref, 1) + copies(pw_hbm, pw_ref, 1)
               + copies(wpbr_hbm, wpbr_ref) + copies(wabr_hbm, wabr_ref) + copies(wout_hbm, wout_ref))
    kv_out = copies(ok_ref, ok_hbm) + copies(ov_ref, ov_hbm)
    p_out = copies(op_ref, op_hbm, 1)
    assert next(sems, None) is None
    issue(qkv_in + state_in + late_in)

    x = x_ref[...]
    hb = _rmsnorm_rows(x, ng_ref[...]).astype(BF16)

    wait_for(qkv_in)
    q = _head_rmsnorm(_dot(hb, win_ref[:, C_Q]), ones_ref, grow_ref[:, :ATTN_WIDTH]).astype(BF16)
    q3 = q.reshape(n_b, n_s, ATTN_WIDTH)
    zkt = _dot_nt(win_ref[:, C_K].T, hb)
    k_gain = gcol_ref[:, 1:2]
    knt = jnp.concatenate([_head_rmsnorm_fm(zkt, k_gain, g) for g in range(N_KV_HEADS)], axis=0)
    vnt = _dot_nt(win_ref[:, C_V].T, hb)

    wait_for(state_in)
    lane = lax.broadcasted_iota(jnp.int32, (1, WINDOW), 1)
    per_tile = LANES // n_s
    for s in range(n_b):
        tile = slice((s // per_tile) * LANES, (s // per_tile + 1) * LANES)
        shift = (WINDOW - n_s - (s % per_tile) * n_s) % LANES
        for new, st_ref, o_ref in ((knt, skt_ref, ok_ref), (vnt, svt_ref, ov_ref)):
            fresh = new[:, tile] if shift == 0 else pltpu.roll(new[:, tile], shift, axis=1)
            o_ref[s] = jnp.where(lane < WINDOW - n_s, pltpu.roll(st_ref[s], WINDOW - n_s, axis=1), fresh)

    knb = knt.astype(BF16)
    vnb = vnt.astype(BF16)
    key_stream = (lax.broadcasted_iota(jnp.int32, (n_b, 1, 2 * n_tok), 2) & (n_tok - 1)) >> (n_s.bit_length() - 1)
    own = key_stream == lax.broadcasted_iota(jnp.int32, (n_b, 1, 2 * n_tok), 0)
    top = lax.broadcasted_iota(jnp.int32, (1, 2 * n_s, 1), 1) < n_s
    low = lax.broadcasted_iota(jnp.int32, (1, 1, LANES), 2) < HEAD_DIM

    def bias_rows(bucket, g):
        return jnp.concatenate([jnp.concatenate(
            [_gather_bias(bucket, tbl_ref, g * GROUP_REP + 2 * c + j) for j in range(2)], axis=1)
            for c in range(2)], axis=0)

    def scores(g):
        hd = slice(g * HEAD_DIM, (g + 1) * HEAD_DIM)
        c = g * MXU_DIM
        lhs = jnp.concatenate([q3[:, :, c:c + LANES], q3[:, :, c + LANES:c + MXU_DIM]], axis=1)
        kst = _diag2(skt_ref[:, hd, :].astype(BF16), 1, 2)
        s_st = jnp.stack([_dot(lhs[s], kst[s]) for s in range(n_b)]) + bias_rows(bks_ref[...], g)[None]
        s_nw = _dot(lhs.reshape(2 * n_tok, LANES), _diag2(knb[hd, :], 0, 1))
        s_nw = jnp.where(own, s_nw.reshape(n_b, 2 * n_s, 2 * n_tok) + bias_rows(bkn_ref[...], g)[None], NEG_INF)
        return s_st, s_nw

    def finish(g, s_st, s_nw):
        hd = slice(g * HEAD_DIM, (g + 1) * HEAD_DIM)
        p_st, p_nw, inv = [], [], []
        for j in range(2):
            a = s_st[:, :, j * WINDOW:(j + 1) * WINDOW]
            b = s_nw[:, :, j * n_tok:(j + 1) * n_tok]
            sink = jnp.where(top, sinks_ref[4 * g + j], sinks_ref[4 * g + 2 + j])
            m = jnp.maximum(jnp.maximum(jnp.max(a, axis=-1, keepdims=True),
                                        jnp.max(b, axis=-1, keepdims=True)), sink)
            ea = jnp.exp(a - m)
            eb = jnp.exp(b - m)
            inv.append(1.0 / (jnp.sum(ea, axis=-1, keepdims=True) + jnp.sum(eb, axis=-1, keepdims=True)
                              + jnp.exp(sink - m)))
            p_st.append(ea.astype(BF16))
            p_nw.append(eb.astype(BF16))
        p_st = jnp.concatenate(p_st, axis=2)
        p_nw = jnp.concatenate(p_nw, axis=2).reshape(2 * n_tok, 2 * n_tok)
        vst = _diag2(svt_ref[:, hd, :].astype(BF16), 1, 2)
        o = jnp.stack([_dot_nt(p_st[s], vst[s]) for s in range(n_b)])
        o = o + _dot_nt(p_nw, _diag2(vnb[hd, :], 0, 1)).reshape(n_b, 2 * n_s, LANES)
        o = o * jnp.where(low, inv[0], inv[1])
        return jnp.concatenate([o[:, :n_s, :], o[:, n_s:, :]], axis=2)

    outs = []
    nxt = scores(0)
    for g in range(N_KV_HEADS):
        cur = nxt
        if g + 1 < N_KV_HEADS:
            nxt = scores(g + 1)
        outs.append(finish(g, *cur))
    attn_o = jnp.concatenate(outs, axis=2).reshape(n_tok, ATTN_WIDTH)

    wait_for(late_in)
    issue(kv_out)
    pu = _dot(hb, win_ref[:, C_PU])
    pu_t = jnp.swapaxes(pu.reshape(n_b, n_s, POOL_WIDTH), 0, 1)
    full = jnp.concatenate([jnp.zeros((1, n_b, POOL_WIDTH), F32), sp_ref[...], pu_t], axis=0)
    op_ref[...] = full[HIST_ROWS + n_s - POOL_HIST:]
    issue(p_out)
    sums = _window_sums(full, _delay_leading)
    pos1 = (start + 1 + lax.broadcasted_iota(jnp.int32, (n_s, 1, 1), 0)).astype(F32)
    mixed = []
    for gi, w in enumerate(POOL_WINDOWS):
        inv_cnt = 1.0 / jnp.minimum(float(w), pos1)
        sw = sums[gi][HIST_ROWS:] * inv_cnt - pu_t[..., gi * POOL_GROUP:(gi + 1) * POOL_GROUP]
        mixed.append(sw.reshape(n_tok, POOL_GROUP))

    def stream_major(po):
        return jnp.swapaxes(po.reshape(n_s, n_b, POOL_WIDTH), 0, 1).reshape(n_tok, POOL_WIDTH)

    gates = tuple(_dot(hb, win_ref[:, c]) for c in (C_AG, C_PG, C_MA, C_MP))
    merged = _merge_branches(attn_o, mixed, gates, pw_ref, ps_ref, wabr_ref, wpbr_ref, stream_major)
    y_ref[...] = x + _dot(merged, wout_ref[...])
    wait_for(kv_out + p_out)


def _sample_layer(x, skt, svt, sp, start, tbl, sinks, ng, w_in, gain_cols, gain_rows, ones, bks, bkn,
                  pw, ps, wabr, wpbr, wout):
    n_b, n_s, _ = x.shape
    n_tok = n_b * n_s
    assert n_s % 16 == 0 and LANES % n_s == 0 and n_s >= POOL_HIST and n_tok % LANES == 0
    assert n_s & (n_s - 1) == 0 and n_tok & (n_tok - 1) == 0
    vmem = pl.BlockSpec(memory_space=pltpu.VMEM)
    anywhere = pl.BlockSpec(memory_space=pl.ANY)
    weights = (w_in, pw, wabr, wpbr, wout)
    state = (skt, svt, sp)
    new_state = [jax.ShapeDtypeStruct(a.shape, a.dtype) for a in state]
    y, ok, ov, op = pl.pallas_call(
        functools.partial(_sample_kernel, n_b, n_s, start),
        in_specs=[pl.BlockSpec(memory_space=pltpu.SMEM)] * 2 + [vmem] * 8
                 + [anywhere] * (len(state) + len(weights)),
        out_specs=[vmem] + [anywhere] * len(state),
        out_shape=[jax.ShapeDtypeStruct((n_tok, D_MODEL), F32)] + new_state,
        scratch_shapes=[pltpu.VMEM(a.shape, a.dtype) for a in state + tuple(new_state) + weights]
                       + [pltpu.SemaphoreType.DMA((N_SAMPLE_MOVES,))],
        compiler_params=pltpu.CompilerParams(vmem_limit_bytes=VMEM_LIMIT),
        name="sample_layer",
    )(tbl, sinks, x.reshape(n_tok, D_MODEL), ng, gain_cols, gain_rows, ones, bks, bkn, ps,
      *state, *weights)
    return y.reshape(n_b, n_s, D_MODEL), ok, ov, op


def _kv_to_feature_major(a):
    return jnp.transpose(a, (0, 2, 3, 1)).reshape(a.shape[0], KV_WIDTH, WINDOW)


def _kv_from_feature_major(a):
    return jnp.transpose(a.reshape(a.shape[0], N_KV_HEADS, HEAD_DIM, WINDOW), (0, 3, 1, 2))


def kernel(x_prompt, x_sample, state_attn_k, state_attn_v, state_pool, norm_gain, w_in, q_norm_gain, k_norm_gain, attn_sinks, rel_bias, pool_w, pool_scale, w_attn_br, w_pool_br, w_out):
    depth = w_in.shape[0]
    n_b, n_s, _ = x_sample.shape
    past_len = 4096
    bkp, bks, bkn = _bucket_tables(n_b, n_s)
    tbl = rel_bias.T
    seg = np.arange(MXU_DIM) // HEAD_DIM
    ones = jnp.asarray(seg[:, None] == seg[None, :], BF16)

    xp, xs = x_prompt, x_sample
    pk, pv, pp, sk, sv, sp = [], [], [], [], [], []
    for l in range(depth):
        ng = norm_gain[l].reshape(1, D_MODEL)
        qg = q_norm_gain[l] * (HEAD_DIM ** -0.5)
        kg = k_norm_gain[l]
        gain_cols = jnp.stack([qg, kg], axis=1)
        gain_rows = jnp.concatenate([jnp.tile(qg, N_HEADS), jnp.tile(kg, N_KV_HEADS)]).reshape(1, -1)
        ps = pool_scale[l].reshape(1, POOL_WIDTH)
        xp, k_l, v_l, p_l, w_in_b, wabr_b, wpbr_b, wout_b, pw_b = _prompt_layer(
            xp, tbl, attn_sinks[l], ng, gain_cols, bkp, ps,
            (w_in[l], w_attn_br[l], w_pool_br[l], w_out[l], pool_w[l].reshape(POOL_WIDTH, POOL_GROUP)))
        tail = (pw_b, ps, wabr_b, wpbr_b, wout_b)
        pk.append(_kv_from_feature_major(k_l))
        pv.append(_kv_from_feature_major(v_l))
        pp.append(jnp.transpose(p_l, (1, 0, 2)))

        xs, k_l, v_l, p_l = _sample_layer(
            xs,
            _kv_to_feature_major(state_attn_k[l]),
            _kv_to_feature_major(state_attn_v[l]),
            jnp.transpose(state_pool[l], (1, 0, 2)),
            past_len, tbl, attn_sinks[l], ng, w_in_b, gain_cols, gain_rows, ones, bks, bkn, *tail)
        sk.append(_kv_from_feature_major(k_l))
        sv.append(_kv_from_feature_major(v_l))
        sp.append(jnp.transpose(p_l, (1, 0, 2)))
    return (xp, xs, jnp.stack(pk), jnp.stack(pv), jnp.stack(pp),
            jnp.stack(sk), jnp.stack(sv), jnp.stack(sp))
```

```python
import functools
import numpy as np

import jax
import jax.numpy as jnp
from jax import lax
from jax.experimental import pallas as pl
from jax.experimental.pallas import tpu as pltpu

D_MODEL = 1024
CHUNK = 64
WINDOW = 128
N_HEADS = 16
N_KV_HEADS = 4
GROUP_REP = N_HEADS // N_KV_HEADS
HEAD_DIM = 64
ATTN_WIDTH = N_HEADS * HEAD_DIM
KV_WIDTH = N_KV_HEADS * HEAD_DIM
POOL_WINDOWS = (2, 4, 8, 16)
POOL_WIDTH = D_MODEL
POOL_GROUP = POOL_WIDTH // len(POOL_WINDOWS)
POOL_HIST = max(POOL_WINDOWS) - 1
HIST_ROWS = POOL_HIST + 1
N_BUCKETS = 32
MAX_DISTANCE = 128
EPS = 1e-6
NEG_INF = -1e30

_SPLITS = (ATTN_WIDTH, KV_WIDTH, KV_WIDTH, ATTN_WIDTH, POOL_WIDTH, POOL_WIDTH, D_MODEL, D_MODEL)
_OFFS = tuple(sum(_SPLITS[:i]) for i in range(len(_SPLITS) + 1))
C_Q, C_K, C_V, C_AG, C_PU, C_PG, C_MA, C_MP = (slice(_OFFS[i], _OFFS[i + 1]) for i in range(8))

LANES = 128
MXU_DIM = 256
PAIR = 2 * CHUNK
PAIR_KEYS = PAIR + WINDOW
CHUNK_KEYS = CHUNK + WINDOW
TM = 512
PREFILL = 8
STAGE_ROWS, STAGE_COLS, N_STAGE = ATTN_WIDTH, TM, 3
N_SAMPLE_MOVES = 12
VMEM_LIMIT = 60 * 1024 * 1024

BF16 = jnp.bfloat16
F32 = jnp.float32


def _dot(a, b):
    return jnp.dot(a, b, preferred_element_type=F32)


def _dot_nt(a, b):
    return lax.dot_general(a, b, (((1,), (1,)), ((), ())), preferred_element_type=F32)


def _rmsnorm_rows(x, gain):
    y = x * lax.rsqrt(jnp.mean(x * x, axis=-1, keepdims=True) + EPS)
    return y * gain


def _head_rmsnorm(z, ones_ref, gain):
    sq = z * z
    hi = sq.astype(BF16)
    lo = (sq - hi.astype(F32)).astype(BF16)
    ones = ones_ref[...]
    parts = []
    for c in range(z.shape[1] // MXU_DIM):
        sl = slice(c * MXU_DIM, (c + 1) * MXU_DIM)
        parts.append(_dot(hi[:, sl], ones) + _dot(lo[:, sl], ones))
    ss = parts[0] if len(parts) == 1 else jnp.concatenate(parts, axis=1)
    return (z * lax.rsqrt(ss * (1.0 / HEAD_DIM) + EPS)) * gain


def _head_rmsnorm_fm(zt, gain_col, h):
    zh = zt[h * HEAD_DIM:(h + 1) * HEAD_DIM, :]
    ms = jnp.sum(zh * zh, axis=0, keepdims=True) * (1.0 / HEAD_DIM)
    return (zh * lax.rsqrt(ms + EPS)) * gain_col


def _window_sums(full, delay):
    outs = []
    for gi, w in enumerate(POOL_WINDOWS):
        s = full[..., gi * POOL_GROUP:(gi + 1) * POOL_GROUP]
        sh = 1
        while sh < w:
            s = s + delay(s, sh)
            sh *= 2
        outs.append(s)
    return outs


def _delay_rows(a, n):
    return pltpu.roll(a, n, axis=0)


def _delay_leading(a, n):
    return jnp.concatenate([a[-n:], a[:-n]], axis=0)


def _merge_branches(attn_o, pool_mixed, gates, pw_ref, ps_ref, wabr_ref, wpbr_ref, pool_rows=None):
    ag, pg, ma, mp = gates
    po = []
    for gi in range(len(POOL_WINDOWS)):
        po.append(_dot(pool_mixed[gi].astype(BF16), pw_ref[gi * POOL_GROUP:(gi + 1) * POOL_GROUP, :]))
    po = jnp.concatenate(po, axis=1)
    if pool_rows is not None:
        po = pool_rows(po)
    po = po * ps_ref[...]
    p = _dot((po * (pg * jax.nn.sigmoid(pg))).astype(BF16), wpbr_ref[...])
    a = _dot((attn_o * (ag * jax.nn.sigmoid(ag))).astype(BF16), wabr_ref[...])
    acc = jax.nn.sigmoid(ma) * a + jax.nn.sigmoid(mp) * p
    return acc.astype(BF16)


def _diag2(a, axis_r, axis_c):
    z = jnp.zeros_like(a)
    return jnp.concatenate([jnp.concatenate([a, z], axis=axis_c),
                            jnp.concatenate([z, a], axis=axis_c)], axis=axis_r)


def _gather_bias(bucket, tbl_ref, h):
    acc = jnp.zeros(bucket.shape, F32)
    for b in range(N_BUCKETS):
        acc = jnp.where(bucket == b, tbl_ref[h, b], acc)
    return acc


def _t5_bucket(rel):
    assert (N_BUCKETS, MAX_DISTANCE) == (32, 128)
    nb = N_BUCKETS // 2
    max_exact = nb // 2
    n = np.abs(rel)
    log2_sq = np.vectorize(lambda v: max(int(v), 1).bit_length() - 1)(n * n)
    large = np.minimum(max_exact + log2_sq - 6, nb - 1)
    return (np.where(rel > 0, nb, 0) + np.where(n < max_exact, n, large)).astype(np.int32)


def _bucket_tables(n_b, n_s):
    bkp = _t5_bucket(np.arange(CHUNK_KEYS)[:, None] - WINDOW - np.arange(LANES)[None, :] % CHUNK)
    fr = np.arange(n_s)[:, None]
    bks = _t5_bucket(np.arange(WINDOW)[None, :] - WINDOW - fr)
    bkn = _t5_bucket(np.arange(n_b * n_s)[None, :] % n_s - fr)
    return bkp, bks, bkn


def _weight_jobs(pairs):
    jobs = []
    for src, dst in pairs:
        n_rows, n_cols = dst.shape
        for r0 in range(0, n_rows, STAGE_ROWS):
            for c0 in range(0, n_cols, STAGE_COLS):
                jobs.append((src, dst, r0, c0, min(STAGE_COLS, n_cols - c0)))
    return jobs


def _prompt_kernel(tbl_ref, sinks_ref, x_ref, xn_ref, ng_ref, gcol_ref, bkp_ref, ps_ref,
                   win_hbm, wabr_hbm, wpbr_hbm, wout_hbm, pw_hbm,
                   y_ref, pk_ref, pv_ref, pp_ref, winb_hbm, wabrb_hbm, wpbrb_hbm, woutb_hbm, pwb_hbm,
                   win_ref, wabr_ref, wpbr_ref, wout_ref, pw_ref, stage, sem_in, sem_out,
                   wt_s, h_s, qt_s, k_s, vt_s, attn_s, pbuf, hist_s, bias_s):
    b = pl.program_id(0)
    t = pl.program_id(1)
    first = (b == 0) & (t == 0)
    q_gain = gcol_ref[:, 0:1]
    k_gain = gcol_ref[:, 1:2]
    zq_s = stage.at[0]
    weights = ((win_hbm, win_ref), (wabr_hbm, wabr_ref), (wpbr_hbm, wpbr_ref), (wout_hbm, wout_ref),
               (pw_hbm, pw_ref))
    exports = (winb_hbm, wabrb_hbm, wpbrb_hbm, woutb_hbm, pwb_hbm)

    def export_copy(i):
        return pltpu.make_async_copy(weights[i][1], exports[i], sem_out.at[i])

    @pl.when(first)
    def _():
        jobs = _weight_jobs(weights)

        def load(i):
            src, _, r0, c0, w = jobs[i]
            return pltpu.make_async_copy(src.at[pl.ds(r0, STAGE_ROWS), pl.ds(c0, w)],
                                         stage.at[i % N_STAGE, :, pl.ds(0, w)], sem_in.at[i % N_STAGE])

        for i in range(min(N_STAGE - 1, len(jobs))):
            load(i).start()
        for i in range(len(jobs)):
            if i + N_STAGE - 1 < len(jobs):
                load(i + N_STAGE - 1).start()
            load(i).wait()
            _, dst, r0, c0, w = jobs[i]
            dst[r0:r0 + STAGE_ROWS, c0:c0 + w] = stage[i % N_STAGE, :, 0:w].astype(BF16)
            if i < N_HEADS // 2:
                bkp = bkp_ref[...]
                pair = jnp.where(lax.broadcasted_iota(jnp.int32, bkp.shape, 1) < CHUNK,
                                 _gather_bias(bkp, tbl_ref, 2 * i), _gather_bias(bkp, tbl_ref, 2 * i + 1))
                bias_s[2 * i // GROUP_REP, :, (2 * i % GROUP_REP) * CHUNK:(2 * i % GROUP_REP + 2) * CHUNK] = pair
            if jobs[i][1] is win_ref and c0 + w == C_V.stop:
                wt_s[...] = win_ref[:, 0:C_V.stop].T
                h_s[...] = _rmsnorm_rows(x_ref[...], ng_ref[...]).astype(BF16)
        assert len(jobs) >= N_HEADS // 2 and C_V.stop % STAGE_COLS == 0
        zq_s[...] = _dot_nt(wt_s[C_Q, :], h_s[...])
        for i in range(len(weights)):
            export_copy(i).start()

    @pl.when(t == 0)
    def _():
        k_s[:, 0:WINDOW, :] = jnp.zeros((2, WINDOW, LANES), BF16)
        vt_s[:, 0:WINDOW] = jnp.zeros((KV_WIDTH, WINDOW), BF16)
        pbuf[0:HIST_ROWS, :] = jnp.zeros((HIST_ROWS, POOL_WIDTH), F32)

    hb = h_s[...]

    n_c = D_MODEL // MXU_DIM
    fill_cols = [slice(c.start + i * MXU_DIM, c.start + (i + 1) * MXU_DIM)
                 for c in (C_PU, C_PG, C_AG, C_MA, C_MP) for i in range(n_c)]
    filled = []

    def fill(upto):
        for cols in fill_cols[len(filled):upto]:
            filled.append(_dot(hb, win_ref[:, cols]))

    zqt = zq_s[...]
    fill(2)
    zkt = _dot_nt(wt_s[C_K, :], hb)
    vt = _dot_nt(wt_s[C_V, :], hb)
    fill(PREFILL)
    kt = jnp.concatenate([_head_rmsnorm_fm(zkt, k_gain, g) for g in range(N_KV_HEADS)], axis=0)
    for h in range(N_HEADS):
        qt_s[h * HEAD_DIM:(h + 1) * HEAD_DIM, :] = _head_rmsnorm_fm(zqt, q_gain, h).astype(BF16)

    pk_ref[...] = kt[:, TM - WINDOW:]
    pv_ref[...] = vt[:, TM - WINDOW:]

    k_s[0, WINDOW:, :] = kt[:LANES, :].T.astype(BF16)
    k_s[1, WINDOW:, :] = kt[LANES:, :].T.astype(BF16)
    vt_s[:, WINDOW:] = vt.astype(BF16)

    zeros_half = jnp.zeros((HEAD_DIM, CHUNK), BF16)
    q_lane = lax.broadcasted_iota(jnp.int32, (1, GROUP_REP * CHUNK), 1)

    def scores(p, g):
        out = []
        for c in range(2):
            tok = slice(p * PAIR + c * CHUNK, p * PAIR + (c + 1) * CHUNK)
            keys = slice(p * PAIR + c * CHUNK, p * PAIR + c * CHUNK + CHUNK_KEYS)
            cols = []
            for r in range(GROUP_REP):
                h = g * GROUP_REP + r
                qh = qt_s[h * HEAD_DIM:(h + 1) * HEAD_DIM, tok]
                cols.append(jnp.concatenate([qh, zeros_half] if g % 2 == 0 else [zeros_half, qh], axis=0))
            rhs = jnp.concatenate(cols, axis=1)
            s = _dot(k_s[g // 2, keys, :], rhs) + bias_s[g]
            if p == 0:
                kk = lax.broadcasted_iota(jnp.int32, s.shape, 0)
                s = jnp.where((kk >= WINDOW - c * CHUNK) | (t > 0), s, NEG_INF)
            out.append(s)
        return out

    def finish(p, g, s_ab):
        tok = slice(p * PAIR, (p + 1) * PAIR)
        keys = slice(p * PAIR, p * PAIR + PAIR_KEYS)
        sink = sinks_ref[g * GROUP_REP]
        for r in range(1, GROUP_REP):
            sink = jnp.where(q_lane < r * CHUNK, sink, sinks_ref[g * GROUP_REP + r])
        pads = jnp.zeros((PAIR_KEYS - CHUNK_KEYS, GROUP_REP * CHUNK), BF16)
        e_ab, inv = [], []
        for c, s in enumerate(s_ab):
            m = jnp.maximum(jnp.max(s, axis=0, keepdims=True), sink)
            e = jnp.exp(s - m)
            inv.append(1.0 / (jnp.sum(e, axis=0, keepdims=True) + jnp.exp(sink - m)))
            e_ab.append(jnp.concatenate([e.astype(BF16), pads] if c == 0 else [pads, e.astype(BF16)], axis=0))
        vg = vt_s[g * HEAD_DIM:(g + 1) * HEAD_DIM, keys]
        o_t = _dot(vg, jnp.concatenate(e_ab, axis=1)) * jnp.concatenate(inv, axis=1)
        n_q = GROUP_REP * CHUNK
        for j in range(GROUP_REP // 2):
            rows = [jnp.concatenate([o_t[:, r * CHUNK:(r + 1) * CHUNK], o_t[:, n_q + r * CHUNK:n_q + (r + 1) * CHUNK]],
                                    axis=1) for r in (2 * j, 2 * j + 1)]
            lanes = slice(g * MXU_DIM + j * LANES, g * MXU_DIM + (j + 1) * LANES)
            attn_s[tok, lanes] = jnp.concatenate(rows, axis=0).T

    pu = jnp.concatenate(filled[:n_c], axis=1)
    pbuf[HIST_ROWS:, :] = pu
    hist_s[b] = pu[TM - HIST_ROWS:, :]
    sums = _window_sums(pbuf[...], _delay_rows)
    pos1 = (t * TM + 1 + lax.broadcasted_iota(jnp.int32, (TM, 1), 0)).astype(F32)
    mixed = []
    for gi, w in enumerate(POOL_WINDOWS):
        inv_cnt = 1.0 / jnp.minimum(float(w), pos1)
        mixed.append(sums[gi][HIST_ROWS:, :] * inv_cnt - pu[:, gi * POOL_GROUP:(gi + 1) * POOL_GROUP])

    blocks = [(p, g) for g in range(N_KV_HEADS) for p in range(TM // PAIR)]
    s_next = scores(*blocks[0])
    for i, blk in enumerate(blocks):
        s_cur = s_next
        if i + 1 < len(blocks):
            s_next = scores(*blocks[i + 1])
        fill(PREFILL + (len(fill_cols) - PREFILL) * (i + 1) // len(blocks))
        finish(*blk, s_cur)
    pg, ag, ma, mp = (jnp.concatenate(filled[j * n_c:(j + 1) * n_c], axis=1) for j in range(1, 5))

    merged = _merge_branches(attn_s[...], mixed, (ag, pg, ma, mp), pw_ref, ps_ref, wabr_ref, wpbr_ref)

    h_s[...] = _rmsnorm_rows(xn_ref[...], ng_ref[...]).astype(BF16)
    zq_s[...] = _dot_nt(wt_s[C_Q, :], h_s[...])

    y_ref[...] = x_ref[...] + _dot(merged, wout_ref[...])

    k_s[:, 0:WINDOW, :] = k_s[:, TM:TM + WINDOW, :]
    vt_s[:, 0:WINDOW] = vt_s[:, TM:TM + WINDOW]
    pbuf[0:HIST_ROWS, :] = pbuf[TM:TM + HIST_ROWS, :]

    @pl.when(first)
    def _():
        for i in range(len(weights)):
            export_copy(i).wait()

    @pl.when((b == pl.num_programs(0) - 1) & (t == pl.num_programs(1) - 1))
    def _():
        pp_ref[...] = jnp.swapaxes(hist_s[...], 0, 1)[HIST_ROWS - POOL_HIST:]


def _resident(shape):
    nd = len(shape)
    return pl.BlockSpec(shape, lambda b, t: (0,) * nd, pipeline_mode=pl.Buffered(1))


def _prompt_layer(x, tbl, sinks, ng, gain_cols, bkp, ps, weights):
    B, S, _ = x.shape
    assert S % TM == 0 and TM % PAIR == 0 and TM >= WINDOW
    assert all(w.shape[0] % STAGE_ROWS == 0 for w in weights)
    consts = (ng, gain_cols, bkp, ps)
    hbm = pl.BlockSpec(memory_space=pl.ANY)
    n_t = S // TM

    def next_tile(b, t):
        i = jnp.minimum(b * n_t + t + 1, B * n_t - 1)
        return i // n_t, i % n_t, 0

    return pl.pallas_call(
        _prompt_kernel,
        grid=(B, S // TM),
        in_specs=[pl.BlockSpec(memory_space=pltpu.SMEM),
                  pl.BlockSpec(memory_space=pltpu.SMEM),
                  pl.BlockSpec((None, TM, D_MODEL), lambda b, t: (b, t, 0)),
                  pl.BlockSpec((None, TM, D_MODEL), next_tile)]
                 + [_resident(c.shape) for c in consts] + [hbm] * len(weights),
        out_specs=[
            pl.BlockSpec((None, TM, D_MODEL), lambda b, t: (b, t, 0)),
            pl.BlockSpec((None, KV_WIDTH, WINDOW), lambda b, t: (b, 0, 0)),
            pl.BlockSpec((None, KV_WIDTH, WINDOW), lambda b, t: (b, 0, 0)),
            pl.BlockSpec((POOL_HIST, B, POOL_WIDTH), lambda b, t: (0, 0, 0)),
        ] + [hbm] * len(weights),
        out_shape=[
            jax.ShapeDtypeStruct((B, S, D_MODEL), F32),
            jax.ShapeDtypeStruct((B, KV_WIDTH, WINDOW), F32),
            jax.ShapeDtypeStruct((B, KV_WIDTH, WINDOW), F32),
            jax.ShapeDtypeStruct((POOL_HIST, B, POOL_WIDTH), F32),
        ] + [jax.ShapeDtypeStruct(w.shape, BF16) for w in weights],
        scratch_shapes=[pltpu.VMEM(w.shape, BF16) for w in weights] + [
            pltpu.VMEM((N_STAGE, STAGE_ROWS, STAGE_COLS), F32),
            pltpu.SemaphoreType.DMA((N_STAGE,)),
            pltpu.SemaphoreType.DMA((len(weights),)),
            pltpu.VMEM((C_V.stop, D_MODEL), BF16),
            pltpu.VMEM((TM, D_MODEL), BF16),
            pltpu.VMEM((ATTN_WIDTH, TM), BF16),
            pltpu.VMEM((2, WINDOW + TM, LANES), BF16),
            pltpu.VMEM((KV_WIDTH, WINDOW + TM), BF16),
            pltpu.VMEM((TM, ATTN_WIDTH), F32),
            pltpu.VMEM((HIST_ROWS + TM, POOL_WIDTH), F32),
            pltpu.VMEM((B, HIST_ROWS, POOL_WIDTH), F32),
            pltpu.VMEM((N_KV_HEADS, CHUNK_KEYS, GROUP_REP * CHUNK), F32),
        ],
        compiler_params=pltpu.CompilerParams(
            dimension_semantics=("arbitrary", "arbitrary"),
            vmem_limit_bytes=VMEM_LIMIT,
        ),
        name="prompt_layer",
    )(tbl, sinks, x, x, *consts, *weights)


def _sample_kernel(n_b, n_s, start,
                   tbl_ref, sinks_ref, x_ref, ng_ref,
                   gcol_ref, grow_ref, ones_ref, bks_ref, bkn_ref, ps_ref,
                   skt_hbm, svt_hbm, sp_hbm, win_hbm, pw_hbm, wabr_hbm, wpbr_hbm, wout_hbm,
                   y_ref, ok_hbm, ov_hbm, op_hbm,
                   skt_ref, svt_ref, sp_ref, ok_ref, ov_ref, op_ref,
                   win_ref, pw_ref, wabr_ref, wpbr_ref, wout_ref, sem):
    n_tok = n_b * n_s
    qkv = slice(0, C_V.stop)
    rest = slice(C_V.stop, win_ref.shape[1])
    moves = ((win_hbm.at[:, qkv], win_ref.at[:, qkv]), (skt_hbm, skt_ref), (svt_hbm, svt_ref),
             (win_hbm.at[:, rest], win_ref.at[:, rest]), (sp_hbm, sp_ref), (pw_hbm, pw_ref),
             (wabr_hbm, wabr_ref), (wpbr_hbm, wpbr_ref), (wout_hbm, wout_ref),
             (ok_ref, ok_hbm), (ov_ref, ov_hbm), (op_ref, op_hbm))
    assert len(moves) == N_SAMPLE_MOVES
    moves = [pltpu.make_async_copy(src, dst, sem.at[i]) for i, (src, dst) in enumerate(moves)]
    (qkv_in, k_in, v_in), late_in, (k_out, v_out, p_out) = moves[:3], moves[3:9], moves[9:]
    for c in moves[:9]:
        c.start()

    x = x_ref[...]
    hb = _rmsnorm_rows(x, ng_ref[...]).astype(BF16)

    qkv_in.wait()
    q = _head_rmsnorm(_dot(hb, win_ref[:, C_Q]), ones_ref, grow_ref[:, :ATTN_WIDTH]).astype(BF16)
    q3 = q.reshape(n_b, n_s, ATTN_WIDTH)
    zkt = _dot_nt(win_ref[:, C_K].T, hb)
    k_gain = gcol_ref[:, 1:2]
    knt = jnp.concatenate([_head_rmsnorm_fm(zkt, k_gain, g) for g in range(N_KV_HEADS)], axis=0)
    vnt = _dot_nt(win_ref[:, C_V].T, hb)

    k_in.wait()
    v_in.wait()
    lane = lax.broadcasted_iota(jnp.int32, (1, WINDOW), 1)
    per_tile = LANES // n_s
    for s in range(n_b):
        tile = slice((s // per_tile) * LANES, (s // per_tile + 1) * LANES)
        shift = (WINDOW - n_s - (s % per_tile) * n_s) % LANES
        for new, st_ref, o_ref in ((knt, skt_ref, ok_ref), (vnt, svt_ref, ov_ref)):
            fresh = new[:, tile] if shift == 0 else pltpu.roll(new[:, tile], shift, axis=1)
            o_ref[s] = jnp.where(lane < WINDOW - n_s, pltpu.roll(st_ref[s], WINDOW - n_s, axis=1), fresh)

    knb = knt.astype(BF16)
    vnb = vnt.astype(BF16)
    key_stream = (lax.broadcasted_iota(jnp.int32, (n_b, 1, 2 * n_tok), 2) & (n_tok - 1)) >> (n_s.bit_length() - 1)
    own = key_stream == lax.broadcasted_iota(jnp.int32, (n_b, 1, 2 * n_tok), 0)
    top = lax.broadcasted_iota(jnp.int32, (1, 2 * n_s, 1), 1) < n_s
    low = lax.broadcasted_iota(jnp.int32, (1, 1, LANES), 2) < HEAD_DIM

    def bias_rows(bucket, g):
        return jnp.concatenate([jnp.concatenate(
            [_gather_bias(bucket, tbl_ref, g * GROUP_REP + 2 * c + j) for j in range(2)], axis=1)
            for c in range(2)], axis=0)

    def scores(g):
        hd = slice(g * HEAD_DIM, (g + 1) * HEAD_DIM)
        c = g * MXU_DIM
        lhs = jnp.concatenate([q3[:, :, c:c + LANES], q3[:, :, c + LANES:c + MXU_DIM]], axis=1)
        kst = _diag2(skt_ref[:, hd, :].astype(BF16), 1, 2)
        s_st = jnp.stack([_dot(lhs[s], kst[s]) for s in range(n_b)]) + bias_rows(bks_ref[...], g)[None]
        s_nw = _dot(lhs.reshape(2 * n_tok, LANES), _diag2(knb[hd, :], 0, 1))
        s_nw = jnp.where(own, s_nw.reshape(n_b, 2 * n_s, 2 * n_tok) + bias_rows(bkn_ref[...], g)[None], NEG_INF)
        return s_st, s_nw

    def finish(g, s_st, s_nw):
        hd = slice(g * HEAD_DIM, (g + 1) * HEAD_DIM)
        p_st, p_nw, inv = [], [], []
        for j in range(2):
            a = s_st[:, :, j * WINDOW:(j + 1) * WINDOW]
            b = s_nw[:, :, j * n_tok:(j + 1) * n_tok]
            sink = jnp.where(top, sinks_ref[4 * g + j], sinks_ref[4 * g + 2 + j])
            m = jnp.maximum(jnp.maximum(jnp.max(a, axis=-1, keepdims=True),
                                        jnp.max(b, axis=-1, keepdims=True)), sink)
            ea = jnp.exp(a - m)
            eb = jnp.exp(b - m)
            inv.append(1.0 / (jnp.sum(ea, axis=-1, keepdims=True) + jnp.sum(eb, axis=-1, keepdims=True)
                              + jnp.exp(sink - m)))
            p_st.append(ea.astype(BF16))
            p_nw.append(eb.astype(BF16))
        p_st = jnp.concatenate(p_st, axis=2)
        p_nw = jnp.concatenate(p_nw, axis=2).reshape(2 * n_tok, 2 * n_tok)
        vst = _diag2(svt_ref[:, hd, :].astype(BF16), 1, 2)
        o = jnp.stack([_dot_nt(p_st[s], vst[s]) for s in range(n_b)])
        o = o + _dot_nt(p_nw, _diag2(vnb[hd, :], 0, 1)).reshape(n_b, 2 * n_s, LANES)
        o = o * jnp.where(low, inv[0], inv[1])
        return jnp.concatenate([o[:, :n_s, :], o[:, n_s:, :]], axis=2)

    outs = []
    nxt = scores(0)
    for g in range(N_KV_HEADS):
        cur = nxt
        if g + 1 < N_KV_HEADS:
            nxt = scores(g + 1)
        outs.append(finish(g, *cur))
    attn_o = jnp.concatenate(outs, axis=2).reshape(n_tok, ATTN_WIDTH)

    for c in late_in:
        c.wait()
    k_out.start()
    v_out.start()
    pu = _dot(hb, win_ref[:, C_PU])
    pu_t = jnp.swapaxes(pu.reshape(n_b, n_s, POOL_WIDTH), 0, 1)
    full = jnp.concatenate([jnp.zeros((1, n_b, POOL_WIDTH), F32), sp_ref[...], pu_t], axis=0)
    op_ref[...] = full[HIST_ROWS + n_s - POOL_HIST:]
    p_out.start()
    sums = _window_sums(full, _delay_leading)
    pos1 = (start + 1 + lax.broadcasted_iota(jnp.int32, (n_s, 1, 1), 0)).astype(F32)
    mixed = []
    for gi, w in enumerate(POOL_WINDOWS):
        inv_cnt = 1.0 / jnp.minimum(float(w), pos1)
        sw = sums[gi][HIST_ROWS:] * inv_cnt - pu_t[..., gi * POOL_GROUP:(gi + 1) * POOL_GROUP]
        mixed.append(sw.reshape(n_tok, POOL_GROUP))

    def stream_major(po):
        return jnp.swapaxes(po.reshape(n_s, n_b, POOL_WIDTH), 0, 1).reshape(n_tok, POOL_WIDTH)

    gates = tuple(_dot(hb, win_ref[:, c]) for c in (C_AG, C_PG, C_MA, C_MP))
    merged = _merge_branches(attn_o, mixed, gates, pw_ref, ps_ref, wabr_ref, wpbr_ref, stream_major)
    y_ref[...] = x + _dot(merged, wout_ref[...])
    for c in (k_out, v_out, p_out):
        c.wait()


def _sample_layer(x, skt, svt, sp, start, tbl, sinks, ng, w_in, gain_cols, gain_rows, ones, bks, bkn,
                  pw, ps, wabr, wpbr, wout):
    n_b, n_s, _ = x.shape
    n_tok = n_b * n_s
    assert n_s % 16 == 0 and LANES % n_s == 0 and n_s >= POOL_HIST and n_tok % LANES == 0
    assert n_s & (n_s - 1) == 0 and n_tok & (n_tok - 1) == 0
    vmem = pl.BlockSpec(memory_space=pltpu.VMEM)
    anywhere = pl.BlockSpec(memory_space=pl.ANY)
    weights = (w_in, pw, wabr, wpbr, wout)
    state = (skt, svt, sp)
    new_state = [jax.ShapeDtypeStruct(a.shape, a.dtype) for a in state]
    y, ok, ov, op = pl.pallas_call(
        functools.partial(_sample_kernel, n_b, n_s, start),
        in_specs=[pl.BlockSpec(memory_space=pltpu.SMEM)] * 2 + [vmem] * 8
                 + [anywhere] * (len(state) + len(weights)),
        out_specs=[vmem] + [anywhere] * len(state),
        out_shape=[jax.ShapeDtypeStruct((n_tok, D_MODEL), F32)] + new_state,
        scratch_shapes=[pltpu.VMEM(a.shape, a.dtype) for a in state + tuple(new_state) + weights]
                       + [pltpu.SemaphoreType.DMA((N_SAMPLE_MOVES,))],
        compiler_params=pltpu.CompilerParams(vmem_limit_bytes=VMEM_LIMIT),
        name="sample_layer",
    )(tbl, sinks, x.reshape(n_tok, D_MODEL), ng, gain_cols, gain_rows, ones, bks, bkn, ps,
      *state, *weights)
    return y.reshape(n_b, n_s, D_MODEL), ok, ov, op


def _kv_to_feature_major(a):
    return jnp.transpose(a, (0, 2, 3, 1)).reshape(a.shape[0], KV_WIDTH, WINDOW)


def _kv_from_feature_major(a):
    return jnp.transpose(a.reshape(a.shape[0], N_KV_HEADS, HEAD_DIM, WINDOW), (0, 3, 1, 2))


def kernel(x_prompt, x_sample, state_attn_k, state_attn_v, state_pool, norm_gain, w_in, q_norm_gain, k_norm_gain, attn_sinks, rel_bias, pool_w, pool_scale, w_attn_br, w_pool_br, w_out):
    depth = w_in.shape[0]
    n_b, n_s, _ = x_sample.shape
    past_len = 4096
    bkp, bks, bkn = _bucket_tables(n_b, n_s)
    tbl = rel_bias.T
    seg = np.arange(MXU_DIM) // HEAD_DIM
    ones = jnp.asarray(seg[:, None] == seg[None, :], BF16)

    xp, xs = x_prompt, x_sample
    pk, pv, pp, sk, sv, sp = [], [], [], [], [], []
    for l in range(depth):
        ng = norm_gain[l].reshape(1, D_MODEL)
        qg = q_norm_gain[l] * (HEAD_DIM ** -0.5)
        kg = k_norm_gain[l]
        gain_cols = jnp.stack([qg, kg], axis=1)
        gain_rows = jnp.concatenate([jnp.tile(qg, N_HEADS), jnp.tile(kg, N_KV_HEADS)]).reshape(1, -1)
        ps = pool_scale[l].reshape(1, POOL_WIDTH)
        xp, k_l, v_l, p_l, w_in_b, wabr_b, wpbr_b, wout_b, pw_b = _prompt_layer(
            xp, tbl, attn_sinks[l], ng, gain_cols, bkp, ps,
            (w_in[l], w_attn_br[l], w_pool_br[l], w_out[l], pool_w[l].reshape(POOL_WIDTH, POOL_GROUP)))
        tail = (pw_b, ps, wabr_b, wpbr_b, wout_b)
        pk.append(_kv_from_feature_major(k_l))
        pv.append(_kv_from_feature_major(v_l))
        pp.append(jnp.transpose(p_l, (1, 0, 2)))

        xs, k_l, v_l, p_l = _sample_layer(
            xs,
            _kv_to_feature_major(state_attn_k[l]),
            _kv_to_feature_major(state_attn_v[l]),
            jnp.transpose(state_pool[l], (1, 0, 2)),
            past_len, tbl, attn_sinks[l], ng, w_in_b, gain_cols, gain_rows, ones, bks, bkn, *tail)
        sk.append(_kv_from_feature_major(k_l))
        sv.append(_kv_from_feature_major(v_l))
        sp.append(jnp.transpose(p_l, (1, 0, 2)))
    return (xp, xs, jnp.stack(pk), jnp.stack(pv), jnp.stack(pp),
            jnp.stack(sk), jnp.stack(sv), jnp.stack(sp))
```

```python
import functools
import numpy as np

import jax
import jax.numpy as jnp
from jax import lax
from jax.experimental import pallas as pl
from jax.experimental.pallas import tpu as pltpu

D_MODEL = 1024
CHUNK = 64
WINDOW = 128
N_HEADS = 16
N_KV_HEADS = 4
GROUP_REP = N_HEADS // N_KV_HEADS
HEAD_DIM = 64
ATTN_WIDTH = N_HEADS * HEAD_DIM
KV_WIDTH = N_KV_HEADS * HEAD_DIM
POOL_WINDOWS = (2, 4, 8, 16)
POOL_WIDTH = D_MODEL
POOL_GROUP = POOL_WIDTH // len(POOL_WINDOWS)
POOL_HIST = max(POOL_WINDOWS) - 1
HIST_ROWS = POOL_HIST + 1
N_BUCKETS = 32
MAX_DISTANCE = 128
EPS = 1e-6
NEG_INF = -1e30

_SPLITS = (ATTN_WIDTH, KV_WIDTH, KV_WIDTH, ATTN_WIDTH, POOL_WIDTH, POOL_WIDTH, D_MODEL, D_MODEL)
_OFFS = tuple(sum(_SPLITS[:i]) for i in range(len(_SPLITS) + 1))
C_Q, C_K, C_V, C_AG, C_PU, C_PG, C_MA, C_MP = (slice(_OFFS[i], _OFFS[i + 1]) for i in range(8))

LANES = 128
MXU_DIM = 256
PAIR = 2 * CHUNK
PAIR_KEYS = PAIR + WINDOW
CHUNK_KEYS = CHUNK + WINDOW
TM = 512
PREFILL = 8
STAGE_ROWS, STAGE_COLS, N_STAGE = ATTN_WIDTH, TM, 3
N_SAMPLE_MOVES = 12
VMEM_LIMIT = 60 * 1024 * 1024

BF16 = jnp.bfloat16
F32 = jnp.float32


def _dot(a, b):
    return jnp.dot(a, b, preferred_element_type=F32)


def _dot_nt(a, b):
    return lax.dot_general(a, b, (((1,), (1,)), ((), ())), preferred_element_type=F32)


def _rmsnorm_rows(x, gain):
    y = x * lax.rsqrt(jnp.mean(x * x, axis=-1, keepdims=True) + EPS)
    return y * gain


def _head_rmsnorm(z, ones_ref, gain):
    sq = z * z
    hi = sq.astype(BF16)
    lo = (sq - hi.astype(F32)).astype(BF16)
    ones = ones_ref[...]
    parts = []
    for c in range(z.shape[1] // MXU_DIM):
        sl = slice(c * MXU_DIM, (c + 1) * MXU_DIM)
        parts.append(_dot(hi[:, sl], ones) + _dot(lo[:, sl], ones))
    ss = parts[0] if len(parts) == 1 else jnp.concatenate(parts, axis=1)
    return (z * lax.rsqrt(ss * (1.0 / HEAD_DIM) + EPS)) * gain


def _as_column(row):
    n = row.shape[1]
    diag = lax.broadcasted_iota(jnp.int32, (n, n), 0) == lax.broadcasted_iota(jnp.int32, (n, n), 1)
    return jnp.sum(jnp.where(diag, row, 0.0), axis=1, keepdims=True)


def _head_rmsnorm_fm(zt, gain_col, h):
    zh = zt[h * HEAD_DIM:(h + 1) * HEAD_DIM, :]
    ms = jnp.sum(zh * zh, axis=0, keepdims=True) * (1.0 / HEAD_DIM)
    return (zh * lax.rsqrt(ms + EPS)) * gain_col


def _window_sums(full, delay):
    outs = []
    for gi, w in enumerate(POOL_WINDOWS):
        s = full[..., gi * POOL_GROUP:(gi + 1) * POOL_GROUP]
        sh = 1
        while sh < w:
            s = s + delay(s, sh)
            sh *= 2
        outs.append(s)
    return outs


def _delay_rows(a, n):
    return pltpu.roll(a, n, axis=0)


def _delay_leading(a, n):
    return jnp.concatenate([a[-n:], a[:-n]], axis=0)


def _merge_branches(attn_o, pool_mixed, gates, pw_ref, ps_ref, wabr_ref, wpbr_ref, pool_rows=None):
    ag, pg, ma, mp = gates
    po = []
    for gi in range(len(POOL_WINDOWS)):
        po.append(_dot(pool_mixed[gi].astype(BF16), pw_ref[gi * POOL_GROUP:(gi + 1) * POOL_GROUP, :]))
    po = jnp.concatenate(po, axis=1)
    if pool_rows is not None:
        po = pool_rows(po)
    po = po * ps_ref[...]
    p = _dot((po * (pg * jax.nn.sigmoid(pg))).astype(BF16), wpbr_ref[...])
    a = _dot((attn_o * (ag * jax.nn.sigmoid(ag))).astype(BF16), wabr_ref[...])
    acc = jax.nn.sigmoid(ma) * a + jax.nn.sigmoid(mp) * p
    return acc.astype(BF16)


def _diag2(a, axis_r, axis_c):
    z = jnp.zeros_like(a)
    return jnp.concatenate([jnp.concatenate([a, z], axis=axis_c),
                            jnp.concatenate([z, a], axis=axis_c)], axis=axis_r)


def _gather_bias(bucket, tbl_ref, h):
    acc = jnp.zeros(bucket.shape, F32)
    for b in range(N_BUCKETS):
        acc = jnp.where(bucket == b, tbl_ref[h, b], acc)
    return acc


def _t5_bucket(rel):
    assert (N_BUCKETS, MAX_DISTANCE) == (32, 128)
    nb = N_BUCKETS // 2
    max_exact = nb // 2
    n = np.abs(rel)
    log2_sq = np.vectorize(lambda v: max(int(v), 1).bit_length() - 1)(n * n)
    large = np.minimum(max_exact + log2_sq - 6, nb - 1)
    return (np.where(rel > 0, nb, 0) + np.where(n < max_exact, n, large)).astype(np.int32)


def _bucket_tables(n_b, n_s):
    bkp = _t5_bucket(np.arange(CHUNK_KEYS)[:, None] - WINDOW - np.arange(LANES)[None, :] % CHUNK)
    fr = np.arange(n_s)[:, None]
    bks = _t5_bucket(np.arange(WINDOW)[None, :] - WINDOW - fr)
    bkn = _t5_bucket(np.arange(n_b * n_s)[None, :] % n_s - fr)
    return bkp, bks, bkn


def _weight_jobs(pairs):
    jobs = []
    for src, dst in pairs:
        n_rows, n_cols = dst.shape
        for r0 in range(0, n_rows, STAGE_ROWS):
            for c0 in range(0, n_cols, STAGE_COLS):
                jobs.append((src, dst, r0, c0, min(STAGE_COLS, n_cols - c0)))
    return jobs


def _prompt_kernel(tbl_ref, sinks_ref, x_ref, xn_ref, ng_ref, qg_ref, kg_ref, bkp_ref, ps_ref,
                   win_hbm, wabr_hbm, wpbr_hbm, wout_hbm, pw_hbm,
                   y_ref, pk_ref, pv_ref, pp_ref, winb_hbm, wabrb_hbm, wpbrb_hbm, woutb_hbm, pwb_hbm,
                   win_ref, wabr_ref, wpbr_ref, wout_ref, pw_ref, stage, sem_in, sem_out,
                   wt_s, h_s, qt_s, k_s, vt_s, attn_s, pbuf, hist_s, bias_s, gain_s):
    b = pl.program_id(0)
    t = pl.program_id(1)
    first = (b == 0) & (t == 0)
    zq_s = stage.at[0]
    weights = ((win_hbm, win_ref), (wabr_hbm, wabr_ref), (wpbr_hbm, wpbr_ref), (wout_hbm, wout_ref),
               (pw_hbm, pw_ref))
    exports = (winb_hbm, wabrb_hbm, wpbrb_hbm, woutb_hbm, pwb_hbm)

    def export_copy(i):
        return pltpu.make_async_copy(weights[i][1], exports[i], sem_out.at[i])

    @pl.when(first)
    def _():
        gain_s[:, 0:1] = _as_column(qg_ref[...] * (HEAD_DIM ** -0.5))
        gain_s[:, 1:2] = _as_column(kg_ref[...])
        jobs = _weight_jobs(weights)

        def load(i):
            src, _, r0, c0, w = jobs[i]
            return pltpu.make_async_copy(src.at[pl.ds(r0, STAGE_ROWS), pl.ds(c0, w)],
                                         stage.at[i % N_STAGE, :, pl.ds(0, w)], sem_in.at[i % N_STAGE])

        for i in range(min(N_STAGE - 1, len(jobs))):
            load(i).start()
        for i in range(len(jobs)):
            if i + N_STAGE - 1 < len(jobs):
                load(i + N_STAGE - 1).start()
            load(i).wait()
            _, dst, r0, c0, w = jobs[i]
            dst[r0:r0 + STAGE_ROWS, c0:c0 + w] = stage[i % N_STAGE, :, 0:w].astype(BF16)
            if i < N_HEADS // 2:
                bkp = bkp_ref[...]
                pair = jnp.where(lax.broadcasted_iota(jnp.int32, bkp.shape, 1) < CHUNK,
                                 _gather_bias(bkp, tbl_ref, 2 * i), _gather_bias(bkp, tbl_ref, 2 * i + 1))
                bias_s[2 * i // GROUP_REP, :, (2 * i % GROUP_REP) * CHUNK:(2 * i % GROUP_REP + 2) * CHUNK] = pair
            if jobs[i][1] is win_ref and c0 + w == C_V.stop:
                wt_s[...] = win_ref[:, 0:C_V.stop].T
                h_s[...] = _rmsnorm_rows(x_ref[...], ng_ref[...]).astype(BF16)
        assert len(jobs) >= N_HEADS // 2 and C_V.stop % STAGE_COLS == 0
        zq_s[...] = _dot_nt(wt_s[C_Q, :], h_s[...])
        for i in range(len(weights)):
            export_copy(i).start()

    @pl.when(t == 0)
    def _():
        k_s[:, 0:WINDOW, :] = jnp.zeros((2, WINDOW, LANES), BF16)
        vt_s[:, 0:WINDOW] = jnp.zeros((KV_WIDTH, WINDOW), BF16)
        pbuf[0:HIST_ROWS, :] = jnp.zeros((HIST_ROWS, POOL_WIDTH), F32)

    q_gain = gain_s[:, 0:1]
    k_gain = gain_s[:, 1:2]
    hb = h_s[...]

    n_c = D_MODEL // MXU_DIM
    fill_cols = [slice(c.start + i * MXU_DIM, c.start + (i + 1) * MXU_DIM)
                 for c in (C_PU, C_PG, C_AG, C_MA, C_MP) for i in range(n_c)]
    filled = []

    def fill(upto):
        for cols in fill_cols[len(filled):upto]:
            filled.append(_dot(hb, win_ref[:, cols]))

    zqt = zq_s[...]
    fill(2)
    zkt = _dot_nt(wt_s[C_K, :], hb)
    vt = _dot_nt(wt_s[C_V, :], hb)
    fill(PREFILL)
    kt = jnp.concatenate([_head_rmsnorm_fm(zkt, k_gain, g) for g in range(N_KV_HEADS)], axis=0)
    for h in range(N_HEADS):
        qt_s[h * HEAD_DIM:(h + 1) * HEAD_DIM, :] = _head_rmsnorm_fm(zqt, q_gain, h).astype(BF16)

    pk_ref[...] = kt[:, TM - WINDOW:]
    pv_ref[...] = vt[:, TM - WINDOW:]

    k_s[0, WINDOW:, :] = kt[:LANES, :].T.astype(BF16)
    k_s[1, WINDOW:, :] = kt[LANES:, :].T.astype(BF16)
    vt_s[:, WINDOW:] = vt.astype(BF16)

    zeros_half = jnp.zeros((HEAD_DIM, CHUNK), BF16)
    q_lane = lax.broadcasted_iota(jnp.int32, (1, GROUP_REP * CHUNK), 1)

    def scores(p, g):
        out = []
        for c in range(2):
            tok = slice(p * PAIR + c * CHUNK, p * PAIR + (c + 1) * CHUNK)
            keys = slice(p * PAIR + c * CHUNK, p * PAIR + c * CHUNK + CHUNK_KEYS)
            cols = []
            for r in range(GROUP_REP):
                h = g * GROUP_REP + r
                qh = qt_s[h * HEAD_DIM:(h + 1) * HEAD_DIM, tok]
                cols.append(jnp.concatenate([qh, zeros_half] if g % 2 == 0 else [zeros_half, qh], axis=0))
            rhs = jnp.concatenate(cols, axis=1)
            s = _dot(k_s[g // 2, keys, :], rhs) + bias_s[g]
            if p == 0:
                kk = lax.broadcasted_iota(jnp.int32, s.shape, 0)
                s = jnp.where((kk >= WINDOW - c * CHUNK) | (t > 0), s, NEG_INF)
            out.append(s)
        return out

    def finish(p, g, s_ab):
        tok = slice(p * PAIR, (p + 1) * PAIR)
        keys = slice(p * PAIR, p * PAIR + PAIR_KEYS)
        sink = sinks_ref[g * GROUP_REP]
        for r in range(1, GROUP_REP):
            sink = jnp.where(q_lane < r * CHUNK, sink, sinks_ref[g * GROUP_REP + r])
        pads = jnp.zeros((PAIR_KEYS - CHUNK_KEYS, GROUP_REP * CHUNK), BF16)
        e_ab, inv = [], []
        for c, s in enumerate(s_ab):
            m = jnp.maximum(jnp.max(s, axis=0, keepdims=True), sink)
            e = jnp.exp(s - m)
            inv.append(1.0 / (jnp.sum(e, axis=0, keepdims=True) + jnp.exp(sink - m)))
            e_ab.append(jnp.concatenate([e.astype(BF16), pads] if c == 0 else [pads, e.astype(BF16)], axis=0))
        vg = vt_s[g * HEAD_DIM:(g + 1) * HEAD_DIM, keys]
        o_t = _dot(vg, jnp.concatenate(e_ab, axis=1)) * jnp.concatenate(inv, axis=1)
        n_q = GROUP_REP * CHUNK
        for j in range(GROUP_REP // 2):
            rows = [jnp.concatenate([o_t[:, r * CHUNK:(r + 1) * CHUNK], o_t[:, n_q + r * CHUNK:n_q + (r + 1) * CHUNK]],
                                    axis=1) for r in (2 * j, 2 * j + 1)]
            lanes = slice(g * MXU_DIM + j * LANES, g * MXU_DIM + (j + 1) * LANES)
            attn_s[tok, lanes] = jnp.concatenate(rows, axis=0).T

    pu = jnp.concatenate(filled[:n_c], axis=1)
    pbuf[HIST_ROWS:, :] = pu
    hist_s[b] = pu[TM - HIST_ROWS:, :]
    sums = _window_sums(pbuf[...], _delay_rows)
    pos1 = (t * TM + 1 + lax.broadcasted_iota(jnp.int32, (TM, 1), 0)).astype(F32)
    mixed = []
    for gi, w in enumerate(POOL_WINDOWS):
        inv_cnt = 1.0 / jnp.minimum(float(w), pos1)
        mixed.append(sums[gi][HIST_ROWS:, :] * inv_cnt - pu[:, gi * POOL_GROUP:(gi + 1) * POOL_GROUP])

    blocks = [(p, g) for g in range(N_KV_HEADS) for p in range(TM // PAIR)]
    s_next = scores(*blocks[0])
    for i, blk in enumerate(blocks):
        s_cur = s_next
        if i + 1 < len(blocks):
            s_next = scores(*blocks[i + 1])
        fill(PREFILL + (len(fill_cols) - PREFILL) * (i + 1) // len(blocks))
        finish(*blk, s_cur)
    pg, ag, ma, mp = (jnp.concatenate(filled[j * n_c:(j + 1) * n_c], axis=1) for j in range(1, 5))

    merged = _merge_branches(attn_s[...], mixed, (ag, pg, ma, mp), pw_ref, ps_ref, wabr_ref, wpbr_ref)

    h_s[...] = _rmsnorm_rows(xn_ref[...], ng_ref[...]).astype(BF16)
    zq_s[...] = _dot_nt(wt_s[C_Q, :], h_s[...])

    y_ref[...] = x_ref[...] + _dot(merged, wout_ref[...])

    k_s[:, 0:WINDOW, :] = k_s[:, TM:TM + WINDOW, :]
    vt_s[:, 0:WINDOW] = vt_s[:, TM:TM + WINDOW]
    pbuf[0:HIST_ROWS, :] = pbuf[TM:TM + HIST_ROWS, :]

    @pl.when(first)
    def _():
        for i in range(len(weights)):
            export_copy(i).wait()

    @pl.when((b == pl.num_programs(0) - 1) & (t == pl.num_programs(1) - 1))
    def _():
        pp_ref[...] = jnp.swapaxes(hist_s[...], 0, 1)[HIST_ROWS - POOL_HIST:]


def _resident(shape):
    nd = len(shape)
    return pl.BlockSpec(shape, lambda b, t: (0,) * nd, pipeline_mode=pl.Buffered(1))


def _prompt_layer(x, tbl, sinks, ng, qg, kg, bkp, ps, weights):
    B, S, _ = x.shape
    assert S % TM == 0 and TM % PAIR == 0 and TM >= WINDOW
    assert all(w.shape[0] % STAGE_ROWS == 0 for w in weights)
    consts = (ng, qg, kg, bkp, ps)
    hbm = pl.BlockSpec(memory_space=pl.ANY)
    n_t = S // TM

    def next_tile(b, t):
        i = jnp.minimum(b * n_t + t + 1, B * n_t - 1)
        return i // n_t, i % n_t, 0

    return pl.pallas_call(
        _prompt_kernel,
        grid=(B, S // TM),
        in_specs=[pl.BlockSpec(memory_space=pltpu.SMEM),
                  pl.BlockSpec(memory_space=pltpu.SMEM),
                  pl.BlockSpec((None, TM, D_MODEL), lambda b, t: (b, t, 0)),
                  pl.BlockSpec((None, TM, D_MODEL), next_tile)]
                 + [_resident(c.shape) for c in consts] + [hbm] * len(weights),
        out_specs=[
            pl.BlockSpec((None, TM, D_MODEL), lambda b, t: (b, t, 0)),
            pl.BlockSpec((None, KV_WIDTH, WINDOW), lambda b, t: (b, 0, 0)),
            pl.BlockSpec((None, KV_WIDTH, WINDOW), lambda b, t: (b, 0, 0)),
            pl.BlockSpec((POOL_HIST, B, POOL_WIDTH), lambda b, t: (0, 0, 0)),
        ] + [hbm] * len(weights),
        out_shape=[
            jax.ShapeDtypeStruct((B, S, D_MODEL), F32),
            jax.ShapeDtypeStruct((B, KV_WIDTH, WINDOW), F32),
            jax.ShapeDtypeStruct((B, KV_WIDTH, WINDOW), F32),
            jax.ShapeDtypeStruct((POOL_HIST, B, POOL_WIDTH), F32),
        ] + [jax.ShapeDtypeStruct(w.shape, BF16) for w in weights],
        scratch_shapes=[pltpu.VMEM(w.shape, BF16) for w in weights] + [
            pltpu.VMEM((N_STAGE, STAGE_ROWS, STAGE_COLS), F32),
            pltpu.SemaphoreType.DMA((N_STAGE,)),
            pltpu.SemaphoreType.DMA((len(weights),)),
            pltpu.VMEM((C_V.stop, D_MODEL), BF16),
            pltpu.VMEM((TM, D_MODEL), BF16),
            pltpu.VMEM((ATTN_WIDTH, TM), BF16),
            pltpu.VMEM((2, WINDOW + TM, LANES), BF16),
            pltpu.VMEM((KV_WIDTH, WINDOW + TM), BF16),
            pltpu.VMEM((TM, ATTN_WIDTH), F32),
            pltpu.VMEM((HIST_ROWS + TM, POOL_WIDTH), F32),
            pltpu.VMEM((B, HIST_ROWS, POOL_WIDTH), F32),
            pltpu.VMEM((N_KV_HEADS, CHUNK_KEYS, GROUP_REP * CHUNK), F32),
            pltpu.VMEM((HEAD_DIM, 2), F32),
        ],
        compiler_params=pltpu.CompilerParams(
            dimension_semantics=("arbitrary", "arbitrary"),
            vmem_limit_bytes=VMEM_LIMIT,
        ),
        name="prompt_layer",
    )(tbl, sinks, x, x, *consts, *weights)


def _sample_kernel(n_b, n_s, start,
                   tbl_ref, sinks_ref, x_ref, ng_ref,
                   qg_ref, kg_ref, ones_ref, bks_ref, bkn_ref, ps_ref,
                   skt_hbm, svt_hbm, sp_hbm, win_hbm, pw_hbm, wabr_hbm, wpbr_hbm, wout_hbm,
                   y_ref, ok_hbm, ov_hbm, op_hbm,
                   skt_ref, svt_ref, sp_ref, ok_ref, ov_ref, op_ref,
                   win_ref, pw_ref, wabr_ref, wpbr_ref, wout_ref, sem):
    n_tok = n_b * n_s
    qkv = slice(0, C_V.stop)
    rest = slice(C_V.stop, win_ref.shape[1])
    moves = ((win_hbm.at[:, qkv], win_ref.at[:, qkv]), (skt_hbm, skt_ref), (svt_hbm, svt_ref),
             (win_hbm.at[:, rest], win_ref.at[:, rest]), (sp_hbm, sp_ref), (pw_hbm, pw_ref),
             (wabr_hbm, wabr_ref), (wpbr_hbm, wpbr_ref), (wout_hbm, wout_ref),
             (ok_ref, ok_hbm), (ov_ref, ov_hbm), (op_ref, op_hbm))
    assert len(moves) == N_SAMPLE_MOVES
    moves = [pltpu.make_async_copy(src, dst, sem.at[i]) for i, (src, dst) in enumerate(moves)]
    (qkv_in, k_in, v_in), late_in, (k_out, v_out, p_out) = moves[:3], moves[3:9], moves[9:]
    for c in moves[:9]:
        c.start()

    x = x_ref[...]
    hb = _rmsnorm_rows(x, ng_ref[...]).astype(BF16)

    qkv_in.wait()
    q_gain = jnp.concatenate([qg_ref[...] * (HEAD_DIM ** -0.5)] * N_HEADS, axis=1)
    q = _head_rmsnorm(_dot(hb, win_ref[:, C_Q]), ones_ref, q_gain).astype(BF16)
    q3 = q.reshape(n_b, n_s, ATTN_WIDTH)
    zkt = _dot_nt(win_ref[:, C_K].T, hb)
    k_gain = _as_column(kg_ref[...])
    knt = jnp.concatenate([_head_rmsnorm_fm(zkt, k_gain, g) for g in range(N_KV_HEADS)], axis=0)
    vnt = _dot_nt(win_ref[:, C_V].T, hb)

    k_in.wait()
    v_in.wait()
    lane = lax.broadcasted_iota(jnp.int32, (1, WINDOW), 1)
    per_tile = LANES // n_s
    for s in range(n_b):
        tile = slice((s // per_tile) * LANES, (s // per_tile + 1) * LANES)
        shift = (WINDOW - n_s - (s % per_tile) * n_s) % LANES
        for new, st_ref, o_ref in ((knt, skt_ref, ok_ref), (vnt, svt_ref, ov_ref)):
            fresh = new[:, tile] if shift == 0 else pltpu.roll(new[:, tile], shift, axis=1)
            o_ref[s] = jnp.where(lane < WINDOW - n_s, pltpu.roll(st_ref[s], WINDOW - n_s, axis=1), fresh)

    knb = knt.astype(BF16)
    vnb = vnt.astype(BF16)
    key_stream = (lax.broadcasted_iota(jnp.int32, (n_b, 1, 2 * n_tok), 2) & (n_tok - 1)) >> (n_s.bit_length() - 1)
    own = key_stream == lax.broadcasted_iota(jnp.int32, (n_b, 1, 2 * n_tok), 0)
    top = lax.broadcasted_iota(jnp.int32, (1, 2 * n_s, 1), 1) < n_s
    low = lax.broadcasted_iota(jnp.int32, (1, 1, LANES), 2) < HEAD_DIM

    def bias_rows(bucket, g):
        return jnp.concatenate([jnp.concatenate(
            [_gather_bias(bucket, tbl_ref, g * GROUP_REP + 2 * c + j) for j in range(2)], axis=1)
            for c in range(2)], axis=0)

    def scores(g):
        hd = slice(g * HEAD_DIM, (g + 1) * HEAD_DIM)
        c = g * MXU_DIM
        lhs = jnp.concatenate([q3[:, :, c:c + LANES], q3[:, :, c + LANES:c + MXU_DIM]], axis=1)
        kst = _diag2(skt_ref[:, hd, :].astype(BF16), 1, 2)
        s_st = jnp.stack([_dot(lhs[s], kst[s]) for s in range(n_b)]) + bias_rows(bks_ref[...], g)[None]
        s_nw = _dot(lhs.reshape(2 * n_tok, LANES), _diag2(knb[hd, :], 0, 1))
        s_nw = jnp.where(own, s_nw.reshape(n_b, 2 * n_s, 2 * n_tok) + bias_rows(bkn_ref[...], g)[None], NEG_INF)
        return s_st, s_nw

    def finish(g, s_st, s_nw):
        hd = slice(g * HEAD_DIM, (g + 1) * HEAD_DIM)
        p_st, p_nw, inv = [], [], []
        for j in range(2):
            a = s_st[:, :, j * WINDOW:(j + 1) * WINDOW]
            b = s_nw[:, :, j * n_tok:(j + 1) * n_tok]
            sink = jnp.where(top, sinks_ref[4 * g + j], sinks_ref[4 * g + 2 + j])
            m = jnp.maximum(jnp.maximum(jnp.max(a, axis=-1, keepdims=True),
                                        jnp.max(b, axis=-1, keepdims=True)), sink)
            ea = jnp.exp(a - m)
            eb = jnp.exp(b - m)
            inv.append(1.0 / (jnp.sum(ea, axis=-1, keepdims=True) + jnp.sum(eb, axis=-1, keepdims=True)
                              + jnp.exp(sink - m)))
            p_st.append(ea.astype(BF16))
            p_nw.append(eb.astype(BF16))
        p_st = jnp.concatenate(p_st, axis=2)
        p_nw = jnp.concatenate(p_nw, axis=2).reshape(2 * n_tok, 2 * n_tok)
        vst = _diag2(svt_ref[:, hd, :].astype(BF16), 1, 2)
        o = jnp.stack([_dot_nt(p_st[s], vst[s]) for s in range(n_b)])
        o = o + _dot_nt(p_nw, _diag2(vnb[hd, :], 0, 1)).reshape(n_b, 2 * n_s, LANES)
        o = o * jnp.where(low, inv[0], inv[1])
        return jnp.concatenate([o[:, :n_s, :], o[:, n_s:, :]], axis=2)

    outs = []
    nxt = scores(0)
    for g in range(N_KV_HEADS):
        cur = nxt
        if g + 1 < N_KV_HEADS:
            nxt = scores(g + 1)
        outs.append(finish(g, *cur))
    attn_o = jnp.concatenate(outs, axis=2).reshape(n_tok, ATTN_WIDTH)

    for c in late_in:
        c.wait()
    k_out.start()
    v_out.start()
    pu = _dot(hb, win_ref[:, C_PU])
    pu_t = jnp.swapaxes(pu.reshape(n_b, n_s, POOL_WIDTH), 0, 1)
    full = jnp.concatenate([jnp.zeros((1, n_b, POOL_WIDTH), F32), sp_ref[...], pu_t], axis=0)
    op_ref[...] = full[HIST_ROWS + n_s - POOL_HIST:]
    p_out.start()
    sums = _window_sums(full, _delay_leading)
    pos1 = (start + 1 + lax.broadcasted_iota(jnp.int32, (n_s, 1, 1), 0)).astype(F32)
    mixed = []
    for gi, w in enumerate(POOL_WINDOWS):
        inv_cnt = 1.0 / jnp.minimum(float(w), pos1)
        sw = sums[gi][HIST_ROWS:] * inv_cnt - pu_t[..., gi * POOL_GROUP:(gi + 1) * POOL_GROUP]
        mixed.append(sw.reshape(n_tok, POOL_GROUP))

    def stream_major(po):
        return jnp.swapaxes(po.reshape(n_s, n_b, POOL_WIDTH), 0, 1).reshape(n_tok, POOL_WIDTH)

    gates = tuple(_dot(hb, win_ref[:, c]) for c in (C_AG, C_PG, C_MA, C_MP))
    merged = _merge_branches(attn_o, mixed, gates, pw_ref, ps_ref, wabr_ref, wpbr_ref, stream_major)
    y_ref[...] = x + _dot(merged, wout_ref[...])
    for c in (k_out, v_out, p_out):
        c.wait()


def _sample_layer(x, skt, svt, sp, start, tbl, sinks, ng, w_in, qg, kg, ones, bks, bkn,
                  pw, ps, wabr, wpbr, wout):
    n_b, n_s, _ = x.shape
    n_tok = n_b * n_s
    assert n_s % 16 == 0 and LANES % n_s == 0 and n_s >= POOL_HIST and n_tok % LANES == 0
    assert n_s & (n_s - 1) == 0 and n_tok & (n_tok - 1) == 0
    vmem = pl.BlockSpec(memory_space=pltpu.VMEM)
    anywhere = pl.BlockSpec(memory_space=pl.ANY)
    weights = (w_in, pw, wabr, wpbr, wout)
    state = (skt, svt, sp)
    new_state = [jax.ShapeDtypeStruct(a.shape, a.dtype) for a in state]
    y, ok, ov, op = pl.pallas_call(
        functools.partial(_sample_kernel, n_b, n_s, start),
        in_specs=[pl.BlockSpec(memory_space=pltpu.SMEM)] * 2 + [vmem] * 8
                 + [anywhere] * (len(state) + len(weights)),
        out_specs=[vmem] + [anywhere] * len(state),
        out_shape=[jax.ShapeDtypeStruct((n_tok, D_MODEL), F32)] + new_state,
        scratch_shapes=[pltpu.VMEM(a.shape, a.dtype) for a in state + tuple(new_state) + weights]
                       + [pltpu.SemaphoreType.DMA((N_SAMPLE_MOVES,))],
        compiler_params=pltpu.CompilerParams(vmem_limit_bytes=VMEM_LIMIT),
        name="sample_layer",
    )(tbl, sinks, x.reshape(n_tok, D_MODEL), ng, qg, kg, ones, bks, bkn, ps,
      *state, *weights)
    return y.reshape(n_b, n_s, D_MODEL), ok, ov, op


def _kv_to_feature_major(a):
    return jnp.transpose(a, (0, 2, 3, 1)).reshape(a.shape[0], KV_WIDTH, WINDOW)


def _kv_from_feature_major(a):
    return jnp.transpose(a.reshape(a.shape[0], N_KV_HEADS, HEAD_DIM, WINDOW), (0, 3, 1, 2))


def kernel(x_prompt, x_sample, state_attn_k, state_attn_v, state_pool, norm_gain, w_in, q_norm_gain, k_norm_gain, attn_sinks, rel_bias, pool_w, pool_scale, w_attn_br, w_pool_br, w_out):
    depth = w_in.shape[0]
    n_b, n_s, _ = x_sample.shape
    past_len = 4096
    bkp, bks, bkn = _bucket_tables(n_b, n_s)
    tbl = rel_bias.T
    seg = np.arange(MXU_DIM) // HEAD_DIM
    ones = jnp.asarray(seg[:, None] == seg[None, :], BF16)

    xp, xs = x_prompt, x_sample
    pk, pv, pp, sk, sv, sp = [], [], [], [], [], []
    for l in range(depth):
        ng = norm_gain[l].reshape(1, D_MODEL)
        qg = q_norm_gain[l].reshape(1, HEAD_DIM)
        kg = k_norm_gain[l].reshape(1, HEAD_DIM)
        ps = pool_scale[l].reshape(1, POOL_WIDTH)
        xp, k_l, v_l, p_l, w_in_b, wabr_b, wpbr_b, wout_b, pw_b = _prompt_layer(
            xp, tbl, attn_sinks[l], ng, qg, kg, bkp, ps,
            (w_in[l], w_attn_br[l], w_pool_br[l], w_out[l], pool_w[l].reshape(POOL_WIDTH, POOL_GROUP)))
        tail = (pw_b, ps, wabr_b, wpbr_b, wout_b)
        pk.append(_kv_from_feature_major(k_l))
        pv.append(_kv_from_feature_major(v_l))
        pp.append(jnp.transpose(p_l, (1, 0, 2)))

        xs, k_l, v_l, p_l = _sample_layer(
            xs,
            _kv_to_feature_major(state_attn_k[l]),
            _kv_to_feature_major(state_attn_v[l]),
            jnp.transpose(state_pool[l], (1, 0, 2)),
            past_len, tbl, attn_sinks[l], ng, w_in_b, qg, kg, ones, bks, bkn, *tail)
        sk.append(_kv_from_feature_major(k_l))
        sv.append(_kv_from_feature_major(v_l))
        sp.append(jnp.transpose(p_l, (1, 0, 2)))
    return (xp, xs, jnp.stack(pk), jnp.stack(pv), jnp.stack(pp),
            jnp.stack(sk), jnp.stack(sv), jnp.stack(sp))
```

```python
import functools
import numpy as np

import jax
import jax.numpy as jnp
from jax import lax
from jax.experimental import pallas as pl
from jax.experimental.pallas import tpu as pltpu

D_MODEL = 1024
CHUNK = 64
WINDOW = 128
N_HEADS = 16
N_KV_HEADS = 4
GROUP_REP = N_HEADS // N_KV_HEADS
HEAD_DIM = 64
ATTN_WIDTH = N_HEADS * HEAD_DIM
KV_WIDTH = N_KV_HEADS * HEAD_DIM
POOL_WINDOWS = (2, 4, 8, 16)
POOL_WIDTH = D_MODEL
POOL_GROUP = POOL_WIDTH // len(POOL_WINDOWS)
POOL_HIST = max(POOL_WINDOWS) - 1
HIST_ROWS = POOL_HIST + 1
N_BUCKETS = 32
MAX_DISTANCE = 128
EPS = 1e-6
NEG_INF = -1e30

_SPLITS = (ATTN_WIDTH, KV_WIDTH, KV_WIDTH, ATTN_WIDTH, POOL_WIDTH, POOL_WIDTH, D_MODEL, D_MODEL)
_OFFS = tuple(sum(_SPLITS[:i]) for i in range(len(_SPLITS) + 1))
C_Q, C_K, C_V, C_AG, C_PU, C_PG, C_MA, C_MP = (slice(_OFFS[i], _OFFS[i + 1]) for i in range(8))

LANES = 128
MXU_DIM = 256
PAIR = 2 * CHUNK
PAIR_KEYS = PAIR + WINDOW
CHUNK_KEYS = CHUNK + WINDOW
TM = 512
PREFILL = 8
STAGE_ROWS, STAGE_COLS, N_STAGE = ATTN_WIDTH, TM, 3
N_SAMPLE_MOVES = 12
VMEM_LIMIT = 60 * 1024 * 1024

BF16 = jnp.bfloat16
F32 = jnp.float32


def _dot(a, b):
    return jnp.dot(a, b, preferred_element_type=F32)


def _dot_nt(a, b):
    return lax.dot_general(a, b, (((1,), (1,)), ((), ())), preferred_element_type=F32)


def _rmsnorm_rows(x, gain):
    y = x * lax.rsqrt(jnp.mean(x * x, axis=-1, keepdims=True) + EPS)
    return y * gain


def _head_rmsnorm(z, ones_ref, gain):
    sq = z * z
    hi = sq.astype(BF16)
    lo = (sq - hi.astype(F32)).astype(BF16)
    ones = ones_ref[...]
    parts = []
    for c in range(z.shape[1] // MXU_DIM):
        sl = slice(c * MXU_DIM, (c + 1) * MXU_DIM)
        parts.append(_dot(hi[:, sl], ones) + _dot(lo[:, sl], ones))
    ss = parts[0] if len(parts) == 1 else jnp.concatenate(parts, axis=1)
    return (z * lax.rsqrt(ss * (1.0 / HEAD_DIM) + EPS)) * gain


def _as_column(row):
    n = row.shape[1]
    diag = lax.broadcasted_iota(jnp.int32, (n, n), 0) == lax.broadcasted_iota(jnp.int32, (n, n), 1)
    return jnp.sum(jnp.where(diag, row, 0.0), axis=1, keepdims=True)


def _head_rmsnorm_fm(zt, gain_col, h):
    zh = zt[h * HEAD_DIM:(h + 1) * HEAD_DIM, :]
    ms = jnp.sum(zh * zh, axis=0, keepdims=True) * (1.0 / HEAD_DIM)
    return (zh * lax.rsqrt(ms + EPS)) * gain_col


def _window_sums(full, delay):
    outs = []
    for gi, w in enumerate(POOL_WINDOWS):
        s = full[..., gi * POOL_GROUP:(gi + 1) * POOL_GROUP]
        sh = 1
        while sh < w:
            s = s + delay(s, sh)
            sh *= 2
        outs.append(s)
    return outs


def _delay_rows(a, n):
    return pltpu.roll(a, n, axis=0)


def _delay_leading(a, n):
    return jnp.concatenate([a[-n:], a[:-n]], axis=0)


def _merge_branches(attn_o, pool_mixed, gates, pw_ref, ps_ref, wabr_ref, wpbr_ref, pool_rows=None):
    ag, pg, ma, mp = gates
    po = []
    for gi in range(len(POOL_WINDOWS)):
        po.append(_dot(pool_mixed[gi].astype(BF16), pw_ref[gi * POOL_GROUP:(gi + 1) * POOL_GROUP, :]))
    po = jnp.concatenate(po, axis=1)
    if pool_rows is not None:
        po = pool_rows(po)
    po = po * ps_ref[...]
    p = _dot((po * (pg * jax.nn.sigmoid(pg))).astype(BF16), wpbr_ref[...])
    a = _dot((attn_o * (ag * jax.nn.sigmoid(ag))).astype(BF16), wabr_ref[...])
    acc = jax.nn.sigmoid(ma) * a + jax.nn.sigmoid(mp) * p
    return acc.astype(BF16)


def _diag2(a, axis_r, axis_c):
    z = jnp.zeros_like(a)
    return jnp.concatenate([jnp.concatenate([a, z], axis=axis_c),
                            jnp.concatenate([z, a], axis=axis_c)], axis=axis_r)


def _gather_bias(bucket, tbl_ref, h):
    acc = jnp.zeros(bucket.shape, F32)
    for b in range(N_BUCKETS):
        acc = jnp.where(bucket == b, tbl_ref[h, b], acc)
    return acc


def _t5_bucket(rel):
    assert (N_BUCKETS, MAX_DISTANCE) == (32, 128)
    nb = N_BUCKETS // 2
    max_exact = nb // 2
    n = np.abs(rel)
    log2_sq = np.vectorize(lambda v: max(int(v), 1).bit_length() - 1)(n * n)
    large = np.minimum(max_exact + log2_sq - 6, nb - 1)
    return (np.where(rel > 0, nb, 0) + np.where(n < max_exact, n, large)).astype(np.int32)


def _bucket_tables(n_b, n_s):
    bkp = _t5_bucket(np.arange(CHUNK_KEYS)[:, None] - WINDOW - np.arange(LANES)[None, :] % CHUNK)
    fr = np.arange(n_s)[:, None]
    bks = _t5_bucket(np.arange(WINDOW)[None, :] - WINDOW - fr)
    bkn = _t5_bucket(np.arange(n_b * n_s)[None, :] % n_s - fr)
    return bkp, bks, bkn


def _weight_jobs(pairs):
    jobs = []
    for src, dst in pairs:
        n_rows, n_cols = dst.shape
        for r0 in range(0, n_rows, STAGE_ROWS):
            for c0 in range(0, n_cols, STAGE_COLS):
                jobs.append((src, dst, r0, c0, min(STAGE_COLS, n_cols - c0)))
    return jobs


def _prompt_kernel(tbl_ref, sinks_ref, x_ref, xn_ref, ng_ref, qg_ref, kg_ref, bkp_ref, ps_ref,
                   win_hbm, wabr_hbm, wpbr_hbm, wout_hbm, pw_hbm,
                   y_ref, pk_ref, pv_ref, pp_ref, winb_hbm, wabrb_hbm, wpbrb_hbm, woutb_hbm, pwb_hbm,
                   win_ref, wabr_ref, wpbr_ref, wout_ref, pw_ref, stage, sem_in, sem_out,
                   wt_s, h_s, qt_s, k_s, vt_s, attn_s, pbuf, hist_s, bias_s, gain_s):
    b = pl.program_id(0)
    t = pl.program_id(1)
    first = (b == 0) & (t == 0)
    zq_s = stage.at[0]
    weights = ((win_hbm, win_ref), (wabr_hbm, wabr_ref), (wpbr_hbm, wpbr_ref), (wout_hbm, wout_ref),
               (pw_hbm, pw_ref))
    exports = (winb_hbm, wabrb_hbm, wpbrb_hbm, woutb_hbm, pwb_hbm)

    def export_copy(i):
        return pltpu.make_async_copy(weights[i][1], exports[i], sem_out.at[i])

    @pl.when(first)
    def _():
        gain_s[:, 0:1] = _as_column(qg_ref[...] * (HEAD_DIM ** -0.5))
        gain_s[:, 1:2] = _as_column(kg_ref[...])
        jobs = _weight_jobs(weights)

        def load(i):
            src, _, r0, c0, w = jobs[i]
            return pltpu.make_async_copy(src.at[pl.ds(r0, STAGE_ROWS), pl.ds(c0, w)],
                                         stage.at[i % N_STAGE, :, pl.ds(0, w)], sem_in.at[i % N_STAGE])

        for i in range(min(N_STAGE - 1, len(jobs))):
            load(i).start()
        for i in range(len(jobs)):
            if i + N_STAGE - 1 < len(jobs):
                load(i + N_STAGE - 1).start()
            load(i).wait()
            _, dst, r0, c0, w = jobs[i]
            dst[r0:r0 + STAGE_ROWS, c0:c0 + w] = stage[i % N_STAGE, :, 0:w].astype(BF16)
            if i < N_HEADS // 2:
                bkp = bkp_ref[...]
                pair = jnp.where(lax.broadcasted_iota(jnp.int32, bkp.shape, 1) < CHUNK,
                                 _gather_bias(bkp, tbl_ref, 2 * i), _gather_bias(bkp, tbl_ref, 2 * i + 1))
                bias_s[2 * i // GROUP_REP, :, (2 * i % GROUP_REP) * CHUNK:(2 * i % GROUP_REP + 2) * CHUNK] = pair
            if jobs[i][1] is win_ref and c0 + w == C_V.stop:
                wt_s[...] = win_ref[:, 0:C_V.stop].T
                h_s[...] = _rmsnorm_rows(x_ref[...], ng_ref[...]).astype(BF16)
        assert len(jobs) >= N_HEADS // 2 and C_V.stop % STAGE_COLS == 0
        zq_s[...] = _dot_nt(wt_s[C_Q, :], h_s[...])
        for i in range(len(weights)):
            export_copy(i).start()

    @pl.when(t == 0)
    def _():
        k_s[:, 0:WINDOW, :] = jnp.zeros((2, WINDOW, LANES), BF16)
        vt_s[:, 0:WINDOW] = jnp.zeros((KV_WIDTH, WINDOW), BF16)
        pbuf[0:HIST_ROWS, :] = jnp.zeros((HIST_ROWS, POOL_WIDTH), F32)

    q_gain = gain_s[:, 0:1]
    k_gain = gain_s[:, 1:2]
    hb = h_s[...]

    n_c = D_MODEL // MXU_DIM
    fill_cols = [slice(c.start + i * MXU_DIM, c.start + (i + 1) * MXU_DIM)
                 for c in (C_PU, C_PG, C_AG, C_MA, C_MP) for i in range(n_c)]
    filled = []

    def fill(upto):
        for cols in fill_cols[len(filled):upto]:
            filled.append(_dot(hb, win_ref[:, cols]))

    zqt = zq_s[...]
    fill(2)
    zkt = _dot_nt(wt_s[C_K, :], hb)
    vt = _dot_nt(wt_s[C_V, :], hb)
    fill(PREFILL)
    kt = jnp.concatenate([_head_rmsnorm_fm(zkt, k_gain, g) for g in range(N_KV_HEADS)], axis=0)
    for h in range(N_HEADS):
        qt_s[h * HEAD_DIM:(h + 1) * HEAD_DIM, :] = _head_rmsnorm_fm(zqt, q_gain, h).astype(BF16)

    pk_ref[...] = kt[:, TM - WINDOW:]
    pv_ref[...] = vt[:, TM - WINDOW:]

    k_s[0, WINDOW:, :] = kt[:LANES, :].T.astype(BF16)
    k_s[1, WINDOW:, :] = kt[LANES:, :].T.astype(BF16)
    vt_s[:, WINDOW:] = vt.astype(BF16)

    zeros_half = jnp.zeros((HEAD_DIM, CHUNK), BF16)
    q_lane = lax.broadcasted_iota(jnp.int32, (1, GROUP_REP * CHUNK), 1)

    def scores(p, g):
        out = []
        for c in range(2):
            tok = slice(p * PAIR + c * CHUNK, p * PAIR + (c + 1) * CHUNK)
            keys = slice(p * PAIR + c * CHUNK, p * PAIR + c * CHUNK + CHUNK_KEYS)
            cols = []
            for r in range(GROUP_REP):
                h = g * GROUP_REP + r
                qh = qt_s[h * HEAD_DIM:(h + 1) * HEAD_DIM, tok]
                cols.append(jnp.concatenate([qh, zeros_half] if g % 2 == 0 else [zeros_half, qh], axis=0))
            rhs = jnp.concatenate(cols, axis=1)
            s = _dot(k_s[g // 2, keys, :], rhs) + bias_s[g]
            if p == 0:
                kk = lax.broadcasted_iota(jnp.int32, s.shape, 0)
                s = jnp.where((kk >= WINDOW - c * CHUNK) | (t > 0), s, NEG_INF)
            out.append(s)
        return out

    def finish(p, g, s_ab):
        tok = slice(p * PAIR, (p + 1) * PAIR)
        keys = slice(p * PAIR, p * PAIR + PAIR_KEYS)
        sink = sinks_ref[g * GROUP_REP]
        for r in range(1, GROUP_REP):
            sink = jnp.where(q_lane < r * CHUNK, sink, sinks_ref[g * GROUP_REP + r])
        pads = jnp.zeros((PAIR_KEYS - CHUNK_KEYS, GROUP_REP * CHUNK), BF16)
        e_ab, inv = [], []
        for c, s in enumerate(s_ab):
            m = jnp.maximum(jnp.max(s, axis=0, keepdims=True), sink)
            e = jnp.exp(s - m)
            inv.append(1.0 / (jnp.sum(e, axis=0, keepdims=True) + jnp.exp(sink - m)))
            e_ab.append(jnp.concatenate([e.astype(BF16), pads] if c == 0 else [pads, e.astype(BF16)], axis=0))
        vg = vt_s[g * HEAD_DIM:(g + 1) * HEAD_DIM, keys]
        o_t = _dot(vg, jnp.concatenate(e_ab, axis=1)) * jnp.concatenate(inv, axis=1)
        n_q = GROUP_REP * CHUNK
        for j in range(GROUP_REP // 2):
            rows = [jnp.concatenate([o_t[:, r * CHUNK:(r + 1) * CHUNK], o_t[:, n_q + r * CHUNK:n_q + (r + 1) * CHUNK]],
                                    axis=1) for r in (2 * j, 2 * j + 1)]
            lanes = slice(g * MXU_DIM + j * LANES, g * MXU_DIM + (j + 1) * LANES)
            attn_s[tok, lanes] = jnp.concatenate(rows, axis=0).T

    pu = jnp.concatenate(filled[:n_c], axis=1)
    pbuf[HIST_ROWS:, :] = pu
    hist_s[b] = pu[TM - HIST_ROWS:, :]
    sums = _window_sums(pbuf[...], _delay_rows)
    pos1 = (t * TM + 1 + lax.broadcasted_iota(jnp.int32, (TM, 1), 0)).astype(F32)
    mixed = []
    for gi, w in enumerate(POOL_WINDOWS):
        inv_cnt = 1.0 / jnp.minimum(float(w), pos1)
        mixed.append(sums[gi][HIST_ROWS:, :] * inv_cnt - pu[:, gi * POOL_GROUP:(gi + 1) * POOL_GROUP])

    blocks = [(p, g) for g in range(N_KV_HEADS) for p in range(TM // PAIR)]
    s_next = scores(*blocks[0])
    for i, blk in enumerate(blocks):
        s_cur = s_next
        if i + 1 < len(blocks):
            s_next = scores(*blocks[i + 1])
        fill(PREFILL + (len(fill_cols) - PREFILL) * (i + 1) // len(blocks))
        finish(*blk, s_cur)
    pg, ag, ma, mp = (jnp.concatenate(filled[j * n_c:(j + 1) * n_c], axis=1) for j in range(1, 5))

    merged = _merge_branches(attn_s[...], mixed, (ag, pg, ma, mp), pw_ref, ps_ref, wabr_ref, wpbr_ref)

    h_s[...] = _rmsnorm_rows(xn_ref[...], ng_ref[...]).astype(BF16)
    zq_s[...] = _dot_nt(wt_s[C_Q, :], h_s[...])

    y_ref[...] = x_ref[...] + _dot(merged, wout_ref[...])

    k_s[:, 0:WINDOW, :] = k_s[:, TM:TM + WINDOW, :]
    vt_s[:, 0:WINDOW] = vt_s[:, TM:TM + WINDOW]
    pbuf[0:HIST_ROWS, :] = pbuf[TM:TM + HIST_ROWS, :]

    @pl.when(first)
    def _():
        for i in range(len(weights)):
            export_copy(i).wait()

    @pl.when((b == pl.num_programs(0) - 1) & (t == pl.num_programs(1) - 1))
    def _():
        pp_ref[...] = jnp.swapaxes(hist_s[...], 0, 1)[HIST_ROWS - POOL_HIST:]


def _resident(shape):
    nd = len(shape)
    return pl.BlockSpec(shape, lambda b, t: (0,) * nd, pipeline_mode=pl.Buffered(1))


def _prompt_layer(x, tbl, sinks, ng, qg, kg, bkp, ps, weights):
    B, S, _ = x.shape
    assert S % TM == 0 and TM % PAIR == 0 and TM >= WINDOW
    assert all(w.shape[0] % STAGE_ROWS == 0 for w in weights)
    consts = (ng, qg, kg, bkp, ps)
    hbm = pl.BlockSpec(memory_space=pl.ANY)
    n_t = S // TM

    def next_tile(b, t):
        i = jnp.minimum(b * n_t + t + 1, B * n_t - 1)
        return i // n_t, i % n_t, 0

    return pl.pallas_call(
        _prompt_kernel,
        grid=(B, S // TM),
        in_specs=[pl.BlockSpec(memory_space=pltpu.SMEM),
                  pl.BlockSpec(memory_space=pltpu.SMEM),
                  pl.BlockSpec((None, TM, D_MODEL), lambda b, t: (b, t, 0)),
                  pl.BlockSpec((None, TM, D_MODEL), next_tile)]
                 + [_resident(c.shape) for c in consts] + [hbm] * len(weights),
        out_specs=[
            pl.BlockSpec((None, TM, D_MODEL), lambda b, t: (b, t, 0)),
            pl.BlockSpec((None, KV_WIDTH, WINDOW), lambda b, t: (b, 0, 0)),
            pl.BlockSpec((None, KV_WIDTH, WINDOW), lambda b, t: (b, 0, 0)),
            pl.BlockSpec((POOL_HIST, B, POOL_WIDTH), lambda b, t: (0, 0, 0)),
        ] + [hbm] * len(weights),
        out_shape=[
            jax.ShapeDtypeStruct((B, S, D_MODEL), F32),
            jax.ShapeDtypeStruct((B, KV_WIDTH, WINDOW), F32),
            jax.ShapeDtypeStruct((B, KV_WIDTH, WINDOW), F32),
            jax.ShapeDtypeStruct((POOL_HIST, B, POOL_WIDTH), F32),
        ] + [jax.ShapeDtypeStruct(w.shape, BF16) for w in weights],
        scratch_shapes=[pltpu.VMEM(w.shape, BF16) for w in weights] + [
            pltpu.VMEM((N_STAGE, STAGE_ROWS, STAGE_COLS), F32),
            pltpu.SemaphoreType.DMA((N_STAGE,)),
            pltpu.SemaphoreType.DMA((len(weights),)),
            pltpu.VMEM((C_V.stop, D_MODEL), BF16),
            pltpu.VMEM((TM, D_MODEL), BF16),
            pltpu.VMEM((ATTN_WIDTH, TM), BF16),
            pltpu.VMEM((2, WINDOW + TM, LANES), BF16),
            pltpu.VMEM((KV_WIDTH, WINDOW + TM), BF16),
            pltpu.VMEM((TM, ATTN_WIDTH), F32),
            pltpu.VMEM((HIST_ROWS + TM, POOL_WIDTH), F32),
            pltpu.VMEM((B, HIST_ROWS, POOL_WIDTH), F32),
            pltpu.VMEM((N_KV_HEADS, CHUNK_KEYS, GROUP_REP * CHUNK), F32),
            pltpu.VMEM((HEAD_DIM, 2), F32),
        ],
        compiler_params=pltpu.CompilerParams(
            dimension_semantics=("arbitrary", "arbitrary"),
            vmem_limit_bytes=VMEM_LIMIT,
        ),
        name="prompt_layer",
    )(tbl, sinks, x, x, *consts, *weights)


def _sample_kernel(n_b, n_s, start,
                   tbl_ref, sinks_ref, x_ref, ng_ref,
                   qg_ref, kg_ref, ones_ref, bks_ref, bkn_ref, ps_ref,
                   skt_hbm, svt_hbm, sp_hbm, win_hbm, pw_hbm, wabr_hbm, wpbr_hbm, wout_hbm,
                   y_ref, ok_hbm, ov_hbm, op_hbm,
                   skt_ref, svt_ref, sp_ref, ok_ref, ov_ref, op_ref,
                   win_ref, pw_ref, wabr_ref, wpbr_ref, wout_ref, sem):
    n_tok = n_b * n_s
    qkv = slice(0, C_V.stop)
    rest = slice(C_V.stop, win_ref.shape[1])
    moves = ((win_hbm.at[:, qkv], win_ref.at[:, qkv]), (skt_hbm, skt_ref), (svt_hbm, svt_ref),
             (win_hbm.at[:, rest], win_ref.at[:, rest]), (sp_hbm, sp_ref), (pw_hbm, pw_ref),
             (wabr_hbm, wabr_ref), (wpbr_hbm, wpbr_ref), (wout_hbm, wout_ref),
             (ok_ref, ok_hbm), (ov_ref, ov_hbm), (op_ref, op_hbm))
    assert len(moves) == N_SAMPLE_MOVES
    moves = [pltpu.make_async_copy(src, dst, sem.at[i]) for i, (src, dst) in enumerate(moves)]
    (qkv_in, k_in, v_in), late_in, (k_out, v_out, p_out) = moves[:3], moves[3:9], moves[9:]
    for c in (qkv_in, k_in, v_in):
        c.start(priority=0)
    for c in late_in:
        c.start(priority=1)

    x = x_ref[...]
    hb = _rmsnorm_rows(x, ng_ref[...]).astype(BF16)

    qkv_in.wait()
    q_gain = jnp.concatenate([qg_ref[...] * (HEAD_DIM ** -0.5)] * N_HEADS, axis=1)
    q = _head_rmsnorm(_dot(hb, win_ref[:, C_Q]), ones_ref, q_gain).astype(BF16)
    q3 = q.reshape(n_b, n_s, ATTN_WIDTH)
    zkt = _dot_nt(win_ref[:, C_K].T, hb)
    k_gain = _as_column(kg_ref[...])
    knt = jnp.concatenate([_head_rmsnorm_fm(zkt, k_gain, g) for g in range(N_KV_HEADS)], axis=0)
    vnt = _dot_nt(win_ref[:, C_V].T, hb)

    k_in.wait()
    v_in.wait()
    lane = lax.broadcasted_iota(jnp.int32, (1, WINDOW), 1)
    per_tile = LANES // n_s
    for s in range(n_b):
        tile = slice((s // per_tile) * LANES, (s // per_tile + 1) * LANES)
        shift = (WINDOW - n_s - (s % per_tile) * n_s) % LANES
        for new, st_ref, o_ref in ((knt, skt_ref, ok_ref), (vnt, svt_ref, ov_ref)):
            fresh = new[:, tile] if shift == 0 else pltpu.roll(new[:, tile], shift, axis=1)
            o_ref[s] = jnp.where(lane < WINDOW - n_s, pltpu.roll(st_ref[s], WINDOW - n_s, axis=1), fresh)

    knb = knt.astype(BF16)
    vnb = vnt.astype(BF16)
    key_stream = (lax.broadcasted_iota(jnp.int32, (n_b, 1, 2 * n_tok), 2) & (n_tok - 1)) >> (n_s.bit_length() - 1)
    own = key_stream == lax.broadcasted_iota(jnp.int32, (n_b, 1, 2 * n_tok), 0)
    top = lax.broadcasted_iota(jnp.int32, (1, 2 * n_s, 1), 1) < n_s
    low = lax.broadcasted_iota(jnp.int32, (1, 1, LANES), 2) < HEAD_DIM

    def bias_rows(bucket, g):
        return jnp.concatenate([jnp.concatenate(
            [_gather_bias(bucket, tbl_ref, g * GROUP_REP + 2 * c + j) for j in range(2)], axis=1)
            for c in range(2)], axis=0)

    def scores(g):
        hd = slice(g * HEAD_DIM, (g + 1) * HEAD_DIM)
        c = g * MXU_DIM
        lhs = jnp.concatenate([q3[:, :, c:c + LANES], q3[:, :, c + LANES:c + MXU_DIM]], axis=1)
        kst = _diag2(skt_ref[:, hd, :].astype(BF16), 1, 2)
        s_st = jnp.stack([_dot(lhs[s], kst[s]) for s in range(n_b)]) + bias_rows(bks_ref[...], g)[None]
        s_nw = _dot(lhs.reshape(2 * n_tok, LANES), _diag2(knb[hd, :], 0, 1))
        s_nw = jnp.where(own, s_nw.reshape(n_b, 2 * n_s, 2 * n_tok) + bias_rows(bkn_ref[...], g)[None], NEG_INF)
        return s_st, s_nw

    def finish(g, s_st, s_nw):
        hd = slice(g * HEAD_DIM, (g + 1) * HEAD_DIM)
        p_st, p_nw, inv = [], [], []
        for j in range(2):
            a = s_st[:, :, j * WINDOW:(j + 1) * WINDOW]
            b = s_nw[:, :, j * n_tok:(j + 1) * n_tok]
            sink = jnp.where(top, sinks_ref[4 * g + j], sinks_ref[4 * g + 2 + j])
            m = jnp.maximum(jnp.maximum(jnp.max(a, axis=-1, keepdims=True),
                                        jnp.max(b, axis=-1, keepdims=True)), sink)
            ea = jnp.exp(a - m)
            eb = jnp.exp(b - m)
            inv.append(1.0 / (jnp.sum(ea, axis=-1, keepdims=True) + jnp.sum(eb, axis=-1, keepdims=True)
                              + jnp.exp(sink - m)))
            p_st.append(ea.astype(BF16))
            p_nw.append(eb.astype(BF16))
        p_st = jnp.concatenate(p_st, axis=2)
        p_nw = jnp.concatenate(p_nw, axis=2).reshape(2 * n_tok, 2 * n_tok)
        vst = _diag2(svt_ref[:, hd, :].astype(BF16), 1, 2)
        o = jnp.stack([_dot_nt(p_st[s], vst[s]) for s in range(n_b)])
        o = o + _dot_nt(p_nw, _diag2(vnb[hd, :], 0, 1)).reshape(n_b, 2 * n_s, LANES)
        o = o * jnp.where(low, inv[0], inv[1])
        return jnp.concatenate([o[:, :n_s, :], o[:, n_s:, :]], axis=2)

    outs = []
    nxt = scores(0)
    for g in range(N_KV_HEADS):
        cur = nxt
        if g + 1 < N_KV_HEADS:
            nxt = scores(g + 1)
        outs.append(finish(g, *cur))
    attn_o = jnp.concatenate(outs, axis=2).reshape(n_tok, ATTN_WIDTH)

    for c in late_in:
        c.wait()
    k_out.start()
    v_out.start()
    pu = _dot(hb, win_ref[:, C_PU])
    pu_t = jnp.swapaxes(pu.reshape(n_b, n_s, POOL_WIDTH), 0, 1)
    full = jnp.concatenate([jnp.zeros((1, n_b, POOL_WIDTH), F32), sp_ref[...], pu_t], axis=0)
    op_ref[...] = full[HIST_ROWS + n_s - POOL_HIST:]
    p_out.start()
    sums = _window_sums(full, _delay_leading)
    pos1 = (start + 1 + lax.broadcasted_iota(jnp.int32, (n_s, 1, 1), 0)).astype(F32)
    mixed = []
    for gi, w in enumerate(POOL_WINDOWS):
        inv_cnt = 1.0 / jnp.minimum(float(w), pos1)
        sw = sums[gi][HIST_ROWS:] * inv_cnt - pu_t[..., gi * POOL_GROUP:(gi + 1) * POOL_GROUP]
        mixed.append(sw.reshape(n_tok, POOL_GROUP))

    def stream_major(po):
        return jnp.swapaxes(po.reshape(n_s, n_b, POOL_WIDTH), 0, 1).reshape(n_tok, POOL_WIDTH)

    gates = tuple(_dot(hb, win_ref[:, c]) for c in (C_AG, C_PG, C_MA, C_MP))
    merged = _merge_branches(attn_o, mixed, gates, pw_ref, ps_ref, wabr_ref, wpbr_ref, stream_major)
    y_ref[...] = x + _dot(merged, wout_ref[...])
    for c in (k_out, v_out, p_out):
        c.wait()


def _sample_layer(x, skt, svt, sp, start, tbl, sinks, ng, w_in, qg, kg, ones, bks, bkn,
                  pw, ps, wabr, wpbr, wout):
    n_b, n_s, _ = x.shape
    n_tok = n_b * n_s
    assert n_s % 16 == 0 and LANES % n_s == 0 and n_s >= POOL_HIST and n_tok % LANES == 0
    assert n_s & (n_s - 1) == 0 and n_tok & (n_tok - 1) == 0
    vmem = pl.BlockSpec(memory_space=pltpu.VMEM)
    anywhere = pl.BlockSpec(memory_space=pl.ANY)
    weights = (w_in, pw, wabr, wpbr, wout)
    state = (skt, svt, sp)
    new_state = [jax.ShapeDtypeStruct(a.shape, a.dtype) for a in state]
    y, ok, ov, op = pl.pallas_call(
        functools.partial(_sample_kernel, n_b, n_s, start),
        in_specs=[pl.BlockSpec(memory_space=pltpu.SMEM)] * 2 + [vmem] * 8
                 + [anywhere] * (len(state) + len(weights)),
        out_specs=[vmem] + [anywhere] * len(state),
        out_shape=[jax.ShapeDtypeStruct((n_tok, D_MODEL), F32)] + new_state,
        scratch_shapes=[pltpu.VMEM(a.shape, a.dtype) for a in state + tuple(new_state) + weights]
                       + [pltpu.SemaphoreType.DMA((N_SAMPLE_MOVES,))],
        compiler_params=pltpu.CompilerParams(vmem_limit_bytes=VMEM_LIMIT),
        name="sample_layer",
    )(tbl, sinks, x.reshape(n_tok, D_MODEL), ng, qg, kg, ones, bks, bkn, ps,
      *state, *weights)
    return y.reshape(n_b, n_s, D_MODEL), ok, ov, op


def _kv_to_feature_major(a):
    return jnp.transpose(a, (0, 2, 3, 1)).reshape(a.shape[0], KV_WIDTH, WINDOW)


def _kv_from_feature_major(a):
    return jnp.transpose(a.reshape(a.shape[0], N_KV_HEADS, HEAD_DIM, WINDOW), (0, 3, 1, 2))


def kernel(x_prompt, x_sample, state_attn_k, state_attn_v, state_pool, norm_gain, w_in, q_norm_gain, k_norm_gain, attn_sinks, rel_bias, pool_w, pool_scale, w_attn_br, w_pool_br, w_out):
    depth = w_in.shape[0]
    n_b, n_s, _ = x_sample.shape
    past_len = 4096
    bkp, bks, bkn = _bucket_tables(n_b, n_s)
    tbl = rel_bias.T
    seg = np.arange(MXU_DIM) // HEAD_DIM
    ones = jnp.asarray(seg[:, None] == seg[None, :], BF16)

    xp, xs = x_prompt, x_sample
    pk, pv, pp, sk, sv, sp = [], [], [], [], [], []
    for l in range(depth):
        ng = norm_gain[l].reshape(1, D_MODEL)
        qg = q_norm_gain[l].reshape(1, HEAD_DIM)
        kg = k_norm_gain[l].reshape(1, HEAD_DIM)
        ps = pool_scale[l].reshape(1, POOL_WIDTH)
        xp, k_l, v_l, p_l, w_in_b, wabr_b, wpbr_b, wout_b, pw_b = _prompt_layer(
            xp, tbl, attn_sinks[l], ng, qg, kg, bkp, ps,
            (w_in[l], w_attn_br[l], w_pool_br[l], w_out[l], pool_w[l].reshape(POOL_WIDTH, POOL_GROUP)))
        tail = (pw_b, ps, wabr_b, wpbr_b, wout_b)
        pk.append(_kv_from_feature_major(k_l))
        pv.append(_kv_from_feature_major(v_l))
        pp.append(jnp.transpose(p_l, (1, 0, 2)))

        xs, k_l, v_l, p_l = _sample_layer(
            xs,
            _kv_to_feature_major(state_attn_k[l]),
            _kv_to_feature_major(state_attn_v[l]),
            jnp.transpose(state_pool[l], (1, 0, 2)),
            past_len, tbl, attn_sinks[l], ng, w_in_b, qg, kg, ones, bks, bkn, *tail)
        sk.append(_kv_from_feature_major(k_l))
        sv.append(_kv_from_feature_major(v_l))
        sp.append(jnp.transpose(p_l, (1, 0, 2)))
    return (xp, xs, jnp.stack(pk), jnp.stack(pv), jnp.stack(pp),
            jnp.stack(sk), jnp.stack(sv), jnp.stack(sp))
```

```python
import functools
import numpy as np

import jax
import jax.numpy as jnp
from jax import lax
from jax.experimental import pallas as pl
from jax.experimental.pallas import tpu as pltpu

D_MODEL = 1024
CHUNK = 64
WINDOW = 128
N_HEADS = 16
N_KV_HEADS = 4
GROUP_REP = N_HEADS // N_KV_HEADS
HEAD_DIM = 64
ATTN_WIDTH = N_HEADS * HEAD_DIM
KV_WIDTH = N_KV_HEADS * HEAD_DIM
POOL_WINDOWS = (2, 4, 8, 16)
POOL_WIDTH = D_MODEL
POOL_GROUP = POOL_WIDTH // len(POOL_WINDOWS)
POOL_HIST = max(POOL_WINDOWS) - 1
HIST_ROWS = POOL_HIST + 1
N_BUCKETS = 32
MAX_DISTANCE = 128
EPS = 1e-6
NEG_INF = -1e30

_SPLITS = (ATTN_WIDTH, KV_WIDTH, KV_WIDTH, ATTN_WIDTH, POOL_WIDTH, POOL_WIDTH, D_MODEL, D_MODEL)
_OFFS = tuple(sum(_SPLITS[:i]) for i in range(len(_SPLITS) + 1))
C_Q, C_K, C_V, C_AG, C_PU, C_PG, C_MA, C_MP = (slice(_OFFS[i], _OFFS[i + 1]) for i in range(8))

LANES = 128
MXU_DIM = 256
PAIR = 2 * CHUNK
PAIR_KEYS = PAIR + WINDOW
CHUNK_KEYS = CHUNK + WINDOW
TM = 512
PREFILL = 8
STAGE_ROWS, STAGE_COLS, N_STAGE = ATTN_WIDTH, TM, 3
N_SAMPLE_MOVES = 12
VMEM_LIMIT = 60 * 1024 * 1024

BF16 = jnp.bfloat16
F32 = jnp.float32


def _dot(a, b):
    return jnp.dot(a, b, preferred_element_type=F32)


def _dot_nt(a, b):
    return lax.dot_general(a, b, (((1,), (1,)), ((), ())), preferred_element_type=F32)


def _rmsnorm_rows(x, gain):
    y = x * lax.rsqrt(jnp.mean(x * x, axis=-1, keepdims=True) + EPS)
    return y * gain


def _head_rmsnorm(z, ones_ref, gain):
    sq = z * z
    hi = sq.astype(BF16)
    lo = (sq - hi.astype(F32)).astype(BF16)
    ones = ones_ref[...]
    parts = []
    for c in range(z.shape[1] // MXU_DIM):
        sl = slice(c * MXU_DIM, (c + 1) * MXU_DIM)
        parts.append(_dot(hi[:, sl], ones) + _dot(lo[:, sl], ones))
    ss = parts[0] if len(parts) == 1 else jnp.concatenate(parts, axis=1)
    return (z * lax.rsqrt(ss * (1.0 / HEAD_DIM) + EPS)) * gain


def _as_column(row):
    n = row.shape[1]
    diag = lax.broadcasted_iota(jnp.int32, (n, n), 0) == lax.broadcasted_iota(jnp.int32, (n, n), 1)
    return jnp.sum(jnp.where(diag, row, 0.0), axis=1, keepdims=True)


def _head_rmsnorm_fm(zt, gain_col, h):
    zh = zt[h * HEAD_DIM:(h + 1) * HEAD_DIM, :]
    ms = jnp.sum(zh * zh, axis=0, keepdims=True) * (1.0 / HEAD_DIM)
    return (zh * lax.rsqrt(ms + EPS)) * gain_col


def _window_sums(full, delay):
    outs = []
    for gi, w in enumerate(POOL_WINDOWS):
        s = full[..., gi * POOL_GROUP:(gi + 1) * POOL_GROUP]
        sh = 1
        while sh < w:
            s = s + delay(s, sh)
            sh *= 2
        outs.append(s)
    return outs


def _delay_rows(a, n):
    return pltpu.roll(a, n, axis=0)


def _delay_leading(a, n):
    return jnp.concatenate([a[-n:], a[:-n]], axis=0)


def _merge_branches(attn_o, pool_mixed, gates, pw_ref, ps_ref, wabr_ref, wpbr_ref, pool_rows=None):
    ag, pg, ma, mp = gates
    po = []
    for gi in range(len(POOL_WINDOWS)):
        po.append(_dot(pool_mixed[gi].astype(BF16), pw_ref[gi * POOL_GROUP:(gi + 1) * POOL_GROUP, :]))
    po = jnp.concatenate(po, axis=1)
    if pool_rows is not None:
        po = pool_rows(po)
    po = po * ps_ref[...]
    p = _dot((po * (pg * jax.nn.sigmoid(pg))).astype(BF16), wpbr_ref[...])
    a = _dot((attn_o * (ag * jax.nn.sigmoid(ag))).astype(BF16), wabr_ref[...])
    acc = jax.nn.sigmoid(ma) * a + jax.nn.sigmoid(mp) * p
    return acc.astype(BF16)


def _diag2(a, axis_r, axis_c):
    z = jnp.zeros_like(a)
    return jnp.concatenate([jnp.concatenate([a, z], axis=axis_c),
                            jnp.concatenate([z, a], axis=axis_c)], axis=axis_r)


def _gather_bias(bucket, tbl_ref, h):
    acc = jnp.zeros(bucket.shape, F32)
    for b in range(N_BUCKETS):
        acc = jnp.where(bucket == b, tbl_ref[h, b], acc)
    return acc


def _t5_bucket(rel):
    assert (N_BUCKETS, MAX_DISTANCE) == (32, 128)
    nb = N_BUCKETS // 2
    max_exact = nb // 2
    n = np.abs(rel)
    log2_sq = np.vectorize(lambda v: max(int(v), 1).bit_length() - 1)(n * n)
    large = np.minimum(max_exact + log2_sq - 6, nb - 1)
    return (np.where(rel > 0, nb, 0) + np.where(n < max_exact, n, large)).astype(np.int32)


def _bucket_tables(n_b, n_s):
    bkp = _t5_bucket(np.arange(CHUNK_KEYS)[:, None] - WINDOW - np.arange(LANES)[None, :] % CHUNK)
    fr = np.arange(n_s)[:, None]
    bks = _t5_bucket(np.arange(WINDOW)[None, :] - WINDOW - fr)
    bkn = _t5_bucket(np.arange(n_b * n_s)[None, :] % n_s - fr)
    return bkp, bks, bkn


def _weight_jobs(pairs):
    jobs = []
    for src, dst in pairs:
        n_rows, n_cols = dst.shape
        for r0 in range(0, n_rows, STAGE_ROWS):
            for c0 in range(0, n_cols, STAGE_COLS):
                jobs.append((src, dst, r0, c0, min(STAGE_COLS, n_cols - c0)))
    return jobs


def _prompt_kernel(tbl_ref, sinks_ref, x_ref, xn_ref, ng_ref, qg_ref, kg_ref, bkp_ref, ps_ref,
                   win_hbm, wabr_hbm, wpbr_hbm, wout_hbm, pw_hbm,
                   y_ref, pk_ref, pv_ref, pp_ref, winb_hbm, wabrb_hbm, wpbrb_hbm, woutb_hbm, pwb_hbm,
                   win_ref, wabr_ref, wpbr_ref, wout_ref, pw_ref, stage, sem_in, sem_out,
                   wt_s, h_s, qt_s, k_s, vt_s, attn_s, pbuf, hist_s, bias_s, gain_s):
    b = pl.program_id(0)
    t = pl.program_id(1)
    first = (b == 0) & (t == 0)
    zq_s = stage.at[0]
    weights = ((win_hbm, win_ref), (wabr_hbm, wabr_ref), (wpbr_hbm, wpbr_ref), (wout_hbm, wout_ref),
               (pw_hbm, pw_ref))
    exports = (winb_hbm, wabrb_hbm, wpbrb_hbm, woutb_hbm, pwb_hbm)

    def export_copy(i):
        return pltpu.make_async_copy(weights[i][1], exports[i], sem_out.at[i])

    @pl.when(first)
    def _():
        gain_s[:, 0:1] = _as_column(qg_ref[...] * (HEAD_DIM ** -0.5))
        gain_s[:, 1:2] = _as_column(kg_ref[...])
        jobs = _weight_jobs(weights)

        def load(i):
            src, _, r0, c0, w = jobs[i]
            return pltpu.make_async_copy(src.at[pl.ds(r0, STAGE_ROWS), pl.ds(c0, w)],
                                         stage.at[i % N_STAGE, :, pl.ds(0, w)], sem_in.at[i % N_STAGE])

        for i in range(min(N_STAGE - 1, len(jobs))):
            load(i).start()
        for i in range(len(jobs)):
            if i + N_STAGE - 1 < len(jobs):
                load(i + N_STAGE - 1).start()
            load(i).wait()
            _, dst, r0, c0, w = jobs[i]
            dst[r0:r0 + STAGE_ROWS, c0:c0 + w] = stage[i % N_STAGE, :, 0:w].astype(BF16)
            if i < N_HEADS // 2:
                bkp = bkp_ref[...]
                pair = jnp.where(lax.broadcasted_iota(jnp.int32, bkp.shape, 1) < CHUNK,
                                 _gather_bias(bkp, tbl_ref, 2 * i), _gather_bias(bkp, tbl_ref, 2 * i + 1))
                bias_s[2 * i // GROUP_REP, :, (2 * i % GROUP_REP) * CHUNK:(2 * i % GROUP_REP + 2) * CHUNK] = pair
            if jobs[i][1] is win_ref and c0 + w == C_V.stop:
                wt_s[...] = win_ref[:, 0:C_V.stop].T
                h_s[...] = _rmsnorm_rows(x_ref[...], ng_ref[...]).astype(BF16)
        assert len(jobs) >= N_HEADS // 2 and C_V.stop % STAGE_COLS == 0
        zq_s[...] = _dot_nt(wt_s[C_Q, :], h_s[...])
        for i in range(len(weights)):
            export_copy(i).start()

    @pl.when(t == 0)
    def _():
        k_s[:, 0:WINDOW, :] = jnp.zeros((2, WINDOW, LANES), BF16)
        vt_s[:, 0:WINDOW] = jnp.zeros((KV_WIDTH, WINDOW), BF16)
        pbuf[0:HIST_ROWS, :] = jnp.zeros((HIST_ROWS, POOL_WIDTH), F32)

    q_gain = gain_s[:, 0:1]
    k_gain = gain_s[:, 1:2]
    hb = h_s[...]

    n_c = D_MODEL // MXU_DIM
    fill_cols = [slice(c.start + i * MXU_DIM, c.start + (i + 1) * MXU_DIM)
                 for c in (C_PU, C_PG, C_AG, C_MA, C_MP) for i in range(n_c)]
    filled = []

    def fill(upto):
        for cols in fill_cols[len(filled):upto]:
            filled.append(_dot(hb, win_ref[:, cols]))

    zqt = zq_s[...]
    fill(2)
    zkt = _dot_nt(wt_s[C_K, :], hb)
    vt = _dot_nt(wt_s[C_V, :], hb)
    fill(PREFILL)
    kt = jnp.concatenate([_head_rmsnorm_fm(zkt, k_gain, g) for g in range(N_KV_HEADS)], axis=0)
    for h in range(N_HEADS):
        qt_s[h * HEAD_DIM:(h + 1) * HEAD_DIM, :] = _head_rmsnorm_fm(zqt, q_gain, h).astype(BF16)

    pk_ref[...] = kt[:, TM - WINDOW:]
    pv_ref[...] = vt[:, TM - WINDOW:]

    k_s[0, WINDOW:, :] = kt[:LANES, :].T.astype(BF16)
    k_s[1, WINDOW:, :] = kt[LANES:, :].T.astype(BF16)
    vt_s[:, WINDOW:] = vt.astype(BF16)

    zeros_half = jnp.zeros((HEAD_DIM, CHUNK), BF16)
    q_lane = lax.broadcasted_iota(jnp.int32, (1, GROUP_REP * CHUNK), 1)

    def scores(p, g):
        out = []
        for c in range(2):
            tok = slice(p * PAIR + c * CHUNK, p * PAIR + (c + 1) * CHUNK)
            keys = slice(p * PAIR + c * CHUNK, p * PAIR + c * CHUNK + CHUNK_KEYS)
            cols = []
            for r in range(GROUP_REP):
                h = g * GROUP_REP + r
                qh = qt_s[h * HEAD_DIM:(h + 1) * HEAD_DIM, tok]
                cols.append(jnp.concatenate([qh, zeros_half] if g % 2 == 0 else [zeros_half, qh], axis=0))
            rhs = jnp.concatenate(cols, axis=1)
            s = _dot(k_s[g // 2, keys, :], rhs) + bias_s[g]
            if p == 0:
                kk = lax.broadcasted_iota(jnp.int32, s.shape, 0)
                s = jnp.where((kk >= WINDOW - c * CHUNK) | (t > 0), s, NEG_INF)
            out.append(s)
        return out

    def softmax(g, s_ab):
        sink = sinks_ref[g * GROUP_REP]
        for r in range(1, GROUP_REP):
            sink = jnp.where(q_lane < r * CHUNK, sink, sinks_ref[g * GROUP_REP + r])
        pads = jnp.zeros((PAIR_KEYS - CHUNK_KEYS, GROUP_REP * CHUNK), BF16)
        e_ab, inv = [], []
        for c, s in enumerate(s_ab):
            m = jnp.maximum(jnp.max(s, axis=0, keepdims=True), sink)
            e = jnp.exp(s - m)
            inv.append(1.0 / (jnp.sum(e, axis=0, keepdims=True) + jnp.exp(sink - m)))
            e_ab.append(jnp.concatenate([e.astype(BF16), pads] if c == 0 else [pads, e.astype(BF16)], axis=0))
        return jnp.concatenate(e_ab, axis=1), jnp.concatenate(inv, axis=1)

    def weighted_values(p, g, e, inv):
        tok = slice(p * PAIR, (p + 1) * PAIR)
        keys = slice(p * PAIR, p * PAIR + PAIR_KEYS)
        vg = vt_s[g * HEAD_DIM:(g + 1) * HEAD_DIM, keys]
        o_t = _dot(vg, e) * inv
        n_q = GROUP_REP * CHUNK
        for j in range(GROUP_REP // 2):
            rows = [jnp.concatenate([o_t[:, r * CHUNK:(r + 1) * CHUNK], o_t[:, n_q + r * CHUNK:n_q + (r + 1) * CHUNK]],
                                    axis=1) for r in (2 * j, 2 * j + 1)]
            lanes = slice(g * MXU_DIM + j * LANES, g * MXU_DIM + (j + 1) * LANES)
            attn_s[tok, lanes] = jnp.concatenate(rows, axis=0).T

    pu = jnp.concatenate(filled[:n_c], axis=1)
    pbuf[HIST_ROWS:, :] = pu
    hist_s[b] = pu[TM - HIST_ROWS:, :]
    sums = _window_sums(pbuf[...], _delay_rows)
    pos1 = (t * TM + 1 + lax.broadcasted_iota(jnp.int32, (TM, 1), 0)).astype(F32)
    mixed = []
    for gi, w in enumerate(POOL_WINDOWS):
        inv_cnt = 1.0 / jnp.minimum(float(w), pos1)
        mixed.append(sums[gi][HIST_ROWS:, :] * inv_cnt - pu[:, gi * POOL_GROUP:(gi + 1) * POOL_GROUP])

    blocks = [(p, g) for g in range(N_KV_HEADS) for p in range(TM // PAIR)]
    s_next = scores(*blocks[0])
    pending = None
    for i, blk in enumerate(blocks):
        s_cur = s_next
        if i + 1 < len(blocks):
            s_next = scores(*blocks[i + 1])
        if pending is not None:
            weighted_values(*pending)
        fill(PREFILL + (len(fill_cols) - PREFILL) * (i + 1) // len(blocks))
        pending = blk + softmax(blk[1], s_cur)
    weighted_values(*pending)
    pg, ag, ma, mp = (jnp.concatenate(filled[j * n_c:(j + 1) * n_c], axis=1) for j in range(1, 5))

    merged = _merge_branches(attn_s[...], mixed, (ag, pg, ma, mp), pw_ref, ps_ref, wabr_ref, wpbr_ref)

    h_s[...] = _rmsnorm_rows(xn_ref[...], ng_ref[...]).astype(BF16)
    zq_s[...] = _dot_nt(wt_s[C_Q, :], h_s[...])

    y_ref[...] = x_ref[...] + _dot(merged, wout_ref[...])

    k_s[:, 0:WINDOW, :] = k_s[:, TM:TM + WINDOW, :]
    vt_s[:, 0:WINDOW] = vt_s[:, TM:TM + WINDOW]
    pbuf[0:HIST_ROWS, :] = pbuf[TM:TM + HIST_ROWS, :]

    @pl.when(first)
    def _():
        for i in range(len(weights)):
            export_copy(i).wait()

    @pl.when((b == pl.num_programs(0) - 1) & (t == pl.num_programs(1) - 1))
    def _():
        pp_ref[...] = jnp.swapaxes(hist_s[...], 0, 1)[HIST_ROWS - POOL_HIST:]


def _resident(shape):
    nd = len(shape)
    return pl.BlockSpec(shape, lambda b, t: (0,) * nd, pipeline_mode=pl.Buffered(1))


def _prompt_layer(x, tbl, sinks, ng, qg, kg, bkp, ps, weights):
    B, S, _ = x.shape
    assert S % TM == 0 and TM % PAIR == 0 and TM >= WINDOW
    assert all(w.shape[0] % STAGE_ROWS == 0 for w in weights)
    consts = (ng, qg, kg, bkp, ps)
    hbm = pl.BlockSpec(memory_space=pl.ANY)
    n_t = S // TM

    def next_tile(b, t):
        i = jnp.minimum(b * n_t + t + 1, B * n_t - 1)
        return i // n_t, i % n_t, 0

    return pl.pallas_call(
        _prompt_kernel,
        grid=(B, S // TM),
        in_specs=[pl.BlockSpec(memory_space=pltpu.SMEM),
                  pl.BlockSpec(memory_space=pltpu.SMEM),
                  pl.BlockSpec((None, TM, D_MODEL), lambda b, t: (b, t, 0)),
                  pl.BlockSpec((None, TM, D_MODEL), next_tile)]
                 + [_resident(c.shape) for c in consts] + [hbm] * len(weights),
        out_specs=[
            pl.BlockSpec((None, TM, D_MODEL), lambda b, t: (b, t, 0)),
            pl.BlockSpec((None, KV_WIDTH, WINDOW), lambda b, t: (b, 0, 0)),
            pl.BlockSpec((None, KV_WIDTH, WINDOW), lambda b, t: (b, 0, 0)),
            pl.BlockSpec((POOL_HIST, B, POOL_WIDTH), lambda b, t: (0, 0, 0)),
        ] + [hbm] * len(weights),
        out_shape=[
            jax.ShapeDtypeStruct((B, S, D_MODEL), F32),
            jax.ShapeDtypeStruct((B, KV_WIDTH, WINDOW), F32),
            jax.ShapeDtypeStruct((B, KV_WIDTH, WINDOW), F32),
            jax.ShapeDtypeStruct((POOL_HIST, B, POOL_WIDTH), F32),
        ] + [jax.ShapeDtypeStruct(w.shape, BF16) for w in weights],
        scratch_shapes=[pltpu.VMEM(w.shape, BF16) for w in weights] + [
            pltpu.VMEM((N_STAGE, STAGE_ROWS, STAGE_COLS), F32),
            pltpu.SemaphoreType.DMA((N_STAGE,)),
            pltpu.SemaphoreType.DMA((len(weights),)),
            pltpu.VMEM((C_V.stop, D_MODEL), BF16),
            pltpu.VMEM((TM, D_MODEL), BF16),
            pltpu.VMEM((ATTN_WIDTH, TM), BF16),
            pltpu.VMEM((2, WINDOW + TM, LANES), BF16),
            pltpu.VMEM((KV_WIDTH, WINDOW + TM), BF16),
            pltpu.VMEM((TM, ATTN_WIDTH), F32),
            pltpu.VMEM((HIST_ROWS + TM, POOL_WIDTH), F32),
            pltpu.VMEM((B, HIST_ROWS, POOL_WIDTH), F32),
            pltpu.VMEM((N_KV_HEADS, CHUNK_KEYS, GROUP_REP * CHUNK), F32),
            pltpu.VMEM((HEAD_DIM, 2), F32),
        ],
        compiler_params=pltpu.CompilerParams(
            dimension_semantics=("arbitrary", "arbitrary"),
            vmem_limit_bytes=VMEM_LIMIT,
        ),
        name="prompt_layer",
    )(tbl, sinks, x, x, *consts, *weights)


def _sample_kernel(n_b, n_s, start,
                   tbl_ref, sinks_ref, x_ref, ng_ref,
                   qg_ref, kg_ref, ones_ref, bks_ref, bkn_ref, ps_ref,
                   skt_hbm, svt_hbm, sp_hbm, win_hbm, pw_hbm, wabr_hbm, wpbr_hbm, wout_hbm,
                   y_ref, ok_hbm, ov_hbm, op_hbm,
                   skt_ref, svt_ref, sp_ref, ok_ref, ov_ref, op_ref,
                   win_ref, pw_ref, wabr_ref, wpbr_ref, wout_ref, sem):
    n_tok = n_b * n_s
    qkv = slice(0, C_V.stop)
    rest = slice(C_V.stop, win_ref.shape[1])
    moves = ((win_hbm.at[:, qkv], win_ref.at[:, qkv]), (skt_hbm, skt_ref), (svt_hbm, svt_ref),
             (win_hbm.at[:, rest], win_ref.at[:, rest]), (sp_hbm, sp_ref), (pw_hbm, pw_ref),
             (wabr_hbm, wabr_ref), (wpbr_hbm, wpbr_ref), (wout_hbm, wout_ref),
             (ok_ref, ok_hbm), (ov_ref, ov_hbm), (op_ref, op_hbm))
    assert len(moves) == N_SAMPLE_MOVES
    moves = [pltpu.make_async_copy(src, dst, sem.at[i]) for i, (src, dst) in enumerate(moves)]
    (qkv_in, k_in, v_in), late_in, (k_out, v_out, p_out) = moves[:3], moves[3:9], moves[9:]
    for c in (qkv_in, k_in, v_in):
        c.start(priority=0)
    for c in late_in:
        c.start(priority=1)

    x = x_ref[...]
    hb = _rmsnorm_rows(x, ng_ref[...]).astype(BF16)

    qkv_in.wait()
    q_gain = jnp.concatenate([qg_ref[...] * (HEAD_DIM ** -0.5)] * N_HEADS, axis=1)
    q = _head_rmsnorm(_dot(hb, win_ref[:, C_Q]), ones_ref, q_gain).astype(BF16)
    q3 = q.reshape(n_b, n_s, ATTN_WIDTH)
    zkt = _dot_nt(win_ref[:, C_K].T, hb)
    k_gain = _as_column(kg_ref[...])
    knt = jnp.concatenate([_head_rmsnorm_fm(zkt, k_gain, g) for g in range(N_KV_HEADS)], axis=0)
    vnt = _dot_nt(win_ref[:, C_V].T, hb)

    k_in.wait()
    v_in.wait()
    lane = lax.broadcasted_iota(jnp.int32, (1, WINDOW), 1)
    per_tile = LANES // n_s
    for s in range(n_b):
        tile = slice((s // per_tile) * LANES, (s // per_tile + 1) * LANES)
        shift = (WINDOW - n_s - (s % per_tile) * n_s) % LANES
        for new, st_ref, o_ref in ((knt, skt_ref, ok_ref), (vnt, svt_ref, ov_ref)):
            fresh = new[:, tile] if shift == 0 else pltpu.roll(new[:, tile], shift, axis=1)
            o_ref[s] = jnp.where(lane < WINDOW - n_s, pltpu.roll(st_ref[s], WINDOW - n_s, axis=1), fresh)

    knb = knt.astype(BF16)
    vnb = vnt.astype(BF16)
    key_stream = (lax.broadcasted_iota(jnp.int32, (n_b, 1, 2 * n_tok), 2) & (n_tok - 1)) >> (n_s.bit_length() - 1)
    own = key_stream == lax.broadcasted_iota(jnp.int32, (n_b, 1, 2 * n_tok), 0)
    top = lax.broadcasted_iota(jnp.int32, (1, 2 * n_s, 1), 1) < n_s
    low = lax.broadcasted_iota(jnp.int32, (1, 1, LANES), 2) < HEAD_DIM

    def bias_rows(bucket, g):
        return jnp.concatenate([jnp.concatenate(
            [_gather_bias(bucket, tbl_ref, g * GROUP_REP + 2 * c + j) for j in range(2)], axis=1)
            for c in range(2)], axis=0)

    def scores(g):
        hd = slice(g * HEAD_DIM, (g + 1) * HEAD_DIM)
        c = g * MXU_DIM
        lhs = jnp.concatenate([q3[:, :, c:c + LANES], q3[:, :, c + LANES:c + MXU_DIM]], axis=1)
        kst = _diag2(skt_ref[:, hd, :].astype(BF16), 1, 2)
        s_st = jnp.stack([_dot(lhs[s], kst[s]) for s in range(n_b)]) + bias_rows(bks_ref[...], g)[None]
        s_nw = _dot(lhs.reshape(2 * n_tok, LANES), _diag2(knb[hd, :], 0, 1))
        s_nw = jnp.where(own, s_nw.reshape(n_b, 2 * n_s, 2 * n_tok) + bias_rows(bkn_ref[...], g)[None], NEG_INF)
        return s_st, s_nw

    def finish(g, s_st, s_nw):
        hd = slice(g * HEAD_DIM, (g + 1) * HEAD_DIM)
        p_st, p_nw, inv = [], [], []
        for j in range(2):
            a = s_st[:, :, j * WINDOW:(j + 1) * WINDOW]
            b = s_nw[:, :, j * n_tok:(j + 1) * n_tok]
            sink = jnp.where(top, sinks_ref[4 * g + j], sinks_ref[4 * g + 2 + j])
            m = jnp.maximum(jnp.maximum(jnp.max(a, axis=-1, keepdims=True),
                                        jnp.max(b, axis=-1, keepdims=True)), sink)
            ea = jnp.exp(a - m)
            eb = jnp.exp(b - m)
            inv.append(1.0 / (jnp.sum(ea, axis=-1, keepdims=True) + jnp.sum(eb, axis=-1, keepdims=True)
                              + jnp.exp(sink - m)))
            p_st.append(ea.astype(BF16))
            p_nw.append(eb.astype(BF16))
        p_st = jnp.concatenate(p_st, axis=2)
        p_nw = jnp.concatenate(p_nw, axis=2).reshape(2 * n_tok, 2 * n_tok)
        vst = _diag2(svt_ref[:, hd, :].astype(BF16), 1, 2)
        o = jnp.stack([_dot_nt(p_st[s], vst[s]) for s in range(n_b)])
        o = o + _dot_nt(p_nw, _diag2(vnb[hd, :], 0, 1)).reshape(n_b, 2 * n_s, LANES)
        o = o * jnp.where(low, inv[0], inv[1])
        return jnp.concatenate([o[:, :n_s, :], o[:, n_s:, :]], axis=2)

    outs = []
    nxt = scores(0)
    for g in range(N_KV_HEADS):
        cur = nxt
        if g + 1 < N_KV_HEADS:
            nxt = scores(g + 1)
        outs.append(finish(g, *cur))
    attn_o = jnp.concatenate(outs, axis=2).reshape(n_tok, ATTN_WIDTH)

    for c in late_in:
        c.wait()
    k_out.start()
    v_out.start()
    pu = _dot(hb, win_ref[:, C_PU])
    pu_t = jnp.swapaxes(pu.reshape(n_b, n_s, POOL_WIDTH), 0, 1)
    full = jnp.concatenate([jnp.zeros((1, n_b, POOL_WIDTH), F32), sp_ref[...], pu_t], axis=0)
    op_ref[...] = full[HIST_ROWS + n_s - POOL_HIST:]
    p_out.start()
    sums = _window_sums(full, _delay_leading)
    pos1 = (start + 1 + lax.broadcasted_iota(jnp.int32, (n_s, 1, 1), 0)).astype(F32)
    mixed = []
    for gi, w in enumerate(POOL_WINDOWS):
        inv_cnt = 1.0 / jnp.minimum(float(w), pos1)
        sw = sums[gi][HIST_ROWS:] * inv_cnt - pu_t[..., gi * POOL_GROUP:(gi + 1) * POOL_GROUP]
        mixed.append(sw.reshape(n_tok, POOL_GROUP))

    def stream_major(po):
        return jnp.swapaxes(po.reshape(n_s, n_b, POOL_WIDTH), 0, 1).reshape(n_tok, POOL_WIDTH)

    gates = tuple(_dot(hb, win_ref[:, c]) for c in (C_AG, C_PG, C_MA, C_MP))
    merged = _merge_branches(attn_o, mixed, gates, pw_ref, ps_ref, wabr_ref, wpbr_ref, stream_major)
    y_ref[...] = x + _dot(merged, wout_ref[...])
    for c in (k_out, v_out, p_out):
        c.wait()


def _sample_layer(x, skt, svt, sp, start, tbl, sinks, ng, w_in, qg, kg, ones, bks, bkn,
                  pw, ps, wabr, wpbr, wout):
    n_b, n_s, _ = x.shape
    n_tok = n_b * n_s
    assert n_s % 16 == 0 and LANES % n_s == 0 and n_s >= POOL_HIST and n_tok % LANES == 0
    assert n_s & (n_s - 1) == 0 and n_tok & (n_tok - 1) == 0
    vmem = pl.BlockSpec(memory_space=pltpu.VMEM)
    anywhere = pl.BlockSpec(memory_space=pl.ANY)
    weights = (w_in, pw, wabr, wpbr, wout)
    state = (skt, svt, sp)
    new_state = [jax.ShapeDtypeStruct(a.shape, a.dtype) for a in state]
    y, ok, ov, op = pl.pallas_call(
        functools.partial(_sample_kernel, n_b, n_s, start),
        in_specs=[pl.BlockSpec(memory_space=pltpu.SMEM)] * 2 + [vmem] * 8
                 + [anywhere] * (len(state) + len(weights)),
        out_specs=[vmem] + [anywhere] * len(state),
        out_shape=[jax.ShapeDtypeStruct((n_tok, D_MODEL), F32)] + new_state,
        scratch_shapes=[pltpu.VMEM(a.shape, a.dtype) for a in state + tuple(new_state) + weights]
                       + [pltpu.SemaphoreType.DMA((N_SAMPLE_MOVES,))],
        compiler_params=pltpu.CompilerParams(vmem_limit_bytes=VMEM_LIMIT),
        name="sample_layer",
    )(tbl, sinks, x.reshape(n_tok, D_MODEL), ng, qg, kg, ones, bks, bkn, ps,
      *state, *weights)
    return y.reshape(n_b, n_s, D_MODEL), ok, ov, op


def _kv_to_feature_major(a):
    return jnp.transpose(a, (0, 2, 3, 1)).reshape(a.shape[0], KV_WIDTH, WINDOW)


def _kv_from_feature_major(a):
    return jnp.transpose(a.reshape(a.shape[0], N_KV_HEADS, HEAD_DIM, WINDOW), (0, 3, 1, 2))


def kernel(x_prompt, x_sample, state_attn_k, state_attn_v, state_pool, norm_gain, w_in, q_norm_gain, k_norm_gain, attn_sinks, rel_bias, pool_w, pool_scale, w_attn_br, w_pool_br, w_out):
    depth = w_in.shape[0]
    n_b, n_s, _ = x_sample.shape
    past_len = 4096
    bkp, bks, bkn = _bucket_tables(n_b, n_s)
    tbl = rel_bias.T
    seg = np.arange(MXU_DIM) // HEAD_DIM
    ones = jnp.asarray(seg[:, None] == seg[None, :], BF16)

    xp, xs = x_prompt, x_sample
    pk, pv, pp, sk, sv, sp = [], [], [], [], [], []
    for l in range(depth):
        ng = norm_gain[l].reshape(1, D_MODEL)
        qg = q_norm_gain[l].reshape(1, HEAD_DIM)
        kg = k_norm_gain[l].reshape(1, HEAD_DIM)
        ps = pool_scale[l].reshape(1, POOL_WIDTH)
        xp, k_l, v_l, p_l, w_in_b, wabr_b, wpbr_b, wout_b, pw_b = _prompt_layer(
            xp, tbl, attn_sinks[l], ng, qg, kg, bkp, ps,
            (w_in[l], w_attn_br[l], w_pool_br[l], w_out[l], pool_w[l].reshape(POOL_WIDTH, POOL_GROUP)))
        tail = (pw_b, ps, wabr_b, wpbr_b, wout_b)
        pk.append(_kv_from_feature_major(k_l))
        pv.append(_kv_from_feature_major(v_l))
        pp.append(jnp.transpose(p_l, (1, 0, 2)))

        xs, k_l, v_l, p_l = _sample_layer(
            xs,
            _kv_to_feature_major(state_attn_k[l]),
            _kv_to_feature_major(state_attn_v[l]),
            jnp.transpose(state_pool[l], (1, 0, 2)),
            past_len, tbl, attn_sinks[l], ng, w_in_b, qg, kg, ones, bks, bkn, *tail)
        sk.append(_kv_from_feature_major(k_l))
        sv.append(_kv_from_feature_major(v_l))
        sp.append(jnp.transpose(p_l, (1, 0, 2)))
    return (xp, xs, jnp.stack(pk), jnp.stack(pv), jnp.stack(pp),
            jnp.stack(sk), jnp.stack(sv), jnp.stack(sp))
```
